```python
import jax, jax.numpy as jnp
from jax import lax
import numpy as np

D_MODEL = 1024
BATCH = 16
SEQ = 256
DEPTH = 4
DEC_BATCH = 2
DEC_SEQ = 1024
PAST_LEN = 512

GRID_W = 64
NA_WIDTH = D_MODEL // 2
NA_HEAD_DIM = 64
NA_HEADS = NA_WIDTH // NA_HEAD_DIM
NA_WIN_ROWS = 8
NA_WIN_COLS = 16
GLA_V_WIDTH = D_MODEL - NA_WIDTH
GLA_HEADS = 4
GLA_DV = GLA_V_WIDTH // GLA_HEADS
GLA_DK = GLA_DV // 2
GLA_K_WIDTH = GLA_HEADS * GLA_DK
GLA_GATE_RANK = 16
GLA_GATE_TAU = 16.0
GLA_CHUNK = 64
D_FF = 4 * D_MODEL
ROPE_BASE = 10000.0
ATTN_QBLOCK = 128
N_MOD = 6
EPS = 1e-6
SPLIT_SIZES = [NA_WIDTH] * 3 + [GLA_K_WIDTH] * 2 + [GLA_V_WIDTH] * 2 + [GLA_GATE_RANK] * 2
SPLITS = np.cumsum(SPLIT_SIZES)[:-1].tolist()
IN_WIDTH = int(sum(SPLIT_SIZES))

kernel_name = "hybrid_na_gla_diffusion_step"


def rmsnorm(x, g):
    xf = x.astype(jnp.float32)
    y = xf * lax.rsqrt(jnp.mean(xf * xf, axis=-1, keepdims=True) + EPS)
    return (y * g.astype(jnp.float32)).astype(x.dtype)


def heads(t, n):
    B, L, _ = t.shape
    return t.reshape(B, L, n, -1).transpose(0, 2, 1, 3)


def ada(cvec, w, b):
    m = jax.nn.silu(cvec) @ w + b
    return jnp.split(m, N_MOD, axis=-1)


def axial_rope(x):
    L, d = x.shape[2], x.shape[3]
    half = d // 2
    nf = half // 2
    inv = ROPE_BASE ** (-jnp.arange(nf, dtype=jnp.float32) / nf)
    t = jnp.arange(L)
    ang_r = (t // GRID_W).astype(jnp.float32)[:, None] * inv
    ang_c = (t % GRID_W).astype(jnp.float32)[:, None] * inv

    def rot(xh, ang):
        cos = jnp.cos(ang).astype(x.dtype)
        sin = jnp.sin(ang).astype(x.dtype)
        x1, x2 = xh[..., :nf], xh[..., nf:]
        return jnp.concatenate([x1 * cos - x2 * sin, x1 * sin + x2 * cos], axis=-1)

    return jnp.concatenate([rot(x[..., :half], ang_r), rot(x[..., half:], ang_c)], axis=-1)


def project(h, w_in, g_q, g_k, w_gf, b_gf, w_gb, b_gb):
    z = h @ w_in
    q, k, v, gq, gk, gv, gate, zf, zb = jnp.split(z, SPLITS, axis=-1)
    q = rmsnorm(heads(q, NA_HEADS), g_q)
    k = rmsnorm(heads(k, NA_HEADS), g_k)
    v = heads(v, NA_HEADS)
    gq = heads(gq, GLA_HEADS)
    gk = heads(gk, GLA_HEADS)
    gv = heads(gv, GLA_HEADS)
    lf = heads(jax.nn.log_sigmoid((zf @ w_gf + b_gf).astype(jnp.float32)) / GLA_GATE_TAU, GLA_HEADS)
    lb = heads(jax.nn.log_sigmoid((zb @ w_gb + b_gb).astype(jnp.float32)) / GLA_GATE_TAU, GLA_HEADS)
    return q, k, v, gq, gk, gv, gate, lf, lb


def gla_chunked(q, k, v, logf, s0):
    B, H, L, dk = q.shape
    dv = v.shape[-1]
    n = L // GLA_CHUNK

    def to_chunks(t):
        return jnp.moveaxis(t.astype(jnp.float32).reshape(B, H, n, GLA_CHUNK, t.shape[-1]), 2, 0)

    causal = jnp.tril(jnp.ones((GLA_CHUNK, GLA_CHUNK), dtype=bool))[..., None]

    def step(S, inp):
        qi, ki, vi, fi = inp
        b = jnp.cumsum(fi, axis=-2)
        diff = b[..., :, None, :] - b[..., None, :, :]
        decay = jnp.exp(jnp.where(causal, diff, -jnp.inf))
        attn = jnp.einsum('bhid,bhjd,bhijd->bhij', qi, ki, decay)
        o = jnp.einsum('bhij,bhje->bhie', attn, vi) + jnp.einsum('bhid,bhde->bhie', qi * jnp.exp(b), S)
        b_last = b[..., -1:, :]
        S = jnp.exp(b_last)[..., 0, :, None] * S + jnp.einsum('bhjd,bhje->bhde', ki * jnp.exp(b_last - b), vi)
        return S, o

    S, o = lax.scan(step, s0.astype(jnp.float32), (to_chunks(q), to_chunks(k), to_chunks(v), to_chunks(logf)))
    o = jnp.moveaxis(o, 0, 2).reshape(B, H, L, dv)
    return o, S


def gla_bidir(q, k, v, lf, lb, sf0, sb0):
    q = q * (GLA_DK ** -0.5)
    of, sf = gla_chunked(q, k, v, lf, sf0)
    flip = lambda t: jnp.flip(t, axis=2)
    ob, sb = gla_chunked(flip(q), flip(k), flip(v), flip(lb), sb0)
    return (of + flip(ob)).astype(v.dtype), sf, sb


def context_attention(q, k, v):
    B, H, L, d = q.shape
    nb = L // ATTN_QBLOCK
    qb = jnp.moveaxis(q.reshape(B, H, nb, ATTN_QBLOCK, d), 2, 0)
    scale = d ** -0.5

    def blk(qi):
        s = jnp.einsum('bhqd,bhkd->bhqk', qi, k).astype(jnp.float32) * scale
        p = jax.nn.softmax(s, axis=-1).astype(v.dtype)
        return jnp.einsum('bhqk,bhkd->bhqd', p, v)

    o = lax.map(blk, qb)
    return jnp.moveaxis(o, 0, 2).reshape(B, H, L, d)


def neighbourhood_attention(q, k, v, kc, vc, rpb):
    B, H, N, d = q.shape
    rows = N // GRID_W
    wr = min(NA_WIN_ROWS, rows)
    wc = NA_WIN_COLS
    nw = wr * wc
    cols = np.arange(GRID_W)
    cs = np.clip(cols - wc // 2, 0, GRID_W - wc)
    col_idx = cs[:, None] + np.arange(wc)[None, :]
    dc = col_idx - cols[:, None] + (NA_WIN_COLS - 1)
    kg = k.reshape(B, H, rows, GRID_W, d)
    vg = v.reshape(B, H, rows, GRID_W, d)
    qg = q.reshape(B, H, rows, GRID_W, d)
    scale = d ** -0.5

    def row_fn(args):
        r, q_row = args
        rs = jnp.clip(r - wr // 2, 0, rows - wr)
        k_rows = lax.dynamic_slice_in_dim(kg, rs, wr, axis=2)
        v_rows = lax.dynamic_slice_in_dim(vg, rs, wr, axis=2)
        k_win = k_rows[:, :, :, col_idx].transpose(0, 1, 3, 2, 4, 5).reshape(B, H, GRID_W, nw, d)
        v_win = v_rows[:, :, :, col_idx].transpose(0, 1, 3, 2, 4, 5).reshape(B, H, GRID_W, nw, d)
        dr = rs + jnp.arange(wr) - r + (NA_WIN_ROWS - 1)
        bias = rpb[:, dr][:, :, dc].transpose(0, 2, 1, 3).reshape(H, GRID_W, nw)
        s_win = jnp.einsum('bhcd,bhckd->bhck', q_row, k_win).astype(jnp.float32) * scale + bias.astype(jnp.float32)
        s_ctx = jnp.einsum('bhcd,bhld->bhcl', q_row, kc).astype(jnp.float32) * scale
        p = jax.nn.softmax(jnp.concatenate([s_win, s_ctx], axis=-1), axis=-1).astype(v.dtype)
        return (jnp.einsum('bhck,bhckd->bhcd', p[..., :nw], v_win)
                + jnp.einsum('bhcl,bhld->bhcd', p[..., nw:], vc))

    o = lax.map(row_fn, (jnp.arange(rows), jnp.moveaxis(qg, 2, 0)))
    return jnp.moveaxis(o, 0, 2).reshape(B, H, N, d)


def merge(o_na, o_gla, gate, g_out, w_o):
    B, _, L, _ = o_na.shape
    a = o_na.transpose(0, 2, 1, 3).reshape(B, L, NA_WIDTH)
    g = rmsnorm(o_gla, g_out).transpose(0, 2, 1, 3).reshape(B, L, GLA_V_WIDTH) * jax.nn.silu(gate)
    return jnp.concatenate([a, g], axis=-1) @ w_o


def mlp(h, w_up, w_down):
    return jnp.square(jax.nn.relu(h @ w_up)) @ w_down


def setup_inputs(seed: int = 0) -> dict:
    key = jax.random.key(seed)
    ks = jax.random.split(key, 24)
    nrm = lambda k, s: jax.random.normal(k, s, dtype=jnp.float32)
    D = D_MODEL
    return {
        "x_prompt": nrm(ks[0], (BATCH, SEQ, D)),
        "x_sample": nrm(ks[1], (DEC_BATCH, DEC_SEQ, D)),
        "cache_k": nrm(ks[2], (DEC_BATCH, DEPTH, NA_HEADS, PAST_LEN, NA_HEAD_DIM)),
        "cache_v": nrm(ks[3], (DEC_BATCH, DEPTH, NA_HEADS, PAST_LEN, NA_HEAD_DIM)),
        "state_fwd": nrm(ks[4], (DEC_BATCH, DEPTH, GLA_HEADS, GLA_DK, GLA_DV)),
        "state_bwd": nrm(ks[5], (DEC_BATCH, DEPTH, GLA_HEADS, GLA_DK, GLA_DV)),
        "c": nrm(ks[6], (DEC_BATCH, D)),
        "c_ctx": nrm(ks[7], (D,)),
        "w_ada": nrm(ks[8], (DEPTH, D, N_MOD * D)) * (0.5 * D ** -0.5),
        "b_ada": nrm(ks[9], (DEPTH, N_MOD * D)) * 0.02,
        "g_attn": 1.0 + 0.02 * nrm(ks[10], (DEPTH, D)),
        "w_in": nrm(ks[11], (DEPTH, D, IN_WIDTH)) * D ** -0.5,
        "g_q": 1.0 + 0.02 * nrm(ks[12], (DEPTH, NA_HEAD_DIM)),
        "g_k": 1.0 + 0.02 * nrm(ks[13], (DEPTH, NA_HEAD_DIM)),
        "rpb": 0.1 * nrm(ks[14], (DEPTH, NA_HEADS, 2 * NA_WIN_ROWS - 1, 2 * NA_WIN_COLS - 1)),
        "w_gf": nrm(ks[15], (DEPTH, GLA_GATE_RANK, GLA_K_WIDTH)) * GLA_GATE_RANK ** -0.5,
        "b_gf": 0.1 * nrm(ks[16], (DEPTH, GLA_K_WIDTH)),
        "w_gb": nrm(ks[17], (DEPTH, GLA_GATE_RANK, GLA_K_WIDTH)) * GLA_GATE_RANK ** -0.5,
        "b_gb": 0.1 * nrm(ks[18], (DEPTH, GLA_K_WIDTH)),
        "g_gla_out": 1.0 + 0.02 * nrm(ks[19], (DEPTH, GLA_DV)),
        "w_o": nrm(ks[20], (DEPTH, D, D)) * D ** -0.5,
        "g_mlp": 1.0 + 0.02 * nrm(ks[21], (DEPTH, D)),
        "w_up": nrm(ks[22], (DEPTH, D, D_FF)) * D ** -0.5,
        "w_down": nrm(ks[23], (DEPTH, D_FF, D)) * D_FF ** -0.5,
    }


def reference(x_prompt, x_sample, cache_k, cache_v, state_fwd, state_bwd, c, c_ctx,
              w_ada, b_ada, g_attn, w_in, g_q, g_k, rpb, w_gf, b_gf, w_gb, b_gb,
              g_gla_out, w_o, g_mlp, w_up, w_down):
    y = x_prompt
    nb = x_prompt.shape[0]
    ks_, vs_, sfs, sbs = [], [], [], []
    for l in range(DEPTH):
        sh1, sc1, ga1, sh2, sc2, ga2 = ada(c_ctx, w_ada[l], b_ada[l])
        h = rmsnorm(y, g_attn[l]) * (1 + sc1) + sh1
        q, k, v, gq, gk, gv, gate, lf, lb = project(h, w_in[l], g_q[l], g_k[l], w_gf[l], b_gf[l], w_gb[l], b_gb[l])
        o_na = context_attention(q, k, v)
        zeros = jnp.zeros((nb, GLA_HEADS, GLA_DK, GLA_DV), jnp.float32)
        o_gla, sf, sb = gla_bidir(gq, gk, gv, lf, lb, zeros, zeros)
        y = y + ga1 * merge(o_na, o_gla, gate, g_gla_out[l], w_o[l])
        h = rmsnorm(y, g_mlp[l]) * (1 + sc2) + sh2
        y = y + ga2 * mlp(h, w_up[l], w_down[l])
        ks_.append(k)
        vs_.append(v)
        sfs.append(sf)
        sbs.append(sb)
    new_k = jnp.stack(ks_, axis=1)
    new_v = jnp.stack(vs_, axis=1)
    new_sf = jnp.stack(sfs, axis=1)
    new_sb = jnp.stack(sbs, axis=1)

    x = x_sample
    for l in range(DEPTH):
        sh1, sc1, ga1, sh2, sc2, ga2 = [m[:, None, :] for m in ada(c, w_ada[l], b_ada[l])]
        h = rmsnorm(x, g_attn[l]) * (1 + sc1) + sh1
        q, k, v, gq, gk, gv, gate, lf, lb = project(h, w_in[l], g_q[l], g_k[l], w_gf[l], b_gf[l], w_gb[l], b_gb[l])
        gq = axial_rope(gq)
        gk = axial_rope(gk)
        o_na = neighbourhood_attention(q, k, v, cache_k[:, l], cache_v[:, l], rpb[l])
        o_gla, _, _ = gla_bidir(gq, gk, gv, lf, lb, state_fwd[:, l], state_bwd[:, l])
        x = x + ga1 * merge(o_na, o_gla, gate, g_gla_out[l], w_o[l])
        h = rmsnorm(x, g_mlp[l]) * (1 + sc2) + sh2
        x = x + ga2 * mlp(h, w_up[l], w_down[l])

    return (y, x, new_k, new_v, new_sf, new_sb)
```

```python
import functools

import numpy as np
import jax
import jax.numpy as jnp
from jax import lax
from jax.experimental import pallas as pl
from jax.experimental.pallas import tpu as pltpu

F32 = jnp.float32
BF16 = jnp.bfloat16

D_MODEL = 1024
BATCH = 16
SEQ = 256
DEPTH = 4
DEC_BATCH = 2
DEC_SEQ = 1024
PAST_LEN = 512
GRID_W = 64
GRID_ROWS = DEC_SEQ // GRID_W
NA_WIDTH = D_MODEL // 2
NA_HEAD_DIM = 64
NA_HEADS = NA_WIDTH // NA_HEAD_DIM
NA_WIN_ROWS = 8
NA_WIN_COLS = 16
GLA_V_WIDTH = D_MODEL - NA_WIDTH
GLA_HEADS = 4
GLA_DV = GLA_V_WIDTH // GLA_HEADS
GLA_DK = GLA_DV // 2
GLA_K_WIDTH = GLA_HEADS * GLA_DK
GLA_GATE_RANK = 16
GLA_GATE_TAU = 16.0
GLA_CHUNK = 64
D_FF = 4 * D_MODEL
ROPE_BASE = 10000.0
N_MOD = 6
EPS = 1e-6

N_CTX = BATCH * SEQ
N_SMP = DEC_BATCH * DEC_SEQ
N_TOK = N_CTX + N_SMP
MOD_ROWS = 8
MASK_VALUE = -1e30

TOKEN_TILE = 512
FF_CHUNK = 1024
ADA_TILE_N = 1536
GATE_PAD = 128
GLA_LEVELS = 6
VMEM_LIMIT = 52 * 1024 * 1024


def _dot(a, b):
    return jnp.dot(a, b, preferred_element_type=F32)


def _dot_nt(a, b):
    return lax.dot_general(a, b, (((1,), (1,)), ((), ())), preferred_element_type=F32)


def _dot_tn(a, b):
    return lax.dot_general(a, b, (((0,), (0,)), ((), ())), preferred_element_type=F32)


def _params(n_grid_dims=1):
    return pltpu.CompilerParams(dimension_semantics=("arbitrary",) * n_grid_dims,
                                vmem_limit_bytes=VMEM_LIMIT)


def _mod_row(i):
    ctx_tiles = N_CTX // TOKEN_TILE
    tiles_per_seq = DEC_SEQ // TOKEN_TILE
    return jnp.where(i < ctx_tiles, 0, (i - ctx_tiles) // tiles_per_seq + 1)


def _ada_kernel(c_ref, w_ref, b_ref, o_ref):
    cv = c_ref[...]
    s = cv * (1.0 / (1.0 + jnp.exp(-cv)))
    o_ref[0] = _dot(s.astype(BF16), w_ref[0].astype(BF16)) + b_ref[0]


def _ada_call(c_rows, w_ada, b_ada):
    n_mod = N_MOD * D_MODEL
    return pl.pallas_call(
        _ada_kernel,
        grid=(DEPTH, n_mod // ADA_TILE_N),
        in_specs=[
            pl.BlockSpec((MOD_ROWS, D_MODEL), lambda l, j: (0, 0)),
            pl.BlockSpec((1, D_MODEL, ADA_TILE_N), lambda l, j: (l, 0, j)),
            pl.BlockSpec((1, 1, ADA_TILE_N), lambda l, j: (l, 0, j)),
        ],
        out_specs=pl.BlockSpec((1, MOD_ROWS, ADA_TILE_N), lambda l, j: (l, 0, j)),
        out_shape=jax.ShapeDtypeStruct((DEPTH, MOD_ROWS, n_mod), F32),
        compiler_params=_params(2),
        name="ada_mod",
    )(c_rows, w_ada, b_ada.reshape(DEPTH, 1, n_mod))


def _log_sigmoid(x):
    return jnp.minimum(x, 0.0) - jnp.log1p(jnp.exp(-jnp.abs(x)))


def _proj_kernel(x_ref, mod_ref, g_ref, wqkv_ref, wg_ref, wz_ref, seg_ref, gq_ref, gk_ref,
                 wgate_ref, bgate_ref,
                 q_out, k_out, v_out, gq_out, gk_out, gv_out, gate_out, lf_out, lb_out):
    x = x_ref[...]
    mod = mod_ref[0]
    sh1 = mod[:, 0:D_MODEL]
    sc1 = mod[:, D_MODEL:2 * D_MODEL]
    ms = jnp.mean(x * x, axis=-1, keepdims=True)
    h = (x * lax.rsqrt(ms + EPS) * g_ref[...]) * (1.0 + sc1) + sh1
    hb = h.astype(BF16)

    z = _dot(hb, wqkv_ref[...])
    q = z[:, 0:NA_WIDTH]
    k = z[:, NA_WIDTH:2 * NA_WIDTH]
    seg = seg_ref[...]
    q_ms = _dot((q * q).astype(BF16), seg)
    k_ms = _dot((k * k).astype(BF16), seg)
    q_out[...] = (q * lax.rsqrt(q_ms + EPS) * gq_ref[...] * (NA_HEAD_DIM ** -0.5)).astype(BF16)
    k_out[...] = k * lax.rsqrt(k_ms + EPS) * gk_ref[...]
    v_out[...] = z[:, 2 * NA_WIDTH:3 * NA_WIDTH]

    z = _dot(hb, wg_ref[...])
    gq_out[...] = z[:, 0:GLA_K_WIDTH] * (GLA_DK ** -0.5)
    gk_out[...] = z[:, GLA_K_WIDTH:2 * GLA_K_WIDTH]
    gv_out[...] = z[:, 2 * GLA_K_WIDTH:2 * GLA_K_WIDTH + GLA_V_WIDTH].astype(BF16)
    gate_out[...] = z[:, 2 * GLA_K_WIDTH + GLA_V_WIDTH:]

    zz = _dot(hb, wz_ref[...])
    pre = _dot(zz.astype(BF16), wgate_ref[...]) + bgate_ref[...]
    ls = _log_sigmoid(pre) * (1.0 / GLA_GATE_TAU)
    lf_out[...] = ls[:, 0:GLA_K_WIDTH]
    lb_out[...] = ls[:, GLA_K_WIDTH:]


def _proj_call(x, mod_l, g_attn_l, wqkv, wg, wz, seg, gq_t, gk_t, wgate, bgate):
    tm = TOKEN_TILE
    row = lambda i: (i, 0)
    const = lambda i: (0, 0)
    wqkv_n = 3 * NA_WIDTH
    wg_n = 2 * GLA_K_WIDTH + 2 * GLA_V_WIDTH
    out_shapes = (
        jax.ShapeDtypeStruct((N_TOK, NA_WIDTH), BF16),
        jax.ShapeDtypeStruct((N_TOK, NA_WIDTH), F32),
        jax.ShapeDtypeStruct((N_TOK, NA_WIDTH), F32),
        jax.ShapeDtypeStruct((N_TOK, GLA_K_WIDTH), F32),
        jax.ShapeDtypeStruct((N_TOK, GLA_K_WIDTH), F32),
        jax.ShapeDtypeStruct((N_TOK, GLA_V_WIDTH), BF16),
        jax.ShapeDtypeStruct((N_TOK, GLA_V_WIDTH), F32),
        jax.ShapeDtypeStruct((N_TOK, GLA_K_WIDTH), F32),
        jax.ShapeDtypeStruct((N_TOK, GLA_K_WIDTH), F32),
    )
    return pl.pallas_call(
        _proj_kernel,
        grid=(N_TOK // tm,),
        in_specs=[
            pl.BlockSpec((tm, D_MODEL), row),
            pl.BlockSpec((1, 1, N_MOD * D_MODEL), lambda i: (_mod_row(i), 0, 0)),
            pl.BlockSpec((1, D_MODEL), const),
            pl.BlockSpec((D_MODEL, wqkv_n), const),
            pl.BlockSpec((D_MODEL, wg_n), const),
            pl.BlockSpec((D_MODEL, GATE_PAD), const),
            pl.BlockSpec((NA_WIDTH, NA_WIDTH), const),
            pl.BlockSpec((1, NA_WIDTH), const),
            pl.BlockSpec((1, NA_WIDTH), const),
            pl.BlockSpec((GATE_PAD, 2 * GLA_K_WIDTH), const),
            pl.BlockSpec((1, 2 * GLA_K_WIDTH), const),
        ],
        out_specs=[pl.BlockSpec((tm, s.shape[1]), row) for s in out_shapes],
        out_shape=out_shapes,
        compiler_params=_params(1),
        name="proj",
    )(x, mod_l, g_attn_l, wqkv, wg, wz, seg, gq_t, gk_t, wgate, bgate)


def _softmax_pv(s, v):
    m = jnp.max(s, axis=-1, keepdims=True)
    p = jnp.exp(s - m)
    l = jnp.sum(p, axis=-1, keepdims=True)
    return _dot(p.astype(BF16), v) / l


def _ctx_attn_kernel(q_ref, k_ref, v_ref, o_ref):
    lane = lax.broadcasted_iota(jnp.int32, (1, 2 * NA_HEAD_DIM), 1)
    first = lane < NA_HEAD_DIM
    for hp in range(NA_HEADS // 2):
        cols = slice(hp * 2 * NA_HEAD_DIM, (hp + 1) * 2 * NA_HEAD_DIM)
        q2 = q_ref[:, cols]
        k2 = k_ref[:, cols].astype(BF16)
        v2 = v_ref[:, cols].astype(BF16)
        o0 = _softmax_pv(_dot_nt(jnp.where(first, q2, jnp.zeros_like(q2)), k2), v2)
        o1 = _softmax_pv(_dot_nt(jnp.where(first, jnp.zeros_like(q2), q2), k2), v2)
        o_ref[:, cols] = jnp.where(first, o0, o1).astype(BF16)


def _ctx_attn_call(q, k, v):
    spec = pl.BlockSpec((SEQ, NA_WIDTH), lambda b: (b, 0))
    return pl.pallas_call(
        _ctx_attn_kernel,
        grid=(BATCH,),
        in_specs=[spec, spec, spec],
        out_specs=spec,
        out_shape=jax.ShapeDtypeStruct((N_CTX, NA_WIDTH), BF16),
        compiler_params=_params(1),
        name="ctx_attn",
    )(q, k, v)


def _na_row_window(r):
    rs = min(max(r - NA_WIN_ROWS // 2, 0), GRID_ROWS - NA_WIN_ROWS)
    return rs, rs - r + NA_WIN_ROWS - 1


def _na_kernel(q_ref, k_ref, v_ref, kc_ref, vc_ref, bias_ref, o_ref, sc_ref):
    lane = lax.broadcasted_iota(jnp.int32, (1, 2 * NA_HEAD_DIM), 1)
    first = lane < NA_HEAD_DIM
    q2 = q_ref[...]
    kc2 = jnp.concatenate([kc_ref[0, 0, 0], kc_ref[0, 0, 1]], axis=1).astype(BF16)
    vc2 = jnp.concatenate([vc_ref[0, 0, 0], vc_ref[0, 0, 1]], axis=1).astype(BF16)
    win = NA_WIN_ROWS * GRID_W
    outs = []
    for hh in range(2):
        qm = jnp.where(first, q2, jnp.zeros_like(q2)) if hh == 0 else jnp.where(first, jnp.zeros_like(q2), q2)
        sc_ref[...] = _dot_nt(qm, kc2)
        rows = []
        for r in range(GRID_ROWS):
            rs, off = _na_row_window(r)
            qr = qm[r * GRID_W:(r + 1) * GRID_W]
            kw = k_ref[rs * GRID_W:rs * GRID_W + win, :].astype(BF16)
            vw = v_ref[rs * GRID_W:rs * GRID_W + win, :].astype(BF16)
            s = _dot_nt(qr, kw) + bias_ref[0, hh, off]
            sc = sc_ref[r * GRID_W:(r + 1) * GRID_W, :]
            m = jnp.maximum(jnp.max(s, axis=-1, keepdims=True), jnp.max(sc, axis=-1, keepdims=True))
            p = jnp.exp(s - m)
            pc = jnp.exp(sc - m)
            l = jnp.sum(p, axis=-1, keepdims=True) + jnp.sum(pc, axis=-1, keepdims=True)
            rows.append((_dot(p.astype(BF16), vw) + _dot(pc.astype(BF16), vc2)) / l)
        outs.append(jnp.concatenate(rows, axis=0))
    o_ref[...] = jnp.where(first, outs[0], outs[1]).astype(BF16)


def _na_call(q, k, v, cache_k, cache_v, bias, layer):
    ctx_blocks = N_CTX // DEC_SEQ
    tok = pl.BlockSpec((DEC_SEQ, 2 * NA_HEAD_DIM), lambda b, hp: (ctx_blocks + b, hp))
    cache = pl.BlockSpec((1, 1, 2, PAST_LEN, NA_HEAD_DIM), lambda b, hp: (b, layer, hp, 0, 0))
    return pl.pallas_call(
        _na_kernel,
        grid=(DEC_BATCH, NA_HEADS // 2),
        in_specs=[
            tok, tok, tok, cache, cache,
            pl.BlockSpec((1, 2, NA_WIN_ROWS, GRID_W, NA_WIN_ROWS * GRID_W), lambda b, hp: (layer, hp, 0, 0, 0)),
        ],
        out_specs=pl.BlockSpec((DEC_SEQ, 2 * NA_HEAD_DIM), lambda b, hp: (b, hp)),
        out_shape=jax.ShapeDtypeStruct((N_SMP, NA_WIDTH), BF16),
        scratch_shapes=[pltpu.VMEM((DEC_SEQ, PAST_LEN), F32)],
        compiler_params=_params(2),
        name="nbr_attn",
    )(q, k, v, cache_k, cache_v, bias)


def _na_bias_tables(rpb):
    cols = np.arange(GRID_W)
    cs = np.clip(cols - NA_WIN_COLS // 2, 0, GRID_W - NA_WIN_COLS)
    c2 = cols[None, :]
    valid = (c2 >= cs[:, None]) & (c2 < cs[:, None] + NA_WIN_COLS)
    dc = np.clip(c2 - cols[:, None] + NA_WIN_COLS - 1, 0, 2 * NA_WIN_COLS - 2)
    dr = np.arange(NA_WIN_ROWS)[:, None] + np.arange(NA_WIN_ROWS)[None, :]
    t = rpb[:, :, dr[:, :, None, None], dc[None, None, :, :]]
    t = jnp.where(valid[None, None, None, None], t, MASK_VALUE)
    t = t.transpose(0, 1, 2, 4, 3, 5)
    return t.reshape(DEPTH, NA_HEADS, NA_WIN_ROWS, GRID_W, NA_WIN_ROWS * GRID_W)


def _gla_constants():
    C = GLA_CHUNK
    wf = np.zeros((GLA_LEVELS + 2, C, C), np.float32)
    wb = np.zeros((GLA_LEVELS + 2, C, C), np.float32)
    for p in range(GLA_LEVELS):
        m = 1 << p
        for i in range(C):
            c = (i // (2 * m)) * (2 * m) + m
            if i & m:
                wf[p, i, c:i + 1] = 1
                wb[p, i, c:i] = 1
            else:
                wf[p, i, i + 1:c] = 1
                wb[p, i, i:c] = 1
    for i in range(C):
        wf[GLA_LEVELS, i, :i + 1] = 1
        wf[GLA_LEVELS + 1, i, i + 1:] = 1
        wb[GLA_LEVELS, i, i:] = 1
        wb[GLA_LEVELS + 1, i, :i] = 1
    w = np.stack([wf, wb]).reshape(2, (GLA_LEVELS + 2) * C, C)
    w = np.concatenate([w, w], axis=-1)

    ii, jj = np.meshgrid(np.arange(C), np.arange(C), indexing="ij")
    x = ii ^ jj
    hb = np.where(x > 0, np.floor(np.log2(np.maximum(x, 1))), -1).astype(np.int64)
    masks = np.zeros((2, GLA_LEVELS + 1, C, C), np.float32)
    for p in range(GLA_LEVELS):
        masks[0, p] = (jj < ii) & (hb == p)
        masks[1, p] = (jj > ii) & (hb == p)
    masks[:, GLA_LEVELS] = np.eye(C)
    return w, masks


def _gla_kernel(seq_len, with_rope, with_state_in, with_state_out, *refs):
    refs = list(refs)
    gq_ref, gk_ref, gv_ref, lf_ref, lb_ref, w_ref, mask_ref = refs[:7]
    refs = refs[7:]
    if with_rope:
        cos_ref, sin_ref = refs[:2]
        refs = refs[2:]
    if with_state_in:
        s0f_ref, s0b_ref = refs[:2]
        refs = refs[2:]
    o_ref = refs[0]
    refs = refs[1:]
    if with_state_out:
        sf_ref, sb_ref = refs[:2]
        refs = refs[2:]
    q_scr, k_scr, of_scr, ob_scr, st_scr = refs

    C = GLA_CHUNK
    n_chunks = seq_len // C

    q = gq_ref[...]
    k = gk_ref[...]
    if with_rope:
        lane = lax.broadcasted_iota(jnp.int32, (1, GLA_K_WIDTH), 1)
        quarter = GLA_DK // 4
        low = (lane % (2 * quarter)) < quarter
        cos = cos_ref[...]
        sin = sin_ref[...]

        def rope(t):
            partner = jnp.where(low, pltpu.roll(t, GLA_K_WIDTH - quarter, axis=1), pltpu.roll(t, quarter, axis=1))
            return t * cos + partner * sin

        q = rope(q)
        k = rope(k)
    q_scr[...] = q
    k_scr[...] = k

    for h in range(GLA_HEADS):
        if with_state_in:
            st_scr[0, h] = s0f_ref[0, 0, h].T
            st_scr[1, h] = s0b_ref[0, 0, h].T
        else:
            st_scr[0, h] = jnp.zeros((GLA_DV, GLA_DK), F32)
            st_scr[1, h] = jnp.zeros((GLA_DV, GLA_DK), F32)

    def one_chunk(d, r0, f_ref, o_scr):
        rows = pl.ds(r0, C)
        f = f_ref[rows, :]
        f_hi = f.astype(BF16)
        f_lo = (f - f_hi.astype(F32)).astype(BF16)
        e = jnp.exp(_dot(w_ref[d], jnp.concatenate([f_hi, f_lo], axis=0)))
        qc = q_scr[rows, :]
        kc = k_scr[rows, :]
        vc = gv_ref[rows, :]
        eq = e[GLA_LEVELS * C:(GLA_LEVELS + 1) * C]
        ek = e[(GLA_LEVELS + 1) * C:(GLA_LEVELS + 2) * C]
        total = eq[C - 1:C] if d == 0 else eq[0:1]
        outs = []
        for h in range(GLA_HEADS):
            kl = slice(h * GLA_DK, (h + 1) * GLA_DK)
            vl = slice(h * GLA_DV, (h + 1) * GLA_DV)
            qh = qc[:, kl]
            kh = kc[:, kl]
            a = _dot_nt(qh.astype(BF16), kh.astype(BF16)) * mask_ref[d, GLA_LEVELS]
            for p in range(GLA_LEVELS):
                ep = e[p * C:(p + 1) * C, kl]
                a = a + _dot_nt((qh * ep).astype(BF16), (kh * ep).astype(BF16)) * mask_ref[d, p]
            st = st_scr[d, h]
            vh = vc[:, vl]
            outs.append(_dot(a.astype(BF16), vh) + _dot_nt((qh * eq[:, kl]).astype(BF16), st.astype(BF16)))
            st_scr[d, h] = st * total[:, kl] + _dot_tn(vh, (kh * ek[:, kl]).astype(BF16))
        o_scr[rows, :] = jnp.concatenate(outs, axis=1)

    def body(n, carry):
        one_chunk(0, pl.multiple_of(n * C, C), lf_ref, of_scr)
        one_chunk(1, pl.multiple_of((n_chunks - 1 - n) * C, C), lb_ref, ob_scr)
        return carry

    lax.fori_loop(0, n_chunks, body, 0)
    o_ref[...] = of_scr[...] + ob_scr[...]
    if with_state_out:
        for h in range(GLA_HEADS):
            sf_ref[0, h] = st_scr[0, h].T
            sb_ref[0, h] = st_scr[1, h].T


def _gla_call(seq_len, n_seq, row_block0, gq, gk, gv, lf, lb, w, masks, rope=None, state_in=None, layer=0):
    tok = lambda width: pl.BlockSpec((seq_len, width), lambda b: (row_block0 + b, 0))
    in_specs = [tok(GLA_K_WIDTH), tok(GLA_K_WIDTH), tok(GLA_V_WIDTH), tok(GLA_K_WIDTH), tok(GLA_K_WIDTH),
                pl.BlockSpec(w.shape, lambda b: (0, 0, 0)),
                pl.BlockSpec(masks.shape, lambda b: (0, 0, 0, 0))]
    args = [gq, gk, gv, lf, lb, w, masks]
    if rope is not None:
        in_specs += [pl.BlockSpec((seq_len, GLA_K_WIDTH), lambda b: (0, 0))] * 2
        args += list(rope)
    if state_in is not None:
        in_specs += [pl.BlockSpec((1, 1, GLA_HEADS, GLA_DK, GLA_DV), lambda b: (b, layer, 0, 0, 0))] * 2
        args += list(state_in)
    out_shape = [jax.ShapeDtypeStruct((n_seq * seq_len, GLA_V_WIDTH), F32)]
    out_specs = [pl.BlockSpec((seq_len, GLA_V_WIDTH), lambda b: (b, 0))]
    with_state_out = state_in is None
    if with_state_out:
        st = jax.ShapeDtypeStruct((n_seq, GLA_HEADS, GLA_DK, GLA_DV), F32)
        out_shape += [st, st]
        out_specs += [pl.BlockSpec((1, GLA_HEADS, GLA_DK, GLA_DV), lambda b: (b, 0, 0, 0))] * 2
    return pl.pallas_call(
        functools.partial(_gla_kernel, seq_len, rope is not None, state_in is not None, with_state_out),
        grid=(n_seq,),
        in_specs=in_specs,
        out_specs=out_specs,
        out_shape=out_shape,
        scratch_shapes=[
            pltpu.VMEM((seq_len, GLA_K_WIDTH), F32),
            pltpu.VMEM((seq_len, GLA_K_WIDTH), F32),
            pltpu.VMEM((seq_len, GLA_V_WIDTH), F32),
            pltpu.VMEM((seq_len, GLA_V_WIDTH), F32),
            pltpu.VMEM((2, GLA_HEADS, GLA_DV, GLA_DK), F32),
        ],
        compiler_params=_params(1),
        name="gla_ctx" if state_in is None else "gla_smp",
    )(*args)


def _rope_tables():
    quarter = GLA_DK // 4
    inv = ROPE_BASE ** (-jnp.arange(quarter, dtype=F32) / quarter)
    t = jnp.arange(DEC_SEQ)
    ang_r = (t // GRID_W).astype(F32)[:, None] * inv
    ang_c = (t % GRID_W).astype(F32)[:, None] * inv
    cos = jnp.concatenate([jnp.cos(ang_r)] * 2 + [jnp.cos(ang_c)] * 2, axis=-1)
    sin = jnp.concatenate([-jnp.sin(ang_r), jnp.sin(ang_r), -jnp.sin(ang_c), jnp.sin(ang_c)], axis=-1)
    return jnp.tile(cos, (1, GLA_HEADS)), jnp.tile(sin, (1, GLA_HEADS))


def _merge_mlp_kernel(x_ref, mod_ref, ona_ref, ogla_ref, gate_ref, gout_ref, wo_ref, gmlp_ref,
                      wup_ref, wdown_ref, o_ref):
    mod = mod_ref[0]
    ga1 = mod[:, 2 * D_MODEL:3 * D_MODEL]
    sh2 = mod[:, 3 * D_MODEL:4 * D_MODEL]
    sc2 = mod[:, 4 * D_MODEL:5 * D_MODEL]
    ga2 = mod[:, 5 * D_MODEL:6 * D_MODEL]

    og = ogla_ref[...]
    normed = []
    for h in range(GLA_HEADS):
        oh = og[:, h * GLA_DV:(h + 1) * GLA_DV]
        normed.append(oh * lax.rsqrt(jnp.mean(oh * oh, axis=-1, keepdims=True) + EPS))
    gate = gate_ref[...]
    g = jnp.concatenate(normed, axis=1) * gout_ref[...] * (gate * (1.0 / (1.0 + jnp.exp(-gate))))
    att = _dot(ona_ref[...], wo_ref[0:NA_WIDTH, :]) + _dot(g.astype(BF16), wo_ref[NA_WIDTH:, :])
    x = x_ref[...] + ga1 * att

    ms = jnp.mean(x * x, axis=-1, keepdims=True)
    h2 = ((x * lax.rsqrt(ms + EPS) * gmlp_ref[...]) * (1.0 + sc2) + sh2).astype(BF16)
    acc = jnp.zeros_like(x)
    for c in range(D_FF // FF_CHUNK):
        u = jnp.maximum(_dot(h2, wup_ref[:, c * FF_CHUNK:(c + 1) * FF_CHUNK]), 0.0)
        acc = acc + _dot((u * u).astype(BF16), wdown_ref[c * FF_CHUNK:(c + 1) * FF_CHUNK, :])
    o_ref[...] = x + ga2 * acc


def _merge_mlp_call(x, mod_l, o_na, o_gla, gate, gout_t, wo, g_mlp_l, wup, wdown):
    tm = TOKEN_TILE
    row = lambda i: (i, 0)
    const = lambda i: (0, 0)
    return pl.pallas_call(
        _merge_mlp_kernel,
        grid=(N_TOK // tm,),
        in_specs=[
            pl.BlockSpec((tm, D_MODEL), row),
            pl.BlockSpec((1, 1, N_MOD * D_MODEL), lambda i: (_mod_row(i), 0, 0)),
            pl.BlockSpec((tm, NA_WIDTH), row),
            pl.BlockSpec((tm, GLA_V_WIDTH), row),
            pl.BlockSpec((tm, GLA_V_WIDTH), row),
            pl.BlockSpec((1, GLA_V_WIDTH), const),
            pl.BlockSpec((D_MODEL, D_MODEL), const, pipeline_mode=pl.Buffered(1)),
            pl.BlockSpec((1, D_MODEL), const),
            pl.BlockSpec((D_MODEL, D_FF), const, pipeline_mode=pl.Buffered(1)),
            pl.BlockSpec((D_FF, D_MODEL), const, pipeline_mode=pl.Buffered(1)),
        ],
        out_specs=pl.BlockSpec((tm, D_MODEL), row),
        out_shape=jax.ShapeDtypeStruct((N_TOK, D_MODEL), F32),
        compiler_params=_params(1),
        name="merge_mlp",
    )(x, mod_l, o_na, o_gla, gate, gout_t, wo, g_mlp_l, wup, wdown)


def kernel(x_prompt, x_sample, cache_k, cache_v, state_fwd, state_bwd, c, c_ctx, w_ada, b_ada, g_attn, w_in,
           g_q, g_k, rpb, w_gf, b_gf, w_gb, b_gb, g_gla_out, w_o, g_mlp, w_up, w_down):
    x = jnp.concatenate([x_prompt.reshape(N_CTX, D_MODEL), x_sample.reshape(N_SMP, D_MODEL)], axis=0)

    c_rows = jnp.concatenate([c_ctx[None, :], c, jnp.zeros((MOD_ROWS - 1 - DEC_BATCH, D_MODEL), F32)], axis=0)
    mods = _ada_call(c_rows, w_ada, b_ada).reshape(DEPTH, MOD_ROWS, 1, N_MOD * D_MODEL)

    qkv_n = 3 * NA_WIDTH
    g_n = 2 * GLA_K_WIDTH + 2 * GLA_V_WIDTH
    w_in_b = w_in.astype(BF16)
    wqkv = w_in_b[:, :, :qkv_n]
    wg = w_in_b[:, :, qkv_n:qkv_n + g_n]
    wz = jnp.pad(w_in_b[:, :, qkv_n + g_n:], ((0, 0), (0, 0), (0, GATE_PAD - 2 * GLA_GATE_RANK)))
    wgate = jnp.zeros((DEPTH, GATE_PAD, 2 * GLA_K_WIDTH), F32)
    wgate = wgate.at[:, :GLA_GATE_RANK, :GLA_K_WIDTH].set(w_gf)
    wgate = wgate.at[:, GLA_GATE_RANK:2 * GLA_GATE_RANK, GLA_K_WIDTH:].set(w_gb).astype(BF16)
    bgate = jnp.concatenate([b_gf, b_gb], axis=-1).reshape(DEPTH, 1, 2 * GLA_K_WIDTH)
    wo_b = w_o.astype(BF16)
    wup_b = w_up.astype(BF16)
    wdown_b = w_down.astype(BF16)
    head_of = np.arange(NA_WIDTH) // NA_HEAD_DIM
    seg = jnp.asarray((head_of[:, None] == head_of[None, :]).astype(np.float32) / NA_HEAD_DIM, BF16)
    gq_t = jnp.tile(g_q, (1, NA_HEADS)).reshape(DEPTH, 1, NA_WIDTH)
    gk_t = jnp.tile(g_k, (1, NA_HEADS)).reshape(DEPTH, 1, NA_WIDTH)
    gout_t = jnp.tile(g_gla_out, (1, GLA_HEADS)).reshape(DEPTH, 1, GLA_V_WIDTH)
    na_bias = _na_bias_tables(rpb)
    gla_w_np, gla_masks_np = _gla_constants()
    gla_w = jnp.asarray(gla_w_np, BF16)
    gla_masks = jnp.asarray(gla_masks_np, F32)
    rope = _rope_tables()

    new_k, new_v, new_sf, new_sb = [], [], [], []
    for l in range(DEPTH):
        q, k, v, gq, gk, gv, gate, lf, lb = _proj_call(
            x, mods[l], g_attn[l].reshape(1, D_MODEL), wqkv[l], wg[l], wz[l], seg, gq_t[l], gk_t[l],
            wgate[l], bgate[l])
        o_na_ctx = _ctx_attn_call(q, k, v)
        o_na_smp = _na_call(q, k, v, cache_k, cache_v, na_bias, l)
        o_gla_ctx, sf, sb = _gla_call(SEQ, BATCH, 0, gq, gk, gv, lf, lb, gla_w, gla_masks)
        (o_gla_smp,) = _gla_call(DEC_SEQ, DEC_BATCH, N_CTX // DEC_SEQ, gq, gk, gv, lf, lb, gla_w, gla_masks,
                                 rope=rope, state_in=(state_fwd, state_bwd), layer=l)
        o_na = jnp.concatenate([o_na_ctx, o_na_smp], axis=0)
        o_gla = jnp.concatenate([o_gla_ctx, o_gla_smp], axis=0)
        x = _merge_mlp_call(x, mods[l], o_na, o_gla, gate, gout_t[l], wo_b[l], g_mlp[l].reshape(1, D_MODEL),
                            wup_b[l], wdown_b[l])
        to_heads = lambda t: t[:N_CTX].reshape(BATCH, SEQ, NA_HEADS, NA_HEAD_DIM).transpose(0, 2, 1, 3)
        new_k.append(to_heads(k))
        new_v.append(to_heads(v))
        new_sf.append(sf)
        new_sb.append(sb)

    y_prompt = x[:N_CTX].reshape(BATCH, SEQ, D_MODEL)
    y_sample = x[N_CTX:].reshape(DEC_BATCH, DEC_SEQ, D_MODEL)
    return (y_prompt, y_sample, jnp.stack(new_k, axis=1), jnp.stack(new_v, axis=1),
            jnp.stack(new_sf, axis=1), jnp.stack(new_sb, axis=1))
```

```python
import functools

import numpy as np
import jax
import jax.numpy as jnp
from jax import lax
from jax.experimental import pallas as pl
from jax.experimental.pallas import tpu as pltpu

F32 = jnp.float32
BF16 = jnp.bfloat16

D_MODEL = 1024
BATCH = 16
SEQ = 256
DEPTH = 4
DEC_BATCH = 2
DEC_SEQ = 1024
PAST_LEN = 512
GRID_W = 64
GRID_ROWS = DEC_SEQ // GRID_W
NA_WIDTH = D_MODEL // 2
NA_HEAD_DIM = 64
NA_HEADS = NA_WIDTH // NA_HEAD_DIM
NA_WIN_ROWS = 8
NA_WIN_COLS = 16
GLA_V_WIDTH = D_MODEL - NA_WIDTH
GLA_HEADS = 4
GLA_DV = GLA_V_WIDTH // GLA_HEADS
GLA_DK = GLA_DV // 2
GLA_K_WIDTH = GLA_HEADS * GLA_DK
GLA_GATE_RANK = 16
GLA_GATE_TAU = 16.0
GLA_CHUNK = 64
D_FF = 4 * D_MODEL
ROPE_BASE = 10000.0
N_MOD = 6
EPS = 1e-6

N_CTX = BATCH * SEQ
N_SMP = DEC_BATCH * DEC_SEQ
MOD_ROWS = 8
MASK_VALUE = -1e30

TOKEN_TILE = 512
FF_CHUNK = 1024
ADA_TILE_N = 1536
GATE_PAD = 128
GLA_LEVELS = 6
VMEM_LIMIT = 52 * 1024 * 1024


def _dot(a, b):
    return jnp.dot(a, b, preferred_element_type=F32)


def _dot_nt(a, b):
    return lax.dot_general(a, b, (((1,), (1,)), ((), ())), preferred_element_type=F32)


def _dot_tn(a, b):
    return lax.dot_general(a, b, (((0,), (0,)), ((), ())), preferred_element_type=F32)


def _params(n_grid_dims=1):
    return pltpu.CompilerParams(dimension_semantics=("arbitrary",) * n_grid_dims,
                                vmem_limit_bytes=VMEM_LIMIT)


def _mod_index_map(is_ctx):
    if is_ctx:
        return lambda i: (0, 0, 0)
    return lambda i: (1 + i // (DEC_SEQ // TOKEN_TILE), 0, 0)


def _ada_kernel(c_ref, w_ref, b_ref, o_ref):
    cv = c_ref[...]
    s = cv * (1.0 / (1.0 + jnp.exp(-cv)))
    o_ref[0] = _dot(s.astype(BF16), w_ref[0].astype(BF16)) + b_ref[0]


def _ada_call(c_rows, w_ada, b_ada):
    n_mod = N_MOD * D_MODEL
    return pl.pallas_call(
        _ada_kernel,
        grid=(DEPTH, n_mod // ADA_TILE_N),
        in_specs=[
            pl.BlockSpec((MOD_ROWS, D_MODEL), lambda l, j: (0, 0)),
            pl.BlockSpec((1, D_MODEL, ADA_TILE_N), lambda l, j: (l, 0, j)),
            pl.BlockSpec((1, 1, ADA_TILE_N), lambda l, j: (l, 0, j)),
        ],
        out_specs=pl.BlockSpec((1, MOD_ROWS, ADA_TILE_N), lambda l, j: (l, 0, j)),
        out_shape=jax.ShapeDtypeStruct((DEPTH, MOD_ROWS, n_mod), F32),
        compiler_params=_params(2),
        name="ada_mod",
    )(c_rows, w_ada, b_ada.reshape(DEPTH, 1, n_mod))


def _log_sigmoid(x):
    return jnp.minimum(x, 0.0) - jnp.log1p(jnp.exp(-jnp.abs(x)))


def _proj_kernel(is_ctx, x_ref, mod_ref, g_ref, wqkv_ref, wg_ref, wz_ref, seg_ref, gq_ref, gk_ref,
                 wgate_ref, bgate_ref, *rest):
    if is_ctx:
        rest = rest[2:]
    q_out, k_out, v_out, gq_out, gk_out, gv_out, gate_out, lf_out, lb_out = rest[:9]

    x = x_ref[...]
    mod = mod_ref[0]
    sh1 = mod[:, 0:D_MODEL]
    sc1 = mod[:, D_MODEL:2 * D_MODEL]
    ms = jnp.mean(x * x, axis=-1, keepdims=True)
    h = (x * lax.rsqrt(ms + EPS) * g_ref[...]) * (1.0 + sc1) + sh1
    hb = h.astype(BF16)

    z = _dot(hb, wqkv_ref[...])
    q = z[:, 0:NA_WIDTH]
    k = z[:, NA_WIDTH:2 * NA_WIDTH]
    v = z[:, 2 * NA_WIDTH:3 * NA_WIDTH]
    seg = seg_ref[...]
    q_ms = _dot((q * q).astype(BF16), seg)
    k_ms = _dot((k * k).astype(BF16), seg)
    q_out[...] = (q * lax.rsqrt(q_ms + EPS) * gq_ref[...] * (NA_HEAD_DIM ** -0.5)).astype(BF16)
    kn = k * lax.rsqrt(k_ms + EPS) * gk_ref[...]
    k_out[...] = kn.astype(BF16)
    v_out[...] = v.astype(BF16)
    if is_ctx:
        knew_out, vnew_out = rest[9:]
        for s in range(TOKEN_TILE // SEQ):
            for hd in range(NA_HEADS):
                rows = slice(s * SEQ, (s + 1) * SEQ)
                cols = slice(hd * NA_HEAD_DIM, (hd + 1) * NA_HEAD_DIM)
                knew_out[s, 0, hd] = kn[rows, cols]
                vnew_out[s, 0, hd] = v[rows, cols]

    z = _dot(hb, wg_ref[...])
    gq_out[...] = z[:, 0:GLA_K_WIDTH] * (GLA_DK ** -0.5)
    gk_out[...] = z[:, GLA_K_WIDTH:2 * GLA_K_WIDTH]
    gv_out[...] = z[:, 2 * GLA_K_WIDTH:2 * GLA_K_WIDTH + GLA_V_WIDTH].astype(BF16)
    gate_out[...] = z[:, 2 * GLA_K_WIDTH + GLA_V_WIDTH:]

    zz = _dot(hb, wz_ref[...])
    pre = _dot(zz.astype(BF16), wgate_ref[...]) + bgate_ref[...]
    ls = _log_sigmoid(pre) * (1.0 / GLA_GATE_TAU)
    lf_out[...] = ls[:, 0:GLA_K_WIDTH]
    lb_out[...] = ls[:, GLA_K_WIDTH:]


def _proj_call(x, mod_l, g_attn_l, wqkv, wg, wz, seg, gq_t, gk_t, wgate, bgate, new_kv=None, layer=0):
    tm = TOKEN_TILE
    n_tok = x.shape[0]
    is_ctx = new_kv is not None
    row = lambda i: (i, 0)
    const = lambda i: (0, 0)
    wqkv_n = 3 * NA_WIDTH
    wg_n = 2 * GLA_K_WIDTH + 2 * GLA_V_WIDTH
    out_shapes = [
        jax.ShapeDtypeStruct((n_tok, NA_WIDTH), BF16),
        jax.ShapeDtypeStruct((n_tok, NA_WIDTH), BF16),
        jax.ShapeDtypeStruct((n_tok, NA_WIDTH), BF16),
        jax.ShapeDtypeStruct((n_tok, GLA_K_WIDTH), F32),
        jax.ShapeDtypeStruct((n_tok, GLA_K_WIDTH), F32),
        jax.ShapeDtypeStruct((n_tok, GLA_V_WIDTH), BF16),
        jax.ShapeDtypeStruct((n_tok, GLA_V_WIDTH), F32),
        jax.ShapeDtypeStruct((n_tok, GLA_K_WIDTH), F32),
        jax.ShapeDtypeStruct((n_tok, GLA_K_WIDTH), F32),
    ]
    out_specs = [pl.BlockSpec((tm, s.shape[1]), row) for s in out_shapes]
    in_specs = [
        pl.BlockSpec((tm, D_MODEL), row),
        pl.BlockSpec((1, 1, N_MOD * D_MODEL), _mod_index_map(is_ctx)),
        pl.BlockSpec((1, D_MODEL), const),
        pl.BlockSpec((D_MODEL, wqkv_n), const),
        pl.BlockSpec((D_MODEL, wg_n), const),
        pl.BlockSpec((D_MODEL, GATE_PAD), const),
        pl.BlockSpec((NA_WIDTH, NA_WIDTH), const),
        pl.BlockSpec((1, NA_WIDTH), const),
        pl.BlockSpec((1, NA_WIDTH), const),
        pl.BlockSpec((GATE_PAD, 2 * GLA_K_WIDTH), const),
        pl.BlockSpec((1, 2 * GLA_K_WIDTH), const),
    ]
    args = [x, mod_l, g_attn_l, wqkv, wg, wz, seg, gq_t, gk_t, wgate, bgate]
    aliases = {}
    if is_ctx:
        seqs = tm // SEQ
        kv_spec = pl.BlockSpec((seqs, 1, NA_HEADS, SEQ, NA_HEAD_DIM), lambda i: (i, layer, 0, 0, 0))
        aliases = {len(args): len(out_shapes), len(args) + 1: len(out_shapes) + 1}
        in_specs += [pl.BlockSpec(memory_space=pl.ANY)] * 2
        args += list(new_kv)
        out_shapes += [jax.ShapeDtypeStruct(new_kv[0].shape, F32)] * 2
        out_specs += [kv_spec, kv_spec]
    return pl.pallas_call(
        functools.partial(_proj_kernel, is_ctx),
        grid=(n_tok // tm,),
        in_specs=in_specs,
        out_specs=out_specs,
        out_shape=out_shapes,
        input_output_aliases=aliases,
        compiler_params=_params(1),
        name="proj_ctx" if is_ctx else "proj_smp",
    )(*args)


def _softmax_pv(s, v):
    m = jnp.max(s, axis=-1, keepdims=True)
    p = jnp.exp(s - m)
    l = jnp.sum(p, axis=-1, keepdims=True)
    return _dot(p.astype(BF16), v) / l


def _ctx_attn_kernel(q_ref, k_ref, v_ref, o_ref):
    lane = lax.broadcasted_iota(jnp.int32, (1, 2 * NA_HEAD_DIM), 1)
    first = lane < NA_HEAD_DIM
    for hp in range(NA_HEADS // 2):
        cols = slice(hp * 2 * NA_HEAD_DIM, (hp + 1) * 2 * NA_HEAD_DIM)
        q2 = q_ref[:, cols]
        k2 = k_ref[:, cols]
        v2 = v_ref[:, cols]
        o0 = _softmax_pv(_dot_nt(jnp.where(first, q2, jnp.zeros_like(q2)), k2), v2)
        o1 = _softmax_pv(_dot_nt(jnp.where(first, jnp.zeros_like(q2), q2), k2), v2)
        o_ref[:, cols] = jnp.where(first, o0, o1).astype(BF16)


def _ctx_attn_call(q, k, v):
    spec = pl.BlockSpec((SEQ, NA_WIDTH), lambda b: (b, 0))
    return pl.pallas_call(
        _ctx_attn_kernel,
        grid=(BATCH,),
        in_specs=[spec, spec, spec],
        out_specs=spec,
        out_shape=jax.ShapeDtypeStruct((N_CTX, NA_WIDTH), BF16),
        compiler_params=_params(1),
        name="ctx_attn",
    )(q, k, v)


def _na_row_window(r):
    rs = min(max(r - NA_WIN_ROWS // 2, 0), GRID_ROWS - NA_WIN_ROWS)
    return rs, rs - r + NA_WIN_ROWS - 1


def _na_kernel(q_ref, k_ref, v_ref, kc_ref, vc_ref, bias_ref, o_ref, sc_ref):
    lane = lax.broadcasted_iota(jnp.int32, (1, 2 * NA_HEAD_DIM), 1)
    first = lane < NA_HEAD_DIM
    q2 = q_ref[...]
    kc2 = jnp.concatenate([kc_ref[0, 0, 0], kc_ref[0, 0, 1]], axis=1).astype(BF16)
    vc2 = jnp.concatenate([vc_ref[0, 0, 0], vc_ref[0, 0, 1]], axis=1).astype(BF16)
    win = NA_WIN_ROWS * GRID_W
    outs = []
    for hh in range(2):
        qm = jnp.where(first, q2, jnp.zeros_like(q2)) if hh == 0 else jnp.where(first, jnp.zeros_like(q2), q2)
        sc_ref[...] = _dot_nt(qm, kc2)
        rows = []
        for r in range(GRID_ROWS):
            rs, off = _na_row_window(r)
            qr = qm[r * GRID_W:(r + 1) * GRID_W]
            kw = k_ref[rs * GRID_W:rs * GRID_W + win, :]
            vw = v_ref[rs * GRID_W:rs * GRID_W + win, :]
            bias = jnp.concatenate([bias_ref[0, hh, off + 2 * t] for t in range(NA_WIN_ROWS // 2)], axis=1)
            s = _dot_nt(qr, kw) + bias
            sc = sc_ref[r * GRID_W:(r + 1) * GRID_W, :]
            m = jnp.maximum(jnp.max(s, axis=-1, keepdims=True), jnp.max(sc, axis=-1, keepdims=True))
            p = jnp.exp(s - m)
            pc = jnp.exp(sc - m)
            l = jnp.sum(p, axis=-1, keepdims=True) + jnp.sum(pc, axis=-1, keepdims=True)
            rows.append((_dot(p.astype(BF16), vw) + _dot(pc.astype(BF16), vc2)) / l)
        outs.append(jnp.concatenate(rows, axis=0))
    o_ref[...] = jnp.where(first, outs[0], outs[1]).astype(BF16)


def _na_call(q, k, v, cache_k, cache_v, bias, layer):
    tok = pl.BlockSpec((DEC_SEQ, 2 * NA_HEAD_DIM), lambda b, hp: (b, hp))
    cache = pl.BlockSpec((1, 1, 2, PAST_LEN, NA_HEAD_DIM), lambda b, hp: (b, layer, hp, 0, 0))
    n_pairs = 2 * NA_WIN_ROWS - 2
    return pl.pallas_call(
        _na_kernel,
        grid=(DEC_BATCH, NA_HEADS // 2),
        in_specs=[
            tok, tok, tok, cache, cache,
            pl.BlockSpec((1, 2, n_pairs, GRID_W, 2 * GRID_W), lambda b, hp: (layer, hp, 0, 0, 0)),
        ],
        out_specs=tok,
        out_shape=jax.ShapeDtypeStruct((N_SMP, NA_WIDTH), BF16),
        scratch_shapes=[pltpu.VMEM((DEC_SEQ, PAST_LEN), F32)],
        compiler_params=_params(2),
        name="nbr_attn",
    )(q, k, v, cache_k, cache_v, bias)


def _na_bias_tables(rpb):
    cols = np.arange(GRID_W)
    cs = np.clip(cols - NA_WIN_COLS // 2, 0, GRID_W - NA_WIN_COLS)
    c2 = cols[None, :]
    valid = (c2 >= cs[:, None]) & (c2 < cs[:, None] + NA_WIN_COLS)
    dc = c2 - cols[:, None] + NA_WIN_COLS - 1
    n_dc = 2 * NA_WIN_COLS - 1
    n_dr = 2 * NA_WIN_ROWS - 1
    onehot = ((dc[None] == np.arange(n_dc)[:, None, None]) & valid[None]).astype(np.float32)
    mask_add = np.where(valid, 0.0, MASK_VALUE).astype(np.float32).reshape(1, GRID_W * GRID_W)
    t = jnp.dot(rpb.reshape(DEPTH * NA_HEADS * n_dr, n_dc), jnp.asarray(onehot.reshape(n_dc, GRID_W * GRID_W)),
                precision=lax.Precision.HIGHEST) + mask_add
    t = t.reshape(DEPTH, NA_HEADS, n_dr, GRID_W, GRID_W)
    return jnp.concatenate([t[:, :, :-1], t[:, :, 1:]], axis=-1)


def _gla_constants():
    C = GLA_CHUNK
    wf = np.zeros((GLA_LEVELS + 2, C, C), np.float32)
    wb = np.zeros((GLA_LEVELS + 2, C, C), np.float32)
    for p in range(GLA_LEVELS):
        m = 1 << p
        for i in range(C):
            c = (i // (2 * m)) * (2 * m) + m
            if i & m:
                wf[p, i, c:i + 1] = 1
                wb[p, i, c:i] = 1
            else:
                wf[p, i, i + 1:c] = 1
                wb[p, i, i:c] = 1
    for i in range(C):
        wf[GLA_LEVELS, i, :i + 1] = 1
        wf[GLA_LEVELS + 1, i, i + 1:] = 1
        wb[GLA_LEVELS, i, i:] = 1
        wb[GLA_LEVELS + 1, i, :i] = 1
    w = np.stack([wf, wb]).reshape(2, (GLA_LEVELS + 2) * C, C)
    w = np.concatenate([w, w], axis=-1)

    ii, jj = np.meshgrid(np.arange(C), np.arange(C), indexing="ij")
    x = ii ^ jj
    hb = np.where(x > 0, np.floor(np.log2(np.maximum(x, 1))), -1).astype(np.int64)
    masks = np.zeros((2, GLA_LEVELS + 1, C, C), np.float32)
    for p in range(GLA_LEVELS):
        masks[0, p] = (jj < ii) & (hb == p)
        masks[1, p] = (jj > ii) & (hb == p)
    masks[:, GLA_LEVELS] = np.eye(C)
    return w, masks


def _gla_kernel(seq_len, is_ctx, *refs):
    refs = list(refs)
    gq_ref, gk_ref, gv_ref, lf_ref, lb_ref, w_ref, mask_ref = refs[:7]
    refs = refs[7:]
    if is_ctx:
        refs = refs[2:]
        o_ref, sf_ref, sb_ref = refs[:3]
        refs = refs[3:]
    else:
        cos_ref, sin_ref, s0f_ref, s0b_ref, o_ref = refs[:5]
        refs = refs[5:]
    q_scr, k_scr, of_scr, ob_scr, st_scr = refs

    C = GLA_CHUNK
    n_chunks = seq_len // C

    q = gq_ref[...]
    k = gk_ref[...]
    if not is_ctx:
        lane = lax.broadcasted_iota(jnp.int32, (1, GLA_K_WIDTH), 1)
        quarter = GLA_DK // 4
        low = (lane % (2 * quarter)) < quarter
        cos = cos_ref[...]
        sin = sin_ref[...]

        def rope(t):
            partner = jnp.where(low, pltpu.roll(t, GLA_K_WIDTH - quarter, axis=1), pltpu.roll(t, quarter, axis=1))
            return t * cos + partner * sin

        q = rope(q)
        k = rope(k)
    q_scr[...] = q
    k_scr[...] = k

    for h in range(GLA_HEADS):
        if is_ctx:
            st_scr[0, h] = jnp.zeros((GLA_DV, GLA_DK), F32)
            st_scr[1, h] = jnp.zeros((GLA_DV, GLA_DK), F32)
        else:
            st_scr[0, h] = s0f_ref[0, 0, h].T
            st_scr[1, h] = s0b_ref[0, 0, h].T

    def one_chunk(d, r0, f_ref, o_scr):
        rows = pl.ds(r0, C)
        f = f_ref[rows, :]
        f_hi = f.astype(BF16)
        f_lo = (f - f_hi.astype(F32)).astype(BF16)
        e = jnp.exp(_dot(w_ref[d], jnp.concatenate([f_hi, f_lo], axis=0)))
        qc = q_scr[rows, :]
        kc = k_scr[rows, :]
        vc = gv_ref[rows, :]
        eq = e[GLA_LEVELS * C:(GLA_LEVELS + 1) * C]
        ek = e[(GLA_LEVELS + 1) * C:(GLA_LEVELS + 2) * C]
        total = eq[C - 1:C] if d == 0 else eq[0:1]
        outs = []
        for h in range(GLA_HEADS):
            kl = slice(h * GLA_DK, (h + 1) * GLA_DK)
            vl = slice(h * GLA_DV, (h + 1) * GLA_DV)
            qh = qc[:, kl]
            kh = kc[:, kl]
            a = _dot_nt(qh.astype(BF16), kh.astype(BF16)) * mask_ref[d, GLA_LEVELS]
            for p in range(GLA_LEVELS):
                ep = e[p * C:(p + 1) * C, kl]
                a = a + _dot_nt((qh * ep).astype(BF16), (kh * ep).astype(BF16)) * mask_ref[d, p]
            st = st_scr[d, h]
            vh = vc[:, vl]
            outs.append(_dot(a.astype(BF16), vh) + _dot_nt((qh * eq[:, kl]).astype(BF16), st.astype(BF16)))
            st_scr[d, h] = st * total[:, kl] + _dot_tn(vh, (kh * ek[:, kl]).astype(BF16))
        o_scr[rows, :] = jnp.concatenate(outs, axis=1)

    def body(n, carry):
        one_chunk(0, pl.multiple_of(n * C, C), lf_ref, of_scr)
        one_chunk(1, pl.multiple_of((n_chunks - 1 - n) * C, C), lb_ref, ob_scr)
        return carry

    lax.fori_loop(0, n_chunks, body, 0)
    o_ref[...] = of_scr[...] + ob_scr[...]
    if is_ctx:
        for h in range(GLA_HEADS):
            sf_ref[0, 0, h] = st_scr[0, h].T
            sb_ref[0, 0, h] = st_scr[1, h].T


def _gla_call(gq, gk, gv, lf, lb, w, masks, new_states=None, rope=None, state_in=None, layer=0):
    is_ctx = new_states is not None
    seq_len = SEQ if is_ctx else DEC_SEQ
    n_seq = gq.shape[0] // seq_len
    tok = lambda width: pl.BlockSpec((seq_len, width), lambda b: (b, 0))
    state_spec = pl.BlockSpec((1, 1, GLA_HEADS, GLA_DK, GLA_DV), lambda b: (b, layer, 0, 0, 0))
    in_specs = [tok(GLA_K_WIDTH), tok(GLA_K_WIDTH), tok(GLA_V_WIDTH), tok(GLA_K_WIDTH), tok(GLA_K_WIDTH),
                pl.BlockSpec(w.shape, lambda b: (0, 0, 0)),
                pl.BlockSpec(masks.shape, lambda b: (0, 0, 0, 0))]
    args = [gq, gk, gv, lf, lb, w, masks]
    out_shape = [jax.ShapeDtypeStruct((n_seq * seq_len, GLA_V_WIDTH), F32)]
    out_specs = [tok(GLA_V_WIDTH)]
    aliases = {}
    if is_ctx:
        aliases = {len(args): 1, len(args) + 1: 2}
        in_specs += [pl.BlockSpec(memory_space=pl.ANY)] * 2
        args += list(new_states)
        out_shape += [jax.ShapeDtypeStruct(new_states[0].shape, F32)] * 2
        out_specs += [state_spec, state_spec]
    else:
        in_specs += [pl.BlockSpec((seq_len, GLA_K_WIDTH), lambda b: (0, 0))] * 2 + [state_spec, state_spec]
        args += list(rope) + list(state_in)
    return pl.pallas_call(
        functools.partial(_gla_kernel, seq_len, is_ctx),
        grid=(n_seq,),
        in_specs=in_specs,
        out_specs=out_specs,
        out_shape=out_shape,
        input_output_aliases=aliases,
        scratch_shapes=[
            pltpu.VMEM((seq_len, GLA_K_WIDTH), F32),
            pltpu.VMEM((seq_len, GLA_K_WIDTH), F32),
            pltpu.VMEM((seq_len, GLA_V_WIDTH), F32),
            pltpu.VMEM((seq_len, GLA_V_WIDTH), F32),
            pltpu.VMEM((2, GLA_HEADS, GLA_DV, GLA_DK), F32),
        ],
        compiler_params=_params(1),
        name="gla_ctx" if is_ctx else "gla_smp",
    )(*args)


def _rope_tables():
    quarter = GLA_DK // 4
    inv = ROPE_BASE ** (-jnp.arange(quarter, dtype=F32) / quarter)
    t = jnp.arange(DEC_SEQ)
    ang_r = (t // GRID_W).astype(F32)[:, None] * inv
    ang_c = (t % GRID_W).astype(F32)[:, None] * inv
    cos = jnp.concatenate([jnp.cos(ang_r)] * 2 + [jnp.cos(ang_c)] * 2, axis=-1)
    sin = jnp.concatenate([-jnp.sin(ang_r), jnp.sin(ang_r), -jnp.sin(ang_c), jnp.sin(ang_c)], axis=-1)
    return jnp.tile(cos, (1, GLA_HEADS)), jnp.tile(sin, (1, GLA_HEADS))


def _merge_mlp_kernel(x_ref, mod_ref, ona_ref, ogla_ref, gate_ref, gout_ref, wo_ref, gmlp_ref,
                      wup_ref, wdown_ref, o_ref):
    mod = mod_ref[0]
    ga1 = mod[:, 2 * D_MODEL:3 * D_MODEL]
    sh2 = mod[:, 3 * D_MODEL:4 * D_MODEL]
    sc2 = mod[:, 4 * D_MODEL:5 * D_MODEL]
    ga2 = mod[:, 5 * D_MODEL:6 * D_MODEL]

    og = ogla_ref[...]
    normed = []
    for h in range(GLA_HEADS):
        oh = og[:, h * GLA_DV:(h + 1) * GLA_DV]
        normed.append(oh * lax.rsqrt(jnp.mean(oh * oh, axis=-1, keepdims=True) + EPS))
    gate = gate_ref[...]
    g = jnp.concatenate(normed, axis=1) * gout_ref[...] * (gate * (1.0 / (1.0 + jnp.exp(-gate))))
    att = _dot(ona_ref[...], wo_ref[0:NA_WIDTH, :]) + _dot(g.astype(BF16), wo_ref[NA_WIDTH:, :])
    x = x_ref[...] + ga1 * att

    ms = jnp.mean(x * x, axis=-1, keepdims=True)
    h2 = ((x * lax.rsqrt(ms + EPS) * gmlp_ref[...]) * (1.0 + sc2) + sh2).astype(BF16)
    acc = jnp.zeros_like(x)
    for c in range(D_FF // FF_CHUNK):
        u = jnp.maximum(_dot(h2, wup_ref[:, c * FF_CHUNK:(c + 1) * FF_CHUNK]), 0.0)
        acc = acc + _dot((u * u).astype(BF16), wdown_ref[c * FF_CHUNK:(c + 1) * FF_CHUNK, :])
    o_ref[...] = x + ga2 * acc


def _merge_mlp_call(x, mod_l, o_na, o_gla, gate, gout_t, wo, g_mlp_l, wup, wdown, is_ctx):
    tm = TOKEN_TILE
    n_tok = x.shape[0]
    row = lambda i: (i, 0)
    const = lambda i: (0, 0)
    resident = dict(pipeline_mode=pl.Buffered(1))
    return pl.pallas_call(
        _merge_mlp_kernel,
        grid=(n_tok // tm,),
        in_specs=[
            pl.BlockSpec((tm, D_MODEL), row),
            pl.BlockSpec((1, 1, N_MOD * D_MODEL), _mod_index_map(is_ctx)),
            pl.BlockSpec((tm, NA_WIDTH), row),
            pl.BlockSpec((tm, GLA_V_WIDTH), row),
            pl.BlockSpec((tm, GLA_V_WIDTH), row),
            pl.BlockSpec((1, GLA_V_WIDTH), const),
            pl.BlockSpec((D_MODEL, D_MODEL), const, **resident),
            pl.BlockSpec((1, D_MODEL), const),
            pl.BlockSpec((D_MODEL, D_FF), const, **resident),
            pl.BlockSpec((D_FF, D_MODEL), const, **resident),
        ],
        out_specs=pl.BlockSpec((tm, D_MODEL), row),
        out_shape=jax.ShapeDtypeStruct((n_tok, D_MODEL), F32),
        compiler_params=_params(1),
        name="merge_mlp_ctx" if is_ctx else "merge_mlp_smp",
    )(x, mod_l, o_na, o_gla, gate, gout_t, wo, g_mlp_l, wup, wdown)


def kernel(x_prompt, x_sample, cache_k, cache_v, state_fwd, state_bwd, c, c_ctx, w_ada, b_ada, g_attn, w_in,
           g_q, g_k, rpb, w_gf, b_gf, w_gb, b_gb, g_gla_out, w_o, g_mlp, w_up, w_down):
    x_c = x_prompt.reshape(N_CTX, D_MODEL)
    x_s = x_sample.reshape(N_SMP, D_MODEL)

    c_rows = jnp.concatenate([c_ctx[None, :], c, jnp.zeros((MOD_ROWS - 1 - DEC_BATCH, D_MODEL), F32)], axis=0)
    mods = _ada_call(c_rows, w_ada, b_ada).reshape(DEPTH, MOD_ROWS, 1, N_MOD * D_MODEL)

    qkv_n = 3 * NA_WIDTH
    g_n = 2 * GLA_K_WIDTH + 2 * GLA_V_WIDTH
    wqkv = w_in[:, :, :qkv_n].astype(BF16)
    wg = w_in[:, :, qkv_n:qkv_n + g_n].astype(BF16)
    wz = jnp.pad(w_in[:, :, qkv_n + g_n:], ((0, 0), (0, 0), (0, GATE_PAD - 2 * GLA_GATE_RANK))).astype(BF16)
    wgate = jnp.zeros((DEPTH, GATE_PAD, 2 * GLA_K_WIDTH), F32)
    wgate = wgate.at[:, :GLA_GATE_RANK, :GLA_K_WIDTH].set(w_gf)
    wgate = wgate.at[:, GLA_GATE_RANK:2 * GLA_GATE_RANK, GLA_K_WIDTH:].set(w_gb).astype(BF16)
    bgate = jnp.concatenate([b_gf, b_gb], axis=-1).reshape(DEPTH, 1, 2 * GLA_K_WIDTH)
    wo_b = w_o.astype(BF16)
    wup_b = w_up.astype(BF16)
    wdown_b = w_down.astype(BF16)
    head_of = np.arange(NA_WIDTH) // NA_HEAD_DIM
    seg = jnp.asarray((head_of[:, None] == head_of[None, :]).astype(np.float32) / NA_HEAD_DIM, BF16)
    gq_t = jnp.tile(g_q, (1, NA_HEADS)).reshape(DEPTH, 1, NA_WIDTH)
    gk_t = jnp.tile(g_k, (1, NA_HEADS)).reshape(DEPTH, 1, NA_WIDTH)
    gout_t = jnp.tile(g_gla_out, (1, GLA_HEADS)).reshape(DEPTH, 1, GLA_V_WIDTH)
    na_bias = _na_bias_tables(rpb)
    gla_w_np, gla_masks_np = _gla_constants()
    gla_w = jnp.asarray(gla_w_np, BF16)
    gla_masks = jnp.asarray(gla_masks_np, F32)
    rope = _rope_tables()

    new_k = jnp.zeros((BATCH, DEPTH, NA_HEADS, SEQ, NA_HEAD_DIM), F32)
    new_v = jnp.zeros((BATCH, DEPTH, NA_HEADS, SEQ, NA_HEAD_DIM), F32)
    new_sf = jnp.zeros((BATCH, DEPTH, GLA_HEADS, GLA_DK, GLA_DV), F32)
    new_sb = jnp.zeros((BATCH, DEPTH, GLA_HEADS, GLA_DK, GLA_DV), F32)

    for l in range(DEPTH):
        proj_w = (g_attn[l].reshape(1, D_MODEL), wqkv[l], wg[l], wz[l], seg, gq_t[l], gk_t[l], wgate[l], bgate[l])
        mlp_w = (gout_t[l], wo_b[l], g_mlp[l].reshape(1, D_MODEL), wup_b[l], wdown_b[l])

        q, k, v, gq, gk, gv, gate, lf, lb, new_k, new_v = _proj_call(
            x_c, mods[l], *proj_w, new_kv=(new_k, new_v), layer=l)
        o_na = _ctx_attn_call(q, k, v)
        o_gla, new_sf, new_sb = _gla_call(gq, gk, gv, lf, lb, gla_w, gla_masks, new_states=(new_sf, new_sb), layer=l)
        x_c = _merge_mlp_call(x_c, mods[l], o_na, o_gla, gate, *mlp_w, is_ctx=True)

        q, k, v, gq, gk, gv, gate, lf, lb = _proj_call(x_s, mods[l], *proj_w)
        o_na = _na_call(q, k, v, cache_k, cache_v, na_bias, l)
        (o_gla,) = _gla_call(gq, gk, gv, lf, lb, gla_w, gla_masks, rope=rope,
                             state_in=(state_fwd, state_bwd), layer=l)
        x_s = _merge_mlp_call(x_s, mods[l], o_na, o_gla, gate, *mlp_w, is_ctx=False)

    return (x_c.reshape(BATCH, SEQ, D_MODEL), x_s.reshape(DEC_BATCH, DEC_SEQ, D_MODEL),
            new_k, new_v, new_sf, new_sb)
```

```python
import functools

import numpy as np
import jax
import jax.numpy as jnp
from jax import lax
from jax.experimental import pallas as pl
from jax.experimental.pallas import tpu as pltpu

F32 = jnp.float32
BF16 = jnp.bfloat16

D_MODEL = 1024
BATCH = 16
SEQ = 256
DEPTH = 4
DEC_BATCH = 2
DEC_SEQ = 1024
PAST_LEN = 512
GRID_W = 64
GRID_ROWS = DEC_SEQ // GRID_W
NA_WIDTH = D_MODEL // 2
NA_HEAD_DIM = 64
NA_HEADS = NA_WIDTH // NA_HEAD_DIM
NA_WIN_ROWS = 8
NA_WIN_COLS = 16
GLA_V_WIDTH = D_MODEL - NA_WIDTH
GLA_HEADS = 4
GLA_DV = GLA_V_WIDTH // GLA_HEADS
GLA_DK = GLA_DV // 2
GLA_K_WIDTH = GLA_HEADS * GLA_DK
GLA_GATE_RANK = 16
GLA_GATE_TAU = 16.0
GLA_CHUNK = 64
D_FF = 4 * D_MODEL
ROPE_BASE = 10000.0
N_MOD = 6
EPS = 1e-6

N_CTX = BATCH * SEQ
N_SMP = DEC_BATCH * DEC_SEQ
MOD_ROWS = 8
MASK_VALUE = -1e30

TOKEN_TILE = 512
FF_CHUNK = 1024
ADA_TILE_N = 1536
GATE_PAD = 128
GLA_LEVELS = 6
NA_ROW_GROUP = 4
VMEM_LIMIT = 52 * 1024 * 1024


def _dot(a, b):
    return jnp.dot(a, b, preferred_element_type=F32)


def _dot_nt(a, b):
    return lax.dot_general(a, b, (((1,), (1,)), ((), ())), preferred_element_type=F32)


def _dot_tn(a, b):
    return lax.dot_general(a, b, (((0,), (0,)), ((), ())), preferred_element_type=F32)


def _params(n_grid_dims=1):
    return pltpu.CompilerParams(dimension_semantics=("arbitrary",) * n_grid_dims,
                                vmem_limit_bytes=VMEM_LIMIT)


def _mod_index_map(is_ctx):
    if is_ctx:
        return lambda i: (0, 0, 0)
    return lambda i: (1 + i // (DEC_SEQ // TOKEN_TILE), 0, 0)


def _ada_kernel(c_ref, w_ref, b_ref, o_ref):
    cv = c_ref[...]
    s = cv * (1.0 / (1.0 + jnp.exp(-cv)))
    o_ref[0] = _dot(s.astype(BF16), w_ref[0].astype(BF16)) + b_ref[0]


def _ada_call(c_rows, w_ada, b_ada):
    n_mod = N_MOD * D_MODEL
    return pl.pallas_call(
        _ada_kernel,
        grid=(DEPTH, n_mod // ADA_TILE_N),
        in_specs=[
            pl.BlockSpec((MOD_ROWS, D_MODEL), lambda l, j: (0, 0)),
            pl.BlockSpec((1, D_MODEL, ADA_TILE_N), lambda l, j: (l, 0, j)),
            pl.BlockSpec((1, 1, ADA_TILE_N), lambda l, j: (l, 0, j)),
        ],
        out_specs=pl.BlockSpec((1, MOD_ROWS, ADA_TILE_N), lambda l, j: (l, 0, j)),
        out_shape=jax.ShapeDtypeStruct((DEPTH, MOD_ROWS, n_mod), F32),
        compiler_params=_params(2),
        name="ada_mod",
    )(c_rows, w_ada, b_ada.reshape(DEPTH, 1, n_mod))


def _log_sigmoid(x):
    return jnp.minimum(x, 0.0) - jnp.log1p(jnp.exp(-jnp.abs(x)))


def _proj_kernel(is_ctx, x_ref, mod_ref, g_ref, wqkv_ref, wg_ref, wz_ref, seg_ref, gq_ref, gk_ref,
                 wgate_ref, bgate_ref, *rest):
    if is_ctx:
        rest = rest[2:]
    q_out, k_out, v_out, gq_out, gk_out, gv_out, gate_out, lf_out, lb_out = rest[:9]

    x = x_ref[...]
    mod = mod_ref[0]
    sh1 = mod[:, 0:D_MODEL]
    sc1 = mod[:, D_MODEL:2 * D_MODEL]
    ms = jnp.mean(x * x, axis=-1, keepdims=True)
    h = (x * lax.rsqrt(ms + EPS) * g_ref[...]) * (1.0 + sc1) + sh1
    hb = h.astype(BF16)

    z = _dot(hb, wqkv_ref[...])
    q = z[:, 0:NA_WIDTH]
    k = z[:, NA_WIDTH:2 * NA_WIDTH]
    v = z[:, 2 * NA_WIDTH:3 * NA_WIDTH]
    seg = seg_ref[...]
    q_ms = _dot((q * q).astype(BF16), seg)
    k_ms = _dot((k * k).astype(BF16), seg)
    q_out[...] = (q * lax.rsqrt(q_ms + EPS) * gq_ref[...] * (NA_HEAD_DIM ** -0.5)).astype(BF16)
    kn = k * lax.rsqrt(k_ms + EPS) * gk_ref[...]
    k_out[...] = kn.astype(BF16)
    v_out[...] = v.astype(BF16)
    if is_ctx:
        knew_out, vnew_out = rest[9:]
        for s in range(TOKEN_TILE // SEQ):
            for hd in range(NA_HEADS):
                rows = slice(s * SEQ, (s + 1) * SEQ)
                cols = slice(hd * NA_HEAD_DIM, (hd + 1) * NA_HEAD_DIM)
                knew_out[s, 0, hd] = kn[rows, cols]
                vnew_out[s, 0, hd] = v[rows, cols]

    z = _dot(hb, wg_ref[...])
    gq_out[...] = z[:, 0:GLA_K_WIDTH] * (GLA_DK ** -0.5)
    gk_out[...] = z[:, GLA_K_WIDTH:2 * GLA_K_WIDTH]
    gv_out[...] = z[:, 2 * GLA_K_WIDTH:2 * GLA_K_WIDTH + GLA_V_WIDTH].astype(BF16)
    gate_out[...] = z[:, 2 * GLA_K_WIDTH + GLA_V_WIDTH:]

    zz = _dot(hb, wz_ref[...])
    pre = _dot(zz.astype(BF16), wgate_ref[...]) + bgate_ref[...]
    ls = _log_sigmoid(pre) * (1.0 / GLA_GATE_TAU)
    lf_out[...] = ls[:, 0:GLA_K_WIDTH]
    lb_out[...] = ls[:, GLA_K_WIDTH:]


def _proj_call(x, mod_l, g_attn_l, wqkv, wg, wz, seg, gq_t, gk_t, wgate, bgate, new_kv=None, layer=0):
    tm = TOKEN_TILE
    n_tok = x.shape[0]
    is_ctx = new_kv is not None
    row = lambda i: (i, 0)
    const = lambda i: (0, 0)
    wqkv_n = 3 * NA_WIDTH
    wg_n = 2 * GLA_K_WIDTH + 2 * GLA_V_WIDTH
    out_shapes = [
        jax.ShapeDtypeStruct((n_tok, NA_WIDTH), BF16),
        jax.ShapeDtypeStruct((n_tok, NA_WIDTH), BF16),
        jax.ShapeDtypeStruct((n_tok, NA_WIDTH), BF16),
        jax.ShapeDtypeStruct((n_tok, GLA_K_WIDTH), F32),
        jax.ShapeDtypeStruct((n_tok, GLA_K_WIDTH), F32),
        jax.ShapeDtypeStruct((n_tok, GLA_V_WIDTH), BF16),
        jax.ShapeDtypeStruct((n_tok, GLA_V_WIDTH), F32),
        jax.ShapeDtypeStruct((n_tok, GLA_K_WIDTH), F32),
        jax.ShapeDtypeStruct((n_tok, GLA_K_WIDTH), F32),
    ]
    out_specs = [pl.BlockSpec((tm, s.shape[1]), row) for s in out_shapes]
    in_specs = [
        pl.BlockSpec((tm, D_MODEL), row),
        pl.BlockSpec((1, 1, N_MOD * D_MODEL), _mod_index_map(is_ctx)),
        pl.BlockSpec((1, D_MODEL), const),
        pl.BlockSpec((D_MODEL, wqkv_n), const),
        pl.BlockSpec((D_MODEL, wg_n), const),
        pl.BlockSpec((D_MODEL, GATE_PAD), const),
        pl.BlockSpec((NA_WIDTH, NA_WIDTH), const),
        pl.BlockSpec((1, NA_WIDTH), const),
        pl.BlockSpec((1, NA_WIDTH), const),
        pl.BlockSpec((GATE_PAD, 2 * GLA_K_WIDTH), const),
        pl.BlockSpec((1, 2 * GLA_K_WIDTH), const),
    ]
    args = [x, mod_l, g_attn_l, wqkv, wg, wz, seg, gq_t, gk_t, wgate, bgate]
    aliases = {}
    if is_ctx:
        seqs = tm // SEQ
        kv_spec = pl.BlockSpec((seqs, 1, NA_HEADS, SEQ, NA_HEAD_DIM), lambda i: (i, layer, 0, 0, 0))
        aliases = {len(args): len(out_shapes), len(args) + 1: len(out_shapes) + 1}
        in_specs += [pl.BlockSpec(memory_space=pl.ANY)] * 2
        args += list(new_kv)
        out_shapes += [jax.ShapeDtypeStruct(new_kv[0].shape, F32)] * 2
        out_specs += [kv_spec, kv_spec]
    return pl.pallas_call(
        functools.partial(_proj_kernel, is_ctx),
        grid=(n_tok // tm,),
        in_specs=in_specs,
        out_specs=out_specs,
        out_shape=out_shapes,
        input_output_aliases=aliases,
        compiler_params=_params(1),
        name="proj_ctx" if is_ctx else "proj_smp",
    )(*args)


def _softmax_pv(s, v):
    m = jnp.max(s, axis=-1, keepdims=True)
    p = jnp.exp(s - m)
    l = jnp.sum(p, axis=-1, keepdims=True)
    return _dot(p.astype(BF16), v) / l


def _ctx_attn_kernel(q_ref, k_ref, v_ref, o_ref):
    lane = lax.broadcasted_iota(jnp.int32, (1, 2 * NA_HEAD_DIM), 1)
    first = lane < NA_HEAD_DIM
    for hp in range(NA_HEADS // 2):
        cols = slice(hp * 2 * NA_HEAD_DIM, (hp + 1) * 2 * NA_HEAD_DIM)
        q2 = q_ref[:, cols]
        k2 = k_ref[:, cols]
        v2 = v_ref[:, cols]
        o0 = _softmax_pv(_dot_nt(jnp.where(first, q2, jnp.zeros_like(q2)), k2), v2)
        o1 = _softmax_pv(_dot_nt(jnp.where(first, jnp.zeros_like(q2), q2), k2), v2)
        o_ref[:, cols] = jnp.where(first, o0, o1).astype(BF16)


def _ctx_attn_call(q, k, v):
    spec = pl.BlockSpec((SEQ, NA_WIDTH), lambda b: (b, 0))
    return pl.pallas_call(
        _ctx_attn_kernel,
        grid=(BATCH,),
        in_specs=[spec, spec, spec],
        out_specs=spec,
        out_shape=jax.ShapeDtypeStruct((N_CTX, NA_WIDTH), BF16),
        compiler_params=_params(1),
        name="ctx_attn",
    )(q, k, v)


def _na_window_start(r):
    return min(max(r - NA_WIN_ROWS // 2, 0), GRID_ROWS - NA_WIN_ROWS)


def _na_row_groups():
    groups = []
    for g in range(GRID_ROWS // NA_ROW_GROUP):
        starts = [_na_window_start(r) for r in range(g * NA_ROW_GROUP, (g + 1) * NA_ROW_GROUP)]
        lo = min(starts) // 2 * 2
        count = max(starts) + NA_WIN_ROWS - lo
        groups.append((lo, count + count % 2))
    return groups


def _na_group_bias(bias_ref, hh, g, key_lo, key_count, first):
    masked = jnp.full((GRID_W, 2 * GRID_W), MASK_VALUE, F32)
    row_tiles = []
    for r in range(g * NA_ROW_GROUP, (g + 1) * NA_ROW_GROUP):
        rs = _na_window_start(r)
        pieces = []
        for u in range(key_count // 2):
            r0 = key_lo + 2 * u
            in0 = rs <= r0 < rs + NA_WIN_ROWS
            in1 = rs <= r0 + 1 < rs + NA_WIN_ROWS
            if not (in0 or in1):
                pieces.append(masked)
                continue
            tile = bias_ref[0, hh, r0 - r + NA_WIN_ROWS - 1]
            if not in1:
                tile = jnp.where(first, tile, MASK_VALUE)
            elif not in0:
                tile = jnp.where(first, MASK_VALUE, tile)
            pieces.append(tile)
        row_tiles.append(jnp.concatenate(pieces, axis=1))
    return jnp.concatenate(row_tiles, axis=0)


def _na_kernel(q_ref, k_ref, v_ref, kc_ref, vc_ref, bias_ref, o_ref, sc_ref):
    lane = lax.broadcasted_iota(jnp.int32, (1, 2 * NA_HEAD_DIM), 1)
    first = lane < NA_HEAD_DIM
    q2 = q_ref[...]
    kc2 = jnp.concatenate([kc_ref[0, 0, 0], kc_ref[0, 0, 1]], axis=1).astype(BF16)
    vc2 = jnp.concatenate([vc_ref[0, 0, 0], vc_ref[0, 0, 1]], axis=1).astype(BF16)
    block = NA_ROW_GROUP * GRID_W
    outs = []
    for hh in range(2):
        qm = jnp.where(first, q2, jnp.zeros_like(q2)) if hh == 0 else jnp.where(first, jnp.zeros_like(q2), q2)
        sc_ref[...] = _dot_nt(qm, kc2)
        blocks = []
        for g, (key_lo, key_count) in enumerate(_na_row_groups()):
            keys = slice(key_lo * GRID_W, (key_lo + key_count) * GRID_W)
            s = _dot_nt(qm[g * block:(g + 1) * block], k_ref[keys, :])
            s = s + _na_group_bias(bias_ref, hh, g, key_lo, key_count, first)
            sc = sc_ref[g * block:(g + 1) * block, :]
            m = jnp.maximum(jnp.max(s, axis=-1, keepdims=True), jnp.max(sc, axis=-1, keepdims=True))
            p = jnp.exp(s - m)
            pc = jnp.exp(sc - m)
            l = jnp.sum(p, axis=-1, keepdims=True) + jnp.sum(pc, axis=-1, keepdims=True)
            blocks.append((_dot(p.astype(BF16), v_ref[keys, :]) + _dot(pc.astype(BF16), vc2)) / l)
        outs.append(jnp.concatenate(blocks, axis=0))
    o_ref[...] = jnp.where(first, outs[0], outs[1]).astype(BF16)


def _na_call(q, k, v, cache_k, cache_v, bias, layer):
    tok = pl.BlockSpec((DEC_SEQ, 2 * NA_HEAD_DIM), lambda b, hp: (b, hp))
    cache = pl.BlockSpec((1, 1, 2, PAST_LEN, NA_HEAD_DIM), lambda b, hp: (b, layer, hp, 0, 0))
    n_pairs = 2 * NA_WIN_ROWS - 2
    return pl.pallas_call(
        _na_kernel,
        grid=(DEC_BATCH, NA_HEADS // 2),
        in_specs=[
            tok, tok, tok, cache, cache,
            pl.BlockSpec((1, 2, n_pairs, GRID_W, 2 * GRID_W), lambda b, hp: (layer, hp, 0, 0, 0)),
        ],
        out_specs=tok,
        out_shape=jax.ShapeDtypeStruct((N_SMP, NA_WIDTH), BF16),
        scratch_shapes=[pltpu.VMEM((DEC_SEQ, PAST_LEN), F32)],
        compiler_params=_params(2),
        name="nbr_attn",
    )(q, k, v, cache_k, cache_v, bias)


def _na_bias_tables(rpb):
    cols = np.arange(GRID_W)
    cs = np.clip(cols - NA_WIN_COLS // 2, 0, GRID_W - NA_WIN_COLS)
    c2 = cols[None, :]
    valid = (c2 >= cs[:, None]) & (c2 < cs[:, None] + NA_WIN_COLS)
    dc = c2 - cols[:, None] + NA_WIN_COLS - 1
    n_dc = 2 * NA_WIN_COLS - 1
    n_dr = 2 * NA_WIN_ROWS - 1
    onehot = ((dc[None] == np.arange(n_dc)[:, None, None]) & valid[None]).astype(np.float32)
    mask_add = np.where(valid, 0.0, MASK_VALUE).astype(np.float32).reshape(1, GRID_W * GRID_W)
    t = jnp.dot(rpb.reshape(DEPTH * NA_HEADS * n_dr, n_dc), jnp.asarray(onehot.reshape(n_dc, GRID_W * GRID_W)),
                precision=lax.Precision.HIGHEST) + mask_add
    t = t.reshape(DEPTH, NA_HEADS, n_dr, GRID_W, GRID_W)
    return jnp.concatenate([t[:, :, :-1], t[:, :, 1:]], axis=-1)


def _gla_constants():
    C = GLA_CHUNK
    wf = np.zeros((GLA_LEVELS + 2, C, C), np.float32)
    wb = np.zeros((GLA_LEVELS + 2, C, C), np.float32)
    for p in range(GLA_LEVELS):
        m = 1 << p
        for i in range(C):
            c = (i // (2 * m)) * (2 * m) + m
            if i & m:
                wf[p, i, c:i + 1] = 1
                wb[p, i, c:i] = 1
            else:
                wf[p, i, i + 1:c] = 1
                wb[p, i, i:c] = 1
    for i in range(C):
        wf[GLA_LEVELS, i, :i + 1] = 1
        wf[GLA_LEVELS + 1, i, i + 1:] = 1
        wb[GLA_LEVELS, i, i:] = 1
        wb[GLA_LEVELS + 1, i, :i] = 1
    w = np.stack([wf, wb]).reshape(2, (GLA_LEVELS + 2) * C, C)
    w = np.concatenate([w, w], axis=-1)

    ii, jj = np.meshgrid(np.arange(C), np.arange(C), indexing="ij")
    x = ii ^ jj
    hb = np.where(x > 0, np.floor(np.log2(np.maximum(x, 1))), -1).astype(np.int64)
    masks = np.zeros((2, GLA_LEVELS + 1, C, C), np.float32)
    for p in range(GLA_LEVELS):
        masks[0, p] = (jj < ii) & (hb == p)
        masks[1, p] = (jj > ii) & (hb == p)
    masks[:, GLA_LEVELS] = np.eye(C)
    masks = np.tile(masks, (1, 1, 2, 2))
    state_mask = (np.arange(2 * GLA_DV)[:, None] // GLA_DV == np.arange(2 * GLA_DK)[None, :] // GLA_DK)
    return w, masks, state_mask.astype(np.float32)


def _gla_kernel(seq_len, is_ctx, *refs):
    refs = list(refs)
    gq_ref, gk_ref, gv_ref, lf_ref, lb_ref, w_ref, mask_ref, smask_ref = refs[:8]
    refs = refs[8:]
    if is_ctx:
        refs = refs[2:]
        o_ref, sf_ref, sb_ref = refs[:3]
        refs = refs[3:]
    else:
        cos_ref, sin_ref, s0f_ref, s0b_ref, o_ref = refs[:5]
        refs = refs[5:]
    q_scr, k_scr, of_scr, ob_scr, st_scr = refs

    C = GLA_CHUNK
    n_chunks = seq_len // C
    n_pairs = GLA_HEADS // 2
    pair_k = 2 * GLA_DK
    pair_v = 2 * GLA_DV

    q = gq_ref[...]
    k = gk_ref[...]
    if not is_ctx:
        lane = lax.broadcasted_iota(jnp.int32, (1, GLA_K_WIDTH), 1)
        quarter = GLA_DK // 4
        low = (lane % (2 * quarter)) < quarter
        cos = cos_ref[...]
        sin = sin_ref[...]

        def rope(t):
            partner = jnp.where(low, pltpu.roll(t, GLA_K_WIDTH - quarter, axis=1), pltpu.roll(t, quarter, axis=1))
            return t * cos + partner * sin

        q = rope(q)
        k = rope(k)
    q_scr[...] = q
    k_scr[...] = k

    st_scr[...] = jnp.zeros(st_scr.shape, F32)
    if not is_ctx:
        for d, s0_ref in enumerate((s0f_ref, s0b_ref)):
            for h in range(GLA_HEADS):
                pr, e = divmod(h, 2)
                st_scr[d, pr, e * GLA_DV:(e + 1) * GLA_DV, e * GLA_DK:(e + 1) * GLA_DK] = s0_ref[0, 0, h].T

    klane = lax.broadcasted_iota(jnp.int32, (1, GLA_K_WIDTH), 1)
    even_head = (klane // GLA_DK) % 2 == 0

    def one_chunk(d, r0, f_ref, o_scr):
        rows = pl.ds(r0, C)
        f = f_ref[rows, :]
        f_hi = f.astype(BF16)
        f_lo = (f - f_hi.astype(F32)).astype(BF16)
        e = jnp.exp(_dot(w_ref[d], jnp.concatenate([f_hi, f_lo], axis=0)))
        qc = q_scr[rows, :]
        kc = k_scr[rows, :]
        vc = gv_ref[rows, :]
        eq = e[GLA_LEVELS * C:(GLA_LEVELS + 1) * C]
        ek = e[(GLA_LEVELS + 1) * C:(GLA_LEVELS + 2) * C]
        total = eq[C - 1:C] if d == 0 else eq[0:1]

        a = None
        for p in range(GLA_LEVELS + 1):
            if p < GLA_LEVELS:
                ep = e[p * C:(p + 1) * C]
                qe = (qc * ep).astype(BF16)
                ke = (kc * ep).astype(BF16)
            else:
                qe = qc.astype(BF16)
                ke = kc.astype(BF16)
            zero = jnp.zeros((C, pair_k), BF16)
            lhs = jnp.concatenate([jnp.concatenate([qe[:, :pair_k], zero], axis=1),
                                   jnp.concatenate([zero, qe[:, pair_k:]], axis=1)], axis=0)
            rhs = jnp.concatenate([jnp.where(even_head, ke, jnp.zeros_like(ke)),
                                   jnp.where(even_head, jnp.zeros_like(ke), ke)], axis=0)
            s = _dot_nt(lhs, rhs) * mask_ref[d, p]
            a = s if a is None else a + s
        a = a.astype(BF16)

        q_in = (qc * eq).astype(BF16)
        k_out = (kc * ek).astype(BF16)
        zero_v = jnp.zeros((C, GLA_DV), BF16)
        for pr in range(n_pairs):
            kl = slice(pr * pair_k, (pr + 1) * pair_k)
            vp = vc[:, pr * pair_v:(pr + 1) * pair_v]
            v_diag = jnp.concatenate([jnp.concatenate([vp[:, :GLA_DV], zero_v], axis=1),
                                      jnp.concatenate([zero_v, vp[:, GLA_DV:]], axis=1)], axis=0)
            st = st_scr[d, pr]
            o_scr[rows, pr * pair_v:(pr + 1) * pair_v] = (
                _dot(a[pr * C:(pr + 1) * C], v_diag) + _dot_nt(q_in[:, kl], st.astype(BF16)))
            st_scr[d, pr] = st * total[:, kl] + _dot_tn(vp, k_out[:, kl]) * smask_ref[...]

    def body(n, carry):
        one_chunk(0, pl.multiple_of(n * C, C), lf_ref, of_scr)
        one_chunk(1, pl.multiple_of((n_chunks - 1 - n) * C, C), lb_ref, ob_scr)
        return carry

    lax.fori_loop(0, n_chunks, body, 0)
    o_ref[...] = of_scr[...] + ob_scr[...]
    if is_ctx:
        for d, s_ref in enumerate((sf_ref, sb_ref)):
            for h in range(GLA_HEADS):
                pr, e = divmod(h, 2)
                s_ref[0, 0, h] = st_scr[d, pr, e * GLA_DV:(e + 1) * GLA_DV, e * GLA_DK:(e + 1) * GLA_DK].T


def _gla_call(gq, gk, gv, lf, lb, consts, new_states=None, rope=None, state_in=None, layer=0):
    is_ctx = new_states is not None
    seq_len = SEQ if is_ctx else DEC_SEQ
    n_seq = gq.shape[0] // seq_len
    tok = lambda width: pl.BlockSpec((seq_len, width), lambda b: (b, 0))
    whole = lambda a: pl.BlockSpec(a.shape, lambda b: (0,) * a.ndim)
    state_spec = pl.BlockSpec((1, 1, GLA_HEADS, GLA_DK, GLA_DV), lambda b: (b, layer, 0, 0, 0))
    in_specs = [tok(GLA_K_WIDTH), tok(GLA_K_WIDTH), tok(GLA_V_WIDTH), tok(GLA_K_WIDTH), tok(GLA_K_WIDTH)]
    in_specs += [whole(a) for a in consts]
    args = [gq, gk, gv, lf, lb, *consts]
    out_shape = [jax.ShapeDtypeStruct((n_seq * seq_len, GLA_V_WIDTH), F32)]
    out_specs = [tok(GLA_V_WIDTH)]
    aliases = {}
    if is_ctx:
        aliases = {len(args): 1, len(args) + 1: 2}
        in_specs += [pl.BlockSpec(memory_space=pl.ANY)] * 2
        args += list(new_states)
        out_shape += [jax.ShapeDtypeStruct(new_states[0].shape, F32)] * 2
        out_specs += [state_spec, state_spec]
    else:
        in_specs += [pl.BlockSpec((seq_len, GLA_K_WIDTH), lambda b: (0, 0))] * 2 + [state_spec, state_spec]
        args += list(rope) + list(state_in)
    return pl.pallas_call(
        functools.partial(_gla_kernel, seq_len, is_ctx),
        grid=(n_seq,),
        in_specs=in_specs,
        out_specs=out_specs,
        out_shape=out_shape,
        input_output_aliases=aliases,
        scratch_shapes=[
            pltpu.VMEM((seq_len, GLA_K_WIDTH), F32),
            pltpu.VMEM((seq_len, GLA_K_WIDTH), F32),
            pltpu.VMEM((seq_len, GLA_V_WIDTH), F32),
            pltpu.VMEM((seq_len, GLA_V_WIDTH), F32),
            pltpu.VMEM((2, GLA_HEADS // 2, 2 * GLA_DV, 2 * GLA_DK), F32),
        ],
        compiler_params=_params(1),
        name="gla_ctx" if is_ctx else "gla_smp",
    )(*args)


def _rope_tables():
    quarter = GLA_DK // 4
    inv = ROPE_BASE ** (-jnp.arange(quarter, dtype=F32) / quarter)
    t = jnp.arange(DEC_SEQ)
    ang_r = (t // GRID_W).astype(F32)[:, None] * inv
    ang_c = (t % GRID_W).astype(F32)[:, None] * inv
    cos = jnp.concatenate([jnp.cos(ang_r)] * 2 + [jnp.cos(ang_c)] * 2, axis=-1)
    sin = jnp.concatenate([-jnp.sin(ang_r), jnp.sin(ang_r), -jnp.sin(ang_c), jnp.sin(ang_c)], axis=-1)
    return jnp.tile(cos, (1, GLA_HEADS)), jnp.tile(sin, (1, GLA_HEADS))


def _merge_mlp_kernel(x_ref, mod_ref, ona_ref, ogla_ref, gate_ref, gout_ref, wo_ref, gmlp_ref,
                      wup_ref, wdown_ref, o_ref):
    mod = mod_ref[0]
    ga1 = mod[:, 2 * D_MODEL:3 * D_MODEL]
    sh2 = mod[:, 3 * D_MODEL:4 * D_MODEL]
    sc2 = mod[:, 4 * D_MODEL:5 * D_MODEL]
    ga2 = mod[:, 5 * D_MODEL:6 * D_MODEL]

    og = ogla_ref[...]
    normed = []
    for h in range(GLA_HEADS):
        oh = og[:, h * GLA_DV:(h + 1) * GLA_DV]
        normed.append(oh * lax.rsqrt(jnp.mean(oh * oh, axis=-1, keepdims=True) + EPS))
    gate = gate_ref[...]
    g = jnp.concatenate(normed, axis=1) * gout_ref[...] * (gate * (1.0 / (1.0 + jnp.exp(-gate))))
    att = _dot(ona_ref[...], wo_ref[0:NA_WIDTH, :]) + _dot(g.astype(BF16), wo_ref[NA_WIDTH:, :])
    x = x_ref[...] + ga1 * att

    ms = jnp.mean(x * x, axis=-1, keepdims=True)
    h2 = ((x * lax.rsqrt(ms + EPS) * gmlp_ref[...]) * (1.0 + sc2) + sh2).astype(BF16)
    acc = jnp.zeros_like(x)
    for c in range(D_FF // FF_CHUNK):
        u = jnp.maximum(_dot(h2, wup_ref[:, c * FF_CHUNK:(c + 1) * FF_CHUNK]), 0.0)
        acc = acc + _dot((u * u).astype(BF16), wdown_ref[c * FF_CHUNK:(c + 1) * FF_CHUNK, :])
    o_ref[...] = x + ga2 * acc


def _merge_mlp_call(x, mod_l, o_na, o_gla, gate, gout_t, wo, g_mlp_l, wup, wdown, is_ctx):
    tm = TOKEN_TILE
    n_tok = x.shape[0]
    row = lambda i: (i, 0)
    const = lambda i: (0, 0)
    resident = dict(pipeline_mode=pl.Buffered(1))
    return pl.pallas_call(
        _merge_mlp_kernel,
        grid=(n_tok // tm,),
        in_specs=[
            pl.BlockSpec((tm, D_MODEL), row),
            pl.BlockSpec((1, 1, N_MOD * D_MODEL), _mod_index_map(is_ctx)),
            pl.BlockSpec((tm, NA_WIDTH), row),
            pl.BlockSpec((tm, GLA_V_WIDTH), row),
            pl.BlockSpec((tm, GLA_V_WIDTH), row),
            pl.BlockSpec((1, GLA_V_WIDTH), const),
            pl.BlockSpec((D_MODEL, D_MODEL), const, **resident),
            pl.BlockSpec((1, D_MODEL), const),
            pl.BlockSpec((D_MODEL, D_FF), const, **resident),
            pl.BlockSpec((D_FF, D_MODEL), const, **resident),
        ],
        out_specs=pl.BlockSpec((tm, D_MODEL), row),
        out_shape=jax.ShapeDtypeStruct((n_tok, D_MODEL), F32),
        compiler_params=_params(1),
        name="merge_mlp_ctx" if is_ctx else "merge_mlp_smp",
    )(x, mod_l, o_na, o_gla, gate, gout_t, wo, g_mlp_l, wup, wdown)


def kernel(x_prompt, x_sample, cache_k, cache_v, state_fwd, state_bwd, c, c_ctx, w_ada, b_ada, g_attn, w_in,
           g_q, g_k, rpb, w_gf, b_gf, w_gb, b_gb, g_gla_out, w_o, g_mlp, w_up, w_down):
    x_c = x_prompt.reshape(N_CTX, D_MODEL)
    x_s = x_sample.reshape(N_SMP, D_MODEL)

    c_rows = jnp.concatenate([c_ctx[None, :], c, jnp.zeros((MOD_ROWS - 1 - DEC_BATCH, D_MODEL), F32)], axis=0)
    mods = _ada_call(c_rows, w_ada, b_ada).reshape(DEPTH, MOD_ROWS, 1, N_MOD * D_MODEL)

    qkv_n = 3 * NA_WIDTH
    g_n = 2 * GLA_K_WIDTH + 2 * GLA_V_WIDTH
    wqkv = w_in[:, :, :qkv_n].astype(BF16)
    wg = w_in[:, :, qkv_n:qkv_n + g_n].astype(BF16)
    wz = jnp.pad(w_in[:, :, qkv_n + g_n:], ((0, 0), (0, 0), (0, GATE_PAD - 2 * GLA_GATE_RANK))).astype(BF16)
    wgate = jnp.zeros((DEPTH, GATE_PAD, 2 * GLA_K_WIDTH), F32)
    wgate = wgate.at[:, :GLA_GATE_RANK, :GLA_K_WIDTH].set(w_gf)
    wgate = wgate.at[:, GLA_GATE_RANK:2 * GLA_GATE_RANK, GLA_K_WIDTH:].set(w_gb).astype(BF16)
    bgate = jnp.concatenate([b_gf, b_gb], axis=-1).reshape(DEPTH, 1, 2 * GLA_K_WIDTH)
    wo_b = w_o.astype(BF16)
    wup_b = w_up.astype(BF16)
    wdown_b = w_down.astype(BF16)
    head_of = np.arange(NA_WIDTH) // NA_HEAD_DIM
    seg = jnp.asarray((head_of[:, None] == head_of[None, :]).astype(np.float32) / NA_HEAD_DIM, BF16)
    gq_t = jnp.tile(g_q, (1, NA_HEADS)).reshape(DEPTH, 1, NA_WIDTH)
    gk_t = jnp.tile(g_k, (1, NA_HEADS)).reshape(DEPTH, 1, NA_WIDTH)
    gout_t = jnp.tile(g_gla_out, (1, GLA_HEADS)).reshape(DEPTH, 1, GLA_V_WIDTH)
    na_bias = _na_bias_tables(rpb)
    gla_w_np, gla_masks_np, gla_smask_np = _gla_constants()
    gla_consts = (jnp.asarray(gla_w_np, BF16), jnp.asarray(gla_masks_np, F32), jnp.asarray(gla_smask_np, F32))
    rope = _rope_tables()

    new_k = jnp.zeros((BATCH, DEPTH, NA_HEADS, SEQ, NA_HEAD_DIM), F32)
    new_v = jnp.zeros((BATCH, DEPTH, NA_HEADS, SEQ, NA_HEAD_DIM), F32)
    new_sf = jnp.zeros((BATCH, DEPTH, GLA_HEADS, GLA_DK, GLA_DV), F32)
    new_sb = jnp.zeros((BATCH, DEPTH, GLA_HEADS, GLA_DK, GLA_DV), F32)

    for l in range(DEPTH):
        proj_w = (g_attn[l].reshape(1, D_MODEL), wqkv[l], wg[l], wz[l], seg, gq_t[l], gk_t[l], wgate[l], bgate[l])
        mlp_w = (gout_t[l], wo_b[l], g_mlp[l].reshape(1, D_MODEL), wup_b[l], wdown_b[l])

        q, k, v, gq, gk, gv, gate, lf, lb, new_k, new_v = _proj_call(
            x_c, mods[l], *proj_w, new_kv=(new_k, new_v), layer=l)
        o_na = _ctx_attn_call(q, k, v)
        o_gla, new_sf, new_sb = _gla_call(gq, gk, gv, lf, lb, gla_consts, new_states=(new_sf, new_sb), layer=l)
        x_c = _merge_mlp_call(x_c, mods[l], o_na, o_gla, gate, *mlp_w, is_ctx=True)

        q, k, v, gq, gk, gv, gate, lf, lb = _proj_call(x_s, mods[l], *proj_w)
        o_na = _na_call(q, k, v, cache_k, cache_v, na_bias, l)
        (o_gla,) = _gla_call(gq, gk, gv, lf, lb, gla_consts, rope=rope,
                             state_in=(state_fwd, state_bwd), layer=l)
        x_s = _merge_mlp_call(x_s, mods[l], o_na, o_gla, gate, *mlp_w, is_ctx=False)

    return (x_c.reshape(BATCH, SEQ, D_MODEL), x_s.reshape(DEC_BATCH, DEC_SEQ, D_MODEL),
            new_k, new_v, new_sf, new_sb)
```

```python
import functools

import numpy as np
import jax
import jax.numpy as jnp
from jax import lax
from jax.experimental import pallas as pl
from jax.experimental.pallas import tpu as pltpu

F32 = jnp.float32
BF16 = jnp.bfloat16

D_MODEL = 1024
BATCH = 16
SEQ = 256
DEPTH = 4
DEC_BATCH = 2
DEC_SEQ = 1024
PAST_LEN = 512
GRID_W = 64
GRID_ROWS = DEC_SEQ // GRID_W
NA_WIDTH = D_MODEL // 2
NA_HEAD_DIM = 64
NA_HEADS = NA_WIDTH // NA_HEAD_DIM
NA_WIN_ROWS = 8
NA_WIN_COLS = 16
GLA_V_WIDTH = D_MODEL - NA_WIDTH
GLA_HEADS = 4
GLA_DV = GLA_V_WIDTH // GLA_HEADS
GLA_DK = GLA_DV // 2
GLA_K_WIDTH = GLA_HEADS * GLA_DK
GLA_GATE_RANK = 16
GLA_GATE_TAU = 16.0
GLA_CHUNK = 64
D_FF = 4 * D_MODEL
ROPE_BASE = 10000.0
N_MOD = 6
EPS = 1e-6

N_CTX = BATCH * SEQ
N_SMP = DEC_BATCH * DEC_SEQ
MOD_ROWS = 8
MASK_VALUE = -1e30

TOKEN_TILE = 512
FF_CHUNK = 1024
ADA_TILE_N = 1536
GLA_LEVELS = 6
NA_ROW_GROUP = 4
GLA_SEQS_PER_STEP = 2
VMEM_LIMIT = 52 * 1024 * 1024


def _dot(a, b):
    return jnp.dot(a, b, preferred_element_type=F32)


def _dot_nt(a, b):
    return lax.dot_general(a, b, (((1,), (1,)), ((), ())), preferred_element_type=F32)


def _dot_tn(a, b):
    return lax.dot_general(a, b, (((0,), (0,)), ((), ())), preferred_element_type=F32)


def _params(n_grid_dims=1):
    return pltpu.CompilerParams(dimension_semantics=("arbitrary",) * n_grid_dims,
                                vmem_limit_bytes=VMEM_LIMIT)


def _mod_index_map(is_ctx):
    if is_ctx:
        return lambda i: (0, 0, 0)
    return lambda i: (1 + i // (DEC_SEQ // TOKEN_TILE), 0, 0)


def _ada_kernel(c_ref, w_ref, b_ref, o_ref):
    cv = c_ref[...]
    s = cv * (1.0 / (1.0 + jnp.exp(-cv)))
    o_ref[0] = _dot(s.astype(BF16), w_ref[0].astype(BF16)) + b_ref[0]


def _ada_call(c_rows, w_ada, b_ada):
    n_mod = N_MOD * D_MODEL
    return pl.pallas_call(
        _ada_kernel,
        grid=(DEPTH, n_mod // ADA_TILE_N),
        in_specs=[
            pl.BlockSpec((MOD_ROWS, D_MODEL), lambda l, j: (0, 0)),
            pl.BlockSpec((1, D_MODEL, ADA_TILE_N), lambda l, j: (l, 0, j)),
            pl.BlockSpec((1, 1, ADA_TILE_N), lambda l, j: (l, 0, j)),
        ],
        out_specs=pl.BlockSpec((1, MOD_ROWS, ADA_TILE_N), lambda l, j: (l, 0, j)),
        out_shape=jax.ShapeDtypeStruct((DEPTH, MOD_ROWS, n_mod), F32),
        compiler_params=_params(2),
        name="ada_mod",
    )(c_rows, w_ada, b_ada.reshape(DEPTH, 1, n_mod))


def _log_sigmoid(x):
    return jnp.minimum(x, 0.0) - jnp.log1p(jnp.exp(-jnp.abs(x)))


def _proj_kernel(mode, x_ref, mod_ref, g_ref, win_ref, seg_ref, gq_ref, gk_ref, wgate_ref, bgate_ref, *rest):
    if mode == "ctx":
        rest = rest[2:]
    q_out, k_out, v_out, gq_out, gk_out, gv_out, gate_out, lf_out, lb_out = rest[:9]
    qkv_n = 3 * NA_WIDTH
    g_n = 2 * GLA_K_WIDTH + 2 * GLA_V_WIDTH

    def project():
        x = x_ref[...]
        mod = mod_ref[0]
        sh1 = mod[:, 0:D_MODEL]
        sc1 = mod[:, D_MODEL:2 * D_MODEL]
        ms = jnp.mean(x * x, axis=-1, keepdims=True)
        h = (x * lax.rsqrt(ms + EPS) * g_ref[...]) * (1.0 + sc1) + sh1
        hb = h.astype(BF16)

        z = _dot(hb, win_ref[0, :, 0:qkv_n])
        q = z[:, 0:NA_WIDTH]
        k = z[:, NA_WIDTH:2 * NA_WIDTH]
        v = z[:, 2 * NA_WIDTH:3 * NA_WIDTH]
        seg = seg_ref[...]
        q_ms = _dot((q * q).astype(BF16), seg)
        k_ms = _dot((k * k).astype(BF16), seg)
        q_out[...] = (q * lax.rsqrt(q_ms + EPS) * gq_ref[...] * (NA_HEAD_DIM ** -0.5)).astype(BF16)
        kn = k * lax.rsqrt(k_ms + EPS) * gk_ref[...]
        k_out[...] = kn.astype(BF16)
        v_out[...] = v.astype(BF16)
        if mode != "smp":
            knew_out, vnew_out = rest[9:]
            for s in range(TOKEN_TILE // SEQ):
                for hd in range(NA_HEADS):
                    rows = slice(s * SEQ, (s + 1) * SEQ)
                    cols = slice(hd * NA_HEAD_DIM, (hd + 1) * NA_HEAD_DIM)
                    knew_out[s, 0, hd] = kn[rows, cols]
                    vnew_out[s, 0, hd] = v[rows, cols]

        z = _dot(hb, win_ref[0, :, qkv_n:qkv_n + g_n])
        gq_out[...] = z[:, 0:GLA_K_WIDTH] * (GLA_DK ** -0.5)
        gk_out[...] = z[:, GLA_K_WIDTH:2 * GLA_K_WIDTH]
        gv_out[...] = z[:, 2 * GLA_K_WIDTH:2 * GLA_K_WIDTH + GLA_V_WIDTH].astype(BF16)
        gate_out[...] = z[:, 2 * GLA_K_WIDTH + GLA_V_WIDTH:]

        zz = _dot(hb, win_ref[0, :, qkv_n + g_n:])
        pre = _dot(zz.astype(BF16), wgate_ref[...]) + bgate_ref[...]
        ls = _log_sigmoid(pre) * (1.0 / GLA_GATE_TAU)
        lf_out[...] = ls[:, 0:GLA_K_WIDTH]
        lb_out[...] = ls[:, GLA_K_WIDTH:]

    if mode == "ctx_first":
        knew_out, vnew_out = rest[9:]
        layer_step = pl.program_id(1)
        pl.when(layer_step == 0)(project)

        @pl.when(layer_step > 0)
        def _():
            knew_out[...] = jnp.zeros(knew_out.shape, F32)
            vnew_out[...] = jnp.zeros(vnew_out.shape, F32)
    else:
        project()


def _proj_call(x, mod_l, g_attn_l, w_in_b, seg, gq_t, gk_t, wgate, bgate, new_kv=None, layer=0):
    tm = TOKEN_TILE
    n_tok = x.shape[0]
    is_ctx = n_tok == N_CTX
    mode = "smp" if not is_ctx else ("ctx_first" if new_kv is None else "ctx")
    row = lambda i, j: (i, 0)
    const = lambda i, j: (0, 0)
    mod_map = _mod_index_map(is_ctx)
    out_shapes = [
        jax.ShapeDtypeStruct((n_tok, NA_WIDTH), BF16),
        jax.ShapeDtypeStruct((n_tok, NA_WIDTH), BF16),
        jax.ShapeDtypeStruct((n_tok, NA_WIDTH), BF16),
        jax.ShapeDtypeStruct((n_tok, GLA_K_WIDTH), F32),
        jax.ShapeDtypeStruct((n_tok, GLA_K_WIDTH), F32),
        jax.ShapeDtypeStruct((n_tok, GLA_V_WIDTH), BF16),
        jax.ShapeDtypeStruct((n_tok, GLA_V_WIDTH), F32),
        jax.ShapeDtypeStruct((n_tok, GLA_K_WIDTH), F32),
        jax.ShapeDtypeStruct((n_tok, GLA_K_WIDTH), F32),
    ]
    out_specs = [pl.BlockSpec((tm, s.shape[1]), row) for s in out_shapes]
    in_specs = [
        pl.BlockSpec((tm, D_MODEL), row),
        pl.BlockSpec((1, 1, N_MOD * D_MODEL), lambda i, j: mod_map(i)),
        pl.BlockSpec((1, D_MODEL), const),
        pl.BlockSpec((1,) + w_in_b.shape[1:], lambda i, j: (layer, 0, 0)),
        pl.BlockSpec((NA_WIDTH, NA_WIDTH), const),
        pl.BlockSpec((1, NA_WIDTH), const),
        pl.BlockSpec((1, NA_WIDTH), const),
        pl.BlockSpec(wgate.shape, const),
        pl.BlockSpec((1, 2 * GLA_K_WIDTH), const),
    ]
    args = [x, mod_l, g_attn_l, w_in_b, seg, gq_t, gk_t, wgate, bgate]
    aliases = {}
    layer_steps = 1
    if is_ctx:
        kv_shape = (BATCH, DEPTH, NA_HEADS, SEQ, NA_HEAD_DIM)
        kv_block = (tm // SEQ, 1, NA_HEADS, SEQ, NA_HEAD_DIM)
        if mode == "ctx_first":
            layer_steps = DEPTH
            kv_spec = pl.BlockSpec(kv_block, lambda i, j: (i, j, 0, 0, 0))
        else:
            kv_spec = pl.BlockSpec(kv_block, lambda i, j: (i, layer, 0, 0, 0))
            aliases = {len(args): len(out_shapes), len(args) + 1: len(out_shapes) + 1}
            in_specs += [pl.BlockSpec(memory_space=pl.ANY)] * 2
            args += list(new_kv)
        out_shapes += [jax.ShapeDtypeStruct(kv_shape, F32)] * 2
        out_specs += [kv_spec, kv_spec]
    return pl.pallas_call(
        functools.partial(_proj_kernel, mode),
        grid=(n_tok // tm, layer_steps),
        in_specs=in_specs,
        out_specs=out_specs,
        out_shape=out_shapes,
        input_output_aliases=aliases,
        compiler_params=_params(2),
        name="proj_" + mode,
    )(*args)


def _softmax_pv(s, v):
    m = jnp.max(s, axis=-1, keepdims=True)
    p = jnp.exp(s - m)
    l = jnp.sum(p, axis=-1, keepdims=True)
    return _dot(p.astype(BF16), v) / l


def _ctx_attn_kernel(q_ref, k_ref, v_ref, o_ref):
    lane = lax.broadcasted_iota(jnp.int32, (1, 2 * NA_HEAD_DIM), 1)
    first = lane < NA_HEAD_DIM
    for hp in range(NA_HEADS // 2):
        cols = slice(hp * 2 * NA_HEAD_DIM, (hp + 1) * 2 * NA_HEAD_DIM)
        q2 = q_ref[:, cols]
        k2 = k_ref[:, cols]
        v2 = v_ref[:, cols]
        o0 = _softmax_pv(_dot_nt(jnp.where(first, q2, jnp.zeros_like(q2)), k2), v2)
        o1 = _softmax_pv(_dot_nt(jnp.where(first, jnp.zeros_like(q2), q2), k2), v2)
        o_ref[:, cols] = jnp.where(first, o0, o1).astype(BF16)


def _ctx_attn_call(q, k, v):
    spec = pl.BlockSpec((SEQ, NA_WIDTH), lambda b: (b, 0))
    return pl.pallas_call(
        _ctx_attn_kernel,
        grid=(BATCH,),
        in_specs=[spec, spec, spec],
        out_specs=spec,
        out_shape=jax.ShapeDtypeStruct((N_CTX, NA_WIDTH), BF16),
        compiler_params=_params(1),
        name="ctx_attn",
    )(q, k, v)


def _na_window_start(r):
    return min(max(r - NA_WIN_ROWS // 2, 0), GRID_ROWS - NA_WIN_ROWS)


def _na_row_groups():
    groups = []
    for g in range(GRID_ROWS // NA_ROW_GROUP):
        starts = [_na_window_start(r) for r in range(g * NA_ROW_GROUP, (g + 1) * NA_ROW_GROUP)]
        lo = min(starts) // 2 * 2
        count = max(starts) + NA_WIN_ROWS - lo
        groups.append((lo, count + count % 2))
    return groups


def _na_bias_tiles(rpb_ref, hh, tiles_scr):
    n_dr = 2 * NA_WIN_ROWS - 1
    lanes = 2 * GRID_W
    lane = lax.broadcasted_iota(jnp.int32, (GRID_W, lanes), 1)
    col = lax.broadcasted_iota(jnp.int32, (GRID_W, lanes), 0)
    first = lane < GRID_W
    c2 = lane % GRID_W
    cs = jnp.clip(col - NA_WIN_COLS // 2, 0, GRID_W - NA_WIN_COLS)
    valid = (c2 >= cs) & (c2 < cs + NA_WIN_COLS)
    rows = rpb_ref[0, hh]
    base = lanes - (NA_WIN_COLS - 1)

    def skew(dr, shift):
        row = jnp.broadcast_to(rows[dr:dr + 1], (GRID_W, lanes))
        return pltpu.roll(row, shift, axis=1, stride=1, stride_axis=0)

    for e in range(n_dr - 1):
        tile = jnp.where(first, skew(e, base), skew(e + 1, (base + GRID_W) % lanes))
        tiles_scr[e] = jnp.where(valid, tile, MASK_VALUE)


def _na_group_bias(tiles_scr, g, key_lo, key_count, first):
    masked = jnp.full((GRID_W, 2 * GRID_W), MASK_VALUE, F32)
    row_tiles = []
    for r in range(g * NA_ROW_GROUP, (g + 1) * NA_ROW_GROUP):
        rs = _na_window_start(r)
        pieces = []
        for u in range(key_count // 2):
            r0 = key_lo + 2 * u
            in0 = rs <= r0 < rs + NA_WIN_ROWS
            in1 = rs <= r0 + 1 < rs + NA_WIN_ROWS
            if not (in0 or in1):
                pieces.append(masked)
                continue
            tile = tiles_scr[r0 - r + NA_WIN_ROWS - 1]
            if not in1:
                tile = jnp.where(first, tile, MASK_VALUE)
            elif not in0:
                tile = jnp.where(first, MASK_VALUE, tile)
            pieces.append(tile)
        row_tiles.append(jnp.concatenate(pieces, axis=1))
    return jnp.concatenate(row_tiles, axis=0)


def _na_kernel(q_ref, k_ref, v_ref, kc_ref, vc_ref, rpb_ref, o_ref, sc_ref, tiles_scr):
    lane = lax.broadcasted_iota(jnp.int32, (1, 2 * NA_HEAD_DIM), 1)
    first = lane < NA_HEAD_DIM
    q2 = q_ref[...]
    kc2 = jnp.concatenate([kc_ref[0, 0, 0], kc_ref[0, 0, 1]], axis=1).astype(BF16)
    vc2 = jnp.concatenate([vc_ref[0, 0, 0], vc_ref[0, 0, 1]], axis=1).astype(BF16)
    block = NA_ROW_GROUP * GRID_W
    outs = []
    for hh in range(2):
        qm = jnp.where(first, q2, jnp.zeros_like(q2)) if hh == 0 else jnp.where(first, jnp.zeros_like(q2), q2)
        sc_ref[...] = _dot_nt(qm, kc2)
        _na_bias_tiles(rpb_ref, hh, tiles_scr)
        blocks = []
        for g, (key_lo, key_count) in enumerate(_na_row_groups()):
            keys = slice(key_lo * GRID_W, (key_lo + key_count) * GRID_W)
            s = _dot_nt(qm[g * block:(g + 1) * block], k_ref[keys, :])
            s = s + _na_group_bias(tiles_scr, g, key_lo, key_count, first)
            sc = sc_ref[g * block:(g + 1) * block, :]
            m = jnp.maximum(jnp.max(s, axis=-1, keepdims=True), jnp.max(sc, axis=-1, keepdims=True))
            p = jnp.exp(s - m)
            pc = jnp.exp(sc - m)
            l = jnp.sum(p, axis=-1, keepdims=True) + jnp.sum(pc, axis=-1, keepdims=True)
            blocks.append((_dot(p.astype(BF16), v_ref[keys, :]) + _dot(pc.astype(BF16), vc2)) / l)
        outs.append(jnp.concatenate(blocks, axis=0))
    o_ref[...] = jnp.where(first, outs[0], outs[1]).astype(BF16)


def _na_call(q, k, v, cache_k, cache_v, rpb_rows, layer):
    tok = pl.BlockSpec((DEC_SEQ, 2 * NA_HEAD_DIM), lambda b, hp: (b, hp))
    cache = pl.BlockSpec((1, 1, 2, PAST_LEN, NA_HEAD_DIM), lambda b, hp: (b, layer, hp, 0, 0))
    n_pairs = 2 * NA_WIN_ROWS - 2
    return pl.pallas_call(
        _na_kernel,
        grid=(DEC_BATCH, NA_HEADS // 2),
        in_specs=[
            tok, tok, tok, cache, cache,
            pl.BlockSpec((1, 2) + rpb_rows.shape[2:], lambda b, hp: (layer, hp, 0, 0)),
        ],
        out_specs=tok,
        out_shape=jax.ShapeDtypeStruct((N_SMP, NA_WIDTH), BF16),
        scratch_shapes=[pltpu.VMEM((DEC_SEQ, PAST_LEN), F32),
                        pltpu.VMEM((n_pairs, GRID_W, 2 * GRID_W), F32)],
        compiler_params=_params(2),
        name="nbr_attn",
    )(q, k, v, cache_k, cache_v, rpb_rows)


def _gla_constants():
    C = GLA_CHUNK
    wf = np.zeros((GLA_LEVELS + 2, C, C), np.float32)
    wb = np.zeros((GLA_LEVELS + 2, C, C), np.float32)
    for p in range(GLA_LEVELS):
        m = 1 << p
        for i in range(C):
            c = (i // (2 * m)) * (2 * m) + m
            if i & m:
                wf[p, i, c:i + 1] = 1
                wb[p, i, c:i] = 1
            else:
                wf[p, i, i + 1:c] = 1
                wb[p, i, i:c] = 1
    for i in range(C):
        wf[GLA_LEVELS, i, :i + 1] = 1
        wf[GLA_LEVELS + 1, i, i + 1:] = 1
        wb[GLA_LEVELS, i, i:] = 1
        wb[GLA_LEVELS + 1, i, :i] = 1
    w = np.stack([wf, wb]).reshape(2, (GLA_LEVELS + 2) * C, C)
    w = np.concatenate([w, w], axis=-1)

    ii, jj = np.meshgrid(np.arange(C), np.arange(C), indexing="ij")
    x = ii ^ jj
    hb = np.where(x > 0, np.floor(np.log2(np.maximum(x, 1))), -1).astype(np.int64)
    masks = np.zeros((2, GLA_LEVELS + 1, C, C), np.float32)
    for p in range(GLA_LEVELS):
        masks[0, p] = (jj < ii) & (hb == p)
        masks[1, p] = (jj > ii) & (hb == p)
    masks[:, GLA_LEVELS] = np.eye(C)
    masks = np.tile(masks, (1, 1, 2, 2))
    state_mask = (np.arange(2 * GLA_DV)[:, None] // GLA_DV == np.arange(2 * GLA_DK)[None, :] // GLA_DK)
    return w, masks, state_mask.astype(np.float32)


def _gla_kernel(seq_len, n_seq, is_ctx, *refs):
    refs = list(refs)
    gq_ref, gk_ref, gv_ref, lf_ref, lb_ref, w_ref, mask_ref, smask_ref = refs[:8]
    refs = refs[8:]
    if is_ctx:
        refs = refs[2:]
        o_ref, sf_ref, sb_ref = refs[:3]
        refs = refs[3:]
    else:
        cos_ref, sin_ref, s0f_ref, s0b_ref, o_ref = refs[:5]
        refs = refs[5:]
    q_scr, k_scr, of_scr, ob_scr, st_scr = refs

    C = GLA_CHUNK
    n_chunks = seq_len // C
    n_pairs = GLA_HEADS // 2
    pair_k = 2 * GLA_DK
    pair_v = 2 * GLA_DV

    q = gq_ref[...]
    k = gk_ref[...]
    if not is_ctx:
        lane = lax.broadcasted_iota(jnp.int32, (1, GLA_K_WIDTH), 1)
        quarter = GLA_DK // 4
        low = (lane % (2 * quarter)) < quarter
        cos = jnp.concatenate([cos_ref[...]] * n_seq, axis=0)
        sin = jnp.concatenate([sin_ref[...]] * n_seq, axis=0)

        def rope(t):
            partner = jnp.where(low, pltpu.roll(t, GLA_K_WIDTH - quarter, axis=1), pltpu.roll(t, quarter, axis=1))
            return t * cos + partner * sin

        q = rope(q)
        k = rope(k)
    q_scr[...] = q
    k_scr[...] = k

    st_scr[...] = jnp.zeros(st_scr.shape, F32)
    if not is_ctx:
        for d, s0_ref in enumerate((s0f_ref, s0b_ref)):
            for s in range(n_seq):
                for h in range(GLA_HEADS):
                    pr, e = divmod(h, 2)
                    st_scr[s, d, pr, e * GLA_DV:(e + 1) * GLA_DV, e * GLA_DK:(e + 1) * GLA_DK] = s0_ref[s, 0, h].T

    klane = lax.broadcasted_iota(jnp.int32, (1, GLA_K_WIDTH), 1)
    even_head = (klane // GLA_DK) % 2 == 0

    def one_chunk(s, d, r0, f_ref, o_scr):
        rows = pl.ds(s * seq_len + r0, C)
        f = f_ref[rows, :]
        f_hi = f.astype(BF16)
        f_lo = (f - f_hi.astype(F32)).astype(BF16)
        e = jnp.exp(_dot(w_ref[d], jnp.concatenate([f_hi, f_lo], axis=0)))
        qc = q_scr[rows, :]
        kc = k_scr[rows, :]
        vc = gv_ref[rows, :]
        eq = e[GLA_LEVELS * C:(GLA_LEVELS + 1) * C]
        ek = e[(GLA_LEVELS + 1) * C:(GLA_LEVELS + 2) * C]
        total = eq[C - 1:C] if d == 0 else eq[0:1]

        a = None
        for p in range(GLA_LEVELS + 1):
            if p < GLA_LEVELS:
                ep = e[p * C:(p + 1) * C]
                qe = (qc * ep).astype(BF16)
                ke = (kc * ep).astype(BF16)
            else:
                qe = qc.astype(BF16)
                ke = kc.astype(BF16)
            zero = jnp.zeros((C, pair_k), BF16)
            lhs = jnp.concatenate([jnp.concatenate([qe[:, :pair_k], zero], axis=1),
                                   jnp.concatenate([zero, qe[:, pair_k:]], axis=1)], axis=0)
            rhs = jnp.concatenate([jnp.where(even_head, ke, jnp.zeros_like(ke)),
                                   jnp.where(even_head, jnp.zeros_like(ke), ke)], axis=0)
            sc = _dot_nt(lhs, rhs) * mask_ref[d, p]
            a = sc if a is None else a + sc
        a = a.astype(BF16)

        q_in = (qc * eq).astype(BF16)
        k_out = (kc * ek).astype(BF16)
        zero_v = jnp.zeros((C, GLA_DV), BF16)
        for pr in range(n_pairs):
            kl = slice(pr * pair_k, (pr + 1) * pair_k)
            vp = vc[:, pr * pair_v:(pr + 1) * pair_v]
            v_diag = jnp.concatenate([jnp.concatenate([vp[:, :GLA_DV], zero_v], axis=1),
                                      jnp.concatenate([zero_v, vp[:, GLA_DV:]], axis=1)], axis=0)
            st = st_scr[s, d, pr]
            o_scr[rows, pr * pair_v:(pr + 1) * pair_v] = (
                _dot(a[pr * C:(pr + 1) * C], v_diag) + _dot_nt(q_in[:, kl], st.astype(BF16)))
            st_scr[s, d, pr] = st * total[:, kl] + _dot_tn(vp, k_out[:, kl]) * smask_ref[...]

    def body(n, carry):
        for s in range(n_seq):
            one_chunk(s, 0, pl.multiple_of(n * C, C), lf_ref, of_scr)
            one_chunk(s, 1, pl.multiple_of((n_chunks - 1 - n) * C, C), lb_ref, ob_scr)
        return carry

    lax.fori_loop(0, n_chunks, body, 0)
    o_ref[...] = of_scr[...] + ob_scr[...]
    if is_ctx:
        for d, s_ref in enumerate((sf_ref, sb_ref)):
            for s in range(n_seq):
                for h in range(GLA_HEADS):
                    pr, e = divmod(h, 2)
                    s_ref[s, 0, h] = st_scr[s, d, pr, e * GLA_DV:(e + 1) * GLA_DV, e * GLA_DK:(e + 1) * GLA_DK].T


def _gla_call(gq, gk, gv, lf, lb, consts, new_states=None, rope=None, state_in=None, layer=0):
    is_ctx = new_states is not None
    seq_len = SEQ if is_ctx else DEC_SEQ
    n_tok = gq.shape[0]
    n_seq = GLA_SEQS_PER_STEP
    rows = n_seq * seq_len
    tok = lambda width: pl.BlockSpec((rows, width), lambda b: (b, 0))
    whole = lambda a: pl.BlockSpec(a.shape, lambda b: (0,) * a.ndim)
    state_spec = pl.BlockSpec((n_seq, 1, GLA_HEADS, GLA_DK, GLA_DV), lambda b: (b, layer, 0, 0, 0))
    in_specs = [tok(GLA_K_WIDTH), tok(GLA_K_WIDTH), tok(GLA_V_WIDTH), tok(GLA_K_WIDTH), tok(GLA_K_WIDTH)]
    in_specs += [whole(a) for a in consts]
    args = [gq, gk, gv, lf, lb, *consts]
    out_shape = [jax.ShapeDtypeStruct((n_tok, GLA_V_WIDTH), F32)]
    out_specs = [tok(GLA_V_WIDTH)]
    aliases = {}
    if is_ctx:
        aliases = {len(args): 1, len(args) + 1: 2}
        in_specs += [pl.BlockSpec(memory_space=pl.ANY)] * 2
        args += list(new_states)
        out_shape += [jax.ShapeDtypeStruct(new_states[0].shape, F32)] * 2
        out_specs += [state_spec, state_spec]
    else:
        in_specs += [pl.BlockSpec((seq_len, GLA_K_WIDTH), lambda b: (0, 0))] * 2 + [state_spec, state_spec]
        args += list(rope) + list(state_in)
    return pl.pallas_call(
        functools.partial(_gla_kernel, seq_len, n_seq, is_ctx),
        grid=(n_tok // rows,),
        in_specs=in_specs,
        out_specs=out_specs,
        out_shape=out_shape,
        input_output_aliases=aliases,
        scratch_shapes=[
            pltpu.VMEM((rows, GLA_K_WIDTH), F32),
            pltpu.VMEM((rows, GLA_K_WIDTH), F32),
            pltpu.VMEM((rows, GLA_V_WIDTH), F32),
            pltpu.VMEM((rows, GLA_V_WIDTH), F32),
            pltpu.VMEM((n_seq, 2, GLA_HEADS // 2, 2 * GLA_DV, 2 * GLA_DK), F32),
        ],
        compiler_params=_params(1),
        name="gla_ctx" if is_ctx else "gla_smp",
    )(*args)


def _rope_tables():
    quarter = GLA_DK // 4
    inv = ROPE_BASE ** (-jnp.arange(quarter, dtype=F32) / quarter)
    t = jnp.arange(DEC_SEQ)
    ang_r = (t // GRID_W).astype(F32)[:, None] * inv
    ang_c = (t % GRID_W).astype(F32)[:, None] * inv
    cos = jnp.concatenate([jnp.cos(ang_r)] * 2 + [jnp.cos(ang_c)] * 2, axis=-1)
    sin = jnp.concatenate([-jnp.sin(ang_r), jnp.sin(ang_r), -jnp.sin(ang_c), jnp.sin(ang_c)], axis=-1)
    return jnp.tile(cos, (1, GLA_HEADS)), jnp.tile(sin, (1, GLA_HEADS))


def _merge_mlp_kernel(x_ref, mod_ref, ona_ref, ogla_ref, gate_ref, gout_ref, wo_ref, gmlp_ref,
                      wup_ref, wdown_ref, o_ref):
    mod = mod_ref[0]
    ga1 = mod[:, 2 * D_MODEL:3 * D_MODEL]
    sh2 = mod[:, 3 * D_MODEL:4 * D_MODEL]
    sc2 = mod[:, 4 * D_MODEL:5 * D_MODEL]
    ga2 = mod[:, 5 * D_MODEL:6 * D_MODEL]

    og = ogla_ref[...]
    normed = []
    for h in range(GLA_HEADS):
        oh = og[:, h * GLA_DV:(h + 1) * GLA_DV]
        normed.append(oh * lax.rsqrt(jnp.mean(oh * oh, axis=-1, keepdims=True) + EPS))
    gate = gate_ref[...]
    g = jnp.concatenate(normed, axis=1) * gout_ref[...] * (gate * (1.0 / (1.0 + jnp.exp(-gate))))
    att = _dot(ona_ref[...], wo_ref[0, 0:NA_WIDTH, :]) + _dot(g.astype(BF16), wo_ref[0, NA_WIDTH:, :])
    x = x_ref[...] + ga1 * att

    ms = jnp.mean(x * x, axis=-1, keepdims=True)
    h2 = ((x * lax.rsqrt(ms + EPS) * gmlp_ref[...]) * (1.0 + sc2) + sh2).astype(BF16)
    acc = jnp.zeros_like(x)
    for c in range(D_FF // FF_CHUNK):
        u = jnp.maximum(_dot(h2, wup_ref[0, :, c * FF_CHUNK:(c + 1) * FF_CHUNK]), 0.0)
        acc = acc + _dot((u * u).astype(BF16), wdown_ref[0, c * FF_CHUNK:(c + 1) * FF_CHUNK, :])
    o_ref[...] = x + ga2 * acc


def _merge_mlp_call(x, mod_l, o_na, o_gla, gate, gout_t, wo, g_mlp_l, wup, wdown, layer):
    tm = TOKEN_TILE
    n_tok = x.shape[0]
    is_ctx = n_tok == N_CTX
    row = lambda i: (i, 0)
    const = lambda i: (0, 0)
    of_layer = lambda i: (layer, 0, 0)
    resident = dict(pipeline_mode=pl.Buffered(1))
    return pl.pallas_call(
        _merge_mlp_kernel,
        grid=(n_tok // tm,),
        in_specs=[
            pl.BlockSpec((tm, D_MODEL), row),
            pl.BlockSpec((1, 1, N_MOD * D_MODEL), _mod_index_map(is_ctx)),
            pl.BlockSpec((tm, NA_WIDTH), row),
            pl.BlockSpec((tm, GLA_V_WIDTH), row),
            pl.BlockSpec((tm, GLA_V_WIDTH), row),
            pl.BlockSpec((1, GLA_V_WIDTH), const),
            pl.BlockSpec((1, D_MODEL, D_MODEL), of_layer, **resident),
            pl.BlockSpec((1, D_MODEL), const),
            pl.BlockSpec((1, D_MODEL, D_FF), of_layer, **resident),
            pl.BlockSpec((1, D_FF, D_MODEL), of_layer, **resident),
        ],
        out_specs=pl.BlockSpec((tm, D_MODEL), row),
        out_shape=jax.ShapeDtypeStruct((n_tok, D_MODEL), F32),
        compiler_params=_params(1),
        name="merge_mlp_ctx" if is_ctx else "merge_mlp_smp",
    )(x, mod_l, o_na, o_gla, gate, gout_t, wo, g_mlp_l, wup, wdown)


def kernel(x_prompt, x_sample, cache_k, cache_v, state_fwd, state_bwd, c, c_ctx, w_ada, b_ada, g_attn, w_in,
           g_q, g_k, rpb, w_gf, b_gf, w_gb, b_gb, g_gla_out, w_o, g_mlp, w_up, w_down):
    x_c = x_prompt.reshape(N_CTX, D_MODEL)
    x_s = x_sample.reshape(N_SMP, D_MODEL)

    c_rows = jnp.concatenate([c_ctx[None, :], c, jnp.zeros((MOD_ROWS - 1 - DEC_BATCH, D_MODEL), F32)], axis=0)
    mods = _ada_call(c_rows, w_ada, b_ada).reshape(DEPTH, MOD_ROWS, 1, N_MOD * D_MODEL)

    w_in_b = w_in.astype(BF16)
    wo_b = w_o.astype(BF16)
    wup_b = w_up.astype(BF16)
    wdown_b = w_down.astype(BF16)
    wgate = jnp.zeros((DEPTH, 2 * GLA_GATE_RANK, 2 * GLA_K_WIDTH), F32)
    wgate = wgate.at[:, :GLA_GATE_RANK, :GLA_K_WIDTH].set(w_gf)
    wgate = wgate.at[:, GLA_GATE_RANK:, GLA_K_WIDTH:].set(w_gb).astype(BF16)
    bgate = jnp.concatenate([b_gf, b_gb], axis=-1).reshape(DEPTH, 1, 2 * GLA_K_WIDTH)
    head_of = np.arange(NA_WIDTH) // NA_HEAD_DIM
    seg = jnp.asarray((head_of[:, None] == head_of[None, :]).astype(np.float32) / NA_HEAD_DIM, BF16)
    gq_t = jnp.tile(g_q, (1, NA_HEADS)).reshape(DEPTH, 1, NA_WIDTH)
    gk_t = jnp.tile(g_k, (1, NA_HEADS)).reshape(DEPTH, 1, NA_WIDTH)
    gout_t = jnp.tile(g_gla_out, (1, GLA_HEADS)).reshape(DEPTH, 1, GLA_V_WIDTH)
    n_dr, n_dc = rpb.shape[2:]
    rpb_rows = jnp.pad(rpb, ((0, 0), (0, 0), (0, -n_dr % 8), (0, 2 * GRID_W - n_dc)))
    gla_w_np, gla_masks_np, gla_smask_np = _gla_constants()
    gla_consts = (jnp.asarray(gla_w_np, BF16), jnp.asarray(gla_masks_np, F32), jnp.asarray(gla_smask_np, F32))
    rope = _rope_tables()

    new_kv = None
    new_sf = jnp.zeros((BATCH, DEPTH, GLA_HEADS, GLA_DK, GLA_DV), F32)
    new_sb = jnp.zeros((BATCH, DEPTH, GLA_HEADS, GLA_DK, GLA_DV), F32)

    for l in range(DEPTH):
        proj_w = (g_attn[l].reshape(1, D_MODEL), w_in_b, seg, gq_t[l], gk_t[l], wgate[l], bgate[l])
        mlp_w = (gout_t[l], wo_b, g_mlp[l].reshape(1, D_MODEL), wup_b, wdown_b)

        q, k, v, gq, gk, gv, gate, lf, lb, new_k, new_v = _proj_call(x_c, mods[l], *proj_w, new_kv=new_kv, layer=l)
        new_kv = (new_k, new_v)
        o_na = _ctx_attn_call(q, k, v)
        o_gla, new_sf, new_sb = _gla_call(gq, gk, gv, lf, lb, gla_consts, new_states=(new_sf, new_sb), layer=l)
        x_c = _merge_mlp_call(x_c, mods[l], o_na, o_gla, gate, *mlp_w, layer=l)

        q, k, v, gq, gk, gv, gate, lf, lb = _proj_call(x_s, mods[l], *proj_w, layer=l)
        o_na = _na_call(q, k, v, cache_k, cache_v, rpb_rows, l)
        (o_gla,) = _gla_call(gq, gk, gv, lf, lb, gla_consts, rope=rope,
                             state_in=(state_fwd, state_bwd), layer=l)
        x_s = _merge_mlp_call(x_s, mods[l], o_na, o_gla, gate, *mlp_w, layer=l)

    return (x_c.reshape(BATCH, SEQ, D_MODEL), x_s.reshape(DEC_BATCH, DEC_SEQ, D_MODEL),
            new_k, new_v, new_sf, new_sb)
```

```python
import functools

import numpy as np
import jax
import jax.numpy as jnp
from jax import lax
from jax.experimental import pallas as pl
from jax.experimental.pallas import tpu as pltpu

F32 = jnp.float32
BF16 = jnp.bfloat16

D_MODEL = 1024
BATCH = 16
SEQ = 256
DEPTH = 4
DEC_BATCH = 2
DEC_SEQ = 1024
PAST_LEN = 512
GRID_W = 64
GRID_ROWS = DEC_SEQ // GRID_W
NA_WIDTH = D_MODEL // 2
NA_HEAD_DIM = 64
NA_HEADS = NA_WIDTH // NA_HEAD_DIM
NA_WIN_ROWS = 8
NA_WIN_COLS = 16
GLA_V_WIDTH = D_MODEL - NA_WIDTH
GLA_HEADS = 4
GLA_DV = GLA_V_WIDTH // GLA_HEADS
GLA_DK = GLA_DV // 2
GLA_K_WIDTH = GLA_HEADS * GLA_DK
GLA_GATE_RANK = 16
GLA_GATE_TAU = 16.0
GLA_CHUNK = 64
D_FF = 4 * D_MODEL
ROPE_BASE = 10000.0
N_MOD = 6
EPS = 1e-6

N_CTX = BATCH * SEQ
N_SMP = DEC_BATCH * DEC_SEQ
MOD_ROWS = 8
MASK_VALUE = -1e30

TOKEN_TILE = 512
FF_CHUNK = 1024
ADA_TILE_N = 1536
GLA_LEVELS = 6
NA_ROW_GROUP = 4
GLA_SEQS_PER_STEP = 2
VMEM_LIMIT = 52 * 1024 * 1024


def _dot(a, b):
    return jnp.dot(a, b, preferred_element_type=F32)


def _dot_nt(a, b):
    return lax.dot_general(a, b, (((1,), (1,)), ((), ())), preferred_element_type=F32)


def _dot_tn(a, b):
    return lax.dot_general(a, b, (((0,), (0,)), ((), ())), preferred_element_type=F32)


def _params(n_grid_dims=1):
    return pltpu.CompilerParams(dimension_semantics=("arbitrary",) * n_grid_dims,
                                vmem_limit_bytes=VMEM_LIMIT)


def _mod_index_map(is_ctx):
    if is_ctx:
        return lambda i: (0, 0, 0)
    return lambda i: (1 + i // (DEC_SEQ // TOKEN_TILE), 0, 0)


def _ada_kernel(c_ref, w_ref, b_ref, o_ref):
    cv = c_ref[...]
    s = cv * (1.0 / (1.0 + jnp.exp(-cv)))
    o_ref[0] = _dot(s.astype(BF16), w_ref[0].astype(BF16)) + b_ref[0]


def _ada_call(c_rows, w_ada, b_ada):
    n_mod = N_MOD * D_MODEL
    return pl.pallas_call(
        _ada_kernel,
        grid=(DEPTH, n_mod // ADA_TILE_N),
        in_specs=[
            pl.BlockSpec((MOD_ROWS, D_MODEL), lambda l, j: (0, 0)),
            pl.BlockSpec((1, D_MODEL, ADA_TILE_N), lambda l, j: (l, 0, j)),
            pl.BlockSpec((1, 1, ADA_TILE_N), lambda l, j: (l, 0, j)),
        ],
        out_specs=pl.BlockSpec((1, MOD_ROWS, ADA_TILE_N), lambda l, j: (l, 0, j)),
        out_shape=jax.ShapeDtypeStruct((DEPTH, MOD_ROWS, n_mod), F32),
        compiler_params=_params(2),
        name="ada_mod",
    )(c_rows, w_ada, b_ada.reshape(DEPTH, 1, n_mod))


def _log_sigmoid(x):
    return jnp.minimum(x, 0.0) - jnp.log1p(jnp.exp(-jnp.abs(x)))


def _proj_kernel(is_ctx, x_ref, mod_ref, g_ref, win_ref, seg_ref, gq_ref, gk_ref, wgate_ref, bgate_ref, *rest):
    if is_ctx:
        rest = rest[2:]
    q_out, k_out, v_out, gq_out, gk_out, gv_out, gate_out, lf_out, lb_out = rest[:9]
    qkv_n = 3 * NA_WIDTH
    g_n = 2 * GLA_K_WIDTH + 2 * GLA_V_WIDTH

    x = x_ref[...]
    mod = mod_ref[0]
    sh1 = mod[:, 0:D_MODEL]
    sc1 = mod[:, D_MODEL:2 * D_MODEL]
    ms = jnp.mean(x * x, axis=-1, keepdims=True)
    h = (x * lax.rsqrt(ms + EPS) * g_ref[...]) * (1.0 + sc1) + sh1
    hb = h.astype(BF16)

    z = _dot(hb, win_ref[0, :, 0:qkv_n])
    q = z[:, 0:NA_WIDTH]
    k = z[:, NA_WIDTH:2 * NA_WIDTH]
    v = z[:, 2 * NA_WIDTH:3 * NA_WIDTH]
    seg = seg_ref[...]
    q_ms = _dot((q * q).astype(BF16), seg)
    k_ms = _dot((k * k).astype(BF16), seg)
    q_out[...] = (q * lax.rsqrt(q_ms + EPS) * gq_ref[...] * (NA_HEAD_DIM ** -0.5)).astype(BF16)
    kn = k * lax.rsqrt(k_ms + EPS) * gk_ref[...]
    k_out[...] = kn.astype(BF16)
    v_out[...] = v.astype(BF16)
    if is_ctx:
        knew_out, vnew_out = rest[9:]
        for s in range(TOKEN_TILE // SEQ):
            for hd in range(NA_HEADS):
                rows = slice(s * SEQ, (s + 1) * SEQ)
                cols = slice(hd * NA_HEAD_DIM, (hd + 1) * NA_HEAD_DIM)
                knew_out[s, 0, hd] = kn[rows, cols]
                vnew_out[s, 0, hd] = v[rows, cols]

    z = _dot(hb, win_ref[0, :, qkv_n:qkv_n + g_n])
    gq_out[...] = z[:, 0:GLA_K_WIDTH] * (GLA_DK ** -0.5)
    gk_out[...] = z[:, GLA_K_WIDTH:2 * GLA_K_WIDTH]
    gv_out[...] = z[:, 2 * GLA_K_WIDTH:2 * GLA_K_WIDTH + GLA_V_WIDTH].astype(BF16)
    gate_out[...] = z[:, 2 * GLA_K_WIDTH + GLA_V_WIDTH:]

    zz = _dot(hb, win_ref[0, :, qkv_n + g_n:])
    pre = _dot(zz.astype(BF16), wgate_ref[...]) + bgate_ref[...]
    ls = _log_sigmoid(pre) * (1.0 / GLA_GATE_TAU)
    lf_out[...] = ls[:, 0:GLA_K_WIDTH]
    lb_out[...] = ls[:, GLA_K_WIDTH:]


def _proj_call(x, mod_l, g_attn_l, w_in_b, seg, gq_t, gk_t, wgate, bgate, new_kv=None, layer=0):
    tm = TOKEN_TILE
    n_tok = x.shape[0]
    is_ctx = new_kv is not None
    row = lambda i: (i, 0)
    const = lambda i: (0, 0)
    out_shapes = [
        jax.ShapeDtypeStruct((n_tok, NA_WIDTH), BF16),
        jax.ShapeDtypeStruct((n_tok, NA_WIDTH), BF16),
        jax.ShapeDtypeStruct((n_tok, NA_WIDTH), BF16),
        jax.ShapeDtypeStruct((n_tok, GLA_K_WIDTH), F32),
        jax.ShapeDtypeStruct((n_tok, GLA_K_WIDTH), F32),
        jax.ShapeDtypeStruct((n_tok, GLA_V_WIDTH), BF16),
        jax.ShapeDtypeStruct((n_tok, GLA_V_WIDTH), F32),
        jax.ShapeDtypeStruct((n_tok, GLA_K_WIDTH), F32),
        jax.ShapeDtypeStruct((n_tok, GLA_K_WIDTH), F32),
    ]
    out_specs = [pl.BlockSpec((tm, s.shape[1]), row) for s in out_shapes]
    in_specs = [
        pl.BlockSpec((tm, D_MODEL), row),
        pl.BlockSpec((1, 1, N_MOD * D_MODEL), _mod_index_map(is_ctx)),
        pl.BlockSpec((1, D_MODEL), const),
        pl.BlockSpec((1,) + w_in_b.shape[1:], lambda i: (layer, 0, 0)),
        pl.BlockSpec((NA_WIDTH, NA_WIDTH), const),
        pl.BlockSpec((1, NA_WIDTH), const),
        pl.BlockSpec((1, NA_WIDTH), const),
        pl.BlockSpec(wgate.shape, const),
        pl.BlockSpec((1, 2 * GLA_K_WIDTH), const),
    ]
    args = [x, mod_l, g_attn_l, w_in_b, seg, gq_t, gk_t, wgate, bgate]
    aliases = {}
    if is_ctx:
        kv_spec = pl.BlockSpec((tm // SEQ, 1, NA_HEADS, SEQ, NA_HEAD_DIM), lambda i: (i, layer, 0, 0, 0))
        aliases = {len(args): len(out_shapes), len(args) + 1: len(out_shapes) + 1}
        in_specs += [pl.BlockSpec(memory_space=pl.ANY)] * 2
        args += list(new_kv)
        out_shapes += [jax.ShapeDtypeStruct(new_kv[0].shape, F32)] * 2
        out_specs += [kv_spec, kv_spec]
    return pl.pallas_call(
        functools.partial(_proj_kernel, is_ctx),
        grid=(n_tok // tm,),
        in_specs=in_specs,
        out_specs=out_specs,
        out_shape=out_shapes,
        input_output_aliases=aliases,
        compiler_params=_params(1),
        name="proj_ctx" if is_ctx else "proj_smp",
    )(*args)


def _softmax_pv(s, v):
    m = jnp.max(s, axis=-1, keepdims=True)
    p = jnp.exp(s - m)
    l = jnp.sum(p, axis=-1, keepdims=True)
    return _dot(p.astype(BF16), v) / l


def _ctx_attn_kernel(q_ref, k_ref, v_ref, o_ref):
    lane = lax.broadcasted_iota(jnp.int32, (1, 2 * NA_HEAD_DIM), 1)
    first = lane < NA_HEAD_DIM
    for hp in range(NA_HEADS // 2):
        cols = slice(hp * 2 * NA_HEAD_DIM, (hp + 1) * 2 * NA_HEAD_DIM)
        q2 = q_ref[:, cols]
        k2 = k_ref[:, cols]
        v2 = v_ref[:, cols]
        o0 = _softmax_pv(_dot_nt(jnp.where(first, q2, jnp.zeros_like(q2)), k2), v2)
        o1 = _softmax_pv(_dot_nt(jnp.where(first, jnp.zeros_like(q2), q2), k2), v2)
        o_ref[:, cols] = jnp.where(first, o0, o1).astype(BF16)


def _ctx_attn_call(q, k, v):
    spec = pl.BlockSpec((SEQ, NA_WIDTH), lambda b: (b, 0))
    return pl.pallas_call(
        _ctx_attn_kernel,
        grid=(BATCH,),
        in_specs=[spec, spec, spec],
        out_specs=spec,
        out_shape=jax.ShapeDtypeStruct((N_CTX, NA_WIDTH), BF16),
        compiler_params=_params(1),
        name="ctx_attn",
    )(q, k, v)


def _na_window_start(r):
    return min(max(r - NA_WIN_ROWS // 2, 0), GRID_ROWS - NA_WIN_ROWS)


def _na_row_groups():
    groups = []
    for g in range(GRID_ROWS // NA_ROW_GROUP):
        starts = [_na_window_start(r) for r in range(g * NA_ROW_GROUP, (g + 1) * NA_ROW_GROUP)]
        lo = min(starts) // 2 * 2
        count = max(starts) + NA_WIN_ROWS - lo
        groups.append((lo, count + count % 2))
    return groups


def _na_bias_tiles(rpb_ref, hh, tiles_scr):
    n_dr = 2 * NA_WIN_ROWS - 1
    lanes = 2 * GRID_W
    lane = lax.broadcasted_iota(jnp.int32, (GRID_W, lanes), 1)
    col = lax.broadcasted_iota(jnp.int32, (GRID_W, lanes), 0)
    first = lane < GRID_W
    c2 = lane % GRID_W
    cs = jnp.clip(col - NA_WIN_COLS // 2, 0, GRID_W - NA_WIN_COLS)
    valid = (c2 >= cs) & (c2 < cs + NA_WIN_COLS)
    rows = rpb_ref[0, hh]
    base = lanes - (NA_WIN_COLS - 1)

    def skew(dr, shift):
        row = jnp.broadcast_to(rows[dr:dr + 1], (GRID_W, lanes))
        return pltpu.roll(row, shift, axis=1, stride=1, stride_axis=0)

    for e in range(n_dr - 1):
        tile = jnp.where(first, skew(e, base), skew(e + 1, (base + GRID_W) % lanes))
        tiles_scr[e] = jnp.where(valid, tile, MASK_VALUE)


def _na_group_bias(tiles_scr, g, key_lo, key_count, first):
    masked = jnp.full((GRID_W, 2 * GRID_W), MASK_VALUE, F32)
    row_tiles = []
    for r in range(g * NA_ROW_GROUP, (g + 1) * NA_ROW_GROUP):
        rs = _na_window_start(r)
        pieces = []
        for u in range(key_count // 2):
            r0 = key_lo + 2 * u
            in0 = rs <= r0 < rs + NA_WIN_ROWS
            in1 = rs <= r0 + 1 < rs + NA_WIN_ROWS
            if not (in0 or in1):
                pieces.append(masked)
                continue
            tile = tiles_scr[r0 - r + NA_WIN_ROWS - 1]
            if not in1:
                tile = jnp.where(first, tile, MASK_VALUE)
            elif not in0:
                tile = jnp.where(first, MASK_VALUE, tile)
            pieces.append(tile)
        row_tiles.append(jnp.concatenate(pieces, axis=1))
    return jnp.concatenate(row_tiles, axis=0)


def _na_kernel(q_ref, k_ref, v_ref, kc_ref, vc_ref, rpb_ref, o_ref, sc_ref, tiles_scr):
    lane = lax.broadcasted_iota(jnp.int32, (1, 2 * NA_HEAD_DIM), 1)
    first = lane < NA_HEAD_DIM
    q2 = q_ref[...]
    kc2 = jnp.concatenate([kc_ref[0, 0, 0], kc_ref[0, 0, 1]], axis=0).astype(BF16)
    vc2 = jnp.concatenate([vc_ref[0, 0, 0], vc_ref[0, 0, 1]], axis=0).astype(BF16)
    block = NA_ROW_GROUP * GRID_W
    outs = []
    for hh in range(2):
        qm = jnp.where(first, q2, jnp.zeros_like(q2)) if hh == 0 else jnp.where(first, jnp.zeros_like(q2), q2)
        sc_ref[...] = _dot(qm, kc2)
        _na_bias_tiles(rpb_ref, hh, tiles_scr)
        blocks = []
        for g, (key_lo, key_count) in enumerate(_na_row_groups()):
            keys = slice(key_lo * GRID_W, (key_lo + key_count) * GRID_W)
            s = _dot_nt(qm[g * block:(g + 1) * block], k_ref[keys, :])
            s = s + _na_group_bias(tiles_scr, g, key_lo, key_count, first)
            sc = sc_ref[g * block:(g + 1) * block, :]
            m = jnp.maximum(jnp.max(s, axis=-1, keepdims=True), jnp.max(sc, axis=-1, keepdims=True))
            p = jnp.exp(s - m)
            pc = jnp.exp(sc - m)
            l = jnp.sum(p, axis=-1, keepdims=True) + jnp.sum(pc, axis=-1, keepdims=True)
            blocks.append((_dot(p.astype(BF16), v_ref[keys, :]) + _dot_nt(pc.astype(BF16), vc2)) / l)
        outs.append(jnp.concatenate(blocks, axis=0))
    o_ref[...] = jnp.where(first, outs[0], outs[1]).astype(BF16)


def _na_call(q, k, v, cache_kt, cache_vt, rpb_rows, layer):
    tok = pl.BlockSpec((DEC_SEQ, 2 * NA_HEAD_DIM), lambda b, hp: (b, hp))
    cache = pl.BlockSpec((1, 1, 2, NA_HEAD_DIM, PAST_LEN), lambda b, hp: (b, layer, hp, 0, 0))
    n_pairs = 2 * NA_WIN_ROWS - 2
    return pl.pallas_call(
        _na_kernel,
        grid=(DEC_BATCH, NA_HEADS // 2),
        in_specs=[
            tok, tok, tok, cache, cache,
            pl.BlockSpec((1, 2) + rpb_rows.shape[2:], lambda b, hp: (layer, hp, 0, 0)),
        ],
        out_specs=tok,
        out_shape=jax.ShapeDtypeStruct((N_SMP, NA_WIDTH), BF16),
        scratch_shapes=[pltpu.VMEM((DEC_SEQ, PAST_LEN), F32),
                        pltpu.VMEM((n_pairs, GRID_W, 2 * GRID_W), F32)],
        compiler_params=_params(2),
        name="nbr_attn",
    )(q, k, v, cache_kt, cache_vt, rpb_rows)


def _gla_constants():
    C = GLA_CHUNK
    ii, jj = np.meshgrid(np.arange(C), np.arange(C), indexing="ij")
    tri = np.stack([jj <= ii, jj >= ii]).astype(np.float32)
    w = np.concatenate([tri, tri], axis=-1)

    x = ii ^ jj
    hb = np.where(x > 0, np.floor(np.log2(np.maximum(x, 1))), -1).astype(np.int64)
    masks = np.zeros((2, GLA_LEVELS + 1, C, C), np.float32)
    for p in range(GLA_LEVELS):
        masks[0, p] = (jj < ii) & (hb == p)
        masks[1, p] = (jj > ii) & (hb == p)
    masks[:, GLA_LEVELS] = np.eye(C)
    masks = np.tile(masks, (1, 1, 2, 2))
    state_mask = (np.arange(2 * GLA_DV)[:, None] // GLA_DV == np.arange(2 * GLA_DK)[None, :] // GLA_DK)
    return w, masks, state_mask.astype(np.float32)


def _gla_kernel(seq_len, n_seq, is_ctx, *refs):
    refs = list(refs)
    gq_ref, gk_ref, gv_ref, lf_ref, lb_ref, w_ref, mask_ref, smask_ref = refs[:8]
    refs = refs[8:]
    if is_ctx:
        refs = refs[2:]
        o_ref, sf_ref, sb_ref = refs[:3]
        refs = refs[3:]
    else:
        cos_ref, sin_ref, s0f_ref, s0b_ref, o_ref = refs[:5]
        refs = refs[5:]
    q_scr, k_scr, of_scr, ob_scr, st_scr, cum_scr = refs

    C = GLA_CHUNK
    n_chunks = seq_len // C
    n_pairs = GLA_HEADS // 2
    pair_k = 2 * GLA_DK
    pair_v = 2 * GLA_DV

    q = gq_ref[...]
    k = gk_ref[...]
    if not is_ctx:
        lane = lax.broadcasted_iota(jnp.int32, (1, GLA_K_WIDTH), 1)
        quarter = GLA_DK // 4
        low = (lane % (2 * quarter)) < quarter
        cos = jnp.concatenate([cos_ref[...]] * n_seq, axis=0)
        sin = jnp.concatenate([sin_ref[...]] * n_seq, axis=0)

        def rope(t):
            partner = jnp.where(low, pltpu.roll(t, GLA_K_WIDTH - quarter, axis=1), pltpu.roll(t, quarter, axis=1))
            return t * cos + partner * sin

        q = rope(q)
        k = rope(k)
    q_scr[...] = q
    k_scr[...] = k

    st_scr[...] = jnp.zeros(st_scr.shape, F32)
    if not is_ctx:
        for d, s0_ref in enumerate((s0f_ref, s0b_ref)):
            for s in range(n_seq):
                for h in range(GLA_HEADS):
                    pr, e = divmod(h, 2)
                    st_scr[s, d, pr, e * GLA_DV:(e + 1) * GLA_DV, e * GLA_DK:(e + 1) * GLA_DK] = s0_ref[s, 0, h].T

    klane = lax.broadcasted_iota(jnp.int32, (1, GLA_K_WIDTH), 1)
    even_head = (klane // GLA_DK) % 2 == 0

    def chunk_decay(s, d, r0, f_ref):
        rows = pl.ds(s * seq_len + r0, C)
        f = f_ref[rows, :]
        f_hi = f.astype(BF16)
        f_lo = (f - f_hi.astype(F32)).astype(BF16)
        cum = _dot(w_ref[d], jnp.concatenate([f_hi, f_lo], axis=0))
        cum_scr[s, d] = cum
        return rows, cum

    def chunk_scores(s, d, rows, cum):
        def cum_rows(r, n):
            return jnp.broadcast_to(cum_scr[s, d, r:r + 1, :], (n, GLA_K_WIDTH))

        def level_factor(p):
            m = 1 << p
            edge = m - 1 if d == 0 else m
            if 2 * m >= 8:
                ref = jnp.concatenate([cum_rows(blk + edge, 2 * m) for blk in range(0, C, 2 * m)], axis=0)
            elif p == 1:
                sub = lax.broadcasted_iota(jnp.int32, (8, GLA_K_WIDTH), 0)
                ref = jnp.concatenate([jnp.where(sub < 4, cum_rows(blk + edge, 8), cum_rows(blk + 4 + edge, 8))
                                       for blk in range(0, C, 8)], axis=0)
            else:
                odd = lax.broadcasted_iota(jnp.int32, (C, GLA_K_WIDTH), 0) % 2 == 1
                if d == 0:
                    ref = jnp.where(odd, pltpu.roll(cum, 1, axis=0), cum)
                else:
                    ref = jnp.where(odd, cum, pltpu.roll(cum, C - 1, axis=0))
            return jnp.exp(-jnp.abs(cum - ref))

        qc = q_scr[rows, :]
        kc = k_scr[rows, :]
        vc = gv_ref[rows, :]
        last = C - 1 if d == 0 else 0
        eq = jnp.exp(cum)
        ek = jnp.exp(cum_rows(last, C) - cum)
        total = jnp.exp(cum_scr[s, d, last:last + 1, :])

        a = None
        for p in range(GLA_LEVELS + 1):
            if p < GLA_LEVELS:
                ep = level_factor(p)
                qe = (qc * ep).astype(BF16)
                ke = (kc * ep).astype(BF16)
            else:
                qe = qc.astype(BF16)
                ke = kc.astype(BF16)
            zero = jnp.zeros((C, pair_k), BF16)
            lhs = jnp.concatenate([jnp.concatenate([qe[:, :pair_k], zero], axis=1),
                                   jnp.concatenate([zero, qe[:, pair_k:]], axis=1)], axis=0)
            rhs = jnp.concatenate([jnp.where(even_head, ke, jnp.zeros_like(ke)),
                                   jnp.where(even_head, jnp.zeros_like(ke), ke)], axis=0)
            sc = _dot_nt(lhs, rhs) * mask_ref[d, p]
            a = sc if a is None else a + sc
        q_in = (qc * eq).astype(BF16)
        k_out = (kc * ek).astype(BF16)
        return a.astype(BF16), q_in, k_out, vc, total

    def chunk_output(s, d, rows, o_scr, a, q_in, k_out, vc, total):
        zero_v = jnp.zeros((C, GLA_DV), BF16)
        for pr in range(n_pairs):
            kl = slice(pr * pair_k, (pr + 1) * pair_k)
            vp = vc[:, pr * pair_v:(pr + 1) * pair_v]
            v_diag = jnp.concatenate([jnp.concatenate([vp[:, :GLA_DV], zero_v], axis=1),
                                      jnp.concatenate([zero_v, vp[:, GLA_DV:]], axis=1)], axis=0)
            st = st_scr[s, d, pr]
            o_scr[rows, pr * pair_v:(pr + 1) * pair_v] = (
                _dot(a[pr * C:(pr + 1) * C], v_diag) + _dot_nt(q_in[:, kl], st.astype(BF16)))
            st_scr[s, d, pr] = st * total[:, kl] + _dot_tn(vp, k_out[:, kl]) * smask_ref[...]

    def body(n, carry):
        fwd = pl.multiple_of(n * C, C)
        bwd = pl.multiple_of((n_chunks - 1 - n) * C, C)
        scans = [(s, d) for s in range(n_seq) for d in range(2)]
        decays = [chunk_decay(s, d, bwd if d else fwd, lb_ref if d else lf_ref) for s, d in scans]
        scores = [chunk_scores(s, d, *dec) for (s, d), dec in zip(scans, decays)]
        for (s, d), (rows, _), sc in zip(scans, decays, scores):
            chunk_output(s, d, rows, ob_scr if d else of_scr, *sc)
        return carry

    lax.fori_loop(0, n_chunks, body, 0)
    o_ref[...] = of_scr[...] + ob_scr[...]
    if is_ctx:
        for d, s_ref in enumerate((sf_ref, sb_ref)):
            for s in range(n_seq):
                for h in range(GLA_HEADS):
                    pr, e = divmod(h, 2)
                    s_ref[s, 0, h] = st_scr[s, d, pr, e * GLA_DV:(e + 1) * GLA_DV, e * GLA_DK:(e + 1) * GLA_DK].T


def _gla_call(gq, gk, gv, lf, lb, consts, new_states=None, rope=None, state_in=None, layer=0):
    is_ctx = new_states is not None
    seq_len = SEQ if is_ctx else DEC_SEQ
    n_tok = gq.shape[0]
    n_seq = GLA_SEQS_PER_STEP
    rows = n_seq * seq_len
    tok = lambda width: pl.BlockSpec((rows, width), lambda b: (b, 0))
    whole = lambda a: pl.BlockSpec(a.shape, lambda b: (0,) * a.ndim)
    state_spec = pl.BlockSpec((n_seq, 1, GLA_HEADS, GLA_DK, GLA_DV), lambda b: (b, layer, 0, 0, 0))
    in_specs = [tok(GLA_K_WIDTH), tok(GLA_K_WIDTH), tok(GLA_V_WIDTH), tok(GLA_K_WIDTH), tok(GLA_K_WIDTH)]
    in_specs += [whole(a) for a in consts]
    args = [gq, gk, gv, lf, lb, *consts]
    out_shape = [jax.ShapeDtypeStruct((n_tok, GLA_V_WIDTH), F32)]
    out_specs = [tok(GLA_V_WIDTH)]
    aliases = {}
    if is_ctx:
        aliases = {len(args): 1, len(args) + 1: 2}
        in_specs += [pl.BlockSpec(memory_space=pl.ANY)] * 2
        args += list(new_states)
        out_shape += [jax.ShapeDtypeStruct(new_states[0].shape, F32)] * 2
        out_specs += [state_spec, state_spec]
    else:
        in_specs += [pl.BlockSpec((seq_len, GLA_K_WIDTH), lambda b: (0, 0))] * 2 + [state_spec, state_spec]
        args += list(rope) + list(state_in)
    return pl.pallas_call(
        functools.partial(_gla_kernel, seq_len, n_seq, is_ctx),
        grid=(n_tok // rows,),
        in_specs=in_specs,
        out_specs=out_specs,
        out_shape=out_shape,
        input_output_aliases=aliases,
        scratch_shapes=[
            pltpu.VMEM((rows, GLA_K_WIDTH), F32),
            pltpu.VMEM((rows, GLA_K_WIDTH), F32),
            pltpu.VMEM((rows, GLA_V_WIDTH), F32),
            pltpu.VMEM((rows, GLA_V_WIDTH), F32),
            pltpu.VMEM((n_seq, 2, GLA_HEADS // 2, 2 * GLA_DV, 2 * GLA_DK), F32),
            pltpu.VMEM((n_seq, 2, GLA_CHUNK, GLA_K_WIDTH), F32),
        ],
        compiler_params=_params(1),
        name="gla_ctx" if is_ctx else "gla_smp",
    )(*args)


def _rope_tables():
    quarter = GLA_DK // 4
    inv = ROPE_BASE ** (-jnp.arange(quarter, dtype=F32) / quarter)
    t = jnp.arange(DEC_SEQ)
    ang_r = (t // GRID_W).astype(F32)[:, None] * inv
    ang_c = (t % GRID_W).astype(F32)[:, None] * inv
    cos = jnp.concatenate([jnp.cos(ang_r)] * 2 + [jnp.cos(ang_c)] * 2, axis=-1)
    sin = jnp.concatenate([-jnp.sin(ang_r), jnp.sin(ang_r), -jnp.sin(ang_c), jnp.sin(ang_c)], axis=-1)
    return jnp.tile(cos, (1, GLA_HEADS)), jnp.tile(sin, (1, GLA_HEADS))


def _merge_mlp_kernel(x_ref, mod_ref, ona_ref, ogla_ref, gate_ref, gout_ref, wo_ref, gmlp_ref,
                      wup_ref, wdown_ref, o_ref):
    mod = mod_ref[0]
    ga1 = mod[:, 2 * D_MODEL:3 * D_MODEL]
    sh2 = mod[:, 3 * D_MODEL:4 * D_MODEL]
    sc2 = mod[:, 4 * D_MODEL:5 * D_MODEL]
    ga2 = mod[:, 5 * D_MODEL:6 * D_MODEL]

    og = ogla_ref[...]
    normed = []
    for h in range(GLA_HEADS):
        oh = og[:, h * GLA_DV:(h + 1) * GLA_DV]
        normed.append(oh * lax.rsqrt(jnp.mean(oh * oh, axis=-1, keepdims=True) + EPS))
    gate = gate_ref[...]
    g = jnp.concatenate(normed, axis=1) * gout_ref[...] * (gate * (1.0 / (1.0 + jnp.exp(-gate))))
    att = _dot(ona_ref[...], wo_ref[0, 0:NA_WIDTH, :]) + _dot(g.astype(BF16), wo_ref[0, NA_WIDTH:, :])
    x = x_ref[...] + ga1 * att

    ms = jnp.mean(x * x, axis=-1, keepdims=True)
    h2 = ((x * lax.rsqrt(ms + EPS) * gmlp_ref[...]) * (1.0 + sc2) + sh2).astype(BF16)
    acc = jnp.zeros_like(x)
    for c in range(D_FF // FF_CHUNK):
        u = jnp.maximum(_dot(h2, wup_ref[0, :, c * FF_CHUNK:(c + 1) * FF_CHUNK]), 0.0)
        acc = acc + _dot((u * u).astype(BF16), wdown_ref[0, c * FF_CHUNK:(c + 1) * FF_CHUNK, :])
    o_ref[...] = x + ga2 * acc


def _merge_mlp_call(x, mod_l, o_na, o_gla, gate, gout_t, wo, g_mlp_l, wup, wdown, layer):
    tm = TOKEN_TILE
    n_tok = x.shape[0]
    is_ctx = n_tok == N_CTX
    row = lambda i: (i, 0)
    const = lambda i: (0, 0)
    of_layer = lambda i: (layer, 0, 0)
    resident = dict(pipeline_mode=pl.Buffered(1))
    return pl.pallas_call(
        _merge_mlp_kernel,
        grid=(n_tok // tm,),
        in_specs=[
            pl.BlockSpec((tm, D_MODEL), row),
            pl.BlockSpec((1, 1, N_MOD * D_MODEL), _mod_index_map(is_ctx)),
            pl.BlockSpec((tm, NA_WIDTH), row),
            pl.BlockSpec((tm, GLA_V_WIDTH), row),
            pl.BlockSpec((tm, GLA_V_WIDTH), row),
            pl.BlockSpec((1, GLA_V_WIDTH), const),
            pl.BlockSpec((1, D_MODEL, D_MODEL), of_layer, **resident),
            pl.BlockSpec((1, D_MODEL), const),
            pl.BlockSpec((1, D_MODEL, D_FF), of_layer, **resident),
            pl.BlockSpec((1, D_FF, D_MODEL), of_layer, **resident),
        ],
        out_specs=pl.BlockSpec((tm, D_MODEL), row),
        out_shape=jax.ShapeDtypeStruct((n_tok, D_MODEL), F32),
        compiler_params=_params(1),
        name="merge_mlp_ctx" if is_ctx else "merge_mlp_smp",
    )(x, mod_l, o_na, o_gla, gate, gout_t, wo, g_mlp_l, wup, wdown)


def kernel(x_prompt, x_sample, cache_k, cache_v, state_fwd, state_bwd, c, c_ctx, w_ada, b_ada, g_attn, w_in,
           g_q, g_k, rpb, w_gf, b_gf, w_gb, b_gb, g_gla_out, w_o, g_mlp, w_up, w_down):
    x_c = x_prompt.reshape(N_CTX, D_MODEL)
    x_s = x_sample.reshape(N_SMP, D_MODEL)

    c_rows = jnp.concatenate([c_ctx[None, :], c, jnp.zeros((MOD_ROWS - 1 - DEC_BATCH, D_MODEL), F32)], axis=0)
    mods = _ada_call(c_rows, w_ada, b_ada).reshape(DEPTH, MOD_ROWS, 1, N_MOD * D_MODEL)

    w_in_b = w_in.astype(BF16)
    wo_b = w_o.astype(BF16)
    wup_b = w_up.astype(BF16)
    wdown_b = w_down.astype(BF16)
    wgate = jnp.zeros((DEPTH, 2 * GLA_GATE_RANK, 2 * GLA_K_WIDTH), F32)
    wgate = wgate.at[:, :GLA_GATE_RANK, :GLA_K_WIDTH].set(w_gf)
    wgate = wgate.at[:, GLA_GATE_RANK:, GLA_K_WIDTH:].set(w_gb).astype(BF16)
    bgate = jnp.concatenate([b_gf, b_gb], axis=-1).reshape(DEPTH, 1, 2 * GLA_K_WIDTH)
    head_of = np.arange(NA_WIDTH) // NA_HEAD_DIM
    seg = jnp.asarray((head_of[:, None] == head_of[None, :]).astype(np.float32) / NA_HEAD_DIM, BF16)
    gq_t = jnp.tile(g_q, (1, NA_HEADS)).reshape(DEPTH, 1, NA_WIDTH)
    gk_t = jnp.tile(g_k, (1, NA_HEADS)).reshape(DEPTH, 1, NA_WIDTH)
    gout_t = jnp.tile(g_gla_out, (1, GLA_HEADS)).reshape(DEPTH, 1, GLA_V_WIDTH)
    n_dr, n_dc = rpb.shape[2:]
    rpb_rows = jnp.pad(rpb, ((0, 0), (0, 0), (0, -n_dr % 8), (0, 2 * GRID_W - n_dc)))
    gla_w_np, gla_masks_np, gla_smask_np = _gla_constants()
    gla_consts = (jnp.asarray(gla_w_np, BF16), jnp.asarray(gla_masks_np, F32), jnp.asarray(gla_smask_np, F32))
    rope = _rope_tables()

    cache_kt = jnp.swapaxes(cache_k, 3, 4)
    cache_vt = jnp.swapaxes(cache_v, 3, 4)
    new_k = jnp.zeros((BATCH, DEPTH, NA_HEADS, SEQ, NA_HEAD_DIM), F32)
    new_v = jnp.zeros((BATCH, DEPTH, NA_HEADS, SEQ, NA_HEAD_DIM), F32)
    new_sf = jnp.zeros((BATCH, DEPTH, GLA_HEADS, GLA_DK, GLA_DV), F32)
    new_sb = jnp.zeros((BATCH, DEPTH, GLA_HEADS, GLA_DK, GLA_DV), F32)

    for l in range(DEPTH):
        proj_w = (g_attn[l].reshape(1, D_MODEL), w_in_b, seg, gq_t[l], gk_t[l], wgate[l], bgate[l])
        mlp_w = (gout_t[l], wo_b, g_mlp[l].reshape(1, D_MODEL), wup_b, wdown_b)

        q, k, v, gq, gk, gv, gate, lf, lb, new_k, new_v = _proj_call(
            x_c, mods[l], *proj_w, new_kv=(new_k, new_v), layer=l)
        o_na = _ctx_attn_call(q, k, v)
        o_gla, new_sf, new_sb = _gla_call(gq, gk, gv, lf, lb, gla_consts, new_states=(new_sf, new_sb), layer=l)
        x_c = _merge_mlp_call(x_c, mods[l], o_na, o_gla, gate, *mlp_w, layer=l)

        q, k, v, gq, gk, gv, gate, lf, lb = _proj_call(x_s, mods[l], *proj_w, layer=l)
        o_na = _na_call(q, k, v, cache_kt, cache_vt, rpb_rows, l)
        (o_gla,) = _gla_call(gq, gk, gv, lf, lb, gla_consts, rope=rope,
                             state_in=(state_fwd, state_bwd), layer=l)
        x_s = _merge_mlp_call(x_s, mods[l], o_na, o_gla, gate, *mlp_w, layer=l)

    return (x_c.reshape(BATCH, SEQ, D_MODEL), x_s.reshape(DEC_BATCH, DEC_SEQ, D_MODEL),
            new_k, new_v, new_sf, new_sb)
```

```python
import functools

import numpy as np
import jax
import jax.numpy as jnp
from jax import lax
from jax.experimental import pallas as pl
from jax.experimental.pallas import tpu as pltpu

F32 = jnp.float32
BF16 = jnp.bfloat16

D_MODEL = 1024
BATCH = 16
SEQ = 256
DEPTH = 4
DEC_BATCH = 2
DEC_SEQ = 1024
PAST_LEN = 512
GRID_W = 64
GRID_ROWS = DEC_SEQ // GRID_W
NA_WIDTH = D_MODEL // 2
NA_HEAD_DIM = 64
NA_HEADS = NA_WIDTH // NA_HEAD_DIM
NA_WIN_ROWS = 8
NA_WIN_COLS = 16
GLA_V_WIDTH = D_MODEL - NA_WIDTH
GLA_HEADS = 4
GLA_DV = GLA_V_WIDTH // GLA_HEADS
GLA_DK = GLA_DV // 2
GLA_K_WIDTH = GLA_HEADS * GLA_DK
GLA_GATE_RANK = 16
GLA_GATE_TAU = 16.0
GLA_CHUNK = 64
D_FF = 4 * D_MODEL
ROPE_BASE = 10000.0
N_MOD = 6
EPS = 1e-6

N_CTX = BATCH * SEQ
N_SMP = DEC_BATCH * DEC_SEQ
MOD_ROWS = 8
MASK_VALUE = -1e30

TOKEN_TILE = 512
PROJ_SUB_TILES = 2
FF_CHUNK = 1024
ADA_TILE_N = 1536
GLA_LEVELS = 6
NA_ROW_GROUP = 4
GLA_SEQS_PER_STEP = 2
VMEM_LIMIT = 52 * 1024 * 1024


def _dot(a, b):
    return jnp.dot(a, b, preferred_element_type=F32)


def _dot_nt(a, b):
    return lax.dot_general(a, b, (((1,), (1,)), ((), ())), preferred_element_type=F32)


def _dot_tn(a, b):
    return lax.dot_general(a, b, (((0,), (0,)), ((), ())), preferred_element_type=F32)


def _params(n_grid_dims=1):
    return pltpu.CompilerParams(dimension_semantics=("arbitrary",) * n_grid_dims,
                                vmem_limit_bytes=VMEM_LIMIT)


def _mod_index_map(is_ctx):
    if is_ctx:
        return lambda i: (0, 0, 0)
    return lambda i: (1 + i // (DEC_SEQ // TOKEN_TILE), 0, 0)


def _ada_kernel(c_ref, w_ref, b_ref, o_ref):
    cv = c_ref[...]
    s = cv * (1.0 / (1.0 + jnp.exp(-cv)))
    o_ref[0] = _dot(s.astype(BF16), w_ref[0].astype(BF16)) + b_ref[0]


def _ada_call(c_rows, w_ada, b_ada):
    n_mod = N_MOD * D_MODEL
    return pl.pallas_call(
        _ada_kernel,
        grid=(DEPTH, n_mod // ADA_TILE_N),
        in_specs=[
            pl.BlockSpec((MOD_ROWS, D_MODEL), lambda l, j: (0, 0)),
            pl.BlockSpec((1, D_MODEL, ADA_TILE_N), lambda l, j: (l, 0, j)),
            pl.BlockSpec((1, 1, ADA_TILE_N), lambda l, j: (l, 0, j)),
        ],
        out_specs=pl.BlockSpec((1, MOD_ROWS, ADA_TILE_N), lambda l, j: (l, 0, j)),
        out_shape=jax.ShapeDtypeStruct((DEPTH, MOD_ROWS, n_mod), F32),
        compiler_params=_params(2),
        name="ada_mod",
    )(c_rows, w_ada, b_ada.reshape(DEPTH, 1, n_mod))


def _log_sigmoid(x):
    return jnp.minimum(x, 0.0) - jnp.log(1.0 + jnp.exp(-jnp.abs(x)))


def _proj_kernel(is_ctx, x_ref, mod_ref, g_ref, win_ref, seg_ref, gq_ref, gk_ref, wgate_ref, bgate_ref, *rest):
    if is_ctx:
        rest = rest[2:]
    q_out, k_out, v_out, gq_out, gk_out, gv_out, gate_out, lf_out, lb_out = rest[:9]
    qkv_n = 3 * NA_WIDTH
    g_n = 2 * GLA_K_WIDTH + 2 * GLA_V_WIDTH

    mod = mod_ref[0]
    sh1 = mod[:, 0:D_MODEL]
    sc1 = mod[:, D_MODEL:2 * D_MODEL]
    sub = TOKEN_TILE // PROJ_SUB_TILES

    row_blocks = [slice(t * sub, (t + 1) * sub) for t in range(PROJ_SUB_TILES)]

    def normed(rows):
        x = x_ref[rows, :]
        ms = jnp.mean(x * x, axis=-1, keepdims=True)
        h = (x * lax.rsqrt(ms + EPS) * g_ref[...]) * (1.0 + sc1) + sh1
        return h.astype(BF16)

    def attention_inputs(t, rows, hb):
        z = _dot(hb, win_ref[0, :, 0:qkv_n])
        q = z[:, 0:NA_WIDTH]
        k = z[:, NA_WIDTH:2 * NA_WIDTH]
        v = z[:, 2 * NA_WIDTH:3 * NA_WIDTH]
        seg = seg_ref[...]
        q_ms = _dot((q * q).astype(BF16), seg)
        k_ms = _dot((k * k).astype(BF16), seg)
        q_out[rows, :] = (q * lax.rsqrt(q_ms + EPS) * gq_ref[...] * (NA_HEAD_DIM ** -0.5)).astype(BF16)
        kn = k * lax.rsqrt(k_ms + EPS) * gk_ref[...]
        k_out[rows, :] = kn.astype(BF16)
        v_out[rows, :] = v.astype(BF16)
        if is_ctx:
            knew_out, vnew_out = rest[9:]
            for s in range(sub // SEQ):
                for hd in range(NA_HEADS):
                    seq_rows = slice(s * SEQ, (s + 1) * SEQ)
                    cols = slice(hd * NA_HEAD_DIM, (hd + 1) * NA_HEAD_DIM)
                    knew_out[t * (sub // SEQ) + s, 0, hd] = kn[seq_rows, cols]
                    vnew_out[t * (sub // SEQ) + s, 0, hd] = v[seq_rows, cols]

    def gla_inputs(rows, hb):
        z = _dot(hb, win_ref[0, :, qkv_n:qkv_n + g_n])
        gq_out[rows, :] = z[:, 0:GLA_K_WIDTH] * (GLA_DK ** -0.5)
        gk_out[rows, :] = z[:, GLA_K_WIDTH:2 * GLA_K_WIDTH]
        gv_out[rows, :] = z[:, 2 * GLA_K_WIDTH:2 * GLA_K_WIDTH + GLA_V_WIDTH].astype(BF16)
        gate_out[rows, :] = z[:, 2 * GLA_K_WIDTH + GLA_V_WIDTH:]

    def gla_decays(rows, hb):
        zz = _dot(hb, win_ref[0, :, qkv_n + g_n:])
        pre = _dot(zz.astype(BF16), wgate_ref[...]) + bgate_ref[...]
        ls = _log_sigmoid(pre) * (1.0 / GLA_GATE_TAU)
        lf_out[rows, :] = ls[:, 0:GLA_K_WIDTH]
        lb_out[rows, :] = ls[:, GLA_K_WIDTH:]

    hbs = [normed(rows) for rows in row_blocks]
    for t, (rows, hb) in enumerate(zip(row_blocks, hbs)):
        attention_inputs(t, rows, hb)
    for rows, hb in zip(row_blocks, hbs):
        gla_inputs(rows, hb)
    for rows, hb in zip(row_blocks, hbs):
        gla_decays(rows, hb)


def _proj_call(x, mod_l, g_attn_l, w_in_b, seg, gq_t, gk_t, wgate, bgate, new_kv=None, layer=0):
    tm = TOKEN_TILE
    n_tok = x.shape[0]
    is_ctx = new_kv is not None
    row = lambda i: (i, 0)
    const = lambda i: (0, 0)
    out_shapes = [
        jax.ShapeDtypeStruct((n_tok, NA_WIDTH), BF16),
        jax.ShapeDtypeStruct((n_tok, NA_WIDTH), BF16),
        jax.ShapeDtypeStruct((n_tok, NA_WIDTH), BF16),
        jax.ShapeDtypeStruct((n_tok, GLA_K_WIDTH), F32),
        jax.ShapeDtypeStruct((n_tok, GLA_K_WIDTH), F32),
        jax.ShapeDtypeStruct((n_tok, GLA_V_WIDTH), BF16),
        jax.ShapeDtypeStruct((n_tok, GLA_V_WIDTH), F32),
        jax.ShapeDtypeStruct((n_tok, GLA_K_WIDTH), F32),
        jax.ShapeDtypeStruct((n_tok, GLA_K_WIDTH), F32),
    ]
    out_specs = [pl.BlockSpec((tm, s.shape[1]), row) for s in out_shapes]
    in_specs = [
        pl.BlockSpec((tm, D_MODEL), row),
        pl.BlockSpec((1, 1, N_MOD * D_MODEL), _mod_index_map(is_ctx)),
        pl.BlockSpec((1, D_MODEL), const),
        pl.BlockSpec((1,) + w_in_b.shape[1:], lambda i: (layer, 0, 0)),
        pl.BlockSpec((NA_WIDTH, NA_WIDTH), const),
        pl.BlockSpec((1, NA_WIDTH), const),
        pl.BlockSpec((1, NA_WIDTH), const),
        pl.BlockSpec(wgate.shape, const),
        pl.BlockSpec((1, 2 * GLA_K_WIDTH), const),
    ]
    args = [x, mod_l, g_attn_l, w_in_b, seg, gq_t, gk_t, wgate, bgate]
    aliases = {}
    if is_ctx:
        kv_spec = pl.BlockSpec((tm // SEQ, 1, NA_HEADS, SEQ, NA_HEAD_DIM), lambda i: (i, layer, 0, 0, 0))
        aliases = {len(args): len(out_shapes), len(args) + 1: len(out_shapes) + 1}
        in_specs += [pl.BlockSpec(memory_space=pl.ANY)] * 2
        args += list(new_kv)
        out_shapes += [jax.ShapeDtypeStruct(new_kv[0].shape, F32)] * 2
        out_specs += [kv_spec, kv_spec]
    return pl.pallas_call(
        functools.partial(_proj_kernel, is_ctx),
        grid=(n_tok // tm,),
        in_specs=in_specs,
        out_specs=out_specs,
        out_shape=out_shapes,
        input_output_aliases=aliases,
        compiler_params=_params(1),
        name="proj_ctx" if is_ctx else "proj_smp",
    )(*args)


def _ctx_attn_kernel(q_ref, k_ref, v_ref, o_ref):
    lane = lax.broadcasted_iota(jnp.int32, (1, 2 * NA_HEAD_DIM), 1)
    first = lane < NA_HEAD_DIM
    pairs = [slice(hp * 2 * NA_HEAD_DIM, (hp + 1) * 2 * NA_HEAD_DIM) for hp in range(NA_HEADS // 2)]
    scores = []
    for cols in pairs:
        q2 = q_ref[:, cols]
        k2 = k_ref[:, cols]
        scores.append(_dot_nt(jnp.where(first, q2, jnp.zeros_like(q2)), k2))
        scores.append(_dot_nt(jnp.where(first, jnp.zeros_like(q2), q2), k2))
    probs = []
    for s in scores:
        p = jnp.exp(s - jnp.max(s, axis=-1, keepdims=True))
        probs.append((p.astype(BF16), jnp.sum(p, axis=-1, keepdims=True)))
    for hp, cols in enumerate(pairs):
        v2 = v_ref[:, cols]
        (p0, l0), (p1, l1) = probs[2 * hp], probs[2 * hp + 1]
        o_ref[:, cols] = jnp.where(first, _dot(p0, v2) / l0, _dot(p1, v2) / l1).astype(BF16)


def _ctx_attn_call(q, k, v):
    spec = pl.BlockSpec((SEQ, NA_WIDTH), lambda b: (b, 0))
    return pl.pallas_call(
        _ctx_attn_kernel,
        grid=(BATCH,),
        in_specs=[spec, spec, spec],
        out_specs=spec,
        out_shape=jax.ShapeDtypeStruct((N_CTX, NA_WIDTH), BF16),
        compiler_params=_params(1),
        name="ctx_attn",
    )(q, k, v)


def _na_window_start(r):
    return min(max(r - NA_WIN_ROWS // 2, 0), GRID_ROWS - NA_WIN_ROWS)


def _na_row_groups():
    groups = []
    for g in range(GRID_ROWS // NA_ROW_GROUP):
        starts = [_na_window_start(r) for r in range(g * NA_ROW_GROUP, (g + 1) * NA_ROW_GROUP)]
        lo = min(starts) // 2 * 2
        count = max(starts) + NA_WIN_ROWS - lo
        groups.append((lo, count + count % 2))
    return groups


def _na_bias_tiles(rpb_ref, hh, tiles_scr):
    n_dr = 2 * NA_WIN_ROWS - 1
    lanes = 2 * GRID_W
    lane = lax.broadcasted_iota(jnp.int32, (GRID_W, lanes), 1)
    col = lax.broadcasted_iota(jnp.int32, (GRID_W, lanes), 0)
    first = lane < GRID_W
    c2 = lane % GRID_W
    cs = jnp.clip(col - NA_WIN_COLS // 2, 0, GRID_W - NA_WIN_COLS)
    valid = (c2 >= cs) & (c2 < cs + NA_WIN_COLS)
    rows = rpb_ref[0, hh]
    base = lanes - (NA_WIN_COLS - 1)

    def skew(dr, shift):
        row = jnp.broadcast_to(rows[dr:dr + 1], (GRID_W, lanes))
        return pltpu.roll(row, shift, axis=1, stride=1, stride_axis=0)

    for e in range(n_dr - 1):
        tile = jnp.where(first, skew(e, base), skew(e + 1, (base + GRID_W) % lanes))
        tiles_scr[e] = jnp.where(valid, tile, MASK_VALUE)


def _na_group_bias(tiles_scr, g, key_lo, key_count, first):
    masked = jnp.full((GRID_W, 2 * GRID_W), MASK_VALUE, F32)
    row_tiles = []
    for r in range(g * NA_ROW_GROUP, (g + 1) * NA_ROW_GROUP):
        rs = _na_window_start(r)
        pieces = []
        for u in range(key_count // 2):
            r0 = key_lo + 2 * u
            in0 = rs <= r0 < rs + NA_WIN_ROWS
            in1 = rs <= r0 + 1 < rs + NA_WIN_ROWS
            if not (in0 or in1):
                pieces.append(masked)
                continue
            tile = tiles_scr[r0 - r + NA_WIN_ROWS - 1]
            if not in1:
                tile = jnp.where(first, tile, MASK_VALUE)
            elif not in0:
                tile = jnp.where(first, MASK_VALUE, tile)
            pieces.append(tile)
        row_tiles.append(jnp.concatenate(pieces, axis=1))
    return jnp.concatenate(row_tiles, axis=0)


def _na_kernel(q_ref, k_ref, v_ref, kc_ref, vc_ref, rpb_ref, o_ref, sc_ref, tiles_scr):
    lane = lax.broadcasted_iota(jnp.int32, (1, 2 * NA_HEAD_DIM), 1)
    first = lane < NA_HEAD_DIM
    q2 = q_ref[...]
    kc2 = jnp.concatenate([kc_ref[0, 0, 0], kc_ref[0, 0, 1]], axis=0).astype(BF16)
    vc2 = jnp.concatenate([vc_ref[0, 0, 0], vc_ref[0, 0, 1]], axis=0).astype(BF16)
    block = NA_ROW_GROUP * GRID_W
    outs = []
    for hh in range(2):
        qm = jnp.where(first, q2, jnp.zeros_like(q2)) if hh == 0 else jnp.where(first, jnp.zeros_like(q2), q2)
        sc_ref[...] = _dot(qm, kc2)
        _na_bias_tiles(rpb_ref, hh, tiles_scr)
        blocks = []
        for g, (key_lo, key_count) in enumerate(_na_row_groups()):
            keys = slice(key_lo * GRID_W, (key_lo + key_count) * GRID_W)
            s = _dot_nt(qm[g * block:(g + 1) * block], k_ref[keys, :])
            s = s + _na_group_bias(tiles_scr, g, key_lo, key_count, first)
            sc = sc_ref[g * block:(g + 1) * block, :]
            m = jnp.maximum(jnp.max(s, axis=-1, keepdims=True), jnp.max(sc, axis=-1, keepdims=True))
            p = jnp.exp(s - m)
            pc = jnp.exp(sc - m)
            l = jnp.sum(p, axis=-1, keepdims=True) + jnp.sum(pc, axis=-1, keepdims=True)
            blocks.append((_dot(p.astype(BF16), v_ref[keys, :]) + _dot_nt(pc.astype(BF16), vc2)) / l)
        outs.append(jnp.concatenate(blocks, axis=0))
    o_ref[...] = jnp.where(first, outs[0], outs[1]).astype(BF16)


def _na_call(q, k, v, cache_kt, cache_vt, rpb_rows, layer):
    tok = pl.BlockSpec((DEC_SEQ, 2 * NA_HEAD_DIM), lambda b, hp: (b, hp))
    cache = pl.BlockSpec((1, 1, 2, NA_HEAD_DIM, PAST_LEN), lambda b, hp: (b, layer, hp, 0, 0))
    n_pairs = 2 * NA_WIN_ROWS - 2
    return pl.pallas_call(
        _na_kernel,
        grid=(DEC_BATCH, NA_HEADS // 2),
        in_specs=[
            tok, tok, tok, cache, cache,
            pl.BlockSpec((1, 2) + rpb_rows.shape[2:], lambda b, hp: (layer, hp, 0, 0)),
        ],
        out_specs=tok,
        out_shape=jax.ShapeDtypeStruct((N_SMP, NA_WIDTH), BF16),
        scratch_shapes=[pltpu.VMEM((DEC_SEQ, PAST_LEN), F32),
                        pltpu.VMEM((n_pairs, GRID_W, 2 * GRID_W), F32)],
        compiler_params=_params(2),
        name="nbr_attn",
    )(q, k, v, cache_kt, cache_vt, rpb_rows)


def _gla_constants():
    C = GLA_CHUNK
    ii, jj = np.meshgrid(np.arange(C), np.arange(C), indexing="ij")
    tri = np.stack([jj <= ii, jj >= ii]).astype(np.float32)
    w = np.concatenate([tri, tri], axis=-1)

    x = ii ^ jj
    hb = np.where(x > 0, np.floor(np.log2(np.maximum(x, 1))), -1).astype(np.int64)
    masks = np.zeros((2, GLA_LEVELS + 1, C, C), np.float32)
    for p in range(GLA_LEVELS):
        masks[0, p] = (jj < ii) & (hb == p)
        masks[1, p] = (jj > ii) & (hb == p)
    masks[:, GLA_LEVELS] = np.eye(C)
    masks = np.tile(masks, (1, 1, 2, 2))
    state_mask = (np.arange(2 * GLA_DV)[:, None] // GLA_DV == np.arange(2 * GLA_DK)[None, :] // GLA_DK)
    upper = np.stack([(np.arange(C) >> p) & 1 for p in range(GLA_LEVELS)]).astype(bool)
    sign = np.stack([np.where(upper, 1.0, -1.0), np.where(upper, -1.0, 1.0)])
    scales = np.broadcast_to((sign * np.log2(np.e))[..., None], (2, GLA_LEVELS, C, GLA_K_WIDTH))
    return w, masks, state_mask.astype(np.float32), np.ascontiguousarray(scales, np.float32)


def _gla_kernel(seq_len, n_seq, is_ctx, *refs):
    refs = list(refs)
    gq_ref, gk_ref, gv_ref, lf_ref, lb_ref, w_ref, mask_ref, smask_ref, scale_ref = refs[:9]
    refs = refs[9:]
    if is_ctx:
        refs = refs[2:]
        o_ref, sf_ref, sb_ref = refs[:3]
        refs = refs[3:]
    else:
        cos_ref, sin_ref, s0f_ref, s0b_ref, o_ref = refs[:5]
        refs = refs[5:]
    q_scr, k_scr, of_scr, ob_scr, st_scr, cum_scr = refs

    C = GLA_CHUNK
    n_chunks = seq_len // C
    n_pairs = GLA_HEADS // 2
    pair_k = 2 * GLA_DK
    pair_v = 2 * GLA_DV

    q = gq_ref[...]
    k = gk_ref[...]
    if not is_ctx:
        lane = lax.broadcasted_iota(jnp.int32, (1, GLA_K_WIDTH), 1)
        quarter = GLA_DK // 4
        low = (lane % (2 * quarter)) < quarter
        cos = jnp.concatenate([cos_ref[...]] * n_seq, axis=0)
        sin = jnp.concatenate([sin_ref[...]] * n_seq, axis=0)

        def rope(t):
            partner = jnp.where(low, pltpu.roll(t, GLA_K_WIDTH - quarter, axis=1), pltpu.roll(t, quarter, axis=1))
            return t * cos + partner * sin

        q = rope(q)
        k = rope(k)
    q_scr[...] = q
    k_scr[...] = k

    st_scr[...] = jnp.zeros(st_scr.shape, F32)
    if not is_ctx:
        for d, s0_ref in enumerate((s0f_ref, s0b_ref)):
            for s in range(n_seq):
                for h in range(GLA_HEADS):
                    pr, e = divmod(h, 2)
                    st_scr[s, d, pr, e * GLA_DV:(e + 1) * GLA_DV, e * GLA_DK:(e + 1) * GLA_DK] = s0_ref[s, 0, h].T

    klane = lax.broadcasted_iota(jnp.int32, (1, GLA_K_WIDTH), 1)
    even_head = (klane // GLA_DK) % 2 == 0

    def chunk_decay(s, d, r0, f_ref):
        rows = pl.ds(s * seq_len + r0, C)
        f = f_ref[rows, :]
        f_hi = f.astype(BF16)
        f_lo = (f - f_hi.astype(F32)).astype(BF16)
        cum = _dot(w_ref[d], jnp.concatenate([f_hi, f_lo], axis=0))
        cum_scr[s, d] = cum
        return rows, cum

    def chunk_scores(s, d, rows, cum):
        def cum_rows(r, n):
            return jnp.broadcast_to(cum_scr[s, d, r:r + 1, :], (n, GLA_K_WIDTH))

        def level_factor(p):
            m = 1 << p
            edge = m - 1 if d == 0 else m
            if 2 * m >= 8:
                ref = jnp.concatenate([cum_rows(blk + edge, 2 * m) for blk in range(0, C, 2 * m)], axis=0)
            elif p == 1:
                sub = lax.broadcasted_iota(jnp.int32, (8, GLA_K_WIDTH), 0)
                ref = jnp.concatenate([jnp.where(sub < 4, cum_rows(blk + edge, 8), cum_rows(blk + 4 + edge, 8))
                                       for blk in range(0, C, 8)], axis=0)
            else:
                odd = lax.broadcasted_iota(jnp.int32, (C, GLA_K_WIDTH), 0) % 2 == 1
                if d == 0:
                    ref = jnp.where(odd, pltpu.roll(cum, 1, axis=0), cum)
                else:
                    ref = jnp.where(odd, cum, pltpu.roll(cum, C - 1, axis=0))
            return jnp.exp2((cum - ref) * scale_ref[d, p])

        qc = q_scr[rows, :]
        kc = k_scr[rows, :]
        vc = gv_ref[rows, :]
        last = C - 1 if d == 0 else 0
        eq = jnp.exp(cum)
        ek = jnp.exp(cum_rows(last, C) - cum)
        total = jnp.exp(cum_scr[s, d, last:last + 1, :])

        a = None
        for p in range(GLA_LEVELS + 1):
            if p < GLA_LEVELS:
                ep = level_factor(p)
                qe = (qc * ep).astype(BF16)
                ke = (kc * ep).astype(BF16)
            else:
                qe = qc.astype(BF16)
                ke = kc.astype(BF16)
            zero = jnp.zeros((C, pair_k), BF16)
            lhs = jnp.concatenate([jnp.concatenate([qe[:, :pair_k], zero], axis=1),
                                   jnp.concatenate([zero, qe[:, pair_k:]], axis=1)], axis=0)
            rhs = jnp.concatenate([jnp.where(even_head, ke, jnp.zeros_like(ke)),
                                   jnp.where(even_head, jnp.zeros_like(ke), ke)], axis=0)
            sc = _dot_nt(lhs, rhs) * mask_ref[d, p]
            a = sc if a is None else a + sc
        q_in = (qc * eq).astype(BF16)
        k_out = (kc * ek).astype(BF16)
        return a.astype(BF16), q_in, k_out, vc, total

    def chunk_output(s, d, rows, o_scr, a, q_in, k_out, vc, total):
        zero_v = jnp.zeros((C, GLA_DV), BF16)
        for pr in range(n_pairs):
            kl = slice(pr * pair_k, (pr + 1) * pair_k)
            vp = vc[:, pr * pair_v:(pr + 1) * pair_v]
            v_diag = jnp.concatenate([jnp.concatenate([vp[:, :GLA_DV], zero_v], axis=1),
                                      jnp.concatenate([zero_v, vp[:, GLA_DV:]], axis=1)], axis=0)
            st = st_scr[s, d, pr]
            o_scr[rows, pr * pair_v:(pr + 1) * pair_v] = (
                _dot(a[pr * C:(pr + 1) * C], v_diag) + _dot_nt(q_in[:, kl], st.astype(BF16)))
            st_scr[s, d, pr] = st * total[:, kl] + _dot_tn(vp, k_out[:, kl]) * smask_ref[...]

    def body(n, carry):
        fwd = pl.multiple_of(n * C, C)
        bwd = pl.multiple_of((n_chunks - 1 - n) * C, C)
        scans = [(s, d) for s in range(n_seq) for d in range(2)]
        decays = [chunk_decay(s, d, bwd if d else fwd, lb_ref if d else lf_ref) for s, d in scans]
        scores = [chunk_scores(s, d, *dec) for (s, d), dec in zip(scans, decays)]
        for (s, d), (rows, _), sc in zip(scans, decays, scores):
            chunk_output(s, d, rows, ob_scr if d else of_scr, *sc)
        return carry

    lax.fori_loop(0, n_chunks, body, 0)
    o_ref[...] = of_scr[...] + ob_scr[...]
    if is_ctx:
        for d, s_ref in enumerate((sf_ref, sb_ref)):
            for s in range(n_seq):
                for h in range(GLA_HEADS):
                    pr, e = divmod(h, 2)
                    s_ref[s, 0, h] = st_scr[s, d, pr, e * GLA_DV:(e + 1) * GLA_DV, e * GLA_DK:(e + 1) * GLA_DK].T


def _gla_call(gq, gk, gv, lf, lb, consts, new_states=None, rope=None, state_in=None, layer=0):
    is_ctx = new_states is not None
    seq_len = SEQ if is_ctx else DEC_SEQ
    n_tok = gq.shape[0]
    n_seq = GLA_SEQS_PER_STEP
    rows = n_seq * seq_len
    tok = lambda width: pl.BlockSpec((rows, width), lambda b: (b, 0))
    whole = lambda a: pl.BlockSpec(a.shape, lambda b: (0,) * a.ndim)
    state_spec = pl.BlockSpec((n_seq, 1, GLA_HEADS, GLA_DK, GLA_DV), lambda b: (b, layer, 0, 0, 0))
    in_specs = [tok(GLA_K_WIDTH), tok(GLA_K_WIDTH), tok(GLA_V_WIDTH), tok(GLA_K_WIDTH), tok(GLA_K_WIDTH)]
    in_specs += [whole(a) for a in consts]
    args = [gq, gk, gv, lf, lb, *consts]
    out_shape = [jax.ShapeDtypeStruct((n_tok, GLA_V_WIDTH), F32)]
    out_specs = [tok(GLA_V_WIDTH)]
    aliases = {}
    if is_ctx:
        aliases = {len(args): 1, len(args) + 1: 2}
        in_specs += [pl.BlockSpec(memory_space=pl.ANY)] * 2
        args += list(new_states)
        out_shape += [jax.ShapeDtypeStruct(new_states[0].shape, F32)] * 2
        out_specs += [state_spec, state_spec]
    else:
        in_specs += [pl.BlockSpec((seq_len, GLA_K_WIDTH), lambda b: (0, 0))] * 2 + [state_spec, state_spec]
        args += list(rope) + list(state_in)
    return pl.pallas_call(
        functools.partial(_gla_kernel, seq_len, n_seq, is_ctx),
        grid=(n_tok // rows,),
        in_specs=in_specs,
        out_specs=out_specs,
        out_shape=out_shape,
        input_output_aliases=aliases,
        scratch_shapes=[
            pltpu.VMEM((rows, GLA_K_WIDTH), F32),
            pltpu.VMEM((rows, GLA_K_WIDTH), F32),
            pltpu.VMEM((rows, GLA_V_WIDTH), F32),
            pltpu.VMEM((rows, GLA_V_WIDTH), F32),
            pltpu.VMEM((n_seq, 2, GLA_HEADS // 2, 2 * GLA_DV, 2 * GLA_DK), F32),
            pltpu.VMEM((n_seq, 2, GLA_CHUNK, GLA_K_WIDTH), F32),
        ],
        compiler_params=_params(1),
        name="gla_ctx" if is_ctx else "gla_smp",
    )(*args)


def _rope_tables():
    quarter = GLA_DK // 4
    inv = ROPE_BASE ** (-jnp.arange(quarter, dtype=F32) / quarter)
    t = jnp.arange(DEC_SEQ)
    ang_r = (t // GRID_W).astype(F32)[:, None] * inv
    ang_c = (t % GRID_W).astype(F32)[:, None] * inv
    cos = jnp.concatenate([jnp.cos(ang_r)] * 2 + [jnp.cos(ang_c)] * 2, axis=-1)
    sin = jnp.concatenate([-jnp.sin(ang_r), jnp.sin(ang_r), -jnp.sin(ang_c), jnp.sin(ang_c)], axis=-1)
    return jnp.tile(cos, (1, GLA_HEADS)), jnp.tile(sin, (1, GLA_HEADS))


def _merge_mlp_kernel(x_ref, mod_ref, ona_ref, ogla_ref, gate_ref, gout_ref, wo_ref, gmlp_ref,
                      wup_ref, wdown_ref, o_ref):
    mod = mod_ref[0]
    ga1 = mod[:, 2 * D_MODEL:3 * D_MODEL]
    sh2 = mod[:, 3 * D_MODEL:4 * D_MODEL]
    sc2 = mod[:, 4 * D_MODEL:5 * D_MODEL]
    ga2 = mod[:, 5 * D_MODEL:6 * D_MODEL]

    og = ogla_ref[...]
    normed = []
    for h in range(GLA_HEADS):
        oh = og[:, h * GLA_DV:(h + 1) * GLA_DV]
        normed.append(oh * lax.rsqrt(jnp.mean(oh * oh, axis=-1, keepdims=True) + EPS))
    gate = gate_ref[...]
    g = jnp.concatenate(normed, axis=1) * gout_ref[...] * (gate * (1.0 / (1.0 + jnp.exp(-gate))))
    att = _dot(ona_ref[...], wo_ref[0, 0:NA_WIDTH, :]) + _dot(g.astype(BF16), wo_ref[0, NA_WIDTH:, :])
    x = x_ref[...] + ga1 * att

    ms = jnp.mean(x * x, axis=-1, keepdims=True)
    h2 = ((x * lax.rsqrt(ms + EPS) * gmlp_ref[...]) * (1.0 + sc2) + sh2).astype(BF16)
    acc = jnp.zeros_like(x)
    for c in range(D_FF // FF_CHUNK):
        u = jnp.maximum(_dot(h2, wup_ref[0, :, c * FF_CHUNK:(c + 1) * FF_CHUNK]), 0.0)
        acc = acc + _dot((u * u).astype(BF16), wdown_ref[0, c * FF_CHUNK:(c + 1) * FF_CHUNK, :])
    o_ref[...] = x + ga2 * acc


def _merge_mlp_call(x, mod_l, o_na, o_gla, gate, gout_t, wo, g_mlp_l, wup, wdown, layer):
    tm = TOKEN_TILE
    n_tok = x.shape[0]
    is_ctx = n_tok == N_CTX
    row = lambda i: (i, 0)
    const = lambda i: (0, 0)
    of_layer = lambda i: (layer, 0, 0)
    resident = dict(pipeline_mode=pl.Buffered(1))
    return pl.pallas_call(
        _merge_mlp_kernel,
        grid=(n_tok // tm,),
        in_specs=[
            pl.BlockSpec((tm, D_MODEL), row),
            pl.BlockSpec((1, 1, N_MOD * D_MODEL), _mod_index_map(is_ctx)),
            pl.BlockSpec((tm, NA_WIDTH), row),
            pl.BlockSpec((tm, GLA_V_WIDTH), row),
            pl.BlockSpec((tm, GLA_V_WIDTH), row),
            pl.BlockSpec((1, GLA_V_WIDTH), const),
            pl.BlockSpec((1, D_MODEL, D_MODEL), of_layer, **resident),
            pl.BlockSpec((1, D_MODEL), const),
            pl.BlockSpec((1, D_MODEL, D_FF), of_layer, **resident),
            pl.BlockSpec((1, D_FF, D_MODEL), of_layer, **resident),
        ],
        out_specs=pl.BlockSpec((tm, D_MODEL), row),
        out_shape=jax.ShapeDtypeStruct((n_tok, D_MODEL), F32),
        compiler_params=_params(1),
        name="merge_mlp_ctx" if is_ctx else "merge_mlp_smp",
    )(x, mod_l, o_na, o_gla, gate, gout_t, wo, g_mlp_l, wup, wdown)


def kernel(x_prompt, x_sample, cache_k, cache_v, state_fwd, state_bwd, c, c_ctx, w_ada, b_ada, g_attn, w_in,
           g_q, g_k, rpb, w_gf, b_gf, w_gb, b_gb, g_gla_out, w_o, g_mlp, w_up, w_down):
    x_c = x_prompt.reshape(N_CTX, D_MODEL)
    x_s = x_sample.reshape(N_SMP, D_MODEL)

    c_rows = jnp.concatenate([c_ctx[None, :], c, jnp.zeros((MOD_ROWS - 1 - DEC_BATCH, D_MODEL), F32)], axis=0)
    mods = _ada_call(c_rows, w_ada, b_ada).reshape(DEPTH, MOD_ROWS, 1, N_MOD * D_MODEL)

    w_in_b = w_in.astype(BF16)
    wo_b = w_o.astype(BF16)
    wup_b = w_up.astype(BF16)
    wdown_b = w_down.astype(BF16)
    wgate = jnp.zeros((DEPTH, 2 * GLA_GATE_RANK, 2 * GLA_K_WIDTH), F32)
    wgate = wgate.at[:, :GLA_GATE_RANK, :GLA_K_WIDTH].set(w_gf)
    wgate = wgate.at[:, GLA_GATE_RANK:, GLA_K_WIDTH:].set(w_gb).astype(BF16)
    bgate = jnp.concatenate([b_gf, b_gb], axis=-1).reshape(DEPTH, 1, 2 * GLA_K_WIDTH)
    head_of = np.arange(NA_WIDTH) // NA_HEAD_DIM
    seg = jnp.asarray((head_of[:, None] == head_of[None, :]).astype(np.float32) / NA_HEAD_DIM, BF16)
    gq_t = jnp.tile(g_q, (1, NA_HEADS)).reshape(DEPTH, 1, NA_WIDTH)
    gk_t = jnp.tile(g_k, (1, NA_HEADS)).reshape(DEPTH, 1, NA_WIDTH)
    gout_t = jnp.tile(g_gla_out, (1, GLA_HEADS)).reshape(DEPTH, 1, GLA_V_WIDTH)
    n_dr, n_dc = rpb.shape[2:]
    rpb_rows = jnp.pad(rpb, ((0, 0), (0, 0), (0, -n_dr % 8), (0, 2 * GRID_W - n_dc)))
    gla_w_np, gla_masks_np, gla_smask_np, gla_scales_np = _gla_constants()
    gla_consts = (jnp.asarray(gla_w_np, BF16), jnp.asarray(gla_masks_np, F32), jnp.asarray(gla_smask_np, F32),
                  jnp.asarray(gla_scales_np, F32))
    rope = _rope_tables()

    cache_kt = jnp.swapaxes(cache_k, 3, 4)
    cache_vt = jnp.swapaxes(cache_v, 3, 4)
    new_k = jnp.zeros((BATCH, DEPTH, NA_HEADS, SEQ, NA_HEAD_DIM), F32)
    new_v = jnp.zeros((BATCH, DEPTH, NA_HEADS, SEQ, NA_HEAD_DIM), F32)
    new_sf = jnp.zeros((BATCH, DEPTH, GLA_HEADS, GLA_DK, GLA_DV), F32)
    new_sb = jnp.zeros((BATCH, DEPTH, GLA_HEADS, GLA_DK, GLA_DV), F32)

    for l in range(DEPTH):
        proj_w = (g_attn[l].reshape(1, D_MODEL), w_in_b, seg, gq_t[l], gk_t[l], wgate[l], bgate[l])
        mlp_w = (gout_t[l], wo_b, g_mlp[l].reshape(1, D_MODEL), wup_b, wdown_b)

        q, k, v, gq, gk, gv, gate, lf, lb, new_k, new_v = _proj_call(
            x_c, mods[l], *proj_w, new_kv=(new_k, new_v), layer=l)
        o_na = _ctx_attn_call(q, k, v)
        o_gla, new_sf, new_sb = _gla_call(gq, gk, gv, lf, lb, gla_consts, new_states=(new_sf, new_sb), layer=l)
        x_c = _merge_mlp_call(x_c, mods[l], o_na, o_gla, gate, *mlp_w, layer=l)

        q, k, v, gq, gk, gv, gate, lf, lb = _proj_call(x_s, mods[l], *proj_w, layer=l)
        o_na = _na_call(q, k, v, cache_kt, cache_vt, rpb_rows, l)
        (o_gla,) = _gla_call(gq, gk, gv, lf, lb, gla_consts, rope=rope,
                             state_in=(state_fwd, state_bwd), layer=l)
        x_s = _merge_mlp_call(x_s, mods[l], o_na, o_gla, gate, *mlp_w, layer=l)

    return (x_c.reshape(BATCH, SEQ, D_MODEL), x_s.reshape(DEC_BATCH, DEC_SEQ, D_MODEL),
            new_k, new_v, new_sf, new_sb)
```

```python
import functools

import numpy as np
import jax
import jax.numpy as jnp
from jax import lax
from jax.experimental import pallas as pl
from jax.experimental.pallas import tpu as pltpu

F32 = jnp.float32
BF16 = jnp.bfloat16

D_MODEL = 1024
BATCH = 16
SEQ = 256
DEPTH = 4
DEC_BATCH = 2
DEC_SEQ = 1024
PAST_LEN = 512
GRID_W = 64
GRID_ROWS = DEC_SEQ // GRID_W
NA_WIDTH = D_MODEL // 2
NA_HEAD_DIM = 64
NA_HEADS = NA_WIDTH // NA_HEAD_DIM
NA_WIN_ROWS = 8
NA_WIN_COLS = 16
GLA_V_WIDTH = D_MODEL - NA_WIDTH
GLA_HEADS = 4
GLA_DV = GLA_V_WIDTH // GLA_HEADS
GLA_DK = GLA_DV // 2
GLA_K_WIDTH = GLA_HEADS * GLA_DK
GLA_GATE_RANK = 16
GLA_GATE_TAU = 16.0
GLA_CHUNK = 64
D_FF = 4 * D_MODEL
ROPE_BASE = 10000.0
N_MOD = 6
EPS = 1e-6

N_CTX = BATCH * SEQ
N_SMP = DEC_BATCH * DEC_SEQ
MOD_ROWS = 8
MASK_VALUE = -1e30

TOKEN_TILE = 512
PROJ_SUB_TILES = 2
FF_CHUNK = 512
MLP_CHUNK = 1024
ADA_TILE_N = 1536
GLA_LEVELS = 6
NA_ROW_GROUP = 4
GLA_SEQS_PER_STEP = 2
VMEM_LIMIT = 52 * 1024 * 1024


def _dot(a, b):
    return jnp.dot(a, b, preferred_element_type=F32)


def _dot_nt(a, b):
    return lax.dot_general(a, b, (((1,), (1,)), ((), ())), preferred_element_type=F32)


def _dot_tn(a, b):
    return lax.dot_general(a, b, (((0,), (0,)), ((), ())), preferred_element_type=F32)


def _params(n_grid_dims=1):
    return pltpu.CompilerParams(dimension_semantics=("arbitrary",) * n_grid_dims,
                                vmem_limit_bytes=VMEM_LIMIT)


def _mod_index_map(is_ctx):
    if is_ctx:
        return lambda i: (0, 0, 0)
    return lambda i: (1 + i // (DEC_SEQ // TOKEN_TILE), 0, 0)


def _ada_kernel(c_ref, w_ref, b_ref, o_ref):
    cv = c_ref[...]
    s = cv * (1.0 / (1.0 + jnp.exp(-cv)))
    o_ref[0] = _dot(s.astype(BF16), w_ref[0].astype(BF16)) + b_ref[0]


def _ada_call(c_rows, w_ada, b_ada):
    n_mod = N_MOD * D_MODEL
    return pl.pallas_call(
        _ada_kernel,
        grid=(DEPTH, n_mod // ADA_TILE_N),
        in_specs=[
            pl.BlockSpec((MOD_ROWS, D_MODEL), lambda l, j: (0, 0)),
            pl.BlockSpec((1, D_MODEL, ADA_TILE_N), lambda l, j: (l, 0, j)),
            pl.BlockSpec((1, 1, ADA_TILE_N), lambda l, j: (l, 0, j)),
        ],
        out_specs=pl.BlockSpec((1, MOD_ROWS, ADA_TILE_N), lambda l, j: (l, 0, j)),
        out_shape=jax.ShapeDtypeStruct((DEPTH, MOD_ROWS, n_mod), F32),
        compiler_params=_params(2),
        name="ada_mod",
    )(c_rows, w_ada, b_ada.reshape(DEPTH, 1, n_mod))


def _log_sigmoid(x):
    return jnp.minimum(x, 0.0) - jnp.log(1.0 + jnp.exp(-jnp.abs(x)))


def _proj_kernel(is_ctx, x_ref, mod_ref, g_ref, win_ref, seg_ref, gq_ref, gk_ref, wgate_ref, bgate_ref, *rest):
    if is_ctx:
        rest = rest[2:]
    q_out, k_out, v_out, gq_out, gk_out, gv_out, gate_out, lf_out, lb_out = rest[:9]
    qkv_n = 3 * NA_WIDTH
    g_n = 2 * GLA_K_WIDTH + 2 * GLA_V_WIDTH

    mod = mod_ref[0]
    sh1 = mod[:, 0:D_MODEL]
    sc1 = mod[:, D_MODEL:2 * D_MODEL]
    sub = TOKEN_TILE // PROJ_SUB_TILES

    row_blocks = [slice(t * sub, (t + 1) * sub) for t in range(PROJ_SUB_TILES)]

    def normed(rows):
        x = x_ref[rows, :]
        ms = jnp.mean(x * x, axis=-1, keepdims=True)
        h = (x * lax.rsqrt(ms + EPS) * g_ref[...]) * (1.0 + sc1) + sh1
        return h.astype(BF16)

    def attention_inputs(t, rows, hb):
        z = _dot(hb, win_ref[0, :, 0:qkv_n])
        q = z[:, 0:NA_WIDTH]
        k = z[:, NA_WIDTH:2 * NA_WIDTH]
        v = z[:, 2 * NA_WIDTH:3 * NA_WIDTH]
        seg = seg_ref[...]
        q_ms = _dot((q * q).astype(BF16), seg)
        k_ms = _dot((k * k).astype(BF16), seg)
        q_out[rows, :] = (q * lax.rsqrt(q_ms + EPS) * gq_ref[...] * (NA_HEAD_DIM ** -0.5)).astype(BF16)
        kn = k * lax.rsqrt(k_ms + EPS) * gk_ref[...]
        k_out[rows, :] = kn.astype(BF16)
        v_out[rows, :] = v.astype(BF16)
        if is_ctx:
            knew_out, vnew_out = rest[9:]
            for s in range(sub // SEQ):
                for hd in range(NA_HEADS):
                    seq_rows = slice(s * SEQ, (s + 1) * SEQ)
                    cols = slice(hd * NA_HEAD_DIM, (hd + 1) * NA_HEAD_DIM)
                    knew_out[t * (sub // SEQ) + s, 0, hd] = kn[seq_rows, cols]
                    vnew_out[t * (sub // SEQ) + s, 0, hd] = v[seq_rows, cols]

    def gla_inputs(rows, hb):
        z = _dot(hb, win_ref[0, :, qkv_n:qkv_n + g_n])
        gq_out[rows, :] = z[:, 0:GLA_K_WIDTH] * (GLA_DK ** -0.5)
        gk_out[rows, :] = z[:, GLA_K_WIDTH:2 * GLA_K_WIDTH]
        gv_out[rows, :] = z[:, 2 * GLA_K_WIDTH:2 * GLA_K_WIDTH + GLA_V_WIDTH].astype(BF16)
        gate_out[rows, :] = z[:, 2 * GLA_K_WIDTH + GLA_V_WIDTH:]

    def gla_decays(rows, hb):
        zz = _dot(hb, win_ref[0, :, qkv_n + g_n:])
        pre = _dot(zz.astype(BF16), wgate_ref[...]) + bgate_ref[...]
        ls = _log_sigmoid(pre) * (1.0 / GLA_GATE_TAU)
        lf_out[rows, :] = ls[:, 0:GLA_K_WIDTH]
        lb_out[rows, :] = ls[:, GLA_K_WIDTH:]

    hbs = [normed(rows) for rows in row_blocks]
    for t, (rows, hb) in enumerate(zip(row_blocks, hbs)):
        attention_inputs(t, rows, hb)
    for rows, hb in zip(row_blocks, hbs):
        gla_inputs(rows, hb)
    for rows, hb in zip(row_blocks, hbs):
        gla_decays(rows, hb)


def _proj_call(x, mod_l, g_attn_l, w_in_b, seg, gq_t, gk_t, wgate, bgate, new_kv=None, layer=0):
    tm = TOKEN_TILE
    n_tok = x.shape[0]
    is_ctx = new_kv is not None
    row = lambda i: (i, 0)
    const = lambda i: (0, 0)
    out_shapes = [
        jax.ShapeDtypeStruct((n_tok, NA_WIDTH), BF16),
        jax.ShapeDtypeStruct((n_tok, NA_WIDTH), BF16),
        jax.ShapeDtypeStruct((n_tok, NA_WIDTH), BF16),
        jax.ShapeDtypeStruct((n_tok, GLA_K_WIDTH), F32),
        jax.ShapeDtypeStruct((n_tok, GLA_K_WIDTH), F32),
        jax.ShapeDtypeStruct((n_tok, GLA_V_WIDTH), BF16),
        jax.ShapeDtypeStruct((n_tok, GLA_V_WIDTH), F32),
        jax.ShapeDtypeStruct((n_tok, GLA_K_WIDTH), F32),
        jax.ShapeDtypeStruct((n_tok, GLA_K_WIDTH), F32),
    ]
    out_specs = [pl.BlockSpec((tm, s.shape[1]), row) for s in out_shapes]
    in_specs = [
        pl.BlockSpec((tm, D_MODEL), row),
        pl.BlockSpec((1, 1, N_MOD * D_MODEL), _mod_index_map(is_ctx)),
        pl.BlockSpec((1, D_MODEL), const),
        pl.BlockSpec((1,) + w_in_b.shape[1:], lambda i: (layer, 0, 0)),
        pl.BlockSpec((NA_WIDTH, NA_WIDTH), const),
        pl.BlockSpec((1, NA_WIDTH), const),
        pl.BlockSpec((1, NA_WIDTH), const),
        pl.BlockSpec(wgate.shape, const),
        pl.BlockSpec((1, 2 * GLA_K_WIDTH), const),
    ]
    args = [x, mod_l, g_attn_l, w_in_b, seg, gq_t, gk_t, wgate, bgate]
    aliases = {}
    if is_ctx:
        kv_spec = pl.BlockSpec((tm // SEQ, 1, NA_HEADS, SEQ, NA_HEAD_DIM), lambda i: (i, layer, 0, 0, 0))
        aliases = {len(args): len(out_shapes), len(args) + 1: len(out_shapes) + 1}
        in_specs += [pl.BlockSpec(memory_space=pl.ANY)] * 2
        args += list(new_kv)
        out_shapes += [jax.ShapeDtypeStruct(new_kv[0].shape, F32)] * 2
        out_specs += [kv_spec, kv_spec]
    return pl.pallas_call(
        functools.partial(_proj_kernel, is_ctx),
        grid=(n_tok // tm,),
        in_specs=in_specs,
        out_specs=out_specs,
        out_shape=out_shapes,
        input_output_aliases=aliases,
        compiler_params=_params(1),
        name="proj_ctx" if is_ctx else "proj_smp",
    )(*args)


def _ctx_attn_kernel(q_ref, k_ref, v_ref, o_ref):
    lane = lax.broadcasted_iota(jnp.int32, (1, 2 * NA_HEAD_DIM), 1)
    first = lane < NA_HEAD_DIM
    pairs = [slice(hp * 2 * NA_HEAD_DIM, (hp + 1) * 2 * NA_HEAD_DIM) for hp in range(NA_HEADS // 2)]
    scores = []
    for cols in pairs:
        q2 = q_ref[:, cols]
        k2 = k_ref[:, cols]
        scores.append(_dot_nt(jnp.where(first, q2, jnp.zeros_like(q2)), k2))
        scores.append(_dot_nt(jnp.where(first, jnp.zeros_like(q2), q2), k2))
    probs = []
    for s in scores:
        p = jnp.exp(s - jnp.max(s, axis=-1, keepdims=True))
        probs.append((p.astype(BF16), jnp.sum(p, axis=-1, keepdims=True)))
    for hp, cols in enumerate(pairs):
        v2 = v_ref[:, cols]
        (p0, l0), (p1, l1) = probs[2 * hp], probs[2 * hp + 1]
        o_ref[:, cols] = jnp.where(first, _dot(p0, v2) / l0, _dot(p1, v2) / l1).astype(BF16)


def _ctx_attn_call(q, k, v):
    spec = pl.BlockSpec((SEQ, NA_WIDTH), lambda b: (b, 0))
    return pl.pallas_call(
        _ctx_attn_kernel,
        grid=(BATCH,),
        in_specs=[spec, spec, spec],
        out_specs=spec,
        out_shape=jax.ShapeDtypeStruct((N_CTX, NA_WIDTH), BF16),
        compiler_params=_params(1),
        name="ctx_attn",
    )(q, k, v)


def _na_window_start(r):
    return min(max(r - NA_WIN_ROWS // 2, 0), GRID_ROWS - NA_WIN_ROWS)


def _na_row_groups():
    groups = []
    for g in range(GRID_ROWS // NA_ROW_GROUP):
        starts = [_na_window_start(r) for r in range(g * NA_ROW_GROUP, (g + 1) * NA_ROW_GROUP)]
        lo = min(starts) // 2 * 2
        count = max(starts) + NA_WIN_ROWS - lo
        groups.append((lo, count + count % 2))
    return groups


def _na_bias_tiles(rpb_ref, hh, tiles_scr):
    n_dr = 2 * NA_WIN_ROWS - 1
    lanes = 2 * GRID_W
    lane = lax.broadcasted_iota(jnp.int32, (GRID_W, lanes), 1)
    col = lax.broadcasted_iota(jnp.int32, (GRID_W, lanes), 0)
    first = lane < GRID_W
    c2 = lane % GRID_W
    cs = jnp.clip(col - NA_WIN_COLS // 2, 0, GRID_W - NA_WIN_COLS)
    valid = (c2 >= cs) & (c2 < cs + NA_WIN_COLS)
    rows = rpb_ref[0, hh]
    base = lanes - (NA_WIN_COLS - 1)

    def skew(dr, shift):
        row = jnp.broadcast_to(rows[dr:dr + 1], (GRID_W, lanes))
        return pltpu.roll(row, shift, axis=1, stride=1, stride_axis=0)

    for e in range(n_dr - 1):
        tile = jnp.where(first, skew(e, base), skew(e + 1, (base + GRID_W) % lanes))
        tiles_scr[e] = jnp.where(valid, tile, MASK_VALUE)


def _na_group_bias(tiles_scr, g, key_lo, key_count, first):
    masked = jnp.full((GRID_W, 2 * GRID_W), MASK_VALUE, F32)
    row_tiles = []
    for r in range(g * NA_ROW_GROUP, (g + 1) * NA_ROW_GROUP):
        rs = _na_window_start(r)
        pieces = []
        for u in range(key_count // 2):
            r0 = key_lo + 2 * u
            in0 = rs <= r0 < rs + NA_WIN_ROWS
            in1 = rs <= r0 + 1 < rs + NA_WIN_ROWS
            if not (in0 or in1):
                pieces.append(masked)
                continue
            tile = tiles_scr[r0 - r + NA_WIN_ROWS - 1]
            if not in1:
                tile = jnp.where(first, tile, MASK_VALUE)
            elif not in0:
                tile = jnp.where(first, MASK_VALUE, tile)
            pieces.append(tile)
        row_tiles.append(jnp.concatenate(pieces, axis=1))
    return jnp.concatenate(row_tiles, axis=0)


def _na_kernel(q_ref, k_ref, v_ref, kc_ref, vc_ref, rpb_ref, o_ref, sc_ref, tiles_scr):
    lane = lax.broadcasted_iota(jnp.int32, (1, 2 * NA_HEAD_DIM), 1)
    first = lane < NA_HEAD_DIM
    q2 = q_ref[...]
    kc2 = jnp.concatenate([kc_ref[0, 0, 0], kc_ref[0, 0, 1]], axis=0).astype(BF16)
    vc2 = jnp.concatenate([vc_ref[0, 0, 0], vc_ref[0, 0, 1]], axis=0).astype(BF16)
    block = NA_ROW_GROUP * GRID_W
    outs = []
    for hh in range(2):
        qm = jnp.where(first, q2, jnp.zeros_like(q2)) if hh == 0 else jnp.where(first, jnp.zeros_like(q2), q2)
        sc_ref[...] = _dot(qm, kc2)
        _na_bias_tiles(rpb_ref, hh, tiles_scr)
        blocks = []
        for g, (key_lo, key_count) in enumerate(_na_row_groups()):
            keys = slice(key_lo * GRID_W, (key_lo + key_count) * GRID_W)
            s = _dot_nt(qm[g * block:(g + 1) * block], k_ref[keys, :])
            s = s + _na_group_bias(tiles_scr, g, key_lo, key_count, first)
            sc = sc_ref[g * block:(g + 1) * block, :]
            m = jnp.maximum(jnp.max(s, axis=-1, keepdims=True), jnp.max(sc, axis=-1, keepdims=True))
            p = jnp.exp(s - m)
            pc = jnp.exp(sc - m)
            l = jnp.sum(p, axis=-1, keepdims=True) + jnp.sum(pc, axis=-1, keepdims=True)
            blocks.append((_dot(p.astype(BF16), v_ref[keys, :]) + _dot_nt(pc.astype(BF16), vc2)) / l)
        outs.append(jnp.concatenate(blocks, axis=0))
    o_ref[...] = jnp.where(first, outs[0], outs[1]).astype(BF16)


def _na_call(q, k, v, cache_kt, cache_vt, rpb_rows, layer):
    tok = pl.BlockSpec((DEC_SEQ, 2 * NA_HEAD_DIM), lambda b, hp: (b, hp))
    cache = pl.BlockSpec((1, 1, 2, NA_HEAD_DIM, PAST_LEN), lambda b, hp: (b, layer, hp, 0, 0))
    n_pairs = 2 * NA_WIN_ROWS - 2
    return pl.pallas_call(
        _na_kernel,
        grid=(DEC_BATCH, NA_HEADS // 2),
        in_specs=[
            tok, tok, tok, cache, cache,
            pl.BlockSpec((1, 2) + rpb_rows.shape[2:], lambda b, hp: (layer, hp, 0, 0)),
        ],
        out_specs=tok,
        out_shape=jax.ShapeDtypeStruct((N_SMP, NA_WIDTH), BF16),
        scratch_shapes=[pltpu.VMEM((DEC_SEQ, PAST_LEN), F32),
                        pltpu.VMEM((n_pairs, GRID_W, 2 * GRID_W), F32)],
        compiler_params=_params(2),
        name="nbr_attn",
    )(q, k, v, cache_kt, cache_vt, rpb_rows)


def _gla_constants():
    C = GLA_CHUNK
    ii, jj = np.meshgrid(np.arange(C), np.arange(C), indexing="ij")
    tri = np.stack([jj <= ii, jj >= ii]).astype(np.float32)
    w = np.concatenate([tri, tri], axis=-1)

    x = ii ^ jj
    hb = np.where(x > 0, np.floor(np.log2(np.maximum(x, 1))), -1).astype(np.int64)
    masks = np.zeros((2, GLA_LEVELS + 1, C, C), np.float32)
    for p in range(GLA_LEVELS):
        masks[0, p] = (jj < ii) & (hb == p)
        masks[1, p] = (jj > ii) & (hb == p)
    masks[:, GLA_LEVELS] = np.eye(C)
    masks = np.tile(masks, (1, 1, 2, 2))
    state_mask = (np.arange(2 * GLA_DV)[:, None] // GLA_DV == np.arange(2 * GLA_DK)[None, :] // GLA_DK)
    upper = np.stack([(np.arange(C) >> p) & 1 for p in range(GLA_LEVELS)]).astype(bool)
    sign = np.stack([np.where(upper, 1.0, -1.0), np.where(upper, -1.0, 1.0)])
    scales = np.broadcast_to((sign * np.log2(np.e))[..., None], (2, GLA_LEVELS, C, GLA_K_WIDTH))
    return w, masks, state_mask.astype(np.float32), np.ascontiguousarray(scales, np.float32)


def _gla_kernel(seq_len, n_seq, is_ctx, *refs):
    refs = list(refs)
    gq_ref, gk_ref, gv_ref, lf_ref, lb_ref, w_ref, mask_ref, smask_ref, scale_ref = refs[:9]
    refs = refs[9:]
    if is_ctx:
        refs = refs[2:]
        o_ref, sf_ref, sb_ref = refs[:3]
        refs = refs[3:]
    else:
        cos_ref, sin_ref, s0f_ref, s0b_ref, o_ref = refs[:5]
        refs = refs[5:]
    q_scr, k_scr, of_scr, ob_scr, st_scr, cum_scr = refs

    C = GLA_CHUNK
    n_chunks = seq_len // C
    n_pairs = GLA_HEADS // 2
    pair_k = 2 * GLA_DK
    pair_v = 2 * GLA_DV

    q = gq_ref[...]
    k = gk_ref[...]
    if not is_ctx:
        lane = lax.broadcasted_iota(jnp.int32, (1, GLA_K_WIDTH), 1)
        quarter = GLA_DK // 4
        low = (lane % (2 * quarter)) < quarter
        cos = jnp.concatenate([cos_ref[...]] * n_seq, axis=0)
        sin = jnp.concatenate([sin_ref[...]] * n_seq, axis=0)

        def rope(t):
            partner = jnp.where(low, pltpu.roll(t, GLA_K_WIDTH - quarter, axis=1), pltpu.roll(t, quarter, axis=1))
            return t * cos + partner * sin

        q = rope(q)
        k = rope(k)
    q_scr[...] = q
    k_scr[...] = k

    st_scr[...] = jnp.zeros(st_scr.shape, F32)
    if not is_ctx:
        for d, s0_ref in enumerate((s0f_ref, s0b_ref)):
            for s in range(n_seq):
                for h in range(GLA_HEADS):
                    pr, e = divmod(h, 2)
                    st_scr[s, d, pr, e * GLA_DV:(e + 1) * GLA_DV, e * GLA_DK:(e + 1) * GLA_DK] = s0_ref[s, 0, h].T

    klane = lax.broadcasted_iota(jnp.int32, (1, GLA_K_WIDTH), 1)
    even_head = (klane // GLA_DK) % 2 == 0

    def chunk_decay(s, d, r0, f_ref):
        rows = pl.ds(s * seq_len + r0, C)
        f = f_ref[rows, :]
        f_hi = f.astype(BF16)
        f_lo = (f - f_hi.astype(F32)).astype(BF16)
        cum = _dot(w_ref[d], jnp.concatenate([f_hi, f_lo], axis=0))
        cum_scr[s, d] = cum
        return rows, cum

    def chunk_scores(s, d, rows, cum):
        def cum_rows(r, n):
            return jnp.broadcast_to(cum_scr[s, d, r:r + 1, :], (n, GLA_K_WIDTH))

        def level_factor(p):
            m = 1 << p
            edge = m - 1 if d == 0 else m
            if 2 * m >= 8:
                ref = jnp.concatenate([cum_rows(blk + edge, 2 * m) for blk in range(0, C, 2 * m)], axis=0)
            elif p == 1:
                sub = lax.broadcasted_iota(jnp.int32, (8, GLA_K_WIDTH), 0)
                ref = jnp.concatenate([jnp.where(sub < 4, cum_rows(blk + edge, 8), cum_rows(blk + 4 + edge, 8))
                                       for blk in range(0, C, 8)], axis=0)
            else:
                odd = lax.broadcasted_iota(jnp.int32, (C, GLA_K_WIDTH), 0) % 2 == 1
                if d == 0:
                    ref = jnp.where(odd, pltpu.roll(cum, 1, axis=0), cum)
                else:
                    ref = jnp.where(odd, cum, pltpu.roll(cum, C - 1, axis=0))
            return jnp.exp2((cum - ref) * scale_ref[d, p])

        qc = q_scr[rows, :]
        kc = k_scr[rows, :]
        vc = gv_ref[rows, :]
        last = C - 1 if d == 0 else 0
        eq = jnp.exp(cum)
        ek = jnp.exp(cum_rows(last, C) - cum)
        total = jnp.exp(cum_scr[s, d, last:last + 1, :])

        a = None
        for p in range(GLA_LEVELS + 1):
            if p < GLA_LEVELS:
                ep = level_factor(p)
                qe = (qc * ep).astype(BF16)
                ke = (kc * ep).astype(BF16)
            else:
                qe = qc.astype(BF16)
                ke = kc.astype(BF16)
            zero = jnp.zeros((C, pair_k), BF16)
            lhs = jnp.concatenate([jnp.concatenate([qe[:, :pair_k], zero], axis=1),
                                   jnp.concatenate([zero, qe[:, pair_k:]], axis=1)], axis=0)
            rhs = jnp.concatenate([jnp.where(even_head, ke, jnp.zeros_like(ke)),
                                   jnp.where(even_head, jnp.zeros_like(ke), ke)], axis=0)
            sc = _dot_nt(lhs, rhs) * mask_ref[d, p]
            a = sc if a is None else a + sc
        q_in = (qc * eq).astype(BF16)
        k_out = (kc * ek).astype(BF16)
        return a.astype(BF16), q_in, k_out, vc, total

    def chunk_output(s, d, rows, o_scr, a, q_in, k_out, vc, total):
        zero_v = jnp.zeros((C, GLA_DV), BF16)
        for pr in range(n_pairs):
            kl = slice(pr * pair_k, (pr + 1) * pair_k)
            vp = vc[:, pr * pair_v:(pr + 1) * pair_v]
            v_diag = jnp.concatenate([jnp.concatenate([vp[:, :GLA_DV], zero_v], axis=1),
                                      jnp.concatenate([zero_v, vp[:, GLA_DV:]], axis=1)], axis=0)
            st = st_scr[s, d, pr]
            o_scr[rows, pr * pair_v:(pr + 1) * pair_v] = (
                _dot(a[pr * C:(pr + 1) * C], v_diag) + _dot_nt(q_in[:, kl], st.astype(BF16)))
            st_scr[s, d, pr] = st * total[:, kl] + _dot_tn(vp, k_out[:, kl]) * smask_ref[...]

    def body(n, carry):
        fwd = pl.multiple_of(n * C, C)
        bwd = pl.multiple_of((n_chunks - 1 - n) * C, C)
        scans = [(s, d) for s in range(n_seq) for d in range(2)]
        decays = [chunk_decay(s, d, bwd if d else fwd, lb_ref if d else lf_ref) for s, d in scans]
        scores = [chunk_scores(s, d, *dec) for (s, d), dec in zip(scans, decays)]
        for (s, d), (rows, _), sc in zip(scans, decays, scores):
            chunk_output(s, d, rows, ob_scr if d else of_scr, *sc)
        return carry

    lax.fori_loop(0, n_chunks, body, 0)
    o_ref[...] = of_scr[...] + ob_scr[...]
    if is_ctx:
        for d, s_ref in enumerate((sf_ref, sb_ref)):
            for s in range(n_seq):
                for h in range(GLA_HEADS):
                    pr, e = divmod(h, 2)
                    s_ref[s, 0, h] = st_scr[s, d, pr, e * GLA_DV:(e + 1) * GLA_DV, e * GLA_DK:(e + 1) * GLA_DK].T


def _gla_call(gq, gk, gv, lf, lb, consts, new_states=None, rope=None, state_in=None, layer=0):
    is_ctx = new_states is not None
    seq_len = SEQ if is_ctx else DEC_SEQ
    n_tok = gq.shape[0]
    n_seq = GLA_SEQS_PER_STEP
    rows = n_seq * seq_len
    tok = lambda width: pl.BlockSpec((rows, width), lambda b: (b, 0))
    whole = lambda a: pl.BlockSpec(a.shape, lambda b: (0,) * a.ndim)
    state_spec = pl.BlockSpec((n_seq, 1, GLA_HEADS, GLA_DK, GLA_DV), lambda b: (b, layer, 0, 0, 0))
    in_specs = [tok(GLA_K_WIDTH), tok(GLA_K_WIDTH), tok(GLA_V_WIDTH), tok(GLA_K_WIDTH), tok(GLA_K_WIDTH)]
    in_specs += [whole(a) for a in consts]
    args = [gq, gk, gv, lf, lb, *consts]
    out_shape = [jax.ShapeDtypeStruct((n_tok, GLA_V_WIDTH), F32)]
    out_specs = [tok(GLA_V_WIDTH)]
    aliases = {}
    if is_ctx:
        aliases = {len(args): 1, len(args) + 1: 2}
        in_specs += [pl.BlockSpec(memory_space=pl.ANY)] * 2
        args += list(new_states)
        out_shape += [jax.ShapeDtypeStruct(new_states[0].shape, F32)] * 2
        out_specs += [state_spec, state_spec]
    else:
        in_specs += [pl.BlockSpec((seq_len, GLA_K_WIDTH), lambda b: (0, 0))] * 2 + [state_spec, state_spec]
        args += list(rope) + list(state_in)
    return pl.pallas_call(
        functools.partial(_gla_kernel, seq_len, n_seq, is_ctx),
        grid=(n_tok // rows,),
        in_specs=in_specs,
        out_specs=out_specs,
        out_shape=out_shape,
        input_output_aliases=aliases,
        scratch_shapes=[
            pltpu.VMEM((rows, GLA_K_WIDTH), F32),
            pltpu.VMEM((rows, GLA_K_WIDTH), F32),
            pltpu.VMEM((rows, GLA_V_WIDTH), F32),
            pltpu.VMEM((rows, GLA_V_WIDTH), F32),
            pltpu.VMEM((n_seq, 2, GLA_HEADS // 2, 2 * GLA_DV, 2 * GLA_DK), F32),
            pltpu.VMEM((n_seq, 2, GLA_CHUNK, GLA_K_WIDTH), F32),
        ],
        compiler_params=_params(1),
        name="gla_ctx" if is_ctx else "gla_smp",
    )(*args)


def _rope_tables():
    quarter = GLA_DK // 4
    inv = ROPE_BASE ** (-jnp.arange(quarter, dtype=F32) / quarter)
    t = jnp.arange(DEC_SEQ)
    ang_r = (t // GRID_W).astype(F32)[:, None] * inv
    ang_c = (t % GRID_W).astype(F32)[:, None] * inv
    cos = jnp.concatenate([jnp.cos(ang_r)] * 2 + [jnp.cos(ang_c)] * 2, axis=-1)
    sin = jnp.concatenate([-jnp.sin(ang_r), jnp.sin(ang_r), -jnp.sin(ang_c), jnp.sin(ang_c)], axis=-1)
    return jnp.tile(cos, (1, GLA_HEADS)), jnp.tile(sin, (1, GLA_HEADS))


def _merge_mlp_kernel(layer, x_ref, mod_ref, ona_ref, ogla_ref, gate_ref, gout_ref, gmlp_ref,
                      wo_hbm, wup_hbm, wdown_hbm, o_ref,
                      wo_scr, wup_scr, wdown_scr, stage_cols, stage_rows, sem):
    first_step = pl.program_id(0) == 0
    n_chunks = D_FF // FF_CHUNK
    wo_halves = D_MODEL // FF_CHUNK

    def col_copy(j):
        slot = j % 2
        if j < wo_halves:
            src = wo_hbm.at[layer, :, j * FF_CHUNK:(j + 1) * FF_CHUNK]
        else:
            c = j - wo_halves
            src = wup_hbm.at[layer, :, c * FF_CHUNK:(c + 1) * FF_CHUNK]
        return pltpu.make_async_copy(src, stage_cols.at[slot], sem.at[slot])

    def row_copy(c):
        slot = c % 2
        src = wdown_hbm.at[layer, c * FF_CHUNK:(c + 1) * FF_CHUNK, :]
        return pltpu.make_async_copy(src, stage_rows.at[slot], sem.at[2 + slot])

    n_col_blocks = wo_halves + n_chunks

    def land_col(j):
        col_copy(j).wait()
        block = stage_cols[j % 2].astype(BF16)
        if j < wo_halves:
            wo_scr[:, j * FF_CHUNK:(j + 1) * FF_CHUNK] = block
        else:
            c = j - wo_halves
            wup_scr[:, c * FF_CHUNK:(c + 1) * FF_CHUNK] = block
        if j + 2 < n_col_blocks:
            col_copy(j + 2).start()

    def land_row(c):
        row_copy(c).wait()
        wdown_scr[c * FF_CHUNK:(c + 1) * FF_CHUNK, :] = stage_rows[c % 2].astype(BF16)
        if c + 2 < n_chunks:
            row_copy(c + 2).start()

    def run(streaming):
        if streaming:
            col_copy(0).start()
            col_copy(1).start()
            row_copy(0).start()
            row_copy(1).start()

        mod = mod_ref[0]
        ga1 = mod[:, 2 * D_MODEL:3 * D_MODEL]
        sh2 = mod[:, 3 * D_MODEL:4 * D_MODEL]
        sc2 = mod[:, 4 * D_MODEL:5 * D_MODEL]
        ga2 = mod[:, 5 * D_MODEL:6 * D_MODEL]

        og = ogla_ref[...]
        normed = []
        for h in range(GLA_HEADS):
            oh = og[:, h * GLA_DV:(h + 1) * GLA_DV]
            normed.append(oh * lax.rsqrt(jnp.mean(oh * oh, axis=-1, keepdims=True) + EPS))
        gate = gate_ref[...]
        g = jnp.concatenate(normed, axis=1) * gout_ref[...] * (gate * (1.0 / (1.0 + jnp.exp(-gate))))

        if streaming:
            for j in range(wo_halves):
                land_col(j)
        att = _dot(ona_ref[...], wo_scr[0:NA_WIDTH, :]) + _dot(g.astype(BF16), wo_scr[NA_WIDTH:, :])
        x = x_ref[...] + ga1 * att
        ms = jnp.mean(x * x, axis=-1, keepdims=True)
        h2 = ((x * lax.rsqrt(ms + EPS) * gmlp_ref[...]) * (1.0 + sc2) + sh2).astype(BF16)

        width = FF_CHUNK if streaming else MLP_CHUNK
        acc = None
        for c in range(D_FF // width):
            if streaming:
                land_col(wo_halves + c)
                land_row(c)
            u = jnp.maximum(_dot(h2, wup_scr[:, c * width:(c + 1) * width]), 0.0)
            down = _dot((u * u).astype(BF16), wdown_scr[c * width:(c + 1) * width, :])
            acc = down if acc is None else acc + down
        o_ref[...] = x + ga2 * acc

    pl.when(first_step)(functools.partial(run, True))
    pl.when(jnp.logical_not(first_step))(functools.partial(run, False))


def _merge_mlp_call(x, mod_l, o_na, o_gla, gate, gout_t, g_mlp_l, w_o, w_up, w_down, layer):
    tm = TOKEN_TILE
    n_tok = x.shape[0]
    is_ctx = n_tok == N_CTX
    row = lambda i: (i, 0)
    const = lambda i: (0, 0)
    in_hbm = pl.BlockSpec(memory_space=pl.ANY)
    return pl.pallas_call(
        functools.partial(_merge_mlp_kernel, layer),
        grid=(n_tok // tm,),
        in_specs=[
            pl.BlockSpec((tm, D_MODEL), row),
            pl.BlockSpec((1, 1, N_MOD * D_MODEL), _mod_index_map(is_ctx)),
            pl.BlockSpec((tm, NA_WIDTH), row),
            pl.BlockSpec((tm, GLA_V_WIDTH), row),
            pl.BlockSpec((tm, GLA_V_WIDTH), row),
            pl.BlockSpec((1, GLA_V_WIDTH), const),
            pl.BlockSpec((1, D_MODEL), const),
            in_hbm, in_hbm, in_hbm,
        ],
        out_specs=pl.BlockSpec((tm, D_MODEL), row),
        out_shape=jax.ShapeDtypeStruct((n_tok, D_MODEL), F32),
        scratch_shapes=[
            pltpu.VMEM((D_MODEL, D_MODEL), BF16),
            pltpu.VMEM((D_MODEL, D_FF), BF16),
            pltpu.VMEM((D_FF, D_MODEL), BF16),
            pltpu.VMEM((2, D_MODEL, FF_CHUNK), F32),
            pltpu.VMEM((2, FF_CHUNK, D_MODEL), F32),
            pltpu.SemaphoreType.DMA((4,)),
        ],
        compiler_params=_params(1),
        name="merge_mlp_ctx" if is_ctx else "merge_mlp_smp",
    )(x, mod_l, o_na, o_gla, gate, gout_t, g_mlp_l, w_o, w_up, w_down)


def kernel(x_prompt, x_sample, cache_k, cache_v, state_fwd, state_bwd, c, c_ctx, w_ada, b_ada, g_attn, w_in,
           g_q, g_k, rpb, w_gf, b_gf, w_gb, b_gb, g_gla_out, w_o, g_mlp, w_up, w_down):
    x_c = x_prompt.reshape(N_CTX, D_MODEL)
    x_s = x_sample.reshape(N_SMP, D_MODEL)

    c_rows = jnp.concatenate([c_ctx[None, :], c, jnp.zeros((MOD_ROWS - 1 - DEC_BATCH, D_MODEL), F32)], axis=0)
    mods = _ada_call(c_rows, w_ada, b_ada).reshape(DEPTH, MOD_ROWS, 1, N_MOD * D_MODEL)

    w_in_b = w_in.astype(BF16)
    wgate = jnp.zeros((DEPTH, 2 * GLA_GATE_RANK, 2 * GLA_K_WIDTH), F32)
    wgate = wgate.at[:, :GLA_GATE_RANK, :GLA_K_WIDTH].set(w_gf)
    wgate = wgate.at[:, GLA_GATE_RANK:, GLA_K_WIDTH:].set(w_gb).astype(BF16)
    bgate = jnp.concatenate([b_gf, b_gb], axis=-1).reshape(DEPTH, 1, 2 * GLA_K_WIDTH)
    head_of = np.arange(NA_WIDTH) // NA_HEAD_DIM
    seg = jnp.asarray((head_of[:, None] == head_of[None, :]).astype(np.float32) / NA_HEAD_DIM, BF16)
    gq_t = jnp.tile(g_q, (1, NA_HEADS)).reshape(DEPTH, 1, NA_WIDTH)
    gk_t = jnp.tile(g_k, (1, NA_HEADS)).reshape(DEPTH, 1, NA_WIDTH)
    gout_t = jnp.tile(g_gla_out, (1, GLA_HEADS)).reshape(DEPTH, 1, GLA_V_WIDTH)
    n_dr, n_dc = rpb.shape[2:]
    rpb_rows = jnp.pad(rpb, ((0, 0), (0, 0), (0, -n_dr % 8), (0, 2 * GRID_W - n_dc)))
    gla_w_np, gla_masks_np, gla_smask_np, gla_scales_np = _gla_constants()
    gla_consts = (jnp.asarray(gla_w_np, BF16), jnp.asarray(gla_masks_np, F32), jnp.asarray(gla_smask_np, F32),
                  jnp.asarray(gla_scales_np, F32))
    rope = _rope_tables()

    cache_kt = jnp.swapaxes(cache_k, 3, 4)
    cache_vt = jnp.swapaxes(cache_v, 3, 4)
    new_k = jnp.zeros((BATCH, DEPTH, NA_HEADS, SEQ, NA_HEAD_DIM), F32)
    new_v = jnp.zeros((BATCH, DEPTH, NA_HEADS, SEQ, NA_HEAD_DIM), F32)
    new_sf = jnp.zeros((BATCH, DEPTH, GLA_HEADS, GLA_DK, GLA_DV), F32)
    new_sb = jnp.zeros((BATCH, DEPTH, GLA_HEADS, GLA_DK, GLA_DV), F32)

    for l in range(DEPTH):
        proj_w = (g_attn[l].reshape(1, D_MODEL), w_in_b, seg, gq_t[l], gk_t[l], wgate[l], bgate[l])
        mlp_w = (gout_t[l], g_mlp[l].reshape(1, D_MODEL), w_o, w_up, w_down)

        q, k, v, gq, gk, gv, gate, lf, lb, new_k, new_v = _proj_call(
            x_c, mods[l], *proj_w, new_kv=(new_k, new_v), layer=l)
        o_na = _ctx_attn_call(q, k, v)
        o_gla, new_sf, new_sb = _gla_call(gq, gk, gv, lf, lb, gla_consts, new_states=(new_sf, new_sb), layer=l)
        x_c = _merge_mlp_call(x_c, mods[l], o_na, o_gla, gate, *mlp_w, layer=l)

        q, k, v, gq, gk, gv, gate, lf, lb = _proj_call(x_s, mods[l], *proj_w, layer=l)
        o_na = _na_call(q, k, v, cache_kt, cache_vt, rpb_rows, l)
        (o_gla,) = _gla_call(gq, gk, gv, lf, lb, gla_consts, rope=rope,
                             state_in=(state_fwd, state_bwd), layer=l)
        x_s = _merge_mlp_call(x_s, mods[l], o_na, o_gla, gate, *mlp_w, layer=l)

    return (x_c.reshape(BATCH, SEQ, D_MODEL), x_s.reshape(DEC_BATCH, DEC_SEQ, D_MODEL),
            new_k, new_v, new_sf, new_sb)
```

```python
import functools

import numpy as np
import jax
import jax.numpy as jnp
from jax import lax
from jax.experimental import pallas as pl
from jax.experimental.pallas import tpu as pltpu

F32 = jnp.float32
BF16 = jnp.bfloat16

D_MODEL = 1024
BATCH = 16
SEQ = 256
DEPTH = 4
DEC_BATCH = 2
DEC_SEQ = 1024
PAST_LEN = 512
GRID_W = 64
GRID_ROWS = DEC_SEQ // GRID_W
NA_WIDTH = D_MODEL // 2
NA_HEAD_DIM = 64
NA_HEADS = NA_WIDTH // NA_HEAD_DIM
NA_WIN_ROWS = 8
NA_WIN_COLS = 16
GLA_V_WIDTH = D_MODEL - NA_WIDTH
GLA_HEADS = 4
GLA_DV = GLA_V_WIDTH // GLA_HEADS
GLA_DK = GLA_DV // 2
GLA_K_WIDTH = GLA_HEADS * GLA_DK
GLA_GATE_RANK = 16
GLA_GATE_TAU = 16.0
GLA_CHUNK = 64
D_FF = 4 * D_MODEL
ROPE_BASE = 10000.0
N_MOD = 6
EPS = 1e-6

IN_WIDTH = 3 * NA_WIDTH + 2 * GLA_K_WIDTH + 2 * GLA_V_WIDTH + 2 * GLA_GATE_RANK
PROJ_WIDE = IN_WIDTH - 2 * GLA_GATE_RANK

N_CTX = BATCH * SEQ
N_SMP = DEC_BATCH * DEC_SEQ
MOD_ROWS = 8
MASK_VALUE = -1e30

TOKEN_TILE = 512
PROJ_BLOCK = 512
FF_CHUNK = 512
MLP_CHUNK = 1024
ADA_TILE_N = 1536
GLA_LEVELS = 6
NA_ROW_GROUP = 4
GLA_SEQS_PER_STEP = 2
VMEM_LIMIT = 52 * 1024 * 1024


def _dot(a, b):
    return jnp.dot(a, b, preferred_element_type=F32)


def _dot_nt(a, b):
    return lax.dot_general(a, b, (((1,), (1,)), ((), ())), preferred_element_type=F32)


def _dot_tn(a, b):
    return lax.dot_general(a, b, (((0,), (0,)), ((), ())), preferred_element_type=F32)


def _params(n_grid_dims=1):
    return pltpu.CompilerParams(dimension_semantics=("arbitrary",) * n_grid_dims,
                                vmem_limit_bytes=VMEM_LIMIT)


def _mod_index_map(is_ctx):
    if is_ctx:
        return lambda i: (0, 0, 0)
    return lambda i: (1 + i // (DEC_SEQ // TOKEN_TILE), 0, 0)


def _ada_kernel(c_ref, w_ref, b_ref, o_ref):
    cv = c_ref[...]
    s = cv * (1.0 / (1.0 + jnp.exp(-cv)))
    o_ref[0] = _dot(s.astype(BF16), w_ref[0].astype(BF16)) + b_ref[0]


def _ada_call(c_rows, w_ada, b_ada):
    n_mod = N_MOD * D_MODEL
    return pl.pallas_call(
        _ada_kernel,
        grid=(DEPTH, n_mod // ADA_TILE_N),
        in_specs=[
            pl.BlockSpec((MOD_ROWS, D_MODEL), lambda l, j: (0, 0)),
            pl.BlockSpec((1, D_MODEL, ADA_TILE_N), lambda l, j: (l, 0, j)),
            pl.BlockSpec((1, 1, ADA_TILE_N), lambda l, j: (l, 0, j)),
        ],
        out_specs=pl.BlockSpec((1, MOD_ROWS, ADA_TILE_N), lambda l, j: (l, 0, j)),
        out_shape=jax.ShapeDtypeStruct((DEPTH, MOD_ROWS, n_mod), F32),
        compiler_params=_params(2),
        name="ada_mod",
    )(c_rows, w_ada, b_ada.reshape(DEPTH, 1, n_mod))


def _log_sigmoid(x):
    return jnp.minimum(x, 0.0) - jnp.log(1.0 + jnp.exp(-jnp.abs(x)))


def _proj_kernel(is_ctx, layer, x_ref, mod_ref, g_ref, seg_ref, gq_ref, gk_ref, wgate_ref, bgate_ref, win_hbm,
                 *rest):
    if is_ctx:
        rest = rest[2:]
    q_out, k_out, v_out, gq_out, gk_out, gv_out, gate_out, lf_out, lb_out = rest[:9]
    rest = rest[9:]
    if is_ctx:
        knew_out, vnew_out = rest[:2]
        rest = rest[2:]
    w_scr, stage, tail_stage, sem = rest

    n_blocks = PROJ_WIDE // PROJ_BLOCK
    tail = slice(PROJ_WIDE, IN_WIDTH)
    first_step = pl.program_id(0) == 0

    def block_copy(j):
        src = win_hbm.at[layer, j * PROJ_BLOCK:(j + 1) * PROJ_BLOCK, :]
        return pltpu.make_async_copy(src, stage.at[j % 2], sem.at[j % 2])

    def tail_copy():
        return pltpu.make_async_copy(win_hbm.at[layer, tail, :], tail_stage, sem.at[2])

    def land(j):
        block_copy(j).wait()
        w_scr[j * PROJ_BLOCK:(j + 1) * PROJ_BLOCK, :] = stage[j % 2].astype(BF16)
        if j + 2 < n_blocks:
            block_copy(j + 2).start()

    def run(streaming):
        if streaming:
            block_copy(0).start()
            block_copy(1).start()
            tail_copy().start()

        x = x_ref[...]
        mod = mod_ref[0]
        sh1 = mod[:, 0:D_MODEL]
        sc1 = mod[:, D_MODEL:2 * D_MODEL]
        ms = jnp.mean(x * x, axis=-1, keepdims=True)
        h = (x * lax.rsqrt(ms + EPS) * g_ref[...]) * (1.0 + sc1) + sh1
        hb = h.astype(BF16)

        def projected(j):
            if streaming:
                land(j)
            return _dot_nt(hb, w_scr[j * PROJ_BLOCK:(j + 1) * PROJ_BLOCK, :])

        seg = seg_ref[...]
        q = projected(0)
        q_ms = _dot((q * q).astype(BF16), seg)
        q_out[...] = (q * lax.rsqrt(q_ms + EPS) * gq_ref[...] * (NA_HEAD_DIM ** -0.5)).astype(BF16)
        k = projected(1)
        k_ms = _dot((k * k).astype(BF16), seg)
        kn = k * lax.rsqrt(k_ms + EPS) * gk_ref[...]
        k_out[...] = kn.astype(BF16)
        v = projected(2)
        v_out[...] = v.astype(BF16)
        if is_ctx:
            for s in range(TOKEN_TILE // SEQ):
                for hd in range(NA_HEADS):
                    rows = slice(s * SEQ, (s + 1) * SEQ)
                    cols = slice(hd * NA_HEAD_DIM, (hd + 1) * NA_HEAD_DIM)
                    knew_out[s, 0, hd] = kn[rows, cols]
                    vnew_out[s, 0, hd] = v[rows, cols]

        z = projected(3)
        gq_out[...] = z[:, 0:GLA_K_WIDTH] * (GLA_DK ** -0.5)
        gk_out[...] = z[:, GLA_K_WIDTH:]
        gv_out[...] = projected(4).astype(BF16)
        gate_out[...] = projected(5)

        if streaming:
            tail_copy().wait()
            w_scr[tail, :] = tail_stage[...].astype(BF16)
        zz = _dot_nt(hb, w_scr[tail, :])
        pre = _dot(zz.astype(BF16), wgate_ref[...]) + bgate_ref[...]
        ls = _log_sigmoid(pre) * (1.0 / GLA_GATE_TAU)
        lf_out[...] = ls[:, 0:GLA_K_WIDTH]
        lb_out[...] = ls[:, GLA_K_WIDTH:]

    pl.when(first_step)(functools.partial(run, True))
    pl.when(jnp.logical_not(first_step))(functools.partial(run, False))


def _proj_call(x, mod_l, g_attn_l, w_in_t, seg, gq_t, gk_t, wgate, bgate, new_kv=None, layer=0):
    tm = TOKEN_TILE
    n_tok = x.shape[0]
    is_ctx = new_kv is not None
    row = lambda i: (i, 0)
    const = lambda i: (0, 0)
    out_shapes = [
        jax.ShapeDtypeStruct((n_tok, NA_WIDTH), BF16),
        jax.ShapeDtypeStruct((n_tok, NA_WIDTH), BF16),
        jax.ShapeDtypeStruct((n_tok, NA_WIDTH), BF16),
        jax.ShapeDtypeStruct((n_tok, GLA_K_WIDTH), F32),
        jax.ShapeDtypeStruct((n_tok, GLA_K_WIDTH), F32),
        jax.ShapeDtypeStruct((n_tok, GLA_V_WIDTH), BF16),
        jax.ShapeDtypeStruct((n_tok, GLA_V_WIDTH), F32),
        jax.ShapeDtypeStruct((n_tok, GLA_K_WIDTH), F32),
        jax.ShapeDtypeStruct((n_tok, GLA_K_WIDTH), F32),
    ]
    out_specs = [pl.BlockSpec((tm, s.shape[1]), row) for s in out_shapes]
    in_specs = [
        pl.BlockSpec((tm, D_MODEL), row),
        pl.BlockSpec((1, 1, N_MOD * D_MODEL), _mod_index_map(is_ctx)),
        pl.BlockSpec((1, D_MODEL), const),
        pl.BlockSpec((NA_WIDTH, NA_WIDTH), const),
        pl.BlockSpec((1, NA_WIDTH), const),
        pl.BlockSpec((1, NA_WIDTH), const),
        pl.BlockSpec(wgate.shape, const),
        pl.BlockSpec((1, 2 * GLA_K_WIDTH), const),
        pl.BlockSpec(memory_space=pl.ANY),
    ]
    args = [x, mod_l, g_attn_l, seg, gq_t, gk_t, wgate, bgate, w_in_t]
    aliases = {}
    if is_ctx:
        kv_spec = pl.BlockSpec((tm // SEQ, 1, NA_HEADS, SEQ, NA_HEAD_DIM), lambda i: (i, layer, 0, 0, 0))
        aliases = {len(args): len(out_shapes), len(args) + 1: len(out_shapes) + 1}
        in_specs += [pl.BlockSpec(memory_space=pl.ANY)] * 2
        args += list(new_kv)
        out_shapes += [jax.ShapeDtypeStruct(new_kv[0].shape, F32)] * 2
        out_specs += [kv_spec, kv_spec]
    return pl.pallas_call(
        functools.partial(_proj_kernel, is_ctx, layer),
        grid=(n_tok // tm,),
        in_specs=in_specs,
        out_specs=out_specs,
        out_shape=out_shapes,
        input_output_aliases=aliases,
        scratch_shapes=[
            pltpu.VMEM((IN_WIDTH, D_MODEL), BF16),
            pltpu.VMEM((2, PROJ_BLOCK, D_MODEL), F32),
            pltpu.VMEM((IN_WIDTH - PROJ_WIDE, D_MODEL), F32),
            pltpu.SemaphoreType.DMA((3,)),
        ],
        compiler_params=_params(1),
        name="proj_ctx" if is_ctx else "proj_smp",
    )(*args)


def _ctx_attn_kernel(q_ref, k_ref, v_ref, o_ref):
    lane = lax.broadcasted_iota(jnp.int32, (1, 2 * NA_HEAD_DIM), 1)
    first = lane < NA_HEAD_DIM
    pairs = [slice(hp * 2 * NA_HEAD_DIM, (hp + 1) * 2 * NA_HEAD_DIM) for hp in range(NA_HEADS // 2)]
    scores = []
    for cols in pairs:
        q2 = q_ref[:, cols]
        k2 = k_ref[:, cols]
        scores.append(_dot_nt(jnp.where(first, q2, jnp.zeros_like(q2)), k2))
        scores.append(_dot_nt(jnp.where(first, jnp.zeros_like(q2), q2), k2))
    probs = []
    for s in scores:
        p = jnp.exp(s - jnp.max(s, axis=-1, keepdims=True))
        probs.append((p.astype(BF16), jnp.sum(p, axis=-1, keepdims=True)))
    for hp, cols in enumerate(pairs):
        v2 = v_ref[:, cols]
        (p0, l0), (p1, l1) = probs[2 * hp], probs[2 * hp + 1]
        o_ref[:, cols] = jnp.where(first, _dot(p0, v2) / l0, _dot(p1, v2) / l1).astype(BF16)


def _ctx_attn_call(q, k, v):
    spec = pl.BlockSpec((SEQ, NA_WIDTH), lambda b: (b, 0))
    return pl.pallas_call(
        _ctx_attn_kernel,
        grid=(BATCH,),
        in_specs=[spec, spec, spec],
        out_specs=spec,
        out_shape=jax.ShapeDtypeStruct((N_CTX, NA_WIDTH), BF16),
        compiler_params=_params(1),
        name="ctx_attn",
    )(q, k, v)


def _na_window_start(r):
    return min(max(r - NA_WIN_ROWS // 2, 0), GRID_ROWS - NA_WIN_ROWS)


def _na_row_groups():
    groups = []
    for g in range(GRID_ROWS // NA_ROW_GROUP):
        starts = [_na_window_start(r) for r in range(g * NA_ROW_GROUP, (g + 1) * NA_ROW_GROUP)]
        lo = min(starts) // 2 * 2
        count = max(starts) + NA_WIN_ROWS - lo
        groups.append((lo, count + count % 2))
    return groups


def _na_bias_tiles(rpb_ref, hh, tiles_scr):
    n_dr = 2 * NA_WIN_ROWS - 1
    lanes = 2 * GRID_W
    lane = lax.broadcasted_iota(jnp.int32, (GRID_W, lanes), 1)
    col = lax.broadcasted_iota(jnp.int32, (GRID_W, lanes), 0)
    first = lane < GRID_W
    c2 = lane % GRID_W
    cs = jnp.clip(col - NA_WIN_COLS // 2, 0, GRID_W - NA_WIN_COLS)
    valid = (c2 >= cs) & (c2 < cs + NA_WIN_COLS)
    rows = rpb_ref[0, hh]
    base = lanes - (NA_WIN_COLS - 1)

    def skew(dr, shift):
        row = jnp.broadcast_to(rows[dr:dr + 1], (GRID_W, lanes))
        return pltpu.roll(row, shift, axis=1, stride=1, stride_axis=0)

    for e in range(n_dr - 1):
        tile = jnp.where(first, skew(e, base), skew(e + 1, (base + GRID_W) % lanes))
        tiles_scr[e] = jnp.where(valid, tile, MASK_VALUE)


def _na_group_bias(tiles_scr, g, key_lo, key_count, first):
    masked = jnp.full((GRID_W, 2 * GRID_W), MASK_VALUE, F32)
    row_tiles = []
    for r in range(g * NA_ROW_GROUP, (g + 1) * NA_ROW_GROUP):
        rs = _na_window_start(r)
        pieces = []
        for u in range(key_count // 2):
            r0 = key_lo + 2 * u
            in0 = rs <= r0 < rs + NA_WIN_ROWS
            in1 = rs <= r0 + 1 < rs + NA_WIN_ROWS
            if not (in0 or in1):
                pieces.append(masked)
                continue
            tile = tiles_scr[r0 - r + NA_WIN_ROWS - 1]
            if not in1:
                tile = jnp.where(first, tile, MASK_VALUE)
            elif not in0:
                tile = jnp.where(first, MASK_VALUE, tile)
            pieces.append(tile)
        row_tiles.append(jnp.concatenate(pieces, axis=1))
    return jnp.concatenate(row_tiles, axis=0)


def _na_kernel(q_ref, k_ref, v_ref, kc_ref, vc_ref, rpb_ref, o_ref, sc_ref, tiles_scr):
    lane = lax.broadcasted_iota(jnp.int32, (1, 2 * NA_HEAD_DIM), 1)
    first = lane < NA_HEAD_DIM
    q2 = q_ref[...]
    kc2 = jnp.concatenate([kc_ref[0, 0, 0], kc_ref[0, 0, 1]], axis=0).astype(BF16)
    vc2 = jnp.concatenate([vc_ref[0, 0, 0], vc_ref[0, 0, 1]], axis=0).astype(BF16)
    block = NA_ROW_GROUP * GRID_W
    outs = []
    for hh in range(2):
        qm = jnp.where(first, q2, jnp.zeros_like(q2)) if hh == 0 else jnp.where(first, jnp.zeros_like(q2), q2)
        sc_ref[...] = _dot(qm, kc2)
        _na_bias_tiles(rpb_ref, hh, tiles_scr)
        blocks = []
        for g, (key_lo, key_count) in enumerate(_na_row_groups()):
            keys = slice(key_lo * GRID_W, (key_lo + key_count) * GRID_W)
            s = _dot_nt(qm[g * block:(g + 1) * block], k_ref[keys, :])
            s = s + _na_group_bias(tiles_scr, g, key_lo, key_count, first)
            sc = sc_ref[g * block:(g + 1) * block, :]
            m = jnp.maximum(jnp.max(s, axis=-1, keepdims=True), jnp.max(sc, axis=-1, keepdims=True))
            p = jnp.exp(s - m)
            pc = jnp.exp(sc - m)
            l = jnp.sum(p, axis=-1, keepdims=True) + jnp.sum(pc, axis=-1, keepdims=True)
            blocks.append((_dot(p.astype(BF16), v_ref[keys, :]) + _dot_nt(pc.astype(BF16), vc2)) / l)
        outs.append(jnp.concatenate(blocks, axis=0))
    o_ref[...] = jnp.where(first, outs[0], outs[1]).astype(BF16)


def _na_call(q, k, v, cache_kt, cache_vt, rpb_rows, layer):
    tok = pl.BlockSpec((DEC_SEQ, 2 * NA_HEAD_DIM), lambda b, hp: (b, hp))
    cache = pl.BlockSpec((1, 1, 2, NA_HEAD_DIM, PAST_LEN), lambda b, hp: (b, layer, hp, 0, 0))
    n_pairs = 2 * NA_WIN_ROWS - 2
    return pl.pallas_call(
        _na_kernel,
        grid=(DEC_BATCH, NA_HEADS // 2),
        in_specs=[
            tok, tok, tok, cache, cache,
            pl.BlockSpec((1, 2) + rpb_rows.shape[2:], lambda b, hp: (layer, hp, 0, 0)),
        ],
        out_specs=tok,
        out_shape=jax.ShapeDtypeStruct((N_SMP, NA_WIDTH), BF16),
        scratch_shapes=[pltpu.VMEM((DEC_SEQ, PAST_LEN), F32),
                        pltpu.VMEM((n_pairs, GRID_W, 2 * GRID_W), F32)],
        compiler_params=_params(2),
        name="nbr_attn",
    )(q, k, v, cache_kt, cache_vt, rpb_rows)


def _gla_constants():
    C = GLA_CHUNK
    ii, jj = np.meshgrid(np.arange(C), np.arange(C), indexing="ij")
    tri = np.stack([jj <= ii, jj >= ii]).astype(np.float32)
    w = np.concatenate([tri, tri], axis=-1)

    x = ii ^ jj
    hb = np.where(x > 0, np.floor(np.log2(np.maximum(x, 1))), -1).astype(np.int64)
    masks = np.zeros((2, GLA_LEVELS + 1, C, C), np.float32)
    for p in range(GLA_LEVELS):
        masks[0, p] = (jj < ii) & (hb == p)
        masks[1, p] = (jj > ii) & (hb == p)
    masks[:, GLA_LEVELS] = np.eye(C)
    masks = np.tile(masks, (1, 1, 2, 2))
    state_mask = (np.arange(2 * GLA_DV)[:, None] // GLA_DV == np.arange(2 * GLA_DK)[None, :] // GLA_DK)
    upper = np.stack([(np.arange(C) >> p) & 1 for p in range(GLA_LEVELS)]).astype(bool)
    sign = np.stack([np.where(upper, 1.0, -1.0), np.where(upper, -1.0, 1.0)])
    scales = np.broadcast_to((sign * np.log2(np.e))[..., None], (2, GLA_LEVELS, C, GLA_K_WIDTH))
    return w, masks, state_mask.astype(np.float32), np.ascontiguousarray(scales, np.float32)


def _gla_kernel(seq_len, n_seq, is_ctx, *refs):
    refs = list(refs)
    gq_ref, gk_ref, gv_ref, lf_ref, lb_ref, w_ref, mask_ref, smask_ref, scale_ref = refs[:9]
    refs = refs[9:]
    if is_ctx:
        refs = refs[2:]
        o_ref, sf_ref, sb_ref = refs[:3]
        refs = refs[3:]
    else:
        cos_ref, sin_ref, s0f_ref, s0b_ref, o_ref = refs[:5]
        refs = refs[5:]
    q_scr, k_scr, of_scr, ob_scr, st_scr, cum_scr = refs

    C = GLA_CHUNK
    n_chunks = seq_len // C
    n_pairs = GLA_HEADS // 2
    pair_k = 2 * GLA_DK
    pair_v = 2 * GLA_DV

    q = gq_ref[...]
    k = gk_ref[...]
    if not is_ctx:
        lane = lax.broadcasted_iota(jnp.int32, (1, GLA_K_WIDTH), 1)
        quarter = GLA_DK // 4
        low = (lane % (2 * quarter)) < quarter
        cos = jnp.concatenate([cos_ref[...]] * n_seq, axis=0)
        sin = jnp.concatenate([sin_ref[...]] * n_seq, axis=0)

        def rope(t):
            partner = jnp.where(low, pltpu.roll(t, GLA_K_WIDTH - quarter, axis=1), pltpu.roll(t, quarter, axis=1))
            return t * cos + partner * sin

        q = rope(q)
        k = rope(k)
    q_scr[...] = q
    k_scr[...] = k

    st_scr[...] = jnp.zeros(st_scr.shape, F32)
    if not is_ctx:
        for d, s0_ref in enumerate((s0f_ref, s0b_ref)):
            for s in range(n_seq):
                for h in range(GLA_HEADS):
                    pr, e = divmod(h, 2)
                    st_scr[s, d, pr, e * GLA_DV:(e + 1) * GLA_DV, e * GLA_DK:(e + 1) * GLA_DK] = s0_ref[s, 0, h].T

    klane = lax.broadcasted_iota(jnp.int32, (1, GLA_K_WIDTH), 1)
    even_head = (klane // GLA_DK) % 2 == 0

    def chunk_decay(s, d, r0, f_ref):
        rows = pl.ds(s * seq_len + r0, C)
        f = f_ref[rows, :]
        f_hi = f.astype(BF16)
        f_lo = (f - f_hi.astype(F32)).astype(BF16)
        cum = _dot(w_ref[d], jnp.concatenate([f_hi, f_lo], axis=0))
        cum_scr[s, d] = cum
        return rows, cum

    def chunk_scores(s, d, rows, cum):
        def cum_rows(r, n):
            return jnp.broadcast_to(cum_scr[s, d, r:r + 1, :], (n, GLA_K_WIDTH))

        def level_factor(p):
            m = 1 << p
            edge = m - 1 if d == 0 else m
            if 2 * m >= 8:
                ref = jnp.concatenate([cum_rows(blk + edge, 2 * m) for blk in range(0, C, 2 * m)], axis=0)
            elif p == 1:
                sub = lax.broadcasted_iota(jnp.int32, (8, GLA_K_WIDTH), 0)
                ref = jnp.concatenate([jnp.where(sub < 4, cum_rows(blk + edge, 8), cum_rows(blk + 4 + edge, 8))
                                       for blk in range(0, C, 8)], axis=0)
            else:
                odd = lax.broadcasted_iota(jnp.int32, (C, GLA_K_WIDTH), 0) % 2 == 1
                if d == 0:
                    ref = jnp.where(odd, pltpu.roll(cum, 1, axis=0), cum)
                else:
                    ref = jnp.where(odd, cum, pltpu.roll(cum, C - 1, axis=0))
            return jnp.exp2((cum - ref) * scale_ref[d, p])

        qc = q_scr[rows, :]
        kc = k_scr[rows, :]
        vc = gv_ref[rows, :]
        last = C - 1 if d == 0 else 0
        eq = jnp.exp(cum)
        ek = jnp.exp(cum_rows(last, C) - cum)
        total = jnp.exp(cum_scr[s, d, last:last + 1, :])

        a = None
        for p in range(GLA_LEVELS + 1):
            if p < GLA_LEVELS:
                ep = level_factor(p)
                qe = (qc * ep).astype(BF16)
                ke = (kc * ep).astype(BF16)
            else:
                qe = qc.astype(BF16)
                ke = kc.astype(BF16)
            zero = jnp.zeros((C, pair_k), BF16)
            lhs = jnp.concatenate([jnp.concatenate([qe[:, :pair_k], zero], axis=1),
                                   jnp.concatenate([zero, qe[:, pair_k:]], axis=1)], axis=0)
            rhs = jnp.concatenate([jnp.where(even_head, ke, jnp.zeros_like(ke)),
                                   jnp.where(even_head, jnp.zeros_like(ke), ke)], axis=0)
            sc = _dot_nt(lhs, rhs) * mask_ref[d, p]
            a = sc if a is None else a + sc
        q_in = (qc * eq).astype(BF16)
        k_out = (kc * ek).astype(BF16)
        return a.astype(BF16), q_in, k_out, vc, total

    def chunk_output(s, d, rows, o_scr, a, q_in, k_out, vc, total):
        zero_v = jnp.zeros((C, GLA_DV), BF16)
        for pr in range(n_pairs):
            kl = slice(pr * pair_k, (pr + 1) * pair_k)
            vp = vc[:, pr * pair_v:(pr + 1) * pair_v]
            v_diag = jnp.concatenate([jnp.concatenate([vp[:, :GLA_DV], zero_v], axis=1),
                                      jnp.concatenate([zero_v, vp[:, GLA_DV:]], axis=1)], axis=0)
            st = st_scr[s, d, pr]
            o_scr[rows, pr * pair_v:(pr + 1) * pair_v] = (
                _dot(a[pr * C:(pr + 1) * C], v_diag) + _dot_nt(q_in[:, kl], st.astype(BF16)))
            st_scr[s, d, pr] = st * total[:, kl] + _dot_tn(vp, k_out[:, kl]) * smask_ref[...]

    def body(n, carry):
        fwd = pl.multiple_of(n * C, C)
        bwd = pl.multiple_of((n_chunks - 1 - n) * C, C)
        scans = [(s, d) for s in range(n_seq) for d in range(2)]
        decays = [chunk_decay(s, d, bwd if d else fwd, lb_ref if d else lf_ref) for s, d in scans]
        scores = [chunk_scores(s, d, *dec) for (s, d), dec in zip(scans, decays)]
        for (s, d), (rows, _), sc in zip(scans, decays, scores):
            chunk_output(s, d, rows, ob_scr if d else of_scr, *sc)
        return carry

    lax.fori_loop(0, n_chunks, body, 0)
    o_ref[...] = of_scr[...] + ob_scr[...]
    if is_ctx:
        for d, s_ref in enumerate((sf_ref, sb_ref)):
            for s in range(n_seq):
                for h in range(GLA_HEADS):
                    pr, e = divmod(h, 2)
                    s_ref[s, 0, h] = st_scr[s, d, pr, e * GLA_DV:(e + 1) * GLA_DV, e * GLA_DK:(e + 1) * GLA_DK].T


def _gla_call(gq, gk, gv, lf, lb, consts, new_states=None, rope=None, state_in=None, layer=0):
    is_ctx = new_states is not None
    seq_len = SEQ if is_ctx else DEC_SEQ
    n_tok = gq.shape[0]
    n_seq = GLA_SEQS_PER_STEP
    rows = n_seq * seq_len
    tok = lambda width: pl.BlockSpec((rows, width), lambda b: (b, 0))
    whole = lambda a: pl.BlockSpec(a.shape, lambda b: (0,) * a.ndim)
    state_spec = pl.BlockSpec((n_seq, 1, GLA_HEADS, GLA_DK, GLA_DV), lambda b: (b, layer, 0, 0, 0))
    in_specs = [tok(GLA_K_WIDTH), tok(GLA_K_WIDTH), tok(GLA_V_WIDTH), tok(GLA_K_WIDTH), tok(GLA_K_WIDTH)]
    in_specs += [whole(a) for a in consts]
    args = [gq, gk, gv, lf, lb, *consts]
    out_shape = [jax.ShapeDtypeStruct((n_tok, GLA_V_WIDTH), F32)]
    out_specs = [tok(GLA_V_WIDTH)]
    aliases = {}
    if is_ctx:
        aliases = {len(args): 1, len(args) + 1: 2}
        in_specs += [pl.BlockSpec(memory_space=pl.ANY)] * 2
        args += list(new_states)
        out_shape += [jax.ShapeDtypeStruct(new_states[0].shape, F32)] * 2
        out_specs += [state_spec, state_spec]
    else:
        in_specs += [pl.BlockSpec((seq_len, GLA_K_WIDTH), lambda b: (0, 0))] * 2 + [state_spec, state_spec]
        args += list(rope) + list(state_in)
    return pl.pallas_call(
        functools.partial(_gla_kernel, seq_len, n_seq, is_ctx),
        grid=(n_tok // rows,),
        in_specs=in_specs,
        out_specs=out_specs,
        out_shape=out_shape,
        input_output_aliases=aliases,
        scratch_shapes=[
            pltpu.VMEM((rows, GLA_K_WIDTH), F32),
            pltpu.VMEM((rows, GLA_K_WIDTH), F32),
            pltpu.VMEM((rows, GLA_V_WIDTH), F32),
            pltpu.VMEM((rows, GLA_V_WIDTH), F32),
            pltpu.VMEM((n_seq, 2, GLA_HEADS // 2, 2 * GLA_DV, 2 * GLA_DK), F32),
            pltpu.VMEM((n_seq, 2, GLA_CHUNK, GLA_K_WIDTH), F32),
        ],
        compiler_params=_params(1),
        name="gla_ctx" if is_ctx else "gla_smp",
    )(*args)


def _rope_tables():
    quarter = GLA_DK // 4
    inv = ROPE_BASE ** (-jnp.arange(quarter, dtype=F32) / quarter)
    t = jnp.arange(DEC_SEQ)
    ang_r = (t // GRID_W).astype(F32)[:, None] * inv
    ang_c = (t % GRID_W).astype(F32)[:, None] * inv
    cos = jnp.concatenate([jnp.cos(ang_r)] * 2 + [jnp.cos(ang_c)] * 2, axis=-1)
    sin = jnp.concatenate([-jnp.sin(ang_r), jnp.sin(ang_r), -jnp.sin(ang_c), jnp.sin(ang_c)], axis=-1)
    return jnp.tile(cos, (1, GLA_HEADS)), jnp.tile(sin, (1, GLA_HEADS))


def _merge_mlp_kernel(layer, x_ref, mod_ref, ona_ref, ogla_ref, gate_ref, gout_ref, gmlp_ref,
                      wo_hbm, wup_hbm, wdown_hbm, o_ref,
                      wo_scr, wup_scr, wdown_scr, stage_cols, stage_rows, sem):
    first_step = pl.program_id(0) == 0
    n_chunks = D_FF // FF_CHUNK
    wo_halves = D_MODEL // FF_CHUNK

    def col_copy(j):
        slot = j % 2
        if j < wo_halves:
            src = wo_hbm.at[layer, :, j * FF_CHUNK:(j + 1) * FF_CHUNK]
        else:
            c = j - wo_halves
            src = wup_hbm.at[layer, :, c * FF_CHUNK:(c + 1) * FF_CHUNK]
        return pltpu.make_async_copy(src, stage_cols.at[slot], sem.at[slot])

    def row_copy(c):
        slot = c % 2
        src = wdown_hbm.at[layer, c * FF_CHUNK:(c + 1) * FF_CHUNK, :]
        return pltpu.make_async_copy(src, stage_rows.at[slot], sem.at[2 + slot])

    n_col_blocks = wo_halves + n_chunks

    def land_col(j):
        col_copy(j).wait()
        block = stage_cols[j % 2].astype(BF16)
        if j < wo_halves:
            wo_scr[:, j * FF_CHUNK:(j + 1) * FF_CHUNK] = block
        else:
            c = j - wo_halves
            wup_scr[:, c * FF_CHUNK:(c + 1) * FF_CHUNK] = block
        if j + 2 < n_col_blocks:
            col_copy(j + 2).start()

    def land_row(c):
        row_copy(c).wait()
        wdown_scr[c * FF_CHUNK:(c + 1) * FF_CHUNK, :] = stage_rows[c % 2].astype(BF16)
        if c + 2 < n_chunks:
            row_copy(c + 2).start()

    def run(streaming):
        if streaming:
            col_copy(0).start()
            col_copy(1).start()
            row_copy(0).start()
            row_copy(1).start()

        mod = mod_ref[0]
        ga1 = mod[:, 2 * D_MODEL:3 * D_MODEL]
        sh2 = mod[:, 3 * D_MODEL:4 * D_MODEL]
        sc2 = mod[:, 4 * D_MODEL:5 * D_MODEL]
        ga2 = mod[:, 5 * D_MODEL:6 * D_MODEL]

        og = ogla_ref[...]
        normed = []
        for h in range(GLA_HEADS):
            oh = og[:, h * GLA_DV:(h + 1) * GLA_DV]
            normed.append(oh * lax.rsqrt(jnp.mean(oh * oh, axis=-1, keepdims=True) + EPS))
        gate = gate_ref[...]
        g = jnp.concatenate(normed, axis=1) * gout_ref[...] * (gate * (1.0 / (1.0 + jnp.exp(-gate))))

        if streaming:
            for j in range(wo_halves):
                land_col(j)
        att = _dot(ona_ref[...], wo_scr[0:NA_WIDTH, :]) + _dot(g.astype(BF16), wo_scr[NA_WIDTH:, :])
        x = x_ref[...] + ga1 * att
        ms = jnp.mean(x * x, axis=-1, keepdims=True)
        h2 = ((x * lax.rsqrt(ms + EPS) * gmlp_ref[...]) * (1.0 + sc2) + sh2).astype(BF16)

        width = FF_CHUNK if streaming else MLP_CHUNK
        acc = None
        for c in range(D_FF // width):
            if streaming:
                land_col(wo_halves + c)
                land_row(c)
            u = jnp.maximum(_dot(h2, wup_scr[:, c * width:(c + 1) * width]), 0.0)
            down = _dot((u * u).astype(BF16), wdown_scr[c * width:(c + 1) * width, :])
            acc = down if acc is None else acc + down
        o_ref[...] = x + ga2 * acc

    pl.when(first_step)(functools.partial(run, True))
    pl.when(jnp.logical_not(first_step))(functools.partial(run, False))


def _merge_mlp_call(x, mod_l, o_na, o_gla, gate, gout_t, g_mlp_l, w_o, w_up, w_down, layer):
    tm = TOKEN_TILE
    n_tok = x.shape[0]
    is_ctx = n_tok == N_CTX
    row = lambda i: (i, 0)
    const = lambda i: (0, 0)
    in_hbm = pl.BlockSpec(memory_space=pl.ANY)
    return pl.pallas_call(
        functools.partial(_merge_mlp_kernel, layer),
        grid=(n_tok // tm,),
        in_specs=[
            pl.BlockSpec((tm, D_MODEL), row),
            pl.BlockSpec((1, 1, N_MOD * D_MODEL), _mod_index_map(is_ctx)),
            pl.BlockSpec((tm, NA_WIDTH), row),
            pl.BlockSpec((tm, GLA_V_WIDTH), row),
            pl.BlockSpec((tm, GLA_V_WIDTH), row),
            pl.BlockSpec((1, GLA_V_WIDTH), const),
            pl.BlockSpec((1, D_MODEL), const),
            in_hbm, in_hbm, in_hbm,
        ],
        out_specs=pl.BlockSpec((tm, D_MODEL), row),
        out_shape=jax.ShapeDtypeStruct((n_tok, D_MODEL), F32),
        scratch_shapes=[
            pltpu.VMEM((D_MODEL, D_MODEL), BF16),
            pltpu.VMEM((D_MODEL, D_FF), BF16),
            pltpu.VMEM((D_FF, D_MODEL), BF16),
            pltpu.VMEM((2, D_MODEL, FF_CHUNK), F32),
            pltpu.VMEM((2, FF_CHUNK, D_MODEL), F32),
            pltpu.SemaphoreType.DMA((4,)),
        ],
        compiler_params=_params(1),
        name="merge_mlp_ctx" if is_ctx else "merge_mlp_smp",
    )(x, mod_l, o_na, o_gla, gate, gout_t, g_mlp_l, w_o, w_up, w_down)


def kernel(x_prompt, x_sample, cache_k, cache_v, state_fwd, state_bwd, c, c_ctx, w_ada, b_ada, g_attn, w_in,
           g_q, g_k, rpb, w_gf, b_gf, w_gb, b_gb, g_gla_out, w_o, g_mlp, w_up, w_down):
    x_c = x_prompt.reshape(N_CTX, D_MODEL)
    x_s = x_sample.reshape(N_SMP, D_MODEL)

    c_rows = jnp.concatenate([c_ctx[None, :], c, jnp.zeros((MOD_ROWS - 1 - DEC_BATCH, D_MODEL), F32)], axis=0)
    mods = _ada_call(c_rows, w_ada, b_ada).reshape(DEPTH, MOD_ROWS, 1, N_MOD * D_MODEL)

    w_in_t = jnp.swapaxes(w_in, 1, 2)
    wgate = jnp.zeros((DEPTH, 2 * GLA_GATE_RANK, 2 * GLA_K_WIDTH), F32)
    wgate = wgate.at[:, :GLA_GATE_RANK, :GLA_K_WIDTH].set(w_gf)
    wgate = wgate.at[:, GLA_GATE_RANK:, GLA_K_WIDTH:].set(w_gb).astype(BF16)
    bgate = jnp.concatenate([b_gf, b_gb], axis=-1).reshape(DEPTH, 1, 2 * GLA_K_WIDTH)
    head_of = np.arange(NA_WIDTH) // NA_HEAD_DIM
    seg = jnp.asarray((head_of[:, None] == head_of[None, :]).astype(np.float32) / NA_HEAD_DIM, BF16)
    gq_t = jnp.tile(g_q, (1, NA_HEADS)).reshape(DEPTH, 1, NA_WIDTH)
    gk_t = jnp.tile(g_k, (1, NA_HEADS)).reshape(DEPTH, 1, NA_WIDTH)
    gout_t = jnp.tile(g_gla_out, (1, GLA_HEADS)).reshape(DEPTH, 1, GLA_V_WIDTH)
    n_dr, n_dc = rpb.shape[2:]
    rpb_rows = jnp.pad(rpb, ((0, 0), (0, 0), (0, -n_dr % 8), (0, 2 * GRID_W - n_dc)))
    gla_w_np, gla_masks_np, gla_smask_np, gla_scales_np = _gla_constants()
    gla_consts = (jnp.asarray(gla_w_np, BF16), jnp.asarray(gla_masks_np, F32), jnp.asarray(gla_smask_np, F32),
                  jnp.asarray(gla_scales_np, F32))
    rope = _rope_tables()

    cache_kt = jnp.swapaxes(cache_k, 3, 4)
    cache_vt = jnp.swapaxes(cache_v, 3, 4)
    new_k = jnp.zeros((BATCH, DEPTH, NA_HEADS, SEQ, NA_HEAD_DIM), F32)
    new_v = jnp.zeros((BATCH, DEPTH, NA_HEADS, SEQ, NA_HEAD_DIM), F32)
    new_sf = jnp.zeros((BATCH, DEPTH, GLA_HEADS, GLA_DK, GLA_DV), F32)
    new_sb = jnp.zeros((BATCH, DEPTH, GLA_HEADS, GLA_DK, GLA_DV), F32)

    for l in range(DEPTH):
        proj_w = (g_attn[l].reshape(1, D_MODEL), w_in_t, seg, gq_t[l], gk_t[l], wgate[l], bgate[l])
        mlp_w = (gout_t[l], g_mlp[l].reshape(1, D_MODEL), w_o, w_up, w_down)

        q, k, v, gq, gk, gv, gate, lf, lb, new_k, new_v = _proj_call(
            x_c, mods[l], *proj_w, new_kv=(new_k, new_v), layer=l)
        o_na = _ctx_attn_call(q, k, v)
        o_gla, new_sf, new_sb = _gla_call(gq, gk, gv, lf, lb, gla_consts, new_states=(new_sf, new_sb), layer=l)
        x_c = _merge_mlp_call(x_c, mods[l], o_na, o_gla, gate, *mlp_w, layer=l)

        q, k, v, gq, gk, gv, gate, lf, lb = _proj_call(x_s, mods[l], *proj_w, layer=l)
        o_na = _na_call(q, k, v, cache_kt, cache_vt, rpb_rows, l)
        (o_gla,) = _gla_call(gq, gk, gv, lf, lb, gla_consts, rope=rope,
                             state_in=(state_fwd, state_bwd), layer=l)
        x_s = _merge_mlp_call(x_s, mods[l], o_na, o_gla, gate, *mlp_w, layer=l)

    return (x_c.reshape(BATCH, SEQ, D_MODEL), x_s.reshape(DEC_BATCH, DEC_SEQ, D_MODEL),
            new_k, new_v, new_sf, new_sb)
```

```python
import functools

import numpy as np
import jax
import jax.numpy as jnp
from jax import lax
from jax.experimental import pallas as pl
from jax.experimental.pallas import tpu as pltpu

F32 = jnp.float32
BF16 = jnp.bfloat16

D_MODEL = 1024
BATCH = 16
SEQ = 256
DEPTH = 4
DEC_BATCH = 2
DEC_SEQ = 1024
PAST_LEN = 512
GRID_W = 64
GRID_ROWS = DEC_SEQ // GRID_W
NA_WIDTH = D_MODEL // 2
NA_HEAD_DIM = 64
NA_HEADS = NA_WIDTH // NA_HEAD_DIM
NA_WIN_ROWS = 8
NA_WIN_COLS = 16
GLA_V_WIDTH = D_MODEL - NA_WIDTH
GLA_HEADS = 4
GLA_DV = GLA_V_WIDTH // GLA_HEADS
GLA_DK = GLA_DV // 2
GLA_K_WIDTH = GLA_HEADS * GLA_DK
GLA_GATE_RANK = 16
GLA_GATE_TAU = 16.0
GLA_CHUNK = 64
D_FF = 4 * D_MODEL
ROPE_BASE = 10000.0
N_MOD = 6
EPS = 1e-6

IN_WIDTH = 3 * NA_WIDTH + 2 * GLA_K_WIDTH + 2 * GLA_V_WIDTH + 2 * GLA_GATE_RANK
PROJ_WIDE = IN_WIDTH - 2 * GLA_GATE_RANK

N_CTX = BATCH * SEQ
N_SMP = DEC_BATCH * DEC_SEQ
MOD_ROWS = 8
MASK_VALUE = -1e30

TOKEN_TILE = 512
PROJ_BLOCK = 512
FF_CHUNK = 512
MLP_CHUNK = 1024
ADA_TILE_N = 1536
GLA_LEVELS = 6
NA_ROW_GROUP = 4
GLA_SEQS_PER_STEP = 2
VMEM_LIMIT = 52 * 1024 * 1024


def _dot(a, b):
    return jnp.dot(a, b, preferred_element_type=F32)


def _dot_nt(a, b):
    return lax.dot_general(a, b, (((1,), (1,)), ((), ())), preferred_element_type=F32)


def _dot_tn(a, b):
    return lax.dot_general(a, b, (((0,), (0,)), ((), ())), preferred_element_type=F32)


def _params(n_grid_dims=1):
    return pltpu.CompilerParams(dimension_semantics=("arbitrary",) * n_grid_dims,
                                vmem_limit_bytes=VMEM_LIMIT)


def _mod_index_map(is_ctx):
    if is_ctx:
        return lambda i: (0, 0, 0)
    return lambda i: (1 + i // (DEC_SEQ // TOKEN_TILE), 0, 0)


def _ada_kernel(c_ref, w_ref, b_ref, o_ref):
    cv = c_ref[...]
    s = cv * (1.0 / (1.0 + jnp.exp(-cv)))
    o_ref[0] = _dot(s.astype(BF16), w_ref[0].astype(BF16)) + b_ref[0]


def _ada_call(c_rows, w_ada, b_ada):
    n_mod = N_MOD * D_MODEL
    return pl.pallas_call(
        _ada_kernel,
        grid=(DEPTH, n_mod // ADA_TILE_N),
        in_specs=[
            pl.BlockSpec((MOD_ROWS, D_MODEL), lambda l, j: (0, 0)),
            pl.BlockSpec((1, D_MODEL, ADA_TILE_N), lambda l, j: (l, 0, j)),
            pl.BlockSpec((1, 1, ADA_TILE_N), lambda l, j: (l, 0, j)),
        ],
        out_specs=pl.BlockSpec((1, MOD_ROWS, ADA_TILE_N), lambda l, j: (l, 0, j)),
        out_shape=jax.ShapeDtypeStruct((DEPTH, MOD_ROWS, n_mod), F32),
        compiler_params=_params(2),
        name="ada_mod",
    )(c_rows, w_ada, b_ada.reshape(DEPTH, 1, n_mod))


def _log_sigmoid(x):
    return jnp.minimum(x, 0.0) - jnp.log(1.0 + jnp.exp(-jnp.abs(x)))


def _proj_kernel(is_ctx, layer, x_ref, mod_ref, g_ref, seg_ref, gq_ref, gk_ref, wgate_ref, bgate_ref, win_hbm,
                 *rest):
    if is_ctx:
        rest = rest[2:]
    q_out, k_out, v_out, gq_out, gk_out, gv_out, gate_out, lf_out, lb_out = rest[:9]
    rest = rest[9:]
    if is_ctx:
        knew_out, vnew_out = rest[:2]
        rest = rest[2:]
    w_scr, stage, tail_stage, sem = rest

    n_blocks = PROJ_WIDE // PROJ_BLOCK
    tail = slice(PROJ_WIDE, IN_WIDTH)
    first_step = pl.program_id(0) == 0

    def block_copy(j):
        src = win_hbm.at[layer, j * PROJ_BLOCK:(j + 1) * PROJ_BLOCK, :]
        return pltpu.make_async_copy(src, stage.at[j % 2], sem.at[j % 2])

    def tail_copy():
        return pltpu.make_async_copy(win_hbm.at[layer, tail, :], tail_stage, sem.at[2])

    def land(j):
        block_copy(j).wait()
        w_scr[j * PROJ_BLOCK:(j + 1) * PROJ_BLOCK, :] = stage[j % 2].astype(BF16)
        if j + 2 < n_blocks:
            block_copy(j + 2).start()

    def run(streaming):
        if streaming:
            block_copy(0).start()
            block_copy(1).start()
            tail_copy().start()

        x = x_ref[...]
        mod = mod_ref[0]
        sh1 = mod[:, 0:D_MODEL]
        sc1 = mod[:, D_MODEL:2 * D_MODEL]
        ms = jnp.mean(x * x, axis=-1, keepdims=True)
        h = (x * lax.rsqrt(ms + EPS) * g_ref[...]) * (1.0 + sc1) + sh1
        hb = h.astype(BF16)

        def projected(j):
            if streaming:
                land(j)
            return _dot_nt(hb, w_scr[j * PROJ_BLOCK:(j + 1) * PROJ_BLOCK, :])

        seg = seg_ref[...]
        q = projected(0)
        q_ms = _dot((q * q).astype(BF16), seg)
        q_out[...] = (q * lax.rsqrt(q_ms + EPS) * gq_ref[...] * (NA_HEAD_DIM ** -0.5)).astype(BF16)
        k = projected(1)
        k_ms = _dot((k * k).astype(BF16), seg)
        kn = k * lax.rsqrt(k_ms + EPS) * gk_ref[...]
        k_out[...] = kn.astype(BF16)
        v = projected(2)
        v_out[...] = v.astype(BF16)
        if is_ctx:
            for s in range(TOKEN_TILE // SEQ):
                rows = slice(s * SEQ, (s + 1) * SEQ)
                for hp in range(NA_HEADS // 2):
                    cols = slice(hp * 2 * NA_HEAD_DIM, (hp + 1) * 2 * NA_HEAD_DIM)
                    for out, t in ((knew_out, kn[rows, cols].T), (vnew_out, v[rows, cols].T)):
                        out[s, 0, 2 * hp] = t[:NA_HEAD_DIM]
                        out[s, 0, 2 * hp + 1] = t[NA_HEAD_DIM:]

        z = projected(3)
        gq_out[...] = z[:, 0:GLA_K_WIDTH] * (GLA_DK ** -0.5)
        gk_out[...] = z[:, GLA_K_WIDTH:]
        gv_out[...] = projected(4).astype(BF16)
        gate_out[...] = projected(5)

        if streaming:
            tail_copy().wait()
            w_scr[tail, :] = tail_stage[...].astype(BF16)
        zz = _dot_nt(hb, w_scr[tail, :])
        pre = _dot(zz.astype(BF16), wgate_ref[...]) + bgate_ref[...]
        ls = _log_sigmoid(pre) * (1.0 / GLA_GATE_TAU)
        lf_out[...] = ls[:, 0:GLA_K_WIDTH]
        lb_out[...] = ls[:, GLA_K_WIDTH:]

    pl.when(first_step)(functools.partial(run, True))
    pl.when(jnp.logical_not(first_step))(functools.partial(run, False))


def _proj_call(x, mod_l, g_attn_l, w_in_t, seg, gq_t, gk_t, wgate, bgate, new_kv=None, layer=0):
    tm = TOKEN_TILE
    n_tok = x.shape[0]
    is_ctx = new_kv is not None
    row = lambda i: (i, 0)
    const = lambda i: (0, 0)
    out_shapes = [
        jax.ShapeDtypeStruct((n_tok, NA_WIDTH), BF16),
        jax.ShapeDtypeStruct((n_tok, NA_WIDTH), BF16),
        jax.ShapeDtypeStruct((n_tok, NA_WIDTH), BF16),
        jax.ShapeDtypeStruct((n_tok, GLA_K_WIDTH), F32),
        jax.ShapeDtypeStruct((n_tok, GLA_K_WIDTH), F32),
        jax.ShapeDtypeStruct((n_tok, GLA_V_WIDTH), BF16),
        jax.ShapeDtypeStruct((n_tok, GLA_V_WIDTH), F32),
        jax.ShapeDtypeStruct((n_tok, GLA_K_WIDTH), F32),
        jax.ShapeDtypeStruct((n_tok, GLA_K_WIDTH), F32),
    ]
    out_specs = [pl.BlockSpec((tm, s.shape[1]), row) for s in out_shapes]
    in_specs = [
        pl.BlockSpec((tm, D_MODEL), row),
        pl.BlockSpec((1, 1, N_MOD * D_MODEL), _mod_index_map(is_ctx)),
        pl.BlockSpec((1, D_MODEL), const),
        pl.BlockSpec((NA_WIDTH, NA_WIDTH), const),
        pl.BlockSpec((1, NA_WIDTH), const),
        pl.BlockSpec((1, NA_WIDTH), const),
        pl.BlockSpec(wgate.shape, const),
        pl.BlockSpec((1, 2 * GLA_K_WIDTH), const),
        pl.BlockSpec(memory_space=pl.ANY),
    ]
    args = [x, mod_l, g_attn_l, seg, gq_t, gk_t, wgate, bgate, w_in_t]
    aliases = {}
    if is_ctx:
        kv_spec = pl.BlockSpec((tm // SEQ, 1, NA_HEADS, NA_HEAD_DIM, SEQ), lambda i: (i, layer, 0, 0, 0))
        aliases = {len(args): len(out_shapes), len(args) + 1: len(out_shapes) + 1}
        in_specs += [pl.BlockSpec(memory_space=pl.ANY)] * 2
        args += list(new_kv)
        out_shapes += [jax.ShapeDtypeStruct(new_kv[0].shape, F32)] * 2
        out_specs += [kv_spec, kv_spec]
    return pl.pallas_call(
        functools.partial(_proj_kernel, is_ctx, layer),
        grid=(n_tok // tm,),
        in_specs=in_specs,
        out_specs=out_specs,
        out_shape=out_shapes,
        input_output_aliases=aliases,
        scratch_shapes=[
            pltpu.VMEM((IN_WIDTH, D_MODEL), BF16),
            pltpu.VMEM((2, PROJ_BLOCK, D_MODEL), F32),
            pltpu.VMEM((IN_WIDTH - PROJ_WIDE, D_MODEL), F32),
            pltpu.SemaphoreType.DMA((3,)),
        ],
        compiler_params=_params(1),
        name="proj_ctx" if is_ctx else "proj_smp",
    )(*args)


def _ctx_attn_kernel(q_ref, k_ref, v_ref, o_ref):
    lane = lax.broadcasted_iota(jnp.int32, (1, 2 * NA_HEAD_DIM), 1)
    first = lane < NA_HEAD_DIM
    pairs = [slice(hp * 2 * NA_HEAD_DIM, (hp + 1) * 2 * NA_HEAD_DIM) for hp in range(NA_HEADS // 2)]
    scores = []
    for cols in pairs:
        q2 = q_ref[:, cols]
        k2 = k_ref[:, cols]
        scores.append(_dot_nt(jnp.where(first, q2, jnp.zeros_like(q2)), k2))
        scores.append(_dot_nt(jnp.where(first, jnp.zeros_like(q2), q2), k2))
    probs = []
    for s in scores:
        p = jnp.exp(s - jnp.max(s, axis=-1, keepdims=True))
        probs.append((p.astype(BF16), jnp.sum(p, axis=-1, keepdims=True)))
    for hp, cols in enumerate(pairs):
        v2 = v_ref[:, cols]
        (p0, l0), (p1, l1) = probs[2 * hp], probs[2 * hp + 1]
        o_ref[:, cols] = jnp.where(first, _dot(p0, v2) / l0, _dot(p1, v2) / l1).astype(BF16)


def _ctx_attn_call(q, k, v):
    spec = pl.BlockSpec((SEQ, NA_WIDTH), lambda b: (b, 0))
    return pl.pallas_call(
        _ctx_attn_kernel,
        grid=(BATCH,),
        in_specs=[spec, spec, spec],
        out_specs=spec,
        out_shape=jax.ShapeDtypeStruct((N_CTX, NA_WIDTH), BF16),
        compiler_params=_params(1),
        name="ctx_attn",
    )(q, k, v)


def _na_window_start(r):
    return min(max(r - NA_WIN_ROWS // 2, 0), GRID_ROWS - NA_WIN_ROWS)


def _na_row_groups():
    groups = []
    for g in range(GRID_ROWS // NA_ROW_GROUP):
        starts = [_na_window_start(r) for r in range(g * NA_ROW_GROUP, (g + 1) * NA_ROW_GROUP)]
        lo = min(starts) // 2 * 2
        count = max(starts) + NA_WIN_ROWS - lo
        groups.append((lo, count + count % 2))
    return groups


def _na_bias_tiles(rpb_ref, hh, tiles_scr):
    n_dr = 2 * NA_WIN_ROWS - 1
    lanes = 2 * GRID_W
    lane = lax.broadcasted_iota(jnp.int32, (GRID_W, lanes), 1)
    col = lax.broadcasted_iota(jnp.int32, (GRID_W, lanes), 0)
    first = lane < GRID_W
    c2 = lane % GRID_W
    cs = jnp.clip(col - NA_WIN_COLS // 2, 0, GRID_W - NA_WIN_COLS)
    valid = (c2 >= cs) & (c2 < cs + NA_WIN_COLS)
    rows = rpb_ref[0, hh]
    base = lanes - (NA_WIN_COLS - 1)

    def skew(dr, shift):
        row = jnp.broadcast_to(rows[dr:dr + 1], (GRID_W, lanes))
        return pltpu.roll(row, shift, axis=1, stride=1, stride_axis=0)

    for e in range(n_dr - 1):
        tile = jnp.where(first, skew(e, base), skew(e + 1, (base + GRID_W) % lanes))
        tiles_scr[e] = jnp.where(valid, tile, MASK_VALUE)


def _na_group_bias(tiles_scr, g, key_lo, key_count, first):
    masked = jnp.full((GRID_W, 2 * GRID_W), MASK_VALUE, F32)
    row_tiles = []
    for r in range(g * NA_ROW_GROUP, (g + 1) * NA_ROW_GROUP):
        rs = _na_window_start(r)
        pieces = []
        for u in range(key_count // 2):
            r0 = key_lo + 2 * u
            in0 = rs <= r0 < rs + NA_WIN_ROWS
            in1 = rs <= r0 + 1 < rs + NA_WIN_ROWS
            if not (in0 or in1):
                pieces.append(masked)
                continue
            tile = tiles_scr[r0 - r + NA_WIN_ROWS - 1]
            if not in1:
                tile = jnp.where(first, tile, MASK_VALUE)
            elif not in0:
                tile = jnp.where(first, MASK_VALUE, tile)
            pieces.append(tile)
        row_tiles.append(jnp.concatenate(pieces, axis=1))
    return jnp.concatenate(row_tiles, axis=0)


def _na_kernel(q_ref, k_ref, v_ref, kc_ref, vc_ref, rpb_ref, o_ref, sc_ref, tiles_scr):
    lane = lax.broadcasted_iota(jnp.int32, (1, 2 * NA_HEAD_DIM), 1)
    first = lane < NA_HEAD_DIM
    q2 = q_ref[...]
    kc2 = jnp.concatenate([kc_ref[0, 0, 0], kc_ref[0, 0, 1]], axis=0).astype(BF16)
    vc2 = jnp.concatenate([vc_ref[0, 0, 0], vc_ref[0, 0, 1]], axis=0).astype(BF16)
    block = NA_ROW_GROUP * GRID_W
    outs = []
    for hh in range(2):
        qm = jnp.where(first, q2, jnp.zeros_like(q2)) if hh == 0 else jnp.where(first, jnp.zeros_like(q2), q2)
        sc_ref[...] = _dot(qm, kc2)
        _na_bias_tiles(rpb_ref, hh, tiles_scr)
        blocks = []
        for g, (key_lo, key_count) in enumerate(_na_row_groups()):
            keys = slice(key_lo * GRID_W, (key_lo + key_count) * GRID_W)
            s = _dot_nt(qm[g * block:(g + 1) * block], k_ref[keys, :])
            s = s + _na_group_bias(tiles_scr, g, key_lo, key_count, first)
            sc = sc_ref[g * block:(g + 1) * block, :]
            m = jnp.maximum(jnp.max(s, axis=-1, keepdims=True), jnp.max(sc, axis=-1, keepdims=True))
            p = jnp.exp(s - m)
            pc = jnp.exp(sc - m)
            l = jnp.sum(p, axis=-1, keepdims=True) + jnp.sum(pc, axis=-1, keepdims=True)
            blocks.append((_dot(p.astype(BF16), v_ref[keys, :]) + _dot_nt(pc.astype(BF16), vc2)) / l)
        outs.append(jnp.concatenate(blocks, axis=0))
    o_ref[...] = jnp.where(first, outs[0], outs[1]).astype(BF16)


def _na_call(q, k, v, cache_kt, cache_vt, rpb_rows, layer):
    tok = pl.BlockSpec((DEC_SEQ, 2 * NA_HEAD_DIM), lambda b, hp: (b, hp))
    cache = pl.BlockSpec((1, 1, 2, NA_HEAD_DIM, PAST_LEN), lambda b, hp: (b, layer, hp, 0, 0))
    n_pairs = 2 * NA_WIN_ROWS - 2
    return pl.pallas_call(
        _na_kernel,
        grid=(DEC_BATCH, NA_HEADS // 2),
        in_specs=[
            tok, tok, tok, cache, cache,
            pl.BlockSpec((1, 2) + rpb_rows.shape[2:], lambda b, hp: (layer, hp, 0, 0)),
        ],
        out_specs=tok,
        out_shape=jax.ShapeDtypeStruct((N_SMP, NA_WIDTH), BF16),
        scratch_shapes=[pltpu.VMEM((DEC_SEQ, PAST_LEN), F32),
                        pltpu.VMEM((n_pairs, GRID_W, 2 * GRID_W), F32)],
        compiler_params=_params(2),
        name="nbr_attn",
    )(q, k, v, cache_kt, cache_vt, rpb_rows)


def _gla_constants():
    C = GLA_CHUNK
    ii, jj = np.meshgrid(np.arange(C), np.arange(C), indexing="ij")
    tri = np.stack([jj <= ii, jj >= ii]).astype(np.float32)
    w = np.concatenate([tri, tri], axis=-1)

    x = ii ^ jj
    hb = np.where(x > 0, np.floor(np.log2(np.maximum(x, 1))), -1).astype(np.int64)
    masks = np.zeros((2, GLA_LEVELS + 1, C, C), np.float32)
    for p in range(GLA_LEVELS):
        masks[0, p] = (jj < ii) & (hb == p)
        masks[1, p] = (jj > ii) & (hb == p)
    masks[:, GLA_LEVELS] = np.eye(C)
    masks = np.tile(masks, (1, 1, 2, 2))
    state_mask = (np.arange(2 * GLA_DV)[:, None] // GLA_DV == np.arange(2 * GLA_DK)[None, :] // GLA_DK)
    upper = np.stack([(np.arange(C) >> p) & 1 for p in range(GLA_LEVELS)]).astype(bool)
    sign = np.stack([np.where(upper, 1.0, -1.0), np.where(upper, -1.0, 1.0)])
    scales = np.broadcast_to((sign * np.log2(np.e))[..., None], (2, GLA_LEVELS, C, GLA_K_WIDTH))
    return w, masks, state_mask.astype(np.float32), np.ascontiguousarray(scales, np.float32)


def _gla_kernel(seq_len, n_seq, is_ctx, *refs):
    refs = list(refs)
    gq_ref, gk_ref, gv_ref, lf_ref, lb_ref, w_ref, mask_ref, smask_ref, scale_ref = refs[:9]
    refs = refs[9:]
    if is_ctx:
        refs = refs[2:]
        o_ref, sf_ref, sb_ref = refs[:3]
        refs = refs[3:]
    else:
        cos_ref, sin_ref, s0f_ref, s0b_ref, o_ref = refs[:5]
        refs = refs[5:]
    q_scr, k_scr, of_scr, ob_scr, st_scr, cum_scr = refs

    C = GLA_CHUNK
    n_chunks = seq_len // C
    n_pairs = GLA_HEADS // 2
    pair_k = 2 * GLA_DK
    pair_v = 2 * GLA_DV

    q = gq_ref[...]
    k = gk_ref[...]
    if not is_ctx:
        lane = lax.broadcasted_iota(jnp.int32, (1, GLA_K_WIDTH), 1)
        quarter = GLA_DK // 4
        low = (lane % (2 * quarter)) < quarter
        cos = jnp.concatenate([cos_ref[...]] * n_seq, axis=0)
        sin = jnp.concatenate([sin_ref[...]] * n_seq, axis=0)

        def rope(t):
            partner = jnp.where(low, pltpu.roll(t, GLA_K_WIDTH - quarter, axis=1), pltpu.roll(t, quarter, axis=1))
            return t * cos + partner * sin

        q = rope(q)
        k = rope(k)
    q_scr[...] = q
    k_scr[...] = k

    st_scr[...] = jnp.zeros(st_scr.shape, F32)
    if not is_ctx:
        for d, s0_ref in enumerate((s0f_ref, s0b_ref)):
            for s in range(n_seq):
                for h in range(GLA_HEADS):
                    pr, e = divmod(h, 2)
                    st_scr[s, d, pr, e * GLA_DV:(e + 1) * GLA_DV, e * GLA_DK:(e + 1) * GLA_DK] = s0_ref[s, 0, h].T

    klane = lax.broadcasted_iota(jnp.int32, (1, GLA_K_WIDTH), 1)
    even_head = (klane // GLA_DK) % 2 == 0

    def chunk_decay(s, d, r0, f_ref):
        rows = pl.ds(s * seq_len + r0, C)
        f = f_ref[rows, :]
        f_hi = f.astype(BF16)
        f_lo = (f - f_hi.astype(F32)).astype(BF16)
        cum = _dot(w_ref[d], jnp.concatenate([f_hi, f_lo], axis=0))
        cum_scr[s, d] = cum
        return rows, cum

    def chunk_scores(s, d, rows, cum):
        def cum_rows(r, n):
            return jnp.broadcast_to(cum_scr[s, d, r:r + 1, :], (n, GLA_K_WIDTH))

        def level_factor(p):
            m = 1 << p
            edge = m - 1 if d == 0 else m
            if 2 * m >= 8:
                ref = jnp.concatenate([cum_rows(blk + edge, 2 * m) for blk in range(0, C, 2 * m)], axis=0)
            elif p == 1:
                sub = lax.broadcasted_iota(jnp.int32, (8, GLA_K_WIDTH), 0)
                ref = jnp.concatenate([jnp.where(sub < 4, cum_rows(blk + edge, 8), cum_rows(blk + 4 + edge, 8))
                                       for blk in range(0, C, 8)], axis=0)
            else:
                odd = lax.broadcasted_iota(jnp.int32, (C, GLA_K_WIDTH), 0) % 2 == 1
                if d == 0:
                    ref = jnp.where(odd, pltpu.roll(cum, 1, axis=0), cum)
                else:
                    ref = jnp.where(odd, cum, pltpu.roll(cum, C - 1, axis=0))
            return jnp.exp2((cum - ref) * scale_ref[d, p])

        qc = q_scr[rows, :]
        kc = k_scr[rows, :]
        vc = gv_ref[rows, :]
        last = C - 1 if d == 0 else 0
        eq = jnp.exp(cum)
        ek = jnp.exp(cum_rows(last, C) - cum)
        total = jnp.exp(cum_scr[s, d, last:last + 1, :])

        a = None
        for p in range(GLA_LEVELS + 1):
            if p < GLA_LEVELS:
                ep = level_factor(p)
                qe = (qc * ep).astype(BF16)
                ke = (kc * ep).astype(BF16)
            else:
                qe = qc.astype(BF16)
                ke = kc.astype(BF16)
            zero = jnp.zeros((C, pair_k), BF16)
            lhs = jnp.concatenate([jnp.concatenate([qe[:, :pair_k], zero], axis=1),
                                   jnp.concatenate([zero, qe[:, pair_k:]], axis=1)], axis=0)
            rhs = jnp.concatenate([jnp.where(even_head, ke, jnp.zeros_like(ke)),
                                   jnp.where(even_head, jnp.zeros_like(ke), ke)], axis=0)
            sc = _dot_nt(lhs, rhs) * mask_ref[d, p]
            a = sc if a is None else a + sc
        q_in = (qc * eq).astype(BF16)
        k_out = (kc * ek).astype(BF16)
        return a.astype(BF16), q_in, k_out, vc, total

    def chunk_output(s, d, rows, o_scr, a, q_in, k_out, vc, total):
        zero_v = jnp.zeros((C, GLA_DV), BF16)
        for pr in range(n_pairs):
            kl = slice(pr * pair_k, (pr + 1) * pair_k)
            vp = vc[:, pr * pair_v:(pr + 1) * pair_v]
            v_diag = jnp.concatenate([jnp.concatenate([vp[:, :GLA_DV], zero_v], axis=1),
                                      jnp.concatenate([zero_v, vp[:, GLA_DV:]], axis=1)], axis=0)
            st = st_scr[s, d, pr]
            o_scr[rows, pr * pair_v:(pr + 1) * pair_v] = (
                _dot(a[pr * C:(pr + 1) * C], v_diag) + _dot_nt(q_in[:, kl], st.astype(BF16)))
            st_scr[s, d, pr] = st * total[:, kl] + _dot_tn(vp, k_out[:, kl]) * smask_ref[...]

    def body(n, carry):
        fwd = pl.multiple_of(n * C, C)
        bwd = pl.multiple_of((n_chunks - 1 - n) * C, C)
        scans = [(s, d) for s in range(n_seq) for d in range(2)]
        decays = [chunk_decay(s, d, bwd if d else fwd, lb_ref if d else lf_ref) for s, d in scans]
        scores = [chunk_scores(s, d, *dec) for (s, d), dec in zip(scans, decays)]
        for (s, d), (rows, _), sc in zip(scans, decays, scores):
            chunk_output(s, d, rows, ob_scr if d else of_scr, *sc)
        return carry

    lax.fori_loop(0, n_chunks, body, 0)
    o_ref[...] = of_scr[...] + ob_scr[...]
    if is_ctx:
        for d, s_ref in enumerate((sf_ref, sb_ref)):
            for s in range(n_seq):
                for h in range(GLA_HEADS):
                    pr, e = divmod(h, 2)
                    s_ref[s, 0, h] = st_scr[s, d, pr, e * GLA_DV:(e + 1) * GLA_DV, e * GLA_DK:(e + 1) * GLA_DK].T


def _gla_call(gq, gk, gv, lf, lb, consts, new_states=None, rope=None, state_in=None, layer=0):
    is_ctx = new_states is not None
    seq_len = SEQ if is_ctx else DEC_SEQ
    n_tok = gq.shape[0]
    n_seq = GLA_SEQS_PER_STEP
    rows = n_seq * seq_len
    tok = lambda width: pl.BlockSpec((rows, width), lambda b: (b, 0))
    whole = lambda a: pl.BlockSpec(a.shape, lambda b: (0,) * a.ndim)
    state_spec = pl.BlockSpec((n_seq, 1, GLA_HEADS, GLA_DK, GLA_DV), lambda b: (b, layer, 0, 0, 0))
    in_specs = [tok(GLA_K_WIDTH), tok(GLA_K_WIDTH), tok(GLA_V_WIDTH), tok(GLA_K_WIDTH), tok(GLA_K_WIDTH)]
    in_specs += [whole(a) for a in consts]
    args = [gq, gk, gv, lf, lb, *consts]
    out_shape = [jax.ShapeDtypeStruct((n_tok, GLA_V_WIDTH), F32)]
    out_specs = [tok(GLA_V_WIDTH)]
    aliases = {}
    if is_ctx:
        aliases = {len(args): 1, len(args) + 1: 2}
        in_specs += [pl.BlockSpec(memory_space=pl.ANY)] * 2
        args += list(new_states)
        out_shape += [jax.ShapeDtypeStruct(new_states[0].shape, F32)] * 2
        out_specs += [state_spec, state_spec]
    else:
        in_specs += [pl.BlockSpec((seq_len, GLA_K_WIDTH), lambda b: (0, 0))] * 2 + [state_spec, state_spec]
        args += list(rope) + list(state_in)
    return pl.pallas_call(
        functools.partial(_gla_kernel, seq_len, n_seq, is_ctx),
        grid=(n_tok // rows,),
        in_specs=in_specs,
        out_specs=out_specs,
        out_shape=out_shape,
        input_output_aliases=aliases,
        scratch_shapes=[
            pltpu.VMEM((rows, GLA_K_WIDTH), F32),
            pltpu.VMEM((rows, GLA_K_WIDTH), F32),
            pltpu.VMEM((rows, GLA_V_WIDTH), F32),
            pltpu.VMEM((rows, GLA_V_WIDTH), F32),
            pltpu.VMEM((n_seq, 2, GLA_HEADS // 2, 2 * GLA_DV, 2 * GLA_DK), F32),
            pltpu.VMEM((n_seq, 2, GLA_CHUNK, GLA_K_WIDTH), F32),
        ],
        compiler_params=_params(1),
        name="gla_ctx" if is_ctx else "gla_smp",
    )(*args)


def _rope_tables():
    quarter = GLA_DK // 4
    inv = ROPE_BASE ** (-jnp.arange(quarter, dtype=F32) / quarter)
    t = jnp.arange(DEC_SEQ)
    ang_r = (t // GRID_W).astype(F32)[:, None] * inv
    ang_c = (t % GRID_W).astype(F32)[:, None] * inv
    cos = jnp.concatenate([jnp.cos(ang_r)] * 2 + [jnp.cos(ang_c)] * 2, axis=-1)
    sin = jnp.concatenate([-jnp.sin(ang_r), jnp.sin(ang_r), -jnp.sin(ang_c), jnp.sin(ang_c)], axis=-1)
    return jnp.tile(cos, (1, GLA_HEADS)), jnp.tile(sin, (1, GLA_HEADS))


def _merge_mlp_kernel(layer, x_ref, mod_ref, ona_ref, ogla_ref, gate_ref, gout_ref, gmlp_ref,
                      wo_hbm, wup_hbm, wdown_hbm, o_ref,
                      wo_scr, wup_scr, wdown_scr, stage_cols, stage_rows, sem):
    first_step = pl.program_id(0) == 0
    n_chunks = D_FF // FF_CHUNK
    wo_halves = D_MODEL // FF_CHUNK

    def col_copy(j):
        slot = j % 2
        if j < wo_halves:
            src = wo_hbm.at[layer, :, j * FF_CHUNK:(j + 1) * FF_CHUNK]
        else:
            c = j - wo_halves
            src = wup_hbm.at[layer, :, c * FF_CHUNK:(c + 1) * FF_CHUNK]
        return pltpu.make_async_copy(src, stage_cols.at[slot], sem.at[slot])

    def row_copy(c):
        slot = c % 2
        src = wdown_hbm.at[layer, c * FF_CHUNK:(c + 1) * FF_CHUNK, :]
        return pltpu.make_async_copy(src, stage_rows.at[slot], sem.at[2 + slot])

    n_col_blocks = wo_halves + n_chunks

    def land_col(j):
        col_copy(j).wait()
        block = stage_cols[j % 2].astype(BF16)
        if j < wo_halves:
            wo_scr[:, j * FF_CHUNK:(j + 1) * FF_CHUNK] = block
        else:
            c = j - wo_halves
            wup_scr[:, c * FF_CHUNK:(c + 1) * FF_CHUNK] = block
        if j + 2 < n_col_blocks:
            col_copy(j + 2).start()

    def land_row(c):
        row_copy(c).wait()
        wdown_scr[c * FF_CHUNK:(c + 1) * FF_CHUNK, :] = stage_rows[c % 2].astype(BF16)
        if c + 2 < n_chunks:
            row_copy(c + 2).start()

    def run(streaming):
        if streaming:
            col_copy(0).start()
            col_copy(1).start()
            row_copy(0).start()
            row_copy(1).start()

        mod = mod_ref[0]
        ga1 = mod[:, 2 * D_MODEL:3 * D_MODEL]
        sh2 = mod[:, 3 * D_MODEL:4 * D_MODEL]
        sc2 = mod[:, 4 * D_MODEL:5 * D_MODEL]
        ga2 = mod[:, 5 * D_MODEL:6 * D_MODEL]

        og = ogla_ref[...]
        normed = []
        for h in range(GLA_HEADS):
            oh = og[:, h * GLA_DV:(h + 1) * GLA_DV]
            normed.append(oh * lax.rsqrt(jnp.mean(oh * oh, axis=-1, keepdims=True) + EPS))
        gate = gate_ref[...]
        g = jnp.concatenate(normed, axis=1) * gout_ref[...] * (gate * (1.0 / (1.0 + jnp.exp(-gate))))

        if streaming:
            for j in range(wo_halves):
                land_col(j)
        att = _dot(ona_ref[...], wo_scr[0:NA_WIDTH, :]) + _dot(g.astype(BF16), wo_scr[NA_WIDTH:, :])
        x = x_ref[...] + ga1 * att
        ms = jnp.mean(x * x, axis=-1, keepdims=True)
        h2 = ((x * lax.rsqrt(ms + EPS) * gmlp_ref[...]) * (1.0 + sc2) + sh2).astype(BF16)

        width = FF_CHUNK if streaming else MLP_CHUNK
        acc = None
        for c in range(D_FF // width):
            if streaming:
                land_col(wo_halves + c)
                land_row(c)
            u = jnp.maximum(_dot(h2, wup_scr[:, c * width:(c + 1) * width]), 0.0)
            down = _dot((u * u).astype(BF16), wdown_scr[c * width:(c + 1) * width, :])
            acc = down if acc is None else acc + down
        o_ref[...] = x + ga2 * acc

    pl.when(first_step)(functools.partial(run, True))
    pl.when(jnp.logical_not(first_step))(functools.partial(run, False))


def _merge_mlp_call(x, mod_l, o_na, o_gla, gate, gout_t, g_mlp_l, w_o, w_up, w_down, layer):
    tm = TOKEN_TILE
    n_tok = x.shape[0]
    is_ctx = n_tok == N_CTX
    row = lambda i: (i, 0)
    const = lambda i: (0, 0)
    in_hbm = pl.BlockSpec(memory_space=pl.ANY)
    return pl.pallas_call(
        functools.partial(_merge_mlp_kernel, layer),
        grid=(n_tok // tm,),
        in_specs=[
            pl.BlockSpec((tm, D_MODEL), row),
            pl.BlockSpec((1, 1, N_MOD * D_MODEL), _mod_index_map(is_ctx)),
            pl.BlockSpec((tm, NA_WIDTH), row),
            pl.BlockSpec((tm, GLA_V_WIDTH), row),
            pl.BlockSpec((tm, GLA_V_WIDTH), row),
            pl.BlockSpec((1, GLA_V_WIDTH), const),
            pl.BlockSpec((1, D_MODEL), const),
            in_hbm, in_hbm, in_hbm,
        ],
        out_specs=pl.BlockSpec((tm, D_MODEL), row),
        out_shape=jax.ShapeDtypeStruct((n_tok, D_MODEL), F32),
        scratch_shapes=[
            pltpu.VMEM((D_MODEL, D_MODEL), BF16),
            pltpu.VMEM((D_MODEL, D_FF), BF16),
            pltpu.VMEM((D_FF, D_MODEL), BF16),
            pltpu.VMEM((2, D_MODEL, FF_CHUNK), F32),
            pltpu.VMEM((2, FF_CHUNK, D_MODEL), F32),
            pltpu.SemaphoreType.DMA((4,)),
        ],
        compiler_params=_params(1),
        name="merge_mlp_ctx" if is_ctx else "merge_mlp_smp",
    )(x, mod_l, o_na, o_gla, gate, gout_t, g_mlp_l, w_o, w_up, w_down)


def kernel(x_prompt, x_sample, cache_k, cache_v, state_fwd, state_bwd, c, c_ctx, w_ada, b_ada, g_attn, w_in,
           g_q, g_k, rpb, w_gf, b_gf, w_gb, b_gb, g_gla_out, w_o, g_mlp, w_up, w_down):
    x_c = x_prompt.reshape(N_CTX, D_MODEL)
    x_s = x_sample.reshape(N_SMP, D_MODEL)

    c_rows = jnp.concatenate([c_ctx[None, :], c, jnp.zeros((MOD_ROWS - 1 - DEC_BATCH, D_MODEL), F32)], axis=0)
    mods = _ada_call(c_rows, w_ada, b_ada).reshape(DEPTH, MOD_ROWS, 1, N_MOD * D_MODEL)

    w_in_t = jnp.swapaxes(w_in, 1, 2)
    wgate = jnp.zeros((DEPTH, 2 * GLA_GATE_RANK, 2 * GLA_K_WIDTH), F32)
    wgate = wgate.at[:, :GLA_GATE_RANK, :GLA_K_WIDTH].set(w_gf)
    wgate = wgate.at[:, GLA_GATE_RANK:, GLA_K_WIDTH:].set(w_gb).astype(BF16)
    bgate = jnp.concatenate([b_gf, b_gb], axis=-1).reshape(DEPTH, 1, 2 * GLA_K_WIDTH)
    head_of = np.arange(NA_WIDTH) // NA_HEAD_DIM
    seg = jnp.asarray((head_of[:, None] == head_of[None, :]).astype(np.float32) / NA_HEAD_DIM, BF16)
    gq_t = jnp.tile(g_q, (1, NA_HEADS)).reshape(DEPTH, 1, NA_WIDTH)
    gk_t = jnp.tile(g_k, (1, NA_HEADS)).reshape(DEPTH, 1, NA_WIDTH)
    gout_t = jnp.tile(g_gla_out, (1, GLA_HEADS)).reshape(DEPTH, 1, GLA_V_WIDTH)
    n_dr, n_dc = rpb.shape[2:]
    rpb_rows = jnp.pad(rpb, ((0, 0), (0, 0), (0, -n_dr % 8), (0, 2 * GRID_W - n_dc)))
    gla_w_np, gla_masks_np, gla_smask_np, gla_scales_np = _gla_constants()
    gla_consts = (jnp.asarray(gla_w_np, BF16), jnp.asarray(gla_masks_np, F32), jnp.asarray(gla_smask_np, F32),
                  jnp.asarray(gla_scales_np, F32))
    rope = _rope_tables()

    cache_kt = jnp.swapaxes(cache_k, 3, 4)
    cache_vt = jnp.swapaxes(cache_v, 3, 4)
    new_k = jnp.zeros((BATCH, DEPTH, NA_HEADS, NA_HEAD_DIM, SEQ), F32)
    new_v = jnp.zeros((BATCH, DEPTH, NA_HEADS, NA_HEAD_DIM, SEQ), F32)
    new_sf = jnp.zeros((BATCH, DEPTH, GLA_HEADS, GLA_DK, GLA_DV), F32)
    new_sb = jnp.zeros((BATCH, DEPTH, GLA_HEADS, GLA_DK, GLA_DV), F32)

    for l in range(DEPTH):
        proj_w = (g_attn[l].reshape(1, D_MODEL), w_in_t, seg, gq_t[l], gk_t[l], wgate[l], bgate[l])
        mlp_w = (gout_t[l], g_mlp[l].reshape(1, D_MODEL), w_o, w_up, w_down)

        q, k, v, gq, gk, gv, gate, lf, lb, new_k, new_v = _proj_call(
            x_c, mods[l], *proj_w, new_kv=(new_k, new_v), layer=l)
        o_na = _ctx_attn_call(q, k, v)
        o_gla, new_sf, new_sb = _gla_call(gq, gk, gv, lf, lb, gla_consts, new_states=(new_sf, new_sb), layer=l)
        x_c = _merge_mlp_call(x_c, mods[l], o_na, o_gla, gate, *mlp_w, layer=l)

        q, k, v, gq, gk, gv, gate, lf, lb = _proj_call(x_s, mods[l], *proj_w, layer=l)
        o_na = _na_call(q, k, v, cache_kt, cache_vt, rpb_rows, l)
        (o_gla,) = _gla_call(gq, gk, gv, lf, lb, gla_consts, rope=rope,
                             state_in=(state_fwd, state_bwd), layer=l)
        x_s = _merge_mlp_call(x_s, mods[l], o_na, o_gla, gate, *mlp_w, layer=l)

    return (x_c.reshape(BATCH, SEQ, D_MODEL), x_s.reshape(DEC_BATCH, DEC_SEQ, D_MODEL),
            jnp.swapaxes(new_k, 3, 4), jnp.swapaxes(new_v, 3, 4), new_sf, new_sb)
```

```python
import functools

import numpy as np
import jax
import jax.numpy as jnp
from jax import lax
from jax.experimental import pallas as pl
from jax.experimental.pallas import tpu as pltpu

F32 = jnp.float32
BF16 = jnp.bfloat16

D_MODEL = 1024
BATCH = 16
SEQ = 256
DEPTH = 4
DEC_BATCH = 2
DEC_SEQ = 1024
PAST_LEN = 512
GRID_W = 64
GRID_ROWS = DEC_SEQ // GRID_W
NA_WIDTH = D_MODEL // 2
NA_HEAD_DIM = 64
NA_HEADS = NA_WIDTH // NA_HEAD_DIM
NA_WIN_ROWS = 8
NA_WIN_COLS = 16
GLA_V_WIDTH = D_MODEL - NA_WIDTH
GLA_HEADS = 4
GLA_DV = GLA_V_WIDTH // GLA_HEADS
GLA_DK = GLA_DV // 2
GLA_K_WIDTH = GLA_HEADS * GLA_DK
GLA_GATE_RANK = 16
GLA_GATE_TAU = 16.0
GLA_CHUNK = 64
D_FF = 4 * D_MODEL
ROPE_BASE = 10000.0
N_MOD = 6
EPS = 1e-6

IN_WIDTH = 3 * NA_WIDTH + 2 * GLA_K_WIDTH + 2 * GLA_V_WIDTH + 2 * GLA_GATE_RANK
PROJ_WIDE = IN_WIDTH - 2 * GLA_GATE_RANK

N_CTX = BATCH * SEQ
N_SMP = DEC_BATCH * DEC_SEQ
MOD_ROWS = 8
MASK_VALUE = -1e30

TOKEN_TILE = 512
PROJ_BLOCK = 512
FF_CHUNK = 512
MLP_CHUNK = 1024
ADA_TILE_N = 1536
GLA_LEVELS = 6
NA_ROW_GROUP = 4
GLA_CTX_SEQS_PER_STEP = 4
CTX_ATTN_SEQS_PER_STEP = 2
VMEM_LIMIT = 52 * 1024 * 1024


def _dot(a, b):
    return jnp.dot(a, b, preferred_element_type=F32)


def _dot_nt(a, b):
    return lax.dot_general(a, b, (((1,), (1,)), ((), ())), preferred_element_type=F32)


def _dot_tn(a, b):
    return lax.dot_general(a, b, (((0,), (0,)), ((), ())), preferred_element_type=F32)


def _params(n_grid_dims=1):
    return pltpu.CompilerParams(dimension_semantics=("arbitrary",) * n_grid_dims,
                                vmem_limit_bytes=VMEM_LIMIT)


def _mod_index_map(is_ctx):
    if is_ctx:
        return lambda i: (0, 0, 0)
    return lambda i: (1 + i // (DEC_SEQ // TOKEN_TILE), 0, 0)


def _ada_kernel(c_ref, w_ref, b_ref, o_ref):
    cv = c_ref[...]
    s = cv * (1.0 / (1.0 + jnp.exp(-cv)))
    o_ref[0] = _dot(s.astype(BF16), w_ref[0].astype(BF16)) + b_ref[0]


def _ada_call(c_rows, w_ada, b_ada):
    n_mod = N_MOD * D_MODEL
    return pl.pallas_call(
        _ada_kernel,
        grid=(DEPTH, n_mod // ADA_TILE_N),
        in_specs=[
            pl.BlockSpec((MOD_ROWS, D_MODEL), lambda l, j: (0, 0)),
            pl.BlockSpec((1, D_MODEL, ADA_TILE_N), lambda l, j: (l, 0, j)),
            pl.BlockSpec((1, 1, ADA_TILE_N), lambda l, j: (l, 0, j)),
        ],
        out_specs=pl.BlockSpec((1, MOD_ROWS, ADA_TILE_N), lambda l, j: (l, 0, j)),
        out_shape=jax.ShapeDtypeStruct((DEPTH, MOD_ROWS, n_mod), F32),
        compiler_params=_params(2),
        name="ada_mod",
    )(c_rows, w_ada, b_ada.reshape(DEPTH, 1, n_mod))


def _log_sigmoid(x):
    return jnp.minimum(x, 0.0) - jnp.log(1.0 + jnp.exp(-jnp.abs(x)))


def _proj_kernel(is_ctx, layer, x_ref, mod_ref, g_ref, seg_ref, gq_ref, gk_ref, wgate_ref, bgate_ref, win_hbm,
                 *rest):
    if is_ctx:
        rest = rest[2:]
    q_out, k_out, v_out, gq_out, gk_out, gv_out, gate_out, lf_out, lb_out = rest[:9]
    rest = rest[9:]
    if is_ctx:
        knew_out, vnew_out = rest[:2]
        rest = rest[2:]
    w_scr, stage, tail_stage, sem = rest

    n_blocks = PROJ_WIDE // PROJ_BLOCK
    tail = slice(PROJ_WIDE, IN_WIDTH)
    first_step = pl.program_id(0) == 0

    def block_copy(j):
        src = win_hbm.at[layer, j * PROJ_BLOCK:(j + 1) * PROJ_BLOCK, :]
        return pltpu.make_async_copy(src, stage.at[j % 2], sem.at[j % 2])

    def tail_copy():
        return pltpu.make_async_copy(win_hbm.at[layer, tail, :], tail_stage, sem.at[2])

    def land(j):
        block_copy(j).wait()
        w_scr[j * PROJ_BLOCK:(j + 1) * PROJ_BLOCK, :] = stage[j % 2].astype(BF16)
        if j + 2 < n_blocks:
            block_copy(j + 2).start()

    def run(streaming):
        if streaming:
            block_copy(0).start()
            block_copy(1).start()
            tail_copy().start()

        x = x_ref[...]
        mod = mod_ref[0]
        sh1 = mod[:, 0:D_MODEL]
        sc1 = mod[:, D_MODEL:2 * D_MODEL]
        ms = jnp.mean(x * x, axis=-1, keepdims=True)
        h = (x * lax.rsqrt(ms + EPS) * g_ref[...]) * (1.0 + sc1) + sh1
        hb = h.astype(BF16)

        def projected(j):
            if streaming:
                land(j)
            return _dot_nt(hb, w_scr[j * PROJ_BLOCK:(j + 1) * PROJ_BLOCK, :])

        seg = seg_ref[...]
        q = projected(0)
        q_ms = _dot((q * q).astype(BF16), seg)
        q_out[...] = (q * lax.rsqrt(q_ms + EPS) * gq_ref[...] * (NA_HEAD_DIM ** -0.5)).astype(BF16)
        k = projected(1)
        k_ms = _dot((k * k).astype(BF16), seg)
        kn = k * lax.rsqrt(k_ms + EPS) * gk_ref[...]
        k_out[...] = kn.astype(BF16)
        v = projected(2)
        v_out[...] = v.astype(BF16)
        if is_ctx:
            for s in range(TOKEN_TILE // SEQ):
                rows = slice(s * SEQ, (s + 1) * SEQ)
                for hp in range(NA_HEADS // 2):
                    cols = slice(hp * 2 * NA_HEAD_DIM, (hp + 1) * 2 * NA_HEAD_DIM)
                    for out, t in ((knew_out, kn[rows, cols].T), (vnew_out, v[rows, cols].T)):
                        out[s, 0, 2 * hp] = t[:NA_HEAD_DIM]
                        out[s, 0, 2 * hp + 1] = t[NA_HEAD_DIM:]

        z = projected(3)
        gq_out[...] = z[:, 0:GLA_K_WIDTH] * (GLA_DK ** -0.5)
        gk_out[...] = z[:, GLA_K_WIDTH:]
        gv_out[...] = projected(4).astype(BF16)
        gate_out[...] = projected(5)

        if streaming:
            tail_copy().wait()
            w_scr[tail, :] = tail_stage[...].astype(BF16)
        zz = _dot_nt(hb, w_scr[tail, :])
        pre = _dot(zz.astype(BF16), wgate_ref[...]) + bgate_ref[...]
        ls = _log_sigmoid(pre) * (1.0 / GLA_GATE_TAU)
        lf_out[...] = ls[:, 0:GLA_K_WIDTH]
        lb_out[...] = ls[:, GLA_K_WIDTH:]

    pl.when(first_step)(functools.partial(run, True))
    pl.when(jnp.logical_not(first_step))(functools.partial(run, False))


def _proj_call(x, mod_l, g_attn_l, w_in_t, seg, gq_t, gk_t, wgate, bgate, new_kv=None, layer=0):
    tm = TOKEN_TILE
    n_tok = x.shape[0]
    is_ctx = new_kv is not None
    row = lambda i: (i, 0)
    const = lambda i: (0, 0)
    out_shapes = [
        jax.ShapeDtypeStruct((n_tok, NA_WIDTH), BF16),
        jax.ShapeDtypeStruct((n_tok, NA_WIDTH), BF16),
        jax.ShapeDtypeStruct((n_tok, NA_WIDTH), BF16),
        jax.ShapeDtypeStruct((n_tok, GLA_K_WIDTH), F32),
        jax.ShapeDtypeStruct((n_tok, GLA_K_WIDTH), F32),
        jax.ShapeDtypeStruct((n_tok, GLA_V_WIDTH), BF16),
        jax.ShapeDtypeStruct((n_tok, GLA_V_WIDTH), F32),
        jax.ShapeDtypeStruct((n_tok, GLA_K_WIDTH), F32),
        jax.ShapeDtypeStruct((n_tok, GLA_K_WIDTH), F32),
    ]
    out_specs = [pl.BlockSpec((tm, s.shape[1]), row) for s in out_shapes]
    in_specs = [
        pl.BlockSpec((tm, D_MODEL), row),
        pl.BlockSpec((1, 1, N_MOD * D_MODEL), _mod_index_map(is_ctx)),
        pl.BlockSpec((1, D_MODEL), const),
        pl.BlockSpec((NA_WIDTH, NA_WIDTH), const),
        pl.BlockSpec((1, NA_WIDTH), const),
        pl.BlockSpec((1, NA_WIDTH), const),
        pl.BlockSpec(wgate.shape, const),
        pl.BlockSpec((1, 2 * GLA_K_WIDTH), const),
        pl.BlockSpec(memory_space=pl.ANY),
    ]
    args = [x, mod_l, g_attn_l, seg, gq_t, gk_t, wgate, bgate, w_in_t]
    aliases = {}
    if is_ctx:
        kv_spec = pl.BlockSpec((tm // SEQ, 1, NA_HEADS, NA_HEAD_DIM, SEQ), lambda i: (i, layer, 0, 0, 0))
        aliases = {len(args): len(out_shapes), len(args) + 1: len(out_shapes) + 1}
        in_specs += [pl.BlockSpec(memory_space=pl.ANY)] * 2
        args += list(new_kv)
        out_shapes += [jax.ShapeDtypeStruct(new_kv[0].shape, F32)] * 2
        out_specs += [kv_spec, kv_spec]
    return pl.pallas_call(
        functools.partial(_proj_kernel, is_ctx, layer),
        grid=(n_tok // tm,),
        in_specs=in_specs,
        out_specs=out_specs,
        out_shape=out_shapes,
        input_output_aliases=aliases,
        scratch_shapes=[
            pltpu.VMEM((IN_WIDTH, D_MODEL), BF16),
            pltpu.VMEM((2, PROJ_BLOCK, D_MODEL), F32),
            pltpu.VMEM((IN_WIDTH - PROJ_WIDE, D_MODEL), F32),
            pltpu.SemaphoreType.DMA((3,)),
        ],
        compiler_params=_params(1),
        name="proj_ctx" if is_ctx else "proj_smp",
    )(*args)


def _ctx_attn_kernel(q_ref, k_ref, v_ref, o_ref):
    lane = lax.broadcasted_iota(jnp.int32, (1, 2 * NA_HEAD_DIM), 1)
    first = lane < NA_HEAD_DIM
    pairs = [(slice(s * SEQ, (s + 1) * SEQ), slice(hp * 2 * NA_HEAD_DIM, (hp + 1) * 2 * NA_HEAD_DIM))
             for s in range(CTX_ATTN_SEQS_PER_STEP) for hp in range(NA_HEADS // 2)]
    scores = []
    for rows, cols in pairs:
        q2 = q_ref[rows, cols]
        k2 = k_ref[rows, cols]
        scores.append(_dot_nt(jnp.where(first, q2, jnp.zeros_like(q2)), k2))
        scores.append(_dot_nt(jnp.where(first, jnp.zeros_like(q2), q2), k2))
    probs = []
    for s in scores:
        p = jnp.exp(s - jnp.max(s, axis=-1, keepdims=True))
        probs.append((p.astype(BF16), jnp.sum(p, axis=-1, keepdims=True)))
    for i, (rows, cols) in enumerate(pairs):
        v2 = v_ref[rows, cols]
        (p0, l0), (p1, l1) = probs[2 * i], probs[2 * i + 1]
        o_ref[rows, cols] = jnp.where(first, _dot(p0, v2) / l0, _dot(p1, v2) / l1).astype(BF16)


def _ctx_attn_call(q, k, v):
    spec = pl.BlockSpec((CTX_ATTN_SEQS_PER_STEP * SEQ, NA_WIDTH), lambda b: (b, 0))
    return pl.pallas_call(
        _ctx_attn_kernel,
        grid=(BATCH // CTX_ATTN_SEQS_PER_STEP,),
        in_specs=[spec, spec, spec],
        out_specs=spec,
        out_shape=jax.ShapeDtypeStruct((N_CTX, NA_WIDTH), BF16),
        compiler_params=_params(1),
        name="ctx_attn",
    )(q, k, v)


def _na_window_start(r):
    return min(max(r - NA_WIN_ROWS // 2, 0), GRID_ROWS - NA_WIN_ROWS)


def _na_row_groups():
    groups = []
    for g in range(GRID_ROWS // NA_ROW_GROUP):
        starts = [_na_window_start(r) for r in range(g * NA_ROW_GROUP, (g + 1) * NA_ROW_GROUP)]
        lo = min(starts) // 2 * 2
        count = max(starts) + NA_WIN_ROWS - lo
        groups.append((lo, count + count % 2))
    return groups


def _na_bias_tiles(rpb_ref, hh, tiles_scr):
    n_dr = 2 * NA_WIN_ROWS - 1
    lanes = 2 * GRID_W
    lane = lax.broadcasted_iota(jnp.int32, (GRID_W, lanes), 1)
    col = lax.broadcasted_iota(jnp.int32, (GRID_W, lanes), 0)
    first = lane < GRID_W
    c2 = lane % GRID_W
    cs = jnp.clip(col - NA_WIN_COLS // 2, 0, GRID_W - NA_WIN_COLS)
    valid = (c2 >= cs) & (c2 < cs + NA_WIN_COLS)
    rows = rpb_ref[0, hh]
    base = lanes - (NA_WIN_COLS - 1)

    def skew(dr, shift):
        row = jnp.broadcast_to(rows[dr:dr + 1], (GRID_W, lanes))
        return pltpu.roll(row, shift, axis=1, stride=1, stride_axis=0)

    for e in range(n_dr - 1):
        tile = jnp.where(first, skew(e, base), skew(e + 1, (base + GRID_W) % lanes))
        tiles_scr[e] = jnp.where(valid, tile, MASK_VALUE)


def _na_group_bias(tiles_scr, g, key_lo, key_count, first):
    masked = jnp.full((GRID_W, 2 * GRID_W), MASK_VALUE, F32)
    row_tiles = []
    for r in range(g * NA_ROW_GROUP, (g + 1) * NA_ROW_GROUP):
        rs = _na_window_start(r)
        pieces = []
        for u in range(key_count // 2):
            r0 = key_lo + 2 * u
            in0 = rs <= r0 < rs + NA_WIN_ROWS
            in1 = rs <= r0 + 1 < rs + NA_WIN_ROWS
            if not (in0 or in1):
                pieces.append(masked)
                continue
            tile = tiles_scr[r0 - r + NA_WIN_ROWS - 1]
            if not in1:
                tile = jnp.where(first, tile, MASK_VALUE)
            elif not in0:
                tile = jnp.where(first, MASK_VALUE, tile)
            pieces.append(tile)
        row_tiles.append(jnp.concatenate(pieces, axis=1))
    return jnp.concatenate(row_tiles, axis=0)


def _na_kernel(q_ref, k_ref, v_ref, kc_ref, vc_ref, rpb_ref, o_ref, sc_ref, tiles_scr):
    lane = lax.broadcasted_iota(jnp.int32, (1, 2 * NA_HEAD_DIM), 1)
    first = lane < NA_HEAD_DIM
    q2 = q_ref[...]
    kc2 = jnp.concatenate([kc_ref[0, 0, 0], kc_ref[0, 0, 1]], axis=0).astype(BF16)
    vc2 = jnp.concatenate([vc_ref[0, 0, 0], vc_ref[0, 0, 1]], axis=0).astype(BF16)
    block = NA_ROW_GROUP * GRID_W
    outs = []
    for hh in range(2):
        qm = jnp.where(first, q2, jnp.zeros_like(q2)) if hh == 0 else jnp.where(first, jnp.zeros_like(q2), q2)
        sc_ref[...] = _dot(qm, kc2)
        _na_bias_tiles(rpb_ref, hh, tiles_scr)
        blocks = []
        for g, (key_lo, key_count) in enumerate(_na_row_groups()):
            keys = slice(key_lo * GRID_W, (key_lo + key_count) * GRID_W)
            s = _dot_nt(qm[g * block:(g + 1) * block], k_ref[keys, :])
            s = s + _na_group_bias(tiles_scr, g, key_lo, key_count, first)
            sc = sc_ref[g * block:(g + 1) * block, :]
            m = jnp.maximum(jnp.max(s, axis=-1, keepdims=True), jnp.max(sc, axis=-1, keepdims=True))
            p = jnp.exp(s - m)
            pc = jnp.exp(sc - m)
            l = jnp.sum(p, axis=-1, keepdims=True) + jnp.sum(pc, axis=-1, keepdims=True)
            blocks.append((_dot(p.astype(BF16), v_ref[keys, :]) + _dot_nt(pc.astype(BF16), vc2)) / l)
        outs.append(jnp.concatenate(blocks, axis=0))
    o_ref[...] = jnp.where(first, outs[0], outs[1]).astype(BF16)


def _na_call(q, k, v, cache_kt, cache_vt, rpb_rows, layer):
    tok = pl.BlockSpec((DEC_SEQ, 2 * NA_HEAD_DIM), lambda b, hp: (b, hp))
    cache = pl.BlockSpec((1, 1, 2, NA_HEAD_DIM, PAST_LEN), lambda b, hp: (b, layer, hp, 0, 0))
    n_pairs = 2 * NA_WIN_ROWS - 2
    return pl.pallas_call(
        _na_kernel,
        grid=(DEC_BATCH, NA_HEADS // 2),
        in_specs=[
            tok, tok, tok, cache, cache,
            pl.BlockSpec((1, 2) + rpb_rows.shape[2:], lambda b, hp: (layer, hp, 0, 0)),
        ],
        out_specs=tok,
        out_shape=jax.ShapeDtypeStruct((N_SMP, NA_WIDTH), BF16),
        scratch_shapes=[pltpu.VMEM((DEC_SEQ, PAST_LEN), F32),
                        pltpu.VMEM((n_pairs, GRID_W, 2 * GRID_W), F32)],
        compiler_params=_params(2),
        name="nbr_attn",
    )(q, k, v, cache_kt, cache_vt, rpb_rows)


def _gla_constants():
    C = GLA_CHUNK
    ii, jj = np.meshgrid(np.arange(C), np.arange(C), indexing="ij")
    tri = np.stack([jj <= ii, jj >= ii]).astype(np.float32)
    w = np.concatenate([tri, tri], axis=-1)

    x = ii ^ jj
    hb = np.where(x > 0, np.floor(np.log2(np.maximum(x, 1))), -1).astype(np.int64)
    masks = np.zeros((2, GLA_LEVELS + 1, C, C), np.float32)
    for p in range(GLA_LEVELS):
        masks[0, p] = (jj < ii) & (hb == p)
        masks[1, p] = (jj > ii) & (hb == p)
    masks[:, GLA_LEVELS] = np.eye(C)
    masks = np.tile(masks, (1, 1, 2, 2))
    state_mask = (np.arange(2 * GLA_DV)[:, None] // GLA_DV == np.arange(2 * GLA_DK)[None, :] // GLA_DK)
    upper = np.stack([(np.arange(C) >> p) & 1 for p in range(GLA_LEVELS)]).astype(bool)
    sign = np.stack([np.where(upper, 1.0, -1.0), np.where(upper, -1.0, 1.0)])
    scales = np.broadcast_to((sign * np.log2(np.e))[..., None], (2, GLA_LEVELS, C, GLA_K_WIDTH))
    return w, masks, state_mask.astype(np.float32), np.ascontiguousarray(scales, np.float32)


def _gla_kernel(seq_len, n_seq, is_ctx, *refs):
    refs = list(refs)
    gq_ref, gk_ref, gv_ref, lf_ref, lb_ref, w_ref, mask_ref, smask_ref, scale_ref = refs[:9]
    refs = refs[9:]
    if is_ctx:
        refs = refs[2:]
        o_ref, sf_ref, sb_ref = refs[:3]
        refs = refs[3:]
    else:
        cos_ref, sin_ref, s0f_ref, s0b_ref, o_ref = refs[:5]
        refs = refs[5:]
    q_scr, k_scr, of_scr, ob_scr, st_scr, cum_scr = refs

    C = GLA_CHUNK
    n_chunks = seq_len // C
    n_pairs = GLA_HEADS // 2
    pair_k = 2 * GLA_DK
    pair_v = 2 * GLA_DV

    if is_ctx:
        q_src, k_src = gq_ref, gk_ref
    else:
        q_src, k_src = q_scr, k_scr
        lane = lax.broadcasted_iota(jnp.int32, (1, GLA_K_WIDTH), 1)
        quarter = GLA_DK // 4
        low = (lane % (2 * quarter)) < quarter
        cos = jnp.concatenate([cos_ref[...]] * n_seq, axis=0)
        sin = jnp.concatenate([sin_ref[...]] * n_seq, axis=0)

        def rope(t):
            partner = jnp.where(low, pltpu.roll(t, GLA_K_WIDTH - quarter, axis=1), pltpu.roll(t, quarter, axis=1))
            return t * cos + partner * sin

        q_scr[...] = rope(gq_ref[...])
        k_scr[...] = rope(gk_ref[...])

    st_scr[...] = jnp.zeros(st_scr.shape, F32)
    if not is_ctx:
        for d, s0_ref in enumerate((s0f_ref, s0b_ref)):
            for s in range(n_seq):
                for h in range(GLA_HEADS):
                    pr, e = divmod(h, 2)
                    st_scr[s, d, pr, e * GLA_DV:(e + 1) * GLA_DV, e * GLA_DK:(e + 1) * GLA_DK] = s0_ref[s, 0, h].T

    klane = lax.broadcasted_iota(jnp.int32, (1, GLA_K_WIDTH), 1)
    even_head = (klane // GLA_DK) % 2 == 0

    def chunk_decay(s, d, r0, f_ref):
        rows = pl.ds(s * seq_len + r0, C)
        f = f_ref[rows, :]
        f_hi = f.astype(BF16)
        f_lo = (f - f_hi.astype(F32)).astype(BF16)
        cum = _dot(w_ref[d], jnp.concatenate([f_hi, f_lo], axis=0))
        cum_scr[s, d] = cum
        return rows, cum

    def chunk_scores(s, d, rows, cum):
        def cum_rows(r, n):
            return jnp.broadcast_to(cum_scr[s, d, r:r + 1, :], (n, GLA_K_WIDTH))

        def level_factor(p):
            m = 1 << p
            edge = m - 1 if d == 0 else m
            if 2 * m >= 8:
                ref = jnp.concatenate([cum_rows(blk + edge, 2 * m) for blk in range(0, C, 2 * m)], axis=0)
            elif p == 1:
                sub = lax.broadcasted_iota(jnp.int32, (8, GLA_K_WIDTH), 0)
                ref = jnp.concatenate([jnp.where(sub < 4, cum_rows(blk + edge, 8), cum_rows(blk + 4 + edge, 8))
                                       for blk in range(0, C, 8)], axis=0)
            else:
                odd = lax.broadcasted_iota(jnp.int32, (C, GLA_K_WIDTH), 0) % 2 == 1
                if d == 0:
                    ref = jnp.where(odd, pltpu.roll(cum, 1, axis=0), cum)
                else:
                    ref = jnp.where(odd, cum, pltpu.roll(cum, C - 1, axis=0))
            return jnp.exp2((cum - ref) * scale_ref[d, p])

        qc = q_src[rows, :]
        kc = k_src[rows, :]
        vc = gv_ref[rows, :]
        last = C - 1 if d == 0 else 0
        eq = jnp.exp(cum)
        ek = jnp.exp(cum_rows(last, C) - cum)
        total = jnp.exp(cum_scr[s, d, last:last + 1, :])

        a = None
        for p in range(GLA_LEVELS + 1):
            if p < GLA_LEVELS:
                ep = level_factor(p)
                qe = (qc * ep).astype(BF16)
                ke = (kc * ep).astype(BF16)
            else:
                qe = qc.astype(BF16)
                ke = kc.astype(BF16)
            zero = jnp.zeros((C, pair_k), BF16)
            lhs = jnp.concatenate([jnp.concatenate([qe[:, :pair_k], zero], axis=1),
                                   jnp.concatenate([zero, qe[:, pair_k:]], axis=1)], axis=0)
            rhs = jnp.concatenate([jnp.where(even_head, ke, jnp.zeros_like(ke)),
                                   jnp.where(even_head, jnp.zeros_like(ke), ke)], axis=0)
            sc = _dot_nt(lhs, rhs) * mask_ref[d, p]
            a = sc if a is None else a + sc
        q_in = (qc * eq).astype(BF16)
        k_out = (kc * ek).astype(BF16)
        return a.astype(BF16), q_in, k_out, vc, total

    def chunk_output(s, d, rows, o_scr, a, q_in, k_out, vc, total):
        zero_v = jnp.zeros((C, GLA_DV), BF16)
        for pr in range(n_pairs):
            kl = slice(pr * pair_k, (pr + 1) * pair_k)
            vp = vc[:, pr * pair_v:(pr + 1) * pair_v]
            v_diag = jnp.concatenate([jnp.concatenate([vp[:, :GLA_DV], zero_v], axis=1),
                                      jnp.concatenate([zero_v, vp[:, GLA_DV:]], axis=1)], axis=0)
            st = st_scr[s, d, pr]
            o_scr[rows, pr * pair_v:(pr + 1) * pair_v] = (
                _dot(a[pr * C:(pr + 1) * C], v_diag) + _dot_nt(q_in[:, kl], st.astype(BF16)))
            st_scr[s, d, pr] = st * total[:, kl] + _dot_tn(vp, k_out[:, kl]) * smask_ref[...]

    def body(n, carry):
        fwd = pl.multiple_of(n * C, C)
        bwd = pl.multiple_of((n_chunks - 1 - n) * C, C)
        scans = [(s, d) for s in range(n_seq) for d in range(2)]
        decays = [chunk_decay(s, d, bwd if d else fwd, lb_ref if d else lf_ref) for s, d in scans]
        scores = [chunk_scores(s, d, *dec) for (s, d), dec in zip(scans, decays)]
        for (s, d), (rows, _), sc in zip(scans, decays, scores):
            chunk_output(s, d, rows, ob_scr if d else of_scr, *sc)
        return carry

    lax.fori_loop(0, n_chunks, body, 0)
    o_ref[...] = of_scr[...] + ob_scr[...]
    if is_ctx:
        for d, s_ref in enumerate((sf_ref, sb_ref)):
            for s in range(n_seq):
                for h in range(GLA_HEADS):
                    pr, e = divmod(h, 2)
                    s_ref[s, 0, h] = st_scr[s, d, pr, e * GLA_DV:(e + 1) * GLA_DV, e * GLA_DK:(e + 1) * GLA_DK].T


def _gla_call(gq, gk, gv, lf, lb, consts, new_states=None, rope=None, state_in=None, layer=0):
    is_ctx = new_states is not None
    seq_len = SEQ if is_ctx else DEC_SEQ
    n_tok = gq.shape[0]
    n_seq = GLA_CTX_SEQS_PER_STEP if is_ctx else DEC_BATCH
    rows = n_seq * seq_len
    tok = lambda width: pl.BlockSpec((rows, width), lambda b: (b, 0))
    whole = lambda a: pl.BlockSpec(a.shape, lambda b: (0,) * a.ndim)
    state_spec = pl.BlockSpec((n_seq, 1, GLA_HEADS, GLA_DK, GLA_DV), lambda b: (b, layer, 0, 0, 0))
    in_specs = [tok(GLA_K_WIDTH), tok(GLA_K_WIDTH), tok(GLA_V_WIDTH), tok(GLA_K_WIDTH), tok(GLA_K_WIDTH)]
    in_specs += [whole(a) for a in consts]
    args = [gq, gk, gv, lf, lb, *consts]
    out_shape = [jax.ShapeDtypeStruct((n_tok, GLA_V_WIDTH), F32)]
    out_specs = [tok(GLA_V_WIDTH)]
    aliases = {}
    if is_ctx:
        aliases = {len(args): 1, len(args) + 1: 2}
        in_specs += [pl.BlockSpec(memory_space=pl.ANY)] * 2
        args += list(new_states)
        out_shape += [jax.ShapeDtypeStruct(new_states[0].shape, F32)] * 2
        out_specs += [state_spec, state_spec]
    else:
        in_specs += [pl.BlockSpec((seq_len, GLA_K_WIDTH), lambda b: (0, 0))] * 2 + [state_spec, state_spec]
        args += list(rope) + list(state_in)
    return pl.pallas_call(
        functools.partial(_gla_kernel, seq_len, n_seq, is_ctx),
        grid=(n_tok // rows,),
        in_specs=in_specs,
        out_specs=out_specs,
        out_shape=out_shape,
        input_output_aliases=aliases,
        scratch_shapes=[
            pltpu.VMEM((rows, GLA_K_WIDTH), F32),
            pltpu.VMEM((rows, GLA_K_WIDTH), F32),
            pltpu.VMEM((rows, GLA_V_WIDTH), F32),
            pltpu.VMEM((rows, GLA_V_WIDTH), F32),
            pltpu.VMEM((n_seq, 2, GLA_HEADS // 2, 2 * GLA_DV, 2 * GLA_DK), F32),
            pltpu.VMEM((n_seq, 2, GLA_CHUNK, GLA_K_WIDTH), F32),
        ],
        compiler_params=_params(1),
        name="gla_ctx" if is_ctx else "gla_smp",
    )(*args)


def _rope_tables():
    quarter = GLA_DK // 4
    inv = ROPE_BASE ** (-jnp.arange(quarter, dtype=F32) / quarter)
    t = jnp.arange(DEC_SEQ)
    ang_r = (t // GRID_W).astype(F32)[:, None] * inv
    ang_c = (t % GRID_W).astype(F32)[:, None] * inv
    cos = jnp.concatenate([jnp.cos(ang_r)] * 2 + [jnp.cos(ang_c)] * 2, axis=-1)
    sin = jnp.concatenate([-jnp.sin(ang_r), jnp.sin(ang_r), -jnp.sin(ang_c), jnp.sin(ang_c)], axis=-1)
    return jnp.tile(cos, (1, GLA_HEADS)), jnp.tile(sin, (1, GLA_HEADS))


def _merge_mlp_kernel(layer, x_ref, mod_ref, ona_ref, ogla_ref, gate_ref, gout_ref, gmlp_ref,
                      wo_hbm, wup_hbm, wdown_hbm, o_ref,
                      wo_scr, wup_scr, wdown_scr, stage_cols, stage_rows, sem):
    first_step = pl.program_id(0) == 0
    n_chunks = D_FF // FF_CHUNK
    wo_halves = D_MODEL // FF_CHUNK

    def col_copy(j):
        slot = j % 2
        if j < wo_halves:
            src = wo_hbm.at[layer, :, j * FF_CHUNK:(j + 1) * FF_CHUNK]
        else:
            c = j - wo_halves
            src = wup_hbm.at[layer, :, c * FF_CHUNK:(c + 1) * FF_CHUNK]
        return pltpu.make_async_copy(src, stage_cols.at[slot], sem.at[slot])

    def row_copy(c):
        slot = c % 2
        src = wdown_hbm.at[layer, c * FF_CHUNK:(c + 1) * FF_CHUNK, :]
        return pltpu.make_async_copy(src, stage_rows.at[slot], sem.at[2 + slot])

    n_col_blocks = wo_halves + n_chunks

    def land_col(j):
        col_copy(j).wait()
        block = stage_cols[j % 2].astype(BF16)
        if j < wo_halves:
            wo_scr[:, j * FF_CHUNK:(j + 1) * FF_CHUNK] = block
        else:
            c = j - wo_halves
            wup_scr[:, c * FF_CHUNK:(c + 1) * FF_CHUNK] = block
        if j + 2 < n_col_blocks:
            col_copy(j + 2).start()

    def land_row(c):
        row_copy(c).wait()
        wdown_scr[c * FF_CHUNK:(c + 1) * FF_CHUNK, :] = stage_rows[c % 2].astype(BF16)
        if c + 2 < n_chunks:
            row_copy(c + 2).start()

    def run(streaming):
        if streaming:
            col_copy(0).start()
            col_copy(1).start()
            row_copy(0).start()
            row_copy(1).start()

        mod = mod_ref[0]
        ga1 = mod[:, 2 * D_MODEL:3 * D_MODEL]
        sh2 = mod[:, 3 * D_MODEL:4 * D_MODEL]
        sc2 = mod[:, 4 * D_MODEL:5 * D_MODEL]
        ga2 = mod[:, 5 * D_MODEL:6 * D_MODEL]

        og = ogla_ref[...]
        normed = []
        for h in range(GLA_HEADS):
            oh = og[:, h * GLA_DV:(h + 1) * GLA_DV]
            normed.append(oh * lax.rsqrt(jnp.mean(oh * oh, axis=-1, keepdims=True) + EPS))
        gate = gate_ref[...]
        g = jnp.concatenate(normed, axis=1) * gout_ref[...] * (gate * (1.0 / (1.0 + jnp.exp(-gate))))

        if streaming:
            for j in range(wo_halves):
                land_col(j)
        att = _dot(ona_ref[...], wo_scr[0:NA_WIDTH, :]) + _dot(g.astype(BF16), wo_scr[NA_WIDTH:, :])
        x = x_ref[...] + ga1 * att
        ms = jnp.mean(x * x, axis=-1, keepdims=True)
        h2 = ((x * lax.rsqrt(ms + EPS) * gmlp_ref[...]) * (1.0 + sc2) + sh2).astype(BF16)

        width = FF_CHUNK if streaming else MLP_CHUNK
        acc = None
        for c in range(D_FF // width):
            if streaming:
                land_col(wo_halves + c)
                land_row(c)
            u = jnp.maximum(_dot(h2, wup_scr[:, c * width:(c + 1) * width]), 0.0)
            down = _dot((u * u).astype(BF16), wdown_scr[c * width:(c + 1) * width, :])
            acc = down if acc is None else acc + down
        o_ref[...] = x + ga2 * acc

    pl.when(first_step)(functools.partial(run, True))
    pl.when(jnp.logical_not(first_step))(functools.partial(run, False))


def _merge_mlp_call(x, mod_l, o_na, o_gla, gate, gout_t, g_mlp_l, w_o, w_up, w_down, layer):
    tm = TOKEN_TILE
    n_tok = x.shape[0]
    is_ctx = n_tok == N_CTX
    row = lambda i: (i, 0)
    const = lambda i: (0, 0)
    in_hbm = pl.BlockSpec(memory_space=pl.ANY)
    return pl.pallas_call(
        functools.partial(_merge_mlp_kernel, layer),
        grid=(n_tok // tm,),
        in_specs=[
            pl.BlockSpec((tm, D_MODEL), row),
            pl.BlockSpec((1, 1, N_MOD * D_MODEL), _mod_index_map(is_ctx)),
            pl.BlockSpec((tm, NA_WIDTH), row),
            pl.BlockSpec((tm, GLA_V_WIDTH), row),
            pl.BlockSpec((tm, GLA_V_WIDTH), row),
            pl.BlockSpec((1, GLA_V_WIDTH), const),
            pl.BlockSpec((1, D_MODEL), const),
            in_hbm, in_hbm, in_hbm,
        ],
        out_specs=pl.BlockSpec((tm, D_MODEL), row),
        out_shape=jax.ShapeDtypeStruct((n_tok, D_MODEL), F32),
        scratch_shapes=[
            pltpu.VMEM((D_MODEL, D_MODEL), BF16),
            pltpu.VMEM((D_MODEL, D_FF), BF16),
            pltpu.VMEM((D_FF, D_MODEL), BF16),
            pltpu.VMEM((2, D_MODEL, FF_CHUNK), F32),
            pltpu.VMEM((2, FF_CHUNK, D_MODEL), F32),
            pltpu.SemaphoreType.DMA((4,)),
        ],
        compiler_params=_params(1),
        name="merge_mlp_ctx" if is_ctx else "merge_mlp_smp",
    )(x, mod_l, o_na, o_gla, gate, gout_t, g_mlp_l, w_o, w_up, w_down)


def kernel(x_prompt, x_sample, cache_k, cache_v, state_fwd, state_bwd, c, c_ctx, w_ada, b_ada, g_attn, w_in,
           g_q, g_k, rpb, w_gf, b_gf, w_gb, b_gb, g_gla_out, w_o, g_mlp, w_up, w_down):
    x_c = x_prompt.reshape(N_CTX, D_MODEL)
    x_s = x_sample.reshape(N_SMP, D_MODEL)

    c_rows = jnp.concatenate([c_ctx[None, :], c, jnp.zeros((MOD_ROWS - 1 - DEC_BATCH, D_MODEL), F32)], axis=0)
    mods = _ada_call(c_rows, w_ada, b_ada).reshape(DEPTH, MOD_ROWS, 1, N_MOD * D_MODEL)

    w_in_t = jnp.swapaxes(w_in, 1, 2)
    wgate = jnp.zeros((DEPTH, 2 * GLA_GATE_RANK, 2 * GLA_K_WIDTH), F32)
    wgate = wgate.at[:, :GLA_GATE_RANK, :GLA_K_WIDTH].set(w_gf)
    wgate = wgate.at[:, GLA_GATE_RANK:, GLA_K_WIDTH:].set(w_gb).astype(BF16)
    bgate = jnp.concatenate([b_gf, b_gb], axis=-1).reshape(DEPTH, 1, 2 * GLA_K_WIDTH)
    head_of = np.arange(NA_WIDTH) // NA_HEAD_DIM
    seg = jnp.asarray((head_of[:, None] == head_of[None, :]).astype(np.float32) / NA_HEAD_DIM, BF16)
    gq_t = jnp.tile(g_q, (1, NA_HEADS)).reshape(DEPTH, 1, NA_WIDTH)
    gk_t = jnp.tile(g_k, (1, NA_HEADS)).reshape(DEPTH, 1, NA_WIDTH)
    gout_t = jnp.tile(g_gla_out, (1, GLA_HEADS)).reshape(DEPTH, 1, GLA_V_WIDTH)
    n_dr, n_dc = rpb.shape[2:]
    rpb_rows = jnp.pad(rpb, ((0, 0), (0, 0), (0, -n_dr % 8), (0, 2 * GRID_W - n_dc)))
    gla_w_np, gla_masks_np, gla_smask_np, gla_scales_np = _gla_constants()
    gla_consts = (jnp.asarray(gla_w_np, BF16), jnp.asarray(gla_masks_np, F32), jnp.asarray(gla_smask_np, F32),
                  jnp.asarray(gla_scales_np, F32))
    rope = _rope_tables()

    cache_kt = jnp.swapaxes(cache_k, 3, 4)
    cache_vt = jnp.swapaxes(cache_v, 3, 4)
    new_k = jnp.zeros((BATCH, DEPTH, NA_HEADS, NA_HEAD_DIM, SEQ), F32)
    new_v = jnp.zeros((BATCH, DEPTH, NA_HEADS, NA_HEAD_DIM, SEQ), F32)
    new_sf = jnp.zeros((BATCH, DEPTH, GLA_HEADS, GLA_DK, GLA_DV), F32)
    new_sb = jnp.zeros((BATCH, DEPTH, GLA_HEADS, GLA_DK, GLA_DV), F32)

    for l in range(DEPTH):
        proj_w = (g_attn[l].reshape(1, D_MODEL), w_in_t, seg, gq_t[l], gk_t[l], wgate[l], bgate[l])
        mlp_w = (gout_t[l], g_mlp[l].reshape(1, D_MODEL), w_o, w_up, w_down)

        q, k, v, gq, gk, gv, gate, lf, lb, new_k, new_v = _proj_call(
            x_c, mods[l], *proj_w, new_kv=(new_k, new_v), layer=l)
        o_na = _ctx_attn_call(q, k, v)
        o_gla, new_sf, new_sb = _gla_call(gq, gk, gv, lf, lb, gla_consts, new_states=(new_sf, new_sb), layer=l)
        x_c = _merge_mlp_call(x_c, mods[l], o_na, o_gla, gate, *mlp_w, layer=l)

        q, k, v, gq, gk, gv, gate, lf, lb = _proj_call(x_s, mods[l], *proj_w, layer=l)
        o_na = _na_call(q, k, v, cache_kt, cache_vt, rpb_rows, l)
        (o_gla,) = _gla_call(gq, gk, gv, lf, lb, gla_consts, rope=rope,
                             state_in=(state_fwd, state_bwd), layer=l)
        x_s = _merge_mlp_call(x_s, mods[l], o_na, o_gla, gate, *mlp_w, layer=l)

    return (x_c.reshape(BATCH, SEQ, D_MODEL), x_s.reshape(DEC_BATCH, DEC_SEQ, D_MODEL),
            jnp.swapaxes(new_k, 3, 4), jnp.swapaxes(new_v, 3, 4), new_sf, new_sb)
```

```python
import functools

import numpy as np
import jax
import jax.numpy as jnp
from jax import lax
from jax.experimental import pallas as pl
from jax.experimental.pallas import tpu as pltpu

F32 = jnp.float32
BF16 = jnp.bfloat16

D_MODEL = 1024
BATCH = 16
SEQ = 256
DEPTH = 4
DEC_BATCH = 2
DEC_SEQ = 1024
PAST_LEN = 512
GRID_W = 64
GRID_ROWS = DEC_SEQ // GRID_W
NA_WIDTH = D_MODEL // 2
NA_HEAD_DIM = 64
NA_HEADS = NA_WIDTH // NA_HEAD_DIM
NA_WIN_ROWS = 8
NA_WIN_COLS = 16
GLA_V_WIDTH = D_MODEL - NA_WIDTH
GLA_HEADS = 4
GLA_DV = GLA_V_WIDTH // GLA_HEADS
GLA_DK = GLA_DV // 2
GLA_K_WIDTH = GLA_HEADS * GLA_DK
GLA_GATE_RANK = 16
GLA_GATE_TAU = 16.0
GLA_CHUNK = 64
D_FF = 4 * D_MODEL
ROPE_BASE = 10000.0
N_MOD = 6
EPS = 1e-6

IN_WIDTH = 3 * NA_WIDTH + 2 * GLA_K_WIDTH + 2 * GLA_V_WIDTH + 2 * GLA_GATE_RANK
PROJ_WIDE = IN_WIDTH - 2 * GLA_GATE_RANK

N_CTX = BATCH * SEQ
N_SMP = DEC_BATCH * DEC_SEQ
MOD_ROWS = 8
MASK_VALUE = -1e30

TOKEN_TILE = 512
PROJ_BLOCK = 512
FF_CHUNK = 512
MLP_CHUNK = 1024
ADA_TILE_N = 1536
GLA_LEVELS = 6
NA_ROW_GROUP = 4
GLA_CTX_SEQS_PER_STEP = 4
CTX_ATTN_SEQS_PER_STEP = 2
VMEM_LIMIT = 52 * 1024 * 1024


def _dot(a, b):
    return jnp.dot(a, b, preferred_element_type=F32)


def _dot_nt(a, b):
    return lax.dot_general(a, b, (((1,), (1,)), ((), ())), preferred_element_type=F32)


def _dot_tn(a, b):
    return lax.dot_general(a, b, (((0,), (0,)), ((), ())), preferred_element_type=F32)


def _params(n_grid_dims=1):
    return pltpu.CompilerParams(dimension_semantics=("arbitrary",) * n_grid_dims,
                                vmem_limit_bytes=VMEM_LIMIT)


def _mod_index_map(is_ctx):
    if is_ctx:
        return lambda i: (0, 0, 0)
    return lambda i: (1 + i // (DEC_SEQ // TOKEN_TILE), 0, 0)


def _ada_kernel(c_ref, w_ref, b_ref, o_ref):
    cv = c_ref[...]
    s = cv * (1.0 / (1.0 + jnp.exp(-cv)))
    o_ref[0] = _dot(s.astype(BF16), w_ref[0].astype(BF16)) + b_ref[0]


def _ada_call(c_rows, w_ada, b_ada):
    n_mod = N_MOD * D_MODEL
    return pl.pallas_call(
        _ada_kernel,
        grid=(DEPTH, n_mod // ADA_TILE_N),
        in_specs=[
            pl.BlockSpec((MOD_ROWS, D_MODEL), lambda l, j: (0, 0)),
            pl.BlockSpec((1, D_MODEL, ADA_TILE_N), lambda l, j: (l, 0, j)),
            pl.BlockSpec((1, 1, ADA_TILE_N), lambda l, j: (l, 0, j)),
        ],
        out_specs=pl.BlockSpec((1, MOD_ROWS, ADA_TILE_N), lambda l, j: (l, 0, j)),
        out_shape=jax.ShapeDtypeStruct((DEPTH, MOD_ROWS, n_mod), F32),
        compiler_params=_params(2),
        name="ada_mod",
    )(c_rows, w_ada, b_ada.reshape(DEPTH, 1, n_mod))


def _log_sigmoid(x):
    return jnp.minimum(x, 0.0) - jnp.log(1.0 + jnp.exp(-jnp.abs(x)))


def _proj_kernel(is_ctx, layer, x_ref, mod_ref, g_ref, seg_ref, gq_ref, gk_ref, wgate_ref, bgate_ref, win_hbm,
                 *rest):
    if is_ctx:
        rest = rest[2:]
    q_out, k_out, v_out, gq_out, gk_out, gv_out, gate_out, lf_out, lb_out = rest[:9]
    rest = rest[9:]
    if is_ctx:
        knew_out, vnew_out = rest[:2]
        rest = rest[2:]
    w_scr, stage, tail_stage, sem = rest

    n_blocks = PROJ_WIDE // PROJ_BLOCK
    tail = slice(PROJ_WIDE, IN_WIDTH)
    first_step = pl.program_id(0) == 0

    def block_copy(j):
        src = win_hbm.at[layer, j * PROJ_BLOCK:(j + 1) * PROJ_BLOCK, :]
        return pltpu.make_async_copy(src, stage.at[j % 2], sem.at[j % 2])

    def tail_copy():
        return pltpu.make_async_copy(win_hbm.at[layer, tail, :], tail_stage, sem.at[2])

    def land(j):
        block_copy(j).wait()
        w_scr[j * PROJ_BLOCK:(j + 1) * PROJ_BLOCK, :] = stage[j % 2].astype(BF16)
        if j + 2 < n_blocks:
            block_copy(j + 2).start()

    def run(streaming):
        if streaming:
            block_copy(0).start()
            block_copy(1).start()
            tail_copy().start()

        x = x_ref[...]
        mod = mod_ref[0]
        sh1 = mod[:, 0:D_MODEL]
        sc1 = mod[:, D_MODEL:2 * D_MODEL]
        ms = jnp.mean(x * x, axis=-1, keepdims=True)
        h = (x * lax.rsqrt(ms + EPS) * g_ref[...]) * (1.0 + sc1) + sh1
        hb = h.astype(BF16)

        def projected(j):
            if streaming:
                land(j)
            return _dot_nt(hb, w_scr[j * PROJ_BLOCK:(j + 1) * PROJ_BLOCK, :])

        seg = seg_ref[...]
        q = projected(0)
        q_ms = _dot((q * q).astype(BF16), seg)
        q_out[...] = (q * lax.rsqrt(q_ms + EPS) * gq_ref[...] * (NA_HEAD_DIM ** -0.5)).astype(BF16)
        k = projected(1)
        k_ms = _dot((k * k).astype(BF16), seg)
        kn = k * lax.rsqrt(k_ms + EPS) * gk_ref[...]
        k_out[...] = kn.astype(BF16)
        v = projected(2)
        v_out[...] = v.astype(BF16)
        if is_ctx:
            for s in range(TOKEN_TILE // SEQ):
                rows = slice(s * SEQ, (s + 1) * SEQ)
                for hp in range(NA_HEADS // 2):
                    cols = slice(hp * 2 * NA_HEAD_DIM, (hp + 1) * 2 * NA_HEAD_DIM)
                    for out, t in ((knew_out, kn[rows, cols].T), (vnew_out, v[rows, cols].T)):
                        out[s, 0, 2 * hp] = t[:NA_HEAD_DIM]
                        out[s, 0, 2 * hp + 1] = t[NA_HEAD_DIM:]

        z = projected(3)
        gq_out[...] = z[:, 0:GLA_K_WIDTH] * (GLA_DK ** -0.5)
        gk_out[...] = z[:, GLA_K_WIDTH:]
        gv_out[...] = projected(4).astype(BF16)
        gate_out[...] = projected(5)

        if streaming:
            tail_copy().wait()
            w_scr[tail, :] = tail_stage[...].astype(BF16)
        zz = _dot_nt(hb, w_scr[tail, :])
        pre = _dot(zz.astype(BF16), wgate_ref[...]) + bgate_ref[...]
        ls = _log_sigmoid(pre) * (1.0 / GLA_GATE_TAU)
        lf_out[...] = ls[:, 0:GLA_K_WIDTH]
        lb_out[...] = ls[:, GLA_K_WIDTH:]

    pl.when(first_step)(functools.partial(run, True))
    pl.when(jnp.logical_not(first_step))(functools.partial(run, False))


def _proj_call(x, mod_l, g_attn_l, w_in_t, seg, gq_t, gk_t, wgate, bgate, new_kv=None, layer=0):
    tm = TOKEN_TILE
    n_tok = x.shape[0]
    is_ctx = new_kv is not None
    row = lambda i: (i, 0)
    const = lambda i: (0, 0)
    out_shapes = [
        jax.ShapeDtypeStruct((n_tok, NA_WIDTH), BF16),
        jax.ShapeDtypeStruct((n_tok, NA_WIDTH), BF16),
        jax.ShapeDtypeStruct((n_tok, NA_WIDTH), BF16),
        jax.ShapeDtypeStruct((n_tok, GLA_K_WIDTH), F32),
        jax.ShapeDtypeStruct((n_tok, GLA_K_WIDTH), F32),
        jax.ShapeDtypeStruct((n_tok, GLA_V_WIDTH), BF16),
        jax.ShapeDtypeStruct((n_tok, GLA_V_WIDTH), F32),
        jax.ShapeDtypeStruct((n_tok, GLA_K_WIDTH), F32),
        jax.ShapeDtypeStruct((n_tok, GLA_K_WIDTH), F32),
    ]
    out_specs = [pl.BlockSpec((tm, s.shape[1]), row) for s in out_shapes]
    in_specs = [
        pl.BlockSpec((tm, D_MODEL), row),
        pl.BlockSpec((1, 1, N_MOD * D_MODEL), _mod_index_map(is_ctx)),
        pl.BlockSpec((1, D_MODEL), const),
        pl.BlockSpec((NA_WIDTH, NA_WIDTH), const),
        pl.BlockSpec((1, NA_WIDTH), const),
        pl.BlockSpec((1, NA_WIDTH), const),
        pl.BlockSpec(wgate.shape, const),
        pl.BlockSpec((1, 2 * GLA_K_WIDTH), const),
        pl.BlockSpec(memory_space=pl.ANY),
    ]
    args = [x, mod_l, g_attn_l, seg, gq_t, gk_t, wgate, bgate, w_in_t]
    aliases = {}
    if is_ctx:
        kv_spec = pl.BlockSpec((tm // SEQ, 1, NA_HEADS, NA_HEAD_DIM, SEQ), lambda i: (i, layer, 0, 0, 0))
        aliases = {len(args): len(out_shapes), len(args) + 1: len(out_shapes) + 1}
        in_specs += [pl.BlockSpec(memory_space=pl.ANY)] * 2
        args += list(new_kv)
        out_shapes += [jax.ShapeDtypeStruct(new_kv[0].shape, F32)] * 2
        out_specs += [kv_spec, kv_spec]
    return pl.pallas_call(
        functools.partial(_proj_kernel, is_ctx, layer),
        grid=(n_tok // tm,),
        in_specs=in_specs,
        out_specs=out_specs,
        out_shape=out_shapes,
        input_output_aliases=aliases,
        scratch_shapes=[
            pltpu.VMEM((IN_WIDTH, D_MODEL), BF16),
            pltpu.VMEM((2, PROJ_BLOCK, D_MODEL), F32),
            pltpu.VMEM((IN_WIDTH - PROJ_WIDE, D_MODEL), F32),
            pltpu.SemaphoreType.DMA((3,)),
        ],
        compiler_params=_params(1),
        name="proj_ctx" if is_ctx else "proj_smp",
    )(*args)


def _ctx_attn_kernel(q_ref, k_ref, v_ref, o_ref):
    lane = lax.broadcasted_iota(jnp.int32, (1, 2 * NA_HEAD_DIM), 1)
    first = lane < NA_HEAD_DIM
    pairs = [(slice(s * SEQ, (s + 1) * SEQ), slice(hp * 2 * NA_HEAD_DIM, (hp + 1) * 2 * NA_HEAD_DIM))
             for s in range(CTX_ATTN_SEQS_PER_STEP) for hp in range(NA_HEADS // 2)]
    scores = []
    for rows, cols in pairs:
        q2 = q_ref[rows, cols]
        k2 = k_ref[rows, cols]
        scores.append(_dot_nt(jnp.where(first, q2, jnp.zeros_like(q2)), k2))
        scores.append(_dot_nt(jnp.where(first, jnp.zeros_like(q2), q2), k2))
    probs = []
    for s in scores:
        p = jnp.exp(s - jnp.max(s, axis=-1, keepdims=True))
        probs.append((p.astype(BF16), jnp.sum(p, axis=-1, keepdims=True)))
    for i, (rows, cols) in enumerate(pairs):
        v2 = v_ref[rows, cols]
        (p0, l0), (p1, l1) = probs[2 * i], probs[2 * i + 1]
        o_ref[rows, cols] = jnp.where(first, _dot(p0, v2) / l0, _dot(p1, v2) / l1).astype(BF16)


def _ctx_attn_call(q, k, v):
    spec = pl.BlockSpec((CTX_ATTN_SEQS_PER_STEP * SEQ, NA_WIDTH), lambda b: (b, 0))
    return pl.pallas_call(
        _ctx_attn_kernel,
        grid=(BATCH // CTX_ATTN_SEQS_PER_STEP,),
        in_specs=[spec, spec, spec],
        out_specs=spec,
        out_shape=jax.ShapeDtypeStruct((N_CTX, NA_WIDTH), BF16),
        compiler_params=_params(1),
        name="ctx_attn",
    )(q, k, v)


def _na_window_start(r):
    return min(max(r - NA_WIN_ROWS // 2, 0), GRID_ROWS - NA_WIN_ROWS)


def _na_row_groups():
    groups = []
    for g in range(GRID_ROWS // NA_ROW_GROUP):
        starts = [_na_window_start(r) for r in range(g * NA_ROW_GROUP, (g + 1) * NA_ROW_GROUP)]
        lo = min(starts) // 2 * 2
        count = max(starts) + NA_WIN_ROWS - lo
        groups.append((lo, count + count % 2))
    return groups


def _na_bias_tiles(rpb_ref, hh, tiles_scr):
    n_dr = 2 * NA_WIN_ROWS - 1
    lanes = 2 * GRID_W
    lane = lax.broadcasted_iota(jnp.int32, (GRID_W, lanes), 1)
    col = lax.broadcasted_iota(jnp.int32, (GRID_W, lanes), 0)
    first = lane < GRID_W
    c2 = lane % GRID_W
    cs = jnp.clip(col - NA_WIN_COLS // 2, 0, GRID_W - NA_WIN_COLS)
    valid = (c2 >= cs) & (c2 < cs + NA_WIN_COLS)
    rows = rpb_ref[0, hh]
    base = lanes - (NA_WIN_COLS - 1)

    def skew(dr, shift):
        row = jnp.broadcast_to(rows[dr:dr + 1], (GRID_W, lanes))
        return pltpu.roll(row, shift, axis=1, stride=1, stride_axis=0)

    for e in range(n_dr - 1):
        tile = jnp.where(first, skew(e, base), skew(e + 1, (base + GRID_W) % lanes))
        tiles_scr[e] = jnp.where(valid, tile, MASK_VALUE)


def _na_group_bias(tiles_scr, g, key_lo, key_count, first):
    masked = jnp.full((GRID_W, 2 * GRID_W), MASK_VALUE, F32)
    row_tiles = []
    for r in range(g * NA_ROW_GROUP, (g + 1) * NA_ROW_GROUP):
        rs = _na_window_start(r)
        pieces = []
        for u in range(key_count // 2):
            r0 = key_lo + 2 * u
            in0 = rs <= r0 < rs + NA_WIN_ROWS
            in1 = rs <= r0 + 1 < rs + NA_WIN_ROWS
            if not (in0 or in1):
                pieces.append(masked)
                continue
            tile = tiles_scr[r0 - r + NA_WIN_ROWS - 1]
            if not in1:
                tile = jnp.where(first, tile, MASK_VALUE)
            elif not in0:
                tile = jnp.where(first, MASK_VALUE, tile)
            pieces.append(tile)
        row_tiles.append(jnp.concatenate(pieces, axis=1))
    return jnp.concatenate(row_tiles, axis=0)


def _na_kernel(q_ref, k_ref, v_ref, kc_ref, vc_ref, rpb_ref, o_ref, sc_ref, tiles_scr):
    lane = lax.broadcasted_iota(jnp.int32, (1, 2 * NA_HEAD_DIM), 1)
    first = lane < NA_HEAD_DIM
    q2 = q_ref[...]
    kc2 = jnp.concatenate([kc_ref[0, 0, 0], kc_ref[0, 0, 1]], axis=0).astype(BF16)
    vc2 = jnp.concatenate([vc_ref[0, 0, 0], vc_ref[0, 0, 1]], axis=0).astype(BF16)
    block = NA_ROW_GROUP * GRID_W
    groups = _na_row_groups()
    qms = [jnp.where(first, q2, jnp.zeros_like(q2)), jnp.where(first, jnp.zeros_like(q2), q2)]
    for hh in range(2):
        sc_ref[hh] = _dot(qms[hh], kc2)
        _na_bias_tiles(rpb_ref, hh, tiles_scr.at[hh])
    problems = [(hh, g) for hh in range(2) for g in range(len(groups))]
    scores = []
    for hh, g in problems:
        key_lo, key_count = groups[g]
        keys = slice(key_lo * GRID_W, (key_lo + key_count) * GRID_W)
        s = _dot_nt(qms[hh][g * block:(g + 1) * block], k_ref[keys, :])
        scores.append(s + _na_group_bias(tiles_scr.at[hh], g, key_lo, key_count, first))
    probs = []
    for (hh, g), s in zip(problems, scores):
        sc = sc_ref[hh, g * block:(g + 1) * block, :]
        m = jnp.maximum(jnp.max(s, axis=-1, keepdims=True), jnp.max(sc, axis=-1, keepdims=True))
        p = jnp.exp(s - m)
        pc = jnp.exp(sc - m)
        l = jnp.sum(p, axis=-1, keepdims=True) + jnp.sum(pc, axis=-1, keepdims=True)
        probs.append((p.astype(BF16), pc.astype(BF16), l))
    outs = [[], []]
    for (hh, g), (p, pc, l) in zip(problems, probs):
        key_lo, key_count = groups[g]
        keys = slice(key_lo * GRID_W, (key_lo + key_count) * GRID_W)
        outs[hh].append((_dot(p, v_ref[keys, :]) + _dot_nt(pc, vc2)) / l)
    outs = [jnp.concatenate(o, axis=0) for o in outs]
    o_ref[...] = jnp.where(first, outs[0], outs[1]).astype(BF16)


def _na_call(q, k, v, cache_kt, cache_vt, rpb_rows, layer):
    tok = pl.BlockSpec((DEC_SEQ, 2 * NA_HEAD_DIM), lambda b, hp: (b, hp))
    cache = pl.BlockSpec((1, 1, 2, NA_HEAD_DIM, PAST_LEN), lambda b, hp: (b, layer, hp, 0, 0))
    n_pairs = 2 * NA_WIN_ROWS - 2
    return pl.pallas_call(
        _na_kernel,
        grid=(DEC_BATCH, NA_HEADS // 2),
        in_specs=[
            tok, tok, tok, cache, cache,
            pl.BlockSpec((1, 2) + rpb_rows.shape[2:], lambda b, hp: (layer, hp, 0, 0)),
        ],
        out_specs=tok,
        out_shape=jax.ShapeDtypeStruct((N_SMP, NA_WIDTH), BF16),
        scratch_shapes=[pltpu.VMEM((2, DEC_SEQ, PAST_LEN), F32),
                        pltpu.VMEM((2, n_pairs, GRID_W, 2 * GRID_W), F32)],
        compiler_params=_params(2),
        name="nbr_attn",
    )(q, k, v, cache_kt, cache_vt, rpb_rows)


def _gla_constants():
    C = GLA_CHUNK
    ii, jj = np.meshgrid(np.arange(C), np.arange(C), indexing="ij")
    tri = np.stack([jj <= ii, jj >= ii]).astype(np.float32)
    w = np.concatenate([tri, tri], axis=-1)

    x = ii ^ jj
    hb = np.where(x > 0, np.floor(np.log2(np.maximum(x, 1))), -1).astype(np.int64)
    masks = np.zeros((2, GLA_LEVELS + 1, C, C), np.float32)
    for p in range(GLA_LEVELS):
        masks[0, p] = (jj < ii) & (hb == p)
        masks[1, p] = (jj > ii) & (hb == p)
    masks[:, GLA_LEVELS] = np.eye(C)
    masks = np.tile(masks, (1, 1, 2, 2))
    state_mask = (np.arange(2 * GLA_DV)[:, None] // GLA_DV == np.arange(2 * GLA_DK)[None, :] // GLA_DK)
    upper = np.stack([(np.arange(C) >> p) & 1 for p in range(GLA_LEVELS)]).astype(bool)
    sign = np.stack([np.where(upper, 1.0, -1.0), np.where(upper, -1.0, 1.0)])
    scales = np.broadcast_to((sign * np.log2(np.e))[..., None], (2, GLA_LEVELS, C, GLA_K_WIDTH))
    return w, masks, state_mask.astype(np.float32), np.ascontiguousarray(scales, np.float32)


def _gla_kernel(seq_len, n_seq, is_ctx, *refs):
    refs = list(refs)
    gq_ref, gk_ref, gv_ref, lf_ref, lb_ref, w_ref, mask_ref, smask_ref, scale_ref = refs[:9]
    refs = refs[9:]
    if is_ctx:
        refs = refs[2:]
        o_ref, sf_ref, sb_ref = refs[:3]
        refs = refs[3:]
    else:
        cos_ref, sin_ref, s0f_ref, s0b_ref, o_ref = refs[:5]
        refs = refs[5:]
    q_scr, k_scr, of_scr, ob_scr, st_scr, cum_scr = refs

    C = GLA_CHUNK
    n_chunks = seq_len // C
    n_pairs = GLA_HEADS // 2
    pair_k = 2 * GLA_DK
    pair_v = 2 * GLA_DV

    if is_ctx:
        q_src, k_src = gq_ref, gk_ref
    else:
        q_src, k_src = q_scr, k_scr
        lane = lax.broadcasted_iota(jnp.int32, (1, GLA_K_WIDTH), 1)
        quarter = GLA_DK // 4
        low = (lane % (2 * quarter)) < quarter
        cos = jnp.concatenate([cos_ref[...]] * n_seq, axis=0)
        sin = jnp.concatenate([sin_ref[...]] * n_seq, axis=0)

        def rope(t):
            partner = jnp.where(low, pltpu.roll(t, GLA_K_WIDTH - quarter, axis=1), pltpu.roll(t, quarter, axis=1))
            return t * cos + partner * sin

        q_scr[...] = rope(gq_ref[...])
        k_scr[...] = rope(gk_ref[...])

    st_scr[...] = jnp.zeros(st_scr.shape, F32)
    if not is_ctx:
        for d, s0_ref in enumerate((s0f_ref, s0b_ref)):
            for s in range(n_seq):
                for h in range(GLA_HEADS):
                    pr, e = divmod(h, 2)
                    st_scr[s, d, pr, e * GLA_DV:(e + 1) * GLA_DV, e * GLA_DK:(e + 1) * GLA_DK] = s0_ref[s, 0, h].T

    klane = lax.broadcasted_iota(jnp.int32, (1, GLA_K_WIDTH), 1)
    even_head = (klane // GLA_DK) % 2 == 0

    def chunk_decay(s, d, r0, f_ref):
        rows = pl.ds(s * seq_len + r0, C)
        f = f_ref[rows, :]
        f_hi = f.astype(BF16)
        f_lo = (f - f_hi.astype(F32)).astype(BF16)
        cum = _dot(w_ref[d], jnp.concatenate([f_hi, f_lo], axis=0))
        cum_scr[s, d] = cum
        return rows, cum

    def chunk_scores(s, d, rows, cum):
        def cum_rows(r, n):
            return jnp.broadcast_to(cum_scr[s, d, r:r + 1, :], (n, GLA_K_WIDTH))

        def level_factor(p):
            m = 1 << p
            edge = m - 1 if d == 0 else m
            if 2 * m >= 8:
                ref = jnp.concatenate([cum_rows(blk + edge, 2 * m) for blk in range(0, C, 2 * m)], axis=0)
            elif p == 1:
                sub = lax.broadcasted_iota(jnp.int32, (8, GLA_K_WIDTH), 0)
                ref = jnp.concatenate([jnp.where(sub < 4, cum_rows(blk + edge, 8), cum_rows(blk + 4 + edge, 8))
                                       for blk in range(0, C, 8)], axis=0)
            else:
                odd = lax.broadcasted_iota(jnp.int32, (C, GLA_K_WIDTH), 0) % 2 == 1
                if d == 0:
                    ref = jnp.where(odd, pltpu.roll(cum, 1, axis=0), cum)
                else:
                    ref = jnp.where(odd, cum, pltpu.roll(cum, C - 1, axis=0))
            return jnp.exp2((cum - ref) * scale_ref[d, p])

        qc = q_src[rows, :]
        kc = k_src[rows, :]
        vc = gv_ref[rows, :]
        last = C - 1 if d == 0 else 0
        eq = jnp.exp(cum)
        ek = jnp.exp(cum_rows(last, C) - cum)
        total = jnp.exp(cum_scr[s, d, last:last + 1, :])

        a = None
        for p in range(GLA_LEVELS + 1):
            if p < GLA_LEVELS:
                ep = level_factor(p)
                qe = (qc * ep).astype(BF16)
                ke = (kc * ep).astype(BF16)
            else:
                qe = qc.astype(BF16)
                ke = kc.astype(BF16)
            zero = jnp.zeros((C, pair_k), BF16)
            lhs = jnp.concatenate([jnp.concatenate([qe[:, :pair_k], zero], axis=1),
                                   jnp.concatenate([zero, qe[:, pair_k:]], axis=1)], axis=0)
            rhs = jnp.concatenate([jnp.where(even_head, ke, jnp.zeros_like(ke)),
                                   jnp.where(even_head, jnp.zeros_like(ke), ke)], axis=0)
            sc = _dot_nt(lhs, rhs) * mask_ref[d, p]
            a = sc if a is None else a + sc
        q_in = (qc * eq).astype(BF16)
        k_out = (kc * ek).astype(BF16)
        return a.astype(BF16), q_in, k_out, vc, total

    def chunk_output(s, d, rows, o_scr, a, q_in, k_out, vc, total):
        zero_v = jnp.zeros((C, GLA_DV), BF16)
        for pr in range(n_pairs):
            kl = slice(pr * pair_k, (pr + 1) * pair_k)
            vp = vc[:, pr * pair_v:(pr + 1) * pair_v]
            v_diag = jnp.concatenate([jnp.concatenate([vp[:, :GLA_DV], zero_v], axis=1),
                                      jnp.concatenate([zero_v, vp[:, GLA_DV:]], axis=1)], axis=0)
            st = st_scr[s, d, pr]
            o_scr[rows, pr * pair_v:(pr + 1) * pair_v] = (
                _dot(a[pr * C:(pr + 1) * C], v_diag) + _dot_nt(q_in[:, kl], st.astype(BF16)))
            st_scr[s, d, pr] = st * total[:, kl] + _dot_tn(vp, k_out[:, kl]) * smask_ref[...]

    def body(n, carry):
        fwd = pl.multiple_of(n * C, C)
        bwd = pl.multiple_of((n_chunks - 1 - n) * C, C)
        scans = [(s, d) for s in range(n_seq) for d in range(2)]
        decays = [chunk_decay(s, d, bwd if d else fwd, lb_ref if d else lf_ref) for s, d in scans]
        scores = [chunk_scores(s, d, *dec) for (s, d), dec in zip(scans, decays)]
        for (s, d), (rows, _), sc in zip(scans, decays, scores):
            chunk_output(s, d, rows, ob_scr if d else of_scr, *sc)
        return carry

    lax.fori_loop(0, n_chunks, body, 0)
    o_ref[...] = of_scr[...] + ob_scr[...]
    if is_ctx:
        for d, s_ref in enumerate((sf_ref, sb_ref)):
            for s in range(n_seq):
                for h in range(GLA_HEADS):
                    pr, e = divmod(h, 2)
                    s_ref[s, 0, h] = st_scr[s, d, pr, e * GLA_DV:(e + 1) * GLA_DV, e * GLA_DK:(e + 1) * GLA_DK].T


def _gla_call(gq, gk, gv, lf, lb, consts, new_states=None, rope=None, state_in=None, layer=0):
    is_ctx = new_states is not None
    seq_len = SEQ if is_ctx else DEC_SEQ
    n_tok = gq.shape[0]
    n_seq = GLA_CTX_SEQS_PER_STEP if is_ctx else DEC_BATCH
    rows = n_seq * seq_len
    tok = lambda width: pl.BlockSpec((rows, width), lambda b: (b, 0))
    whole = lambda a: pl.BlockSpec(a.shape, lambda b: (0,) * a.ndim)
    state_spec = pl.BlockSpec((n_seq, 1, GLA_HEADS, GLA_DK, GLA_DV), lambda b: (b, layer, 0, 0, 0))
    in_specs = [tok(GLA_K_WIDTH), tok(GLA_K_WIDTH), tok(GLA_V_WIDTH), tok(GLA_K_WIDTH), tok(GLA_K_WIDTH)]
    in_specs += [whole(a) for a in consts]
    args = [gq, gk, gv, lf, lb, *consts]
    out_shape = [jax.ShapeDtypeStruct((n_tok, GLA_V_WIDTH), F32)]
    out_specs = [tok(GLA_V_WIDTH)]
    aliases = {}
    if is_ctx:
        aliases = {len(args): 1, len(args) + 1: 2}
        in_specs += [pl.BlockSpec(memory_space=pl.ANY)] * 2
        args += list(new_states)
        out_shape += [jax.ShapeDtypeStruct(new_states[0].shape, F32)] * 2
        out_specs += [state_spec, state_spec]
    else:
        in_specs += [pl.BlockSpec((seq_len, GLA_K_WIDTH), lambda b: (0, 0))] * 2 + [state_spec, state_spec]
        args += list(rope) + list(state_in)
    return pl.pallas_call(
        functools.partial(_gla_kernel, seq_len, n_seq, is_ctx),
        grid=(n_tok // rows,),
        in_specs=in_specs,
        out_specs=out_specs,
        out_shape=out_shape,
        input_output_aliases=aliases,
        scratch_shapes=[
            pltpu.VMEM((rows, GLA_K_WIDTH), F32),
            pltpu.VMEM((rows, GLA_K_WIDTH), F32),
            pltpu.VMEM((rows, GLA_V_WIDTH), F32),
            pltpu.VMEM((rows, GLA_V_WIDTH), F32),
            pltpu.VMEM((n_seq, 2, GLA_HEADS // 2, 2 * GLA_DV, 2 * GLA_DK), F32),
            pltpu.VMEM((n_seq, 2, GLA_CHUNK, GLA_K_WIDTH), F32),
        ],
        compiler_params=_params(1),
        name="gla_ctx" if is_ctx else "gla_smp",
    )(*args)


def _rope_tables():
    quarter = GLA_DK // 4
    inv = ROPE_BASE ** (-jnp.arange(quarter, dtype=F32) / quarter)
    t = jnp.arange(DEC_SEQ)
    ang_r = (t // GRID_W).astype(F32)[:, None] * inv
    ang_c = (t % GRID_W).astype(F32)[:, None] * inv
    cos = jnp.concatenate([jnp.cos(ang_r)] * 2 + [jnp.cos(ang_c)] * 2, axis=-1)
    sin = jnp.concatenate([-jnp.sin(ang_r), jnp.sin(ang_r), -jnp.sin(ang_c), jnp.sin(ang_c)], axis=-1)
    return jnp.tile(cos, (1, GLA_HEADS)), jnp.tile(sin, (1, GLA_HEADS))


def _merge_mlp_kernel(layer, x_ref, mod_ref, ona_ref, ogla_ref, gate_ref, gout_ref, gmlp_ref,
                      wo_hbm, wup_hbm, wdown_hbm, o_ref,
                      wo_scr, wup_scr, wdown_scr, stage_cols, stage_rows, sem):
    first_step = pl.program_id(0) == 0
    n_chunks = D_FF // FF_CHUNK
    wo_halves = D_MODEL // FF_CHUNK

    def col_copy(j):
        slot = j % 2
        if j < wo_halves:
            src = wo_hbm.at[layer, :, j * FF_CHUNK:(j + 1) * FF_CHUNK]
        else:
            c = j - wo_halves
            src = wup_hbm.at[layer, :, c * FF_CHUNK:(c + 1) * FF_CHUNK]
        return pltpu.make_async_copy(src, stage_cols.at[slot], sem.at[slot])

    def row_copy(c):
        slot = c % 2
        src = wdown_hbm.at[layer, c * FF_CHUNK:(c + 1) * FF_CHUNK, :]
        return pltpu.make_async_copy(src, stage_rows.at[slot], sem.at[2 + slot])

    n_col_blocks = wo_halves + n_chunks

    def land_col(j):
        col_copy(j).wait()
        block = stage_cols[j % 2].astype(BF16)
        if j < wo_halves:
            wo_scr[:, j * FF_CHUNK:(j + 1) * FF_CHUNK] = block
        else:
            c = j - wo_halves
            wup_scr[:, c * FF_CHUNK:(c + 1) * FF_CHUNK] = block
        if j + 2 < n_col_blocks:
            col_copy(j + 2).start()

    def land_row(c):
        row_copy(c).wait()
        wdown_scr[c * FF_CHUNK:(c + 1) * FF_CHUNK, :] = stage_rows[c % 2].astype(BF16)
        if c + 2 < n_chunks:
            row_copy(c + 2).start()

    def run(streaming):
        if streaming:
            col_copy(0).start()
            col_copy(1).start()
            row_copy(0).start()
            row_copy(1).start()

        mod = mod_ref[0]
        ga1 = mod[:, 2 * D_MODEL:3 * D_MODEL]
        sh2 = mod[:, 3 * D_MODEL:4 * D_MODEL]
        sc2 = mod[:, 4 * D_MODEL:5 * D_MODEL]
        ga2 = mod[:, 5 * D_MODEL:6 * D_MODEL]

        og = ogla_ref[...]
        normed = []
        for h in range(GLA_HEADS):
            oh = og[:, h * GLA_DV:(h + 1) * GLA_DV]
            normed.append(oh * lax.rsqrt(jnp.mean(oh * oh, axis=-1, keepdims=True) + EPS))
        gate = gate_ref[...]
        g = jnp.concatenate(normed, axis=1) * gout_ref[...] * (gate * (1.0 / (1.0 + jnp.exp(-gate))))

        if streaming:
            for j in range(wo_halves):
                land_col(j)
        att = _dot(ona_ref[...], wo_scr[0:NA_WIDTH, :]) + _dot(g.astype(BF16), wo_scr[NA_WIDTH:, :])
        x = x_ref[...] + ga1 * att
        ms = jnp.mean(x * x, axis=-1, keepdims=True)
        h2 = ((x * lax.rsqrt(ms + EPS) * gmlp_ref[...]) * (1.0 + sc2) + sh2).astype(BF16)

        width = FF_CHUNK if streaming else MLP_CHUNK
        acc = None
        for c in range(D_FF // width):
            if streaming:
                land_col(wo_halves + c)
                land_row(c)
            u = jnp.maximum(_dot(h2, wup_scr[:, c * width:(c + 1) * width]), 0.0)
            down = _dot((u * u).astype(BF16), wdown_scr[c * width:(c + 1) * width, :])
            acc = down if acc is None else acc + down
        o_ref[...] = x + ga2 * acc

    pl.when(first_step)(functools.partial(run, True))
    pl.when(jnp.logical_not(first_step))(functools.partial(run, False))


def _merge_mlp_call(x, mod_l, o_na, o_gla, gate, gout_t, g_mlp_l, w_o, w_up, w_down, layer):
    tm = TOKEN_TILE
    n_tok = x.shape[0]
    is_ctx = n_tok == N_CTX
    row = lambda i: (i, 0)
    const = lambda i: (0, 0)
    in_hbm = pl.BlockSpec(memory_space=pl.ANY)
    return pl.pallas_call(
        functools.partial(_merge_mlp_kernel, layer),
        grid=(n_tok // tm,),
        in_specs=[
            pl.BlockSpec((tm, D_MODEL), row),
            pl.BlockSpec((1, 1, N_MOD * D_MODEL), _mod_index_map(is_ctx)),
            pl.BlockSpec((tm, NA_WIDTH), row),
            pl.BlockSpec((tm, GLA_V_WIDTH), row),
            pl.BlockSpec((tm, GLA_V_WIDTH), row),
            pl.BlockSpec((1, GLA_V_WIDTH), const),
            pl.BlockSpec((1, D_MODEL), const),
            in_hbm, in_hbm, in_hbm,
        ],
        out_specs=pl.BlockSpec((tm, D_MODEL), row),
        out_shape=jax.ShapeDtypeStruct((n_tok, D_MODEL), F32),
        scratch_shapes=[
            pltpu.VMEM((D_MODEL, D_MODEL), BF16),
            pltpu.VMEM((D_MODEL, D_FF), BF16),
            pltpu.VMEM((D_FF, D_MODEL), BF16),
            pltpu.VMEM((2, D_MODEL, FF_CHUNK), F32),
            pltpu.VMEM((2, FF_CHUNK, D_MODEL), F32),
            pltpu.SemaphoreType.DMA((4,)),
        ],
        compiler_params=_params(1),
        name="merge_mlp_ctx" if is_ctx else "merge_mlp_smp",
    )(x, mod_l, o_na, o_gla, gate, gout_t, g_mlp_l, w_o, w_up, w_down)


def kernel(x_prompt, x_sample, cache_k, cache_v, state_fwd, state_bwd, c, c_ctx, w_ada, b_ada, g_attn, w_in,
           g_q, g_k, rpb, w_gf, b_gf, w_gb, b_gb, g_gla_out, w_o, g_mlp, w_up, w_down):
    x_c = x_prompt.reshape(N_CTX, D_MODEL)
    x_s = x_sample.reshape(N_SMP, D_MODEL)

    c_rows = jnp.concatenate([c_ctx[None, :], c, jnp.zeros((MOD_ROWS - 1 - DEC_BATCH, D_MODEL), F32)], axis=0)
    mods = _ada_call(c_rows, w_ada, b_ada).reshape(DEPTH, MOD_ROWS, 1, N_MOD * D_MODEL)

    w_in_t = jnp.swapaxes(w_in, 1, 2)
    wgate = jnp.zeros((DEPTH, 2 * GLA_GATE_RANK, 2 * GLA_K_WIDTH), F32)
    wgate = wgate.at[:, :GLA_GATE_RANK, :GLA_K_WIDTH].set(w_gf)
    wgate = wgate.at[:, GLA_GATE_RANK:, GLA_K_WIDTH:].set(w_gb).astype(BF16)
    bgate = jnp.concatenate([b_gf, b_gb], axis=-1).reshape(DEPTH, 1, 2 * GLA_K_WIDTH)
    head_of = np.arange(NA_WIDTH) // NA_HEAD_DIM
    seg = jnp.asarray((head_of[:, None] == head_of[None, :]).astype(np.float32) / NA_HEAD_DIM, BF16)
    gq_t = jnp.tile(g_q, (1, NA_HEADS)).reshape(DEPTH, 1, NA_WIDTH)
    gk_t = jnp.tile(g_k, (1, NA_HEADS)).reshape(DEPTH, 1, NA_WIDTH)
    gout_t = jnp.tile(g_gla_out, (1, GLA_HEADS)).reshape(DEPTH, 1, GLA_V_WIDTH)
    n_dr, n_dc = rpb.shape[2:]
    rpb_rows = jnp.pad(rpb, ((0, 0), (0, 0), (0, -n_dr % 8), (0, 2 * GRID_W - n_dc)))
    gla_w_np, gla_masks_np, gla_smask_np, gla_scales_np = _gla_constants()
    gla_consts = (jnp.asarray(gla_w_np, BF16), jnp.asarray(gla_masks_np, F32), jnp.asarray(gla_smask_np, F32),
                  jnp.asarray(gla_scales_np, F32))
    rope = _rope_tables()

    cache_kt = jnp.swapaxes(cache_k, 3, 4)
    cache_vt = jnp.swapaxes(cache_v, 3, 4)
    new_k = jnp.zeros((BATCH, DEPTH, NA_HEADS, NA_HEAD_DIM, SEQ), F32)
    new_v = jnp.zeros((BATCH, DEPTH, NA_HEADS, NA_HEAD_DIM, SEQ), F32)
    new_sf = jnp.zeros((BATCH, DEPTH, GLA_HEADS, GLA_DK, GLA_DV), F32)
    new_sb = jnp.zeros((BATCH, DEPTH, GLA_HEADS, GLA_DK, GLA_DV), F32)

    for l in range(DEPTH):
        proj_w = (g_attn[l].reshape(1, D_MODEL), w_in_t, seg, gq_t[l], gk_t[l], wgate[l], bgate[l])
        mlp_w = (gout_t[l], g_mlp[l].reshape(1, D_MODEL), w_o, w_up, w_down)

        q, k, v, gq, gk, gv, gate, lf, lb, new_k, new_v = _proj_call(
            x_c, mods[l], *proj_w, new_kv=(new_k, new_v), layer=l)
        o_na = _ctx_attn_call(q, k, v)
        o_gla, new_sf, new_sb = _gla_call(gq, gk, gv, lf, lb, gla_consts, new_states=(new_sf, new_sb), layer=l)
        x_c = _merge_mlp_call(x_c, mods[l], o_na, o_gla, gate, *mlp_w, layer=l)

        q, k, v, gq, gk, gv, gate, lf, lb = _proj_call(x_s, mods[l], *proj_w, layer=l)
        o_na = _na_call(q, k, v, cache_kt, cache_vt, rpb_rows, l)
        (o_gla,) = _gla_call(gq, gk, gv, lf, lb, gla_consts, rope=rope,
                             state_in=(state_fwd, state_bwd), layer=l)
        x_s = _merge_mlp_call(x_s, mods[l], o_na, o_gla, gate, *mlp_w, layer=l)

    return (x_c.reshape(BATCH, SEQ, D_MODEL), x_s.reshape(DEC_BATCH, DEC_SEQ, D_MODEL),
            jnp.swapaxes(new_k, 3, 4), jnp.swapaxes(new_v, 3, 4), new_sf, new_sb)
```

```python
import functools

import numpy as np
import jax
import jax.numpy as jnp
from jax import lax
from jax.experimental import pallas as pl
from jax.experimental.pallas import tpu as pltpu

F32 = jnp.float32
BF16 = jnp.bfloat16

D_MODEL = 1024
BATCH = 16
SEQ = 256
DEPTH = 4
DEC_BATCH = 2
DEC_SEQ = 1024
PAST_LEN = 512
GRID_W = 64
GRID_ROWS = DEC_SEQ // GRID_W
NA_WIDTH = D_MODEL // 2
NA_HEAD_DIM = 64
NA_HEADS = NA_WIDTH // NA_HEAD_DIM
NA_WIN_ROWS = 8
NA_WIN_COLS = 16
GLA_V_WIDTH = D_MODEL - NA_WIDTH
GLA_HEADS = 4
GLA_DV = GLA_V_WIDTH // GLA_HEADS
GLA_DK = GLA_DV // 2
GLA_K_WIDTH = GLA_HEADS * GLA_DK
GLA_GATE_RANK = 16
GLA_GATE_TAU = 16.0
GLA_CHUNK = 64
D_FF = 4 * D_MODEL
ROPE_BASE = 10000.0
N_MOD = 6
EPS = 1e-6

IN_WIDTH = 3 * NA_WIDTH + 2 * GLA_K_WIDTH + 2 * GLA_V_WIDTH + 2 * GLA_GATE_RANK
PROJ_WIDE = IN_WIDTH - 2 * GLA_GATE_RANK

N_CTX = BATCH * SEQ
N_SMP = DEC_BATCH * DEC_SEQ
MOD_ROWS = 8
MASK_VALUE = -1e30

TOKEN_TILE = 512
PROJ_BLOCK = 512
FF_CHUNK = 512
MLP_CHUNK = 1024
ADA_TILE_N = 1536
GLA_LEVELS = 6
NA_ROW_GROUP = 4
GLA_CTX_SEQS_PER_STEP = 4
VMEM_LIMIT = 52 * 1024 * 1024


def _dot(a, b):
    return jnp.dot(a, b, preferred_element_type=F32)


def _dot_nt(a, b):
    return lax.dot_general(a, b, (((1,), (1,)), ((), ())), preferred_element_type=F32)


def _dot_tn(a, b):
    return lax.dot_general(a, b, (((0,), (0,)), ((), ())), preferred_element_type=F32)


def _params(n_grid_dims=1):
    return pltpu.CompilerParams(dimension_semantics=("arbitrary",) * n_grid_dims,
                                vmem_limit_bytes=VMEM_LIMIT)


def _mod_index_map(is_ctx):
    if is_ctx:
        return lambda i: (0, 0, 0)
    return lambda i: (1 + i // (DEC_SEQ // TOKEN_TILE), 0, 0)


def _ada_kernel(c_ref, w_ref, b_ref, o_ref):
    cv = c_ref[...]
    s = cv * (1.0 / (1.0 + jnp.exp(-cv)))
    o_ref[0] = _dot(s.astype(BF16), w_ref[0].astype(BF16)) + b_ref[0]


def _ada_call(c_rows, w_ada, b_ada):
    n_mod = N_MOD * D_MODEL
    return pl.pallas_call(
        _ada_kernel,
        grid=(DEPTH, n_mod // ADA_TILE_N),
        in_specs=[
            pl.BlockSpec((MOD_ROWS, D_MODEL), lambda l, j: (0, 0)),
            pl.BlockSpec((1, D_MODEL, ADA_TILE_N), lambda l, j: (l, 0, j)),
            pl.BlockSpec((1, 1, ADA_TILE_N), lambda l, j: (l, 0, j)),
        ],
        out_specs=pl.BlockSpec((1, MOD_ROWS, ADA_TILE_N), lambda l, j: (l, 0, j)),
        out_shape=jax.ShapeDtypeStruct((DEPTH, MOD_ROWS, n_mod), F32),
        compiler_params=_params(2),
        name="ada_mod",
    )(c_rows, w_ada, b_ada.reshape(DEPTH, 1, n_mod))


def _ctx_attention(q, k, v):
    n_seq = q.shape[0] // SEQ
    lane = lax.broadcasted_iota(jnp.int32, (1, 2 * NA_HEAD_DIM), 1)
    first = lane < NA_HEAD_DIM
    pairs = [(slice(s * SEQ, (s + 1) * SEQ), slice(hp * 2 * NA_HEAD_DIM, (hp + 1) * 2 * NA_HEAD_DIM))
             for s in range(n_seq) for hp in range(NA_HEADS // 2)]
    scores = []
    for rows, cols in pairs:
        q2 = q[rows, cols]
        k2 = k[rows, cols]
        scores.append(_dot_nt(jnp.where(first, q2, jnp.zeros_like(q2)), k2))
        scores.append(_dot_nt(jnp.where(first, jnp.zeros_like(q2), q2), k2))
    probs = []
    for s in scores:
        p = jnp.exp(s - jnp.max(s, axis=-1, keepdims=True))
        probs.append((p.astype(BF16), jnp.sum(p, axis=-1, keepdims=True)))
    outs = []
    for i, (rows, cols) in enumerate(pairs):
        v2 = v[rows, cols]
        (p0, l0), (p1, l1) = probs[2 * i], probs[2 * i + 1]
        outs.append(jnp.where(first, _dot(p0, v2) / l0, _dot(p1, v2) / l1))
    per_seq = NA_HEADS // 2
    return jnp.concatenate([jnp.concatenate(outs[s * per_seq:(s + 1) * per_seq], axis=1) for s in range(n_seq)],
                           axis=0)


def _log_sigmoid(x):
    return jnp.minimum(x, 0.0) - jnp.log(1.0 + jnp.exp(-jnp.abs(x)))


def _proj_kernel(is_ctx, layer, x_ref, mod_ref, g_ref, seg_ref, gq_ref, gk_ref, wgate_ref, bgate_ref, win_hbm,
                 *rest):
    if is_ctx:
        rest = rest[2:]
        ona_out = rest[0]
        rest = rest[1:]
    else:
        q_out, k_out, v_out = rest[:3]
        rest = rest[3:]
    gq_out, gk_out, gv_out, gate_out, lf_out, lb_out = rest[:6]
    rest = rest[6:]
    if is_ctx:
        knew_out, vnew_out = rest[:2]
        rest = rest[2:]
    w_scr, stage, tail_stage, sem = rest

    n_blocks = PROJ_WIDE // PROJ_BLOCK
    tail = slice(PROJ_WIDE, IN_WIDTH)
    first_step = pl.program_id(0) == 0

    def block_copy(j):
        src = win_hbm.at[layer, j * PROJ_BLOCK:(j + 1) * PROJ_BLOCK, :]
        return pltpu.make_async_copy(src, stage.at[j % 2], sem.at[j % 2])

    def tail_copy():
        return pltpu.make_async_copy(win_hbm.at[layer, tail, :], tail_stage, sem.at[2])

    def land(j):
        block_copy(j).wait()
        w_scr[j * PROJ_BLOCK:(j + 1) * PROJ_BLOCK, :] = stage[j % 2].astype(BF16)
        if j + 2 < n_blocks:
            block_copy(j + 2).start()

    def run(streaming):
        if streaming:
            block_copy(0).start()
            block_copy(1).start()
            tail_copy().start()

        x = x_ref[...]
        mod = mod_ref[0]
        sh1 = mod[:, 0:D_MODEL]
        sc1 = mod[:, D_MODEL:2 * D_MODEL]
        ms = jnp.mean(x * x, axis=-1, keepdims=True)
        h = (x * lax.rsqrt(ms + EPS) * g_ref[...]) * (1.0 + sc1) + sh1
        hb = h.astype(BF16)

        def projected(j):
            if streaming:
                land(j)
            return _dot_nt(hb, w_scr[j * PROJ_BLOCK:(j + 1) * PROJ_BLOCK, :])

        seg = seg_ref[...]
        q = projected(0)
        q_ms = _dot((q * q).astype(BF16), seg)
        qb = (q * lax.rsqrt(q_ms + EPS) * gq_ref[...] * (NA_HEAD_DIM ** -0.5)).astype(BF16)
        k = projected(1)
        k_ms = _dot((k * k).astype(BF16), seg)
        kn = k * lax.rsqrt(k_ms + EPS) * gk_ref[...]
        v = projected(2)
        if not is_ctx:
            q_out[...] = qb
            k_out[...] = kn.astype(BF16)
            v_out[...] = v.astype(BF16)
        else:
            ona_out[...] = _ctx_attention(qb, kn.astype(BF16), v.astype(BF16)).astype(BF16)
            for s in range(TOKEN_TILE // SEQ):
                rows = slice(s * SEQ, (s + 1) * SEQ)
                for hp in range(NA_HEADS // 2):
                    cols = slice(hp * 2 * NA_HEAD_DIM, (hp + 1) * 2 * NA_HEAD_DIM)
                    for out, t in ((knew_out, kn[rows, cols].T), (vnew_out, v[rows, cols].T)):
                        out[s, 0, 2 * hp] = t[:NA_HEAD_DIM]
                        out[s, 0, 2 * hp + 1] = t[NA_HEAD_DIM:]

        z = projected(3)
        gq_out[...] = z[:, 0:GLA_K_WIDTH] * (GLA_DK ** -0.5)
        gk_out[...] = z[:, GLA_K_WIDTH:]
        gv_out[...] = projected(4).astype(BF16)
        gate_out[...] = projected(5)

        if streaming:
            tail_copy().wait()
            w_scr[tail, :] = tail_stage[...].astype(BF16)
        zz_t = _dot_nt(w_scr[tail, :], hb)
        pre = _dot_tn(zz_t.astype(BF16), wgate_ref[...]) + bgate_ref[...]
        ls = _log_sigmoid(pre) * (1.0 / GLA_GATE_TAU)
        lf_out[...] = ls[:, 0:GLA_K_WIDTH]
        lb_out[...] = ls[:, GLA_K_WIDTH:]

    pl.when(first_step)(functools.partial(run, True))
    pl.when(jnp.logical_not(first_step))(functools.partial(run, False))


def _proj_call(x, mod_l, g_attn_l, w_in_t, seg, gq_t, gk_t, wgate, bgate, new_kv=None, layer=0):
    tm = TOKEN_TILE
    n_tok = x.shape[0]
    is_ctx = new_kv is not None
    row = lambda i: (i, 0)
    const = lambda i: (0, 0)
    if is_ctx:
        out_shapes = [jax.ShapeDtypeStruct((n_tok, NA_WIDTH), BF16)]
    else:
        out_shapes = [jax.ShapeDtypeStruct((n_tok, NA_WIDTH), BF16)] * 3
    out_shapes += [
        jax.ShapeDtypeStruct((n_tok, GLA_K_WIDTH), F32),
        jax.ShapeDtypeStruct((n_tok, GLA_K_WIDTH), F32),
        jax.ShapeDtypeStruct((n_tok, GLA_V_WIDTH), BF16),
        jax.ShapeDtypeStruct((n_tok, GLA_V_WIDTH), F32),
        jax.ShapeDtypeStruct((n_tok, GLA_K_WIDTH), F32),
        jax.ShapeDtypeStruct((n_tok, GLA_K_WIDTH), F32),
    ]
    out_specs = [pl.BlockSpec((tm, s.shape[1]), row) for s in out_shapes]
    in_specs = [
        pl.BlockSpec((tm, D_MODEL), row),
        pl.BlockSpec((1, 1, N_MOD * D_MODEL), _mod_index_map(is_ctx)),
        pl.BlockSpec((1, D_MODEL), const),
        pl.BlockSpec((NA_WIDTH, NA_WIDTH), const),
        pl.BlockSpec((1, NA_WIDTH), const),
        pl.BlockSpec((1, NA_WIDTH), const),
        pl.BlockSpec(wgate.shape, const),
        pl.BlockSpec((1, 2 * GLA_K_WIDTH), const),
        pl.BlockSpec(memory_space=pl.ANY),
    ]
    args = [x, mod_l, g_attn_l, seg, gq_t, gk_t, wgate, bgate, w_in_t]
    aliases = {}
    if is_ctx:
        kv_spec = pl.BlockSpec((tm // SEQ, 1, NA_HEADS, NA_HEAD_DIM, SEQ), lambda i: (i, layer, 0, 0, 0))
        aliases = {len(args): len(out_shapes), len(args) + 1: len(out_shapes) + 1}
        in_specs += [pl.BlockSpec(memory_space=pl.ANY)] * 2
        args += list(new_kv)
        out_shapes += [jax.ShapeDtypeStruct(new_kv[0].shape, F32)] * 2
        out_specs += [kv_spec, kv_spec]
    return pl.pallas_call(
        functools.partial(_proj_kernel, is_ctx, layer),
        grid=(n_tok // tm,),
        in_specs=in_specs,
        out_specs=out_specs,
        out_shape=out_shapes,
        input_output_aliases=aliases,
        scratch_shapes=[
            pltpu.VMEM((IN_WIDTH, D_MODEL), BF16),
            pltpu.VMEM((2, PROJ_BLOCK, D_MODEL), F32),
            pltpu.VMEM((IN_WIDTH - PROJ_WIDE, D_MODEL), F32),
            pltpu.SemaphoreType.DMA((3,)),
        ],
        compiler_params=_params(1),
        name="proj_ctx" if is_ctx else "proj_smp",
    )(*args)


def _na_window_start(r):
    return min(max(r - NA_WIN_ROWS // 2, 0), GRID_ROWS - NA_WIN_ROWS)


def _na_row_groups():
    groups = []
    for g in range(GRID_ROWS // NA_ROW_GROUP):
        starts = [_na_window_start(r) for r in range(g * NA_ROW_GROUP, (g + 1) * NA_ROW_GROUP)]
        lo = min(starts) // 2 * 2
        count = max(starts) + NA_WIN_ROWS - lo
        groups.append((lo, count + count % 2))
    return groups


def _na_bias_tiles(rpb_ref, hh, tiles_scr):
    n_dr = 2 * NA_WIN_ROWS - 1
    lanes = 2 * GRID_W
    lane = lax.broadcasted_iota(jnp.int32, (GRID_W, lanes), 1)
    col = lax.broadcasted_iota(jnp.int32, (GRID_W, lanes), 0)
    first = lane < GRID_W
    c2 = lane % GRID_W
    cs = jnp.clip(col - NA_WIN_COLS // 2, 0, GRID_W - NA_WIN_COLS)
    valid = (c2 >= cs) & (c2 < cs + NA_WIN_COLS)
    rows = rpb_ref[0, hh]
    base = lanes - (NA_WIN_COLS - 1)

    def skew(dr, shift):
        row = jnp.broadcast_to(rows[dr:dr + 1], (GRID_W, lanes))
        return pltpu.roll(row, shift, axis=1, stride=1, stride_axis=0)

    for e in range(n_dr - 1):
        tile = jnp.where(first, skew(e, base), skew(e + 1, (base + GRID_W) % lanes))
        tiles_scr[e] = jnp.where(valid, tile, MASK_VALUE)


def _na_group_bias(tiles_scr, g, key_lo, key_count, first):
    masked = jnp.full((GRID_W, 2 * GRID_W), MASK_VALUE, F32)
    row_tiles = []
    for r in range(g * NA_ROW_GROUP, (g + 1) * NA_ROW_GROUP):
        rs = _na_window_start(r)
        pieces = []
        for u in range(key_count // 2):
            r0 = key_lo + 2 * u
            in0 = rs <= r0 < rs + NA_WIN_ROWS
            in1 = rs <= r0 + 1 < rs + NA_WIN_ROWS
            if not (in0 or in1):
                pieces.append(masked)
                continue
            tile = tiles_scr[r0 - r + NA_WIN_ROWS - 1]
            if not in1:
                tile = jnp.where(first, tile, MASK_VALUE)
            elif not in0:
                tile = jnp.where(first, MASK_VALUE, tile)
            pieces.append(tile)
        row_tiles.append(jnp.concatenate(pieces, axis=1))
    return jnp.concatenate(row_tiles, axis=0)


def _na_kernel(q_ref, k_ref, v_ref, kc_ref, vc_ref, rpb_ref, o_ref, sc_ref, tiles_scr):
    lane = lax.broadcasted_iota(jnp.int32, (1, 2 * NA_HEAD_DIM), 1)
    first = lane < NA_HEAD_DIM
    q2 = q_ref[...]
    kc2 = jnp.concatenate([kc_ref[0, 0, 0], kc_ref[0, 0, 1]], axis=0).astype(BF16)
    vc2 = jnp.concatenate([vc_ref[0, 0, 0], vc_ref[0, 0, 1]], axis=0).astype(BF16)
    block = NA_ROW_GROUP * GRID_W
    groups = _na_row_groups()
    qms = [jnp.where(first, q2, jnp.zeros_like(q2)), jnp.where(first, jnp.zeros_like(q2), q2)]
    for hh in range(2):
        sc_ref[hh] = _dot(qms[hh], kc2)
        _na_bias_tiles(rpb_ref, hh, tiles_scr.at[hh])
    problems = [(hh, g) for hh in range(2) for g in range(len(groups))]
    scores = []
    for hh, g in problems:
        key_lo, key_count = groups[g]
        keys = slice(key_lo * GRID_W, (key_lo + key_count) * GRID_W)
        s = _dot_nt(qms[hh][g * block:(g + 1) * block], k_ref[keys, :])
        scores.append(s + _na_group_bias(tiles_scr.at[hh], g, key_lo, key_count, first))
    probs = []
    for (hh, g), s in zip(problems, scores):
        sc = sc_ref[hh, g * block:(g + 1) * block, :]
        m = jnp.maximum(jnp.max(s, axis=-1, keepdims=True), jnp.max(sc, axis=-1, keepdims=True))
        p = jnp.exp(s - m)
        pc = jnp.exp(sc - m)
        l = jnp.sum(p, axis=-1, keepdims=True) + jnp.sum(pc, axis=-1, keepdims=True)
        probs.append((p.astype(BF16), pc.astype(BF16), l))
    outs = [[], []]
    for (hh, g), (p, pc, l) in zip(problems, probs):
        key_lo, key_count = groups[g]
        keys = slice(key_lo * GRID_W, (key_lo + key_count) * GRID_W)
        outs[hh].append((_dot(p, v_ref[keys, :]) + _dot_nt(pc, vc2)) / l)
    outs = [jnp.concatenate(o, axis=0) for o in outs]
    o_ref[...] = jnp.where(first, outs[0], outs[1]).astype(BF16)


def _na_call(q, k, v, cache_kt, cache_vt, rpb_rows, layer):
    tok = pl.BlockSpec((DEC_SEQ, 2 * NA_HEAD_DIM), lambda b, hp: (b, hp))
    cache = pl.BlockSpec((1, 1, 2, NA_HEAD_DIM, PAST_LEN), lambda b, hp: (b, layer, hp, 0, 0))
    n_pairs = 2 * NA_WIN_ROWS - 2
    return pl.pallas_call(
        _na_kernel,
        grid=(DEC_BATCH, NA_HEADS // 2),
        in_specs=[
            tok, tok, tok, cache, cache,
            pl.BlockSpec((1, 2) + rpb_rows.shape[2:], lambda b, hp: (layer, hp, 0, 0)),
        ],
        out_specs=tok,
        out_shape=jax.ShapeDtypeStruct((N_SMP, NA_WIDTH), BF16),
        scratch_shapes=[pltpu.VMEM((2, DEC_SEQ, PAST_LEN), F32),
                        pltpu.VMEM((2, n_pairs, GRID_W, 2 * GRID_W), F32)],
        compiler_params=_params(2),
        name="nbr_attn",
    )(q, k, v, cache_kt, cache_vt, rpb_rows)


def _gla_constants():
    C = GLA_CHUNK
    ii, jj = np.meshgrid(np.arange(C), np.arange(C), indexing="ij")
    tri = np.stack([jj <= ii, jj >= ii]).astype(np.float32)
    w = np.concatenate([tri, tri], axis=-1)

    x = ii ^ jj
    hb = np.where(x > 0, np.floor(np.log2(np.maximum(x, 1))), -1).astype(np.int64)
    masks = np.zeros((2, GLA_LEVELS + 1, C, C), np.float32)
    for p in range(GLA_LEVELS):
        masks[0, p] = (jj < ii) & (hb == p)
        masks[1, p] = (jj > ii) & (hb == p)
    masks[:, GLA_LEVELS] = np.eye(C)
    masks = np.tile(masks, (1, 1, 2, 2))
    state_mask = (np.arange(2 * GLA_DV)[:, None] // GLA_DV == np.arange(2 * GLA_DK)[None, :] // GLA_DK)
    upper = np.stack([(np.arange(C) >> p) & 1 for p in range(GLA_LEVELS)]).astype(bool)
    sign = np.stack([np.where(upper, 1.0, -1.0), np.where(upper, -1.0, 1.0)])
    scales = np.broadcast_to((sign * np.log2(np.e))[..., None], (2, GLA_LEVELS, C, GLA_K_WIDTH))
    return w, masks, state_mask.astype(np.float32), np.ascontiguousarray(scales, np.float32)


def _gla_kernel(seq_len, n_seq, is_ctx, *refs):
    refs = list(refs)
    gq_ref, gk_ref, gv_ref, lf_ref, lb_ref, w_ref, mask_ref, smask_ref, scale_ref = refs[:9]
    refs = refs[9:]
    if is_ctx:
        refs = refs[2:]
        o_ref, sf_ref, sb_ref = refs[:3]
        refs = refs[3:]
    else:
        cos_ref, sin_ref, s0f_ref, s0b_ref, o_ref = refs[:5]
        refs = refs[5:]
    q_scr, k_scr, of_scr, ob_scr, st_scr, cum_scr = refs

    C = GLA_CHUNK
    n_chunks = seq_len // C
    n_pairs = GLA_HEADS // 2
    pair_k = 2 * GLA_DK
    pair_v = 2 * GLA_DV

    if is_ctx:
        q_src, k_src = gq_ref, gk_ref
    else:
        q_src, k_src = q_scr, k_scr
        lane = lax.broadcasted_iota(jnp.int32, (1, GLA_K_WIDTH), 1)
        quarter = GLA_DK // 4
        low = (lane % (2 * quarter)) < quarter
        cos = jnp.concatenate([cos_ref[...]] * n_seq, axis=0)
        sin = jnp.concatenate([sin_ref[...]] * n_seq, axis=0)

        def rope(t):
            partner = jnp.where(low, pltpu.roll(t, GLA_K_WIDTH - quarter, axis=1), pltpu.roll(t, quarter, axis=1))
            return t * cos + partner * sin

        q_scr[...] = rope(gq_ref[...])
        k_scr[...] = rope(gk_ref[...])

    st_scr[...] = jnp.zeros(st_scr.shape, F32)
    if not is_ctx:
        for d, s0_ref in enumerate((s0f_ref, s0b_ref)):
            for s in range(n_seq):
                for h in range(GLA_HEADS):
                    pr, e = divmod(h, 2)
                    st_scr[s, d, pr, e * GLA_DV:(e + 1) * GLA_DV, e * GLA_DK:(e + 1) * GLA_DK] = s0_ref[s, 0, h].T

    klane = lax.broadcasted_iota(jnp.int32, (1, GLA_K_WIDTH), 1)
    even_head = (klane // GLA_DK) % 2 == 0

    def chunk_decay(s, d, r0, f_ref):
        rows = pl.ds(s * seq_len + r0, C)
        f = f_ref[rows, :]
        f_hi = f.astype(BF16)
        f_lo = (f - f_hi.astype(F32)).astype(BF16)
        cum = _dot(w_ref[d], jnp.concatenate([f_hi, f_lo], axis=0))
        cum_scr[s, d] = cum
        return rows, cum

    def chunk_scores(s, d, rows, cum):
        def cum_rows(r, n):
            return jnp.broadcast_to(cum_scr[s, d, r:r + 1, :], (n, GLA_K_WIDTH))

        def level_factor(p):
            m = 1 << p
            edge = m - 1 if d == 0 else m
            if 2 * m >= 8:
                ref = jnp.concatenate([cum_rows(blk + edge, 2 * m) for blk in range(0, C, 2 * m)], axis=0)
            elif p == 1:
                sub = lax.broadcasted_iota(jnp.int32, (8, GLA_K_WIDTH), 0)
                ref = jnp.concatenate([jnp.where(sub < 4, cum_rows(blk + edge, 8), cum_rows(blk + 4 + edge, 8))
                                       for blk in range(0, C, 8)], axis=0)
            else:
                odd = lax.broadcasted_iota(jnp.int32, (C, GLA_K_WIDTH), 0) % 2 == 1
                if d == 0:
                    ref = jnp.where(odd, pltpu.roll(cum, 1, axis=0), cum)
                else:
                    ref = jnp.where(odd, cum, pltpu.roll(cum, C - 1, axis=0))
            return jnp.exp2((cum - ref) * scale_ref[d, p])

        qc = q_src[rows, :]
        kc = k_src[rows, :]
        vc = gv_ref[rows, :]
        last = C - 1 if d == 0 else 0
        eq = jnp.exp(cum)
        ek = jnp.exp(cum_rows(last, C) - cum)
        total = jnp.exp(cum_scr[s, d, last:last + 1, :])

        a = None
        for p in range(GLA_LEVELS + 1):
            if p < GLA_LEVELS:
                ep = level_factor(p)
                qe = (qc * ep).astype(BF16)
                ke = (kc * ep).astype(BF16)
            else:
                qe = qc.astype(BF16)
                ke = kc.astype(BF16)
            zero = jnp.zeros((C, pair_k), BF16)
            lhs = jnp.concatenate([jnp.concatenate([qe[:, :pair_k], zero], axis=1),
                                   jnp.concatenate([zero, qe[:, pair_k:]], axis=1)], axis=0)
            rhs = jnp.concatenate([jnp.where(even_head, ke, jnp.zeros_like(ke)),
                                   jnp.where(even_head, jnp.zeros_like(ke), ke)], axis=0)
            sc = _dot_nt(lhs, rhs) * mask_ref[d, p]
            a = sc if a is None else a + sc
        q_in = (qc * eq).astype(BF16)
        k_out = (kc * ek).astype(BF16)
        return a.astype(BF16), q_in, k_out, vc, total

    def chunk_output(s, d, rows, o_scr, a, q_in, k_out, vc, total):
        zero_v = jnp.zeros((C, GLA_DV), BF16)
        for pr in range(n_pairs):
            kl = slice(pr * pair_k, (pr + 1) * pair_k)
            vp = vc[:, pr * pair_v:(pr + 1) * pair_v]
            v_diag = jnp.concatenate([jnp.concatenate([vp[:, :GLA_DV], zero_v], axis=1),
                                      jnp.concatenate([zero_v, vp[:, GLA_DV:]], axis=1)], axis=0)
            st = st_scr[s, d, pr]
            o_scr[rows, pr * pair_v:(pr + 1) * pair_v] = (
                _dot(a[pr * C:(pr + 1) * C], v_diag) + _dot_nt(q_in[:, kl], st.astype(BF16)))
            st_scr[s, d, pr] = st * total[:, kl] + _dot_tn(vp, k_out[:, kl]) * smask_ref[...]

    def body(n, carry):
        fwd = pl.multiple_of(n * C, C)
        bwd = pl.multiple_of((n_chunks - 1 - n) * C, C)
        scans = [(s, d) for s in range(n_seq) for d in range(2)]
        decays = [chunk_decay(s, d, bwd if d else fwd, lb_ref if d else lf_ref) for s, d in scans]
        scores = [chunk_scores(s, d, *dec) for (s, d), dec in zip(scans, decays)]
        for (s, d), (rows, _), sc in zip(scans, decays, scores):
            chunk_output(s, d, rows, ob_scr if d else of_scr, *sc)
        return carry

    lax.fori_loop(0, n_chunks, body, 0)
    o_ref[...] = of_scr[...] + ob_scr[...]
    if is_ctx:
        for d, s_ref in enumerate((sf_ref, sb_ref)):
            for s in range(n_seq):
                for h in range(GLA_HEADS):
                    pr, e = divmod(h, 2)
                    s_ref[s, 0, h] = st_scr[s, d, pr, e * GLA_DV:(e + 1) * GLA_DV, e * GLA_DK:(e + 1) * GLA_DK].T


def _gla_call(gq, gk, gv, lf, lb, consts, new_states=None, rope=None, state_in=None, layer=0):
    is_ctx = new_states is not None
    seq_len = SEQ if is_ctx else DEC_SEQ
    n_tok = gq.shape[0]
    n_seq = GLA_CTX_SEQS_PER_STEP if is_ctx else DEC_BATCH
    rows = n_seq * seq_len
    tok = lambda width: pl.BlockSpec((rows, width), lambda b: (b, 0))
    whole = lambda a: pl.BlockSpec(a.shape, lambda b: (0,) * a.ndim)
    state_spec = pl.BlockSpec((n_seq, 1, GLA_HEADS, GLA_DK, GLA_DV), lambda b: (b, layer, 0, 0, 0))
    in_specs = [tok(GLA_K_WIDTH), tok(GLA_K_WIDTH), tok(GLA_V_WIDTH), tok(GLA_K_WIDTH), tok(GLA_K_WIDTH)]
    in_specs += [whole(a) for a in consts]
    args = [gq, gk, gv, lf, lb, *consts]
    out_shape = [jax.ShapeDtypeStruct((n_tok, GLA_V_WIDTH), F32)]
    out_specs = [tok(GLA_V_WIDTH)]
    aliases = {}
    if is_ctx:
        aliases = {len(args): 1, len(args) + 1: 2}
        in_specs += [pl.BlockSpec(memory_space=pl.ANY)] * 2
        args += list(new_states)
        out_shape += [jax.ShapeDtypeStruct(new_states[0].shape, F32)] * 2
        out_specs += [state_spec, state_spec]
    else:
        in_specs += [pl.BlockSpec((seq_len, GLA_K_WIDTH), lambda b: (0, 0))] * 2 + [state_spec, state_spec]
        args += list(rope) + list(state_in)
    return pl.pallas_call(
        functools.partial(_gla_kernel, seq_len, n_seq, is_ctx),
        grid=(n_tok // rows,),
        in_specs=in_specs,
        out_specs=out_specs,
        out_shape=out_shape,
        input_output_aliases=aliases,
        scratch_shapes=[
            pltpu.VMEM((rows, GLA_K_WIDTH), F32),
            pltpu.VMEM((rows, GLA_K_WIDTH), F32),
            pltpu.VMEM((rows, GLA_V_WIDTH), F32),
            pltpu.VMEM((rows, GLA_V_WIDTH), F32),
            pltpu.VMEM((n_seq, 2, GLA_HEADS // 2, 2 * GLA_DV, 2 * GLA_DK), F32),
            pltpu.VMEM((n_seq, 2, GLA_CHUNK, GLA_K_WIDTH), F32),
        ],
        compiler_params=_params(1),
        name="gla_ctx" if is_ctx else "gla_smp",
    )(*args)


def _rope_tables():
    quarter = GLA_DK // 4
    inv = ROPE_BASE ** (-jnp.arange(quarter, dtype=F32) / quarter)
    t = jnp.arange(DEC_SEQ)
    ang_r = (t // GRID_W).astype(F32)[:, None] * inv
    ang_c = (t % GRID_W).astype(F32)[:, None] * inv
    cos = jnp.concatenate([jnp.cos(ang_r)] * 2 + [jnp.cos(ang_c)] * 2, axis=-1)
    sin = jnp.concatenate([-jnp.sin(ang_r), jnp.sin(ang_r), -jnp.sin(ang_c), jnp.sin(ang_c)], axis=-1)
    return jnp.tile(cos, (1, GLA_HEADS)), jnp.tile(sin, (1, GLA_HEADS))


def _merge_mlp_kernel(layer, x_ref, mod_ref, ona_ref, ogla_ref, gate_ref, gout_ref, gmlp_ref,
                      wo_hbm, wup_hbm, wdown_hbm, o_ref,
                      wo_scr, wup_scr, wdown_scr, stage_cols, stage_rows, sem):
    first_step = pl.program_id(0) == 0
    n_chunks = D_FF // FF_CHUNK
    wo_halves = D_MODEL // FF_CHUNK

    def col_copy(j):
        slot = j % 2
        if j < wo_halves:
            src = wo_hbm.at[layer, :, j * FF_CHUNK:(j + 1) * FF_CHUNK]
        else:
            c = j - wo_halves
            src = wup_hbm.at[layer, :, c * FF_CHUNK:(c + 1) * FF_CHUNK]
        return pltpu.make_async_copy(src, stage_cols.at[slot], sem.at[slot])

    def row_copy(c):
        slot = c % 2
        src = wdown_hbm.at[layer, c * FF_CHUNK:(c + 1) * FF_CHUNK, :]
        return pltpu.make_async_copy(src, stage_rows.at[slot], sem.at[2 + slot])

    n_col_blocks = wo_halves + n_chunks

    def land_col(j):
        col_copy(j).wait()
        block = stage_cols[j % 2].astype(BF16)
        if j < wo_halves:
            wo_scr[:, j * FF_CHUNK:(j + 1) * FF_CHUNK] = block
        else:
            c = j - wo_halves
            wup_scr[:, c * FF_CHUNK:(c + 1) * FF_CHUNK] = block
        if j + 2 < n_col_blocks:
            col_copy(j + 2).start()

    def land_row(c):
        row_copy(c).wait()
        wdown_scr[c * FF_CHUNK:(c + 1) * FF_CHUNK, :] = stage_rows[c % 2].astype(BF16)
        if c + 2 < n_chunks:
            row_copy(c + 2).start()

    def run(streaming):
        if streaming:
            col_copy(0).start()
            col_copy(1).start()
            row_copy(0).start()
            row_copy(1).start()

        mod = mod_ref[0]
        ga1 = mod[:, 2 * D_MODEL:3 * D_MODEL]
        sh2 = mod[:, 3 * D_MODEL:4 * D_MODEL]
        sc2 = mod[:, 4 * D_MODEL:5 * D_MODEL]
        ga2 = mod[:, 5 * D_MODEL:6 * D_MODEL]

        og = ogla_ref[...]
        normed = []
        for h in range(GLA_HEADS):
            oh = og[:, h * GLA_DV:(h + 1) * GLA_DV]
            normed.append(oh * lax.rsqrt(jnp.mean(oh * oh, axis=-1, keepdims=True) + EPS))
        gate = gate_ref[...]
        g = jnp.concatenate(normed, axis=1) * gout_ref[...] * (gate * (1.0 / (1.0 + jnp.exp(-gate))))

        if streaming:
            for j in range(wo_halves):
                land_col(j)
        att = _dot(ona_ref[...], wo_scr[0:NA_WIDTH, :]) + _dot(g.astype(BF16), wo_scr[NA_WIDTH:, :])
        x = x_ref[...] + ga1 * att
        ms = jnp.mean(x * x, axis=-1, keepdims=True)
        h2 = ((x * lax.rsqrt(ms + EPS) * gmlp_ref[...]) * (1.0 + sc2) + sh2).astype(BF16)

        width = FF_CHUNK if streaming else MLP_CHUNK
        acc = None
        for c in range(D_FF // width):
            if streaming:
                land_col(wo_halves + c)
                land_row(c)
            u = jnp.maximum(_dot(h2, wup_scr[:, c * width:(c + 1) * width]), 0.0)
            down = _dot((u * u).astype(BF16), wdown_scr[c * width:(c + 1) * width, :])
            acc = down if acc is None else acc + down
        o_ref[...] = x + ga2 * acc

    pl.when(first_step)(functools.partial(run, True))
    pl.when(jnp.logical_not(first_step))(functools.partial(run, False))


def _merge_mlp_call(x, mod_l, o_na, o_gla, gate, gout_t, g_mlp_l, w_o, w_up, w_down, layer):
    tm = TOKEN_TILE
    n_tok = x.shape[0]
    is_ctx = n_tok == N_CTX
    row = lambda i: (i, 0)
    const = lambda i: (0, 0)
    in_hbm = pl.BlockSpec(memory_space=pl.ANY)
    return pl.pallas_call(
        functools.partial(_merge_mlp_kernel, layer),
        grid=(n_tok // tm,),
        in_specs=[
            pl.BlockSpec((tm, D_MODEL), row),
            pl.BlockSpec((1, 1, N_MOD * D_MODEL), _mod_index_map(is_ctx)),
            pl.BlockSpec((tm, NA_WIDTH), row),
            pl.BlockSpec((tm, GLA_V_WIDTH), row),
            pl.BlockSpec((tm, GLA_V_WIDTH), row),
            pl.BlockSpec((1, GLA_V_WIDTH), const),
            pl.BlockSpec((1, D_MODEL), const),
            in_hbm, in_hbm, in_hbm,
        ],
        out_specs=pl.BlockSpec((tm, D_MODEL), row),
        out_shape=jax.ShapeDtypeStruct((n_tok, D_MODEL), F32),
        scratch_shapes=[
            pltpu.VMEM((D_MODEL, D_MODEL), BF16),
            pltpu.VMEM((D_MODEL, D_FF), BF16),
            pltpu.VMEM((D_FF, D_MODEL), BF16),
            pltpu.VMEM((2, D_MODEL, FF_CHUNK), F32),
            pltpu.VMEM((2, FF_CHUNK, D_MODEL), F32),
            pltpu.SemaphoreType.DMA((4,)),
        ],
        compiler_params=_params(1),
        name="merge_mlp_ctx" if is_ctx else "merge_mlp_smp",
    )(x, mod_l, o_na, o_gla, gate, gout_t, g_mlp_l, w_o, w_up, w_down)


def kernel(x_prompt, x_sample, cache_k, cache_v, state_fwd, state_bwd, c, c_ctx, w_ada, b_ada, g_attn, w_in,
           g_q, g_k, rpb, w_gf, b_gf, w_gb, b_gb, g_gla_out, w_o, g_mlp, w_up, w_down):
    x_c = x_prompt.reshape(N_CTX, D_MODEL)
    x_s = x_sample.reshape(N_SMP, D_MODEL)

    c_rows = jnp.concatenate([c_ctx[None, :], c, jnp.zeros((MOD_ROWS - 1 - DEC_BATCH, D_MODEL), F32)], axis=0)
    mods = _ada_call(c_rows, w_ada, b_ada).reshape(DEPTH, MOD_ROWS, 1, N_MOD * D_MODEL)

    w_in_t = jnp.swapaxes(w_in, 1, 2)
    wgate = jnp.zeros((DEPTH, 2 * GLA_GATE_RANK, 2 * GLA_K_WIDTH), F32)
    wgate = wgate.at[:, :GLA_GATE_RANK, :GLA_K_WIDTH].set(w_gf)
    wgate = wgate.at[:, GLA_GATE_RANK:, GLA_K_WIDTH:].set(w_gb).astype(BF16)
    bgate = jnp.concatenate([b_gf, b_gb], axis=-1).reshape(DEPTH, 1, 2 * GLA_K_WIDTH)
    head_of = np.arange(NA_WIDTH) // NA_HEAD_DIM
    seg = jnp.asarray((head_of[:, None] == head_of[None, :]).astype(np.float32) / NA_HEAD_DIM, BF16)
    gq_t = jnp.tile(g_q, (1, NA_HEADS)).reshape(DEPTH, 1, NA_WIDTH)
    gk_t = jnp.tile(g_k, (1, NA_HEADS)).reshape(DEPTH, 1, NA_WIDTH)
    gout_t = jnp.tile(g_gla_out, (1, GLA_HEADS)).reshape(DEPTH, 1, GLA_V_WIDTH)
    n_dr, n_dc = rpb.shape[2:]
    rpb_rows = jnp.pad(rpb, ((0, 0), (0, 0), (0, -n_dr % 8), (0, 2 * GRID_W - n_dc)))
    gla_w_np, gla_masks_np, gla_smask_np, gla_scales_np = _gla_constants()
    gla_consts = (jnp.asarray(gla_w_np, BF16), jnp.asarray(gla_masks_np, F32), jnp.asarray(gla_smask_np, F32),
                  jnp.asarray(gla_scales_np, F32))
    rope = _rope_tables()

    cache_kt = jnp.swapaxes(cache_k, 3, 4)
    cache_vt = jnp.swapaxes(cache_v, 3, 4)
    new_k = jnp.zeros((BATCH, DEPTH, NA_HEADS, NA_HEAD_DIM, SEQ), F32)
    new_v = jnp.zeros((BATCH, DEPTH, NA_HEADS, NA_HEAD_DIM, SEQ), F32)
    new_sf = jnp.zeros((BATCH, DEPTH, GLA_HEADS, GLA_DK, GLA_DV), F32)
    new_sb = jnp.zeros((BATCH, DEPTH, GLA_HEADS, GLA_DK, GLA_DV), F32)

    for l in range(DEPTH):
        proj_w = (g_attn[l].reshape(1, D_MODEL), w_in_t, seg, gq_t[l], gk_t[l], wgate[l], bgate[l])
        mlp_w = (gout_t[l], g_mlp[l].reshape(1, D_MODEL), w_o, w_up, w_down)

        o_na, gq, gk, gv, gate, lf, lb, new_k, new_v = _proj_call(
            x_c, mods[l], *proj_w, new_kv=(new_k, new_v), layer=l)
        o_gla, new_sf, new_sb = _gla_call(gq, gk, gv, lf, lb, gla_consts, new_states=(new_sf, new_sb), layer=l)
        x_c = _merge_mlp_call(x_c, mods[l], o_na, o_gla, gate, *mlp_w, layer=l)

        q, k, v, gq, gk, gv, gate, lf, lb = _proj_call(x_s, mods[l], *proj_w, layer=l)
        o_na = _na_call(q, k, v, cache_kt, cache_vt, rpb_rows, l)
        (o_gla,) = _gla_call(gq, gk, gv, lf, lb, gla_consts, rope=rope,
                             state_in=(state_fwd, state_bwd), layer=l)
        x_s = _merge_mlp_call(x_s, mods[l], o_na, o_gla, gate, *mlp_w, layer=l)

    return (x_c.reshape(BATCH, SEQ, D_MODEL), x_s.reshape(DEC_BATCH, DEC_SEQ, D_MODEL),
            jnp.swapaxes(new_k, 3, 4), jnp.swapaxes(new_v, 3, 4), new_sf, new_sb)
```

```python
import functools

import numpy as np
import jax
import jax.numpy as jnp
from jax import lax
from jax.experimental import pallas as pl
from jax.experimental.pallas import tpu as pltpu

F32 = jnp.float32
BF16 = jnp.bfloat16

D_MODEL = 1024
BATCH = 16
SEQ = 256
DEPTH = 4
DEC_BATCH = 2
DEC_SEQ = 1024
PAST_LEN = 512
GRID_W = 64
GRID_ROWS = DEC_SEQ // GRID_W
NA_WIDTH = D_MODEL // 2
NA_HEAD_DIM = 64
NA_HEADS = NA_WIDTH // NA_HEAD_DIM
NA_WIN_ROWS = 8
NA_WIN_COLS = 16
GLA_V_WIDTH = D_MODEL - NA_WIDTH
GLA_HEADS = 4
GLA_DV = GLA_V_WIDTH // GLA_HEADS
GLA_DK = GLA_DV // 2
GLA_K_WIDTH = GLA_HEADS * GLA_DK
GLA_GATE_RANK = 16
GLA_GATE_TAU = 16.0
GLA_CHUNK = 64
D_FF = 4 * D_MODEL
ROPE_BASE = 10000.0
N_MOD = 6
EPS = 1e-6

IN_WIDTH = 3 * NA_WIDTH + 2 * GLA_K_WIDTH + 2 * GLA_V_WIDTH + 2 * GLA_GATE_RANK
PROJ_WIDE = IN_WIDTH - 2 * GLA_GATE_RANK

N_CTX = BATCH * SEQ
N_SMP = DEC_BATCH * DEC_SEQ
MOD_ROWS = 8
MASK_VALUE = -1e30

TOKEN_TILE = 512
PROJ_BLOCK = 512
FF_CHUNK = 512
MLP_CHUNK = 1024
ADA_TILE_N = 1536
GLA_LEVELS = 6
NA_ROW_GROUP = 4
GLA_CTX_SEQS_PER_STEP = 4
VMEM_LIMIT = 52 * 1024 * 1024


def _dot(a, b):
    return jnp.dot(a, b, preferred_element_type=F32)


def _dot_nt(a, b):
    return lax.dot_general(a, b, (((1,), (1,)), ((), ())), preferred_element_type=F32)


def _dot_tn(a, b):
    return lax.dot_general(a, b, (((0,), (0,)), ((), ())), preferred_element_type=F32)


def _params(n_grid_dims=1):
    return pltpu.CompilerParams(dimension_semantics=("arbitrary",) * n_grid_dims,
                                vmem_limit_bytes=VMEM_LIMIT)


def _mod_index_map(is_ctx):
    if is_ctx:
        return lambda i: (0, 0, 0)
    return lambda i: (1 + i // (DEC_SEQ // TOKEN_TILE), 0, 0)


def _ada_kernel(c_ref, w_ref, b_ref, o_ref):
    cv = c_ref[...]
    s = cv * (1.0 / (1.0 + jnp.exp(-cv)))
    o_ref[0] = _dot(s.astype(BF16), w_ref[0].astype(BF16)) + b_ref[0]


def _ada_call(c_rows, w_ada, b_ada):
    n_mod = N_MOD * D_MODEL
    return pl.pallas_call(
        _ada_kernel,
        grid=(DEPTH, n_mod // ADA_TILE_N),
        in_specs=[
            pl.BlockSpec((MOD_ROWS, D_MODEL), lambda l, j: (0, 0)),
            pl.BlockSpec((1, D_MODEL, ADA_TILE_N), lambda l, j: (l, 0, j)),
            pl.BlockSpec((1, 1, ADA_TILE_N), lambda l, j: (l, 0, j)),
        ],
        out_specs=pl.BlockSpec((1, MOD_ROWS, ADA_TILE_N), lambda l, j: (l, 0, j)),
        out_shape=jax.ShapeDtypeStruct((DEPTH, MOD_ROWS, n_mod), F32),
        compiler_params=_params(2),
        name="ada_mod",
    )(c_rows, w_ada, b_ada.reshape(DEPTH, 1, n_mod))


def _ctx_attention(q, k, v):
    n_seq = q.shape[0] // SEQ
    lane = lax.broadcasted_iota(jnp.int32, (1, 2 * NA_HEAD_DIM), 1)
    first = lane < NA_HEAD_DIM
    pairs = [(slice(s * SEQ, (s + 1) * SEQ), slice(hp * 2 * NA_HEAD_DIM, (hp + 1) * 2 * NA_HEAD_DIM))
             for s in range(n_seq) for hp in range(NA_HEADS // 2)]
    scores = []
    for rows, cols in pairs:
        q2 = q[rows, cols]
        k2 = k[rows, cols]
        scores.append(_dot_nt(jnp.where(first, q2, jnp.zeros_like(q2)), k2))
        scores.append(_dot_nt(jnp.where(first, jnp.zeros_like(q2), q2), k2))
    probs = []
    for s in scores:
        p = jnp.exp(s - jnp.max(s, axis=-1, keepdims=True))
        probs.append((p.astype(BF16), jnp.sum(p, axis=-1, keepdims=True)))
    outs = []
    for i, (rows, cols) in enumerate(pairs):
        v2 = v[rows, cols]
        (p0, l0), (p1, l1) = probs[2 * i], probs[2 * i + 1]
        outs.append(jnp.where(first, _dot(p0, v2) / l0, _dot(p1, v2) / l1))
    per_seq = NA_HEADS // 2
    return jnp.concatenate([jnp.concatenate(outs[s * per_seq:(s + 1) * per_seq], axis=1) for s in range(n_seq)],
                           axis=0)


def _log_sigmoid(x):
    return jnp.minimum(x, 0.0) - jnp.log(1.0 + jnp.exp(-jnp.abs(x)))


def _proj_kernel(is_ctx, creates_kv, layer, x_ref, mod_ref, g_ref, seg_ref, gq_ref, gk_ref, wgate_ref, bgate_ref,
                 win_hbm, *rest):
    if is_ctx:
        if not creates_kv:
            rest = rest[2:]
        ona_out = rest[0]
        rest = rest[1:]
    else:
        q_out, k_out, v_out = rest[:3]
        rest = rest[3:]
    gq_out, gk_out, gv_out, gate_out, lf_out, lb_out = rest[:6]
    rest = rest[6:]
    if is_ctx:
        knew_out, vnew_out = rest[:2]
        rest = rest[2:]
    w_scr, stage, tail_stage, sem = rest

    n_blocks = PROJ_WIDE // PROJ_BLOCK
    tail = slice(PROJ_WIDE, IN_WIDTH)
    first_step = pl.program_id(0) == 0

    def block_copy(j):
        src = win_hbm.at[layer, j * PROJ_BLOCK:(j + 1) * PROJ_BLOCK, :]
        return pltpu.make_async_copy(src, stage.at[j % 2], sem.at[j % 2])

    def tail_copy():
        return pltpu.make_async_copy(win_hbm.at[layer, tail, :], tail_stage, sem.at[2])

    def land(j):
        block_copy(j).wait()
        w_scr[j * PROJ_BLOCK:(j + 1) * PROJ_BLOCK, :] = stage[j % 2].astype(BF16)
        if j + 2 < n_blocks:
            block_copy(j + 2).start()

    def run(streaming):
        if streaming:
            block_copy(0).start()
            block_copy(1).start()
            tail_copy().start()

        x = x_ref[...]
        mod = mod_ref[0]
        sh1 = mod[:, 0:D_MODEL]
        sc1 = mod[:, D_MODEL:2 * D_MODEL]
        ms = jnp.mean(x * x, axis=-1, keepdims=True)
        h = (x * lax.rsqrt(ms + EPS) * g_ref[...]) * (1.0 + sc1) + sh1
        hb = h.astype(BF16)

        def projected(j):
            if streaming:
                land(j)
            return _dot_nt(hb, w_scr[j * PROJ_BLOCK:(j + 1) * PROJ_BLOCK, :])

        seg = seg_ref[...]
        q = projected(0)
        q_ms = _dot((q * q).astype(BF16), seg)
        qb = (q * lax.rsqrt(q_ms + EPS) * gq_ref[...] * (NA_HEAD_DIM ** -0.5)).astype(BF16)
        k = projected(1)
        k_ms = _dot((k * k).astype(BF16), seg)
        kn = k * lax.rsqrt(k_ms + EPS) * gk_ref[...]
        v = projected(2)
        if not is_ctx:
            q_out[...] = qb
            k_out[...] = kn.astype(BF16)
            v_out[...] = v.astype(BF16)
        else:
            ona_out[...] = _ctx_attention(qb, kn.astype(BF16), v.astype(BF16)).astype(BF16)
            slot = layer if creates_kv else 0
            for s in range(TOKEN_TILE // SEQ):
                rows = slice(s * SEQ, (s + 1) * SEQ)
                for hp in range(NA_HEADS // 2):
                    cols = slice(hp * 2 * NA_HEAD_DIM, (hp + 1) * 2 * NA_HEAD_DIM)
                    for out, t in ((knew_out, kn[rows, cols].T), (vnew_out, v[rows, cols].T)):
                        out[s, slot, 2 * hp] = t[:NA_HEAD_DIM]
                        out[s, slot, 2 * hp + 1] = t[NA_HEAD_DIM:]
            if creates_kv:
                for out in (knew_out, vnew_out):
                    for other in range(DEPTH):
                        if other != layer:
                            out[:, other] = jnp.zeros((TOKEN_TILE // SEQ, NA_HEADS, NA_HEAD_DIM, SEQ), F32)

        z = projected(3)
        gq_out[...] = z[:, 0:GLA_K_WIDTH] * (GLA_DK ** -0.5)
        gk_out[...] = z[:, GLA_K_WIDTH:]
        gv_out[...] = projected(4).astype(BF16)
        gate_out[...] = projected(5)

        if streaming:
            tail_copy().wait()
            w_scr[tail, :] = tail_stage[...].astype(BF16)
        zz_t = _dot_nt(w_scr[tail, :], hb)
        pre = _dot_tn(zz_t.astype(BF16), wgate_ref[...]) + bgate_ref[...]
        ls = _log_sigmoid(pre) * (1.0 / GLA_GATE_TAU)
        lf_out[...] = ls[:, 0:GLA_K_WIDTH]
        lb_out[...] = ls[:, GLA_K_WIDTH:]

    pl.when(first_step)(functools.partial(run, True))
    pl.when(jnp.logical_not(first_step))(functools.partial(run, False))


def _proj_call(x, mod_l, g_attn_l, w_in_t, seg, gq_t, gk_t, wgate, bgate, is_ctx, new_kv=None, layer=0):
    tm = TOKEN_TILE
    n_tok = x.shape[0]
    creates_kv = is_ctx and new_kv is None
    row = lambda i: (i, 0)
    const = lambda i: (0, 0)
    if is_ctx:
        out_shapes = [jax.ShapeDtypeStruct((n_tok, NA_WIDTH), BF16)]
    else:
        out_shapes = [jax.ShapeDtypeStruct((n_tok, NA_WIDTH), BF16)] * 3
    out_shapes += [
        jax.ShapeDtypeStruct((n_tok, GLA_K_WIDTH), F32),
        jax.ShapeDtypeStruct((n_tok, GLA_K_WIDTH), F32),
        jax.ShapeDtypeStruct((n_tok, GLA_V_WIDTH), BF16),
        jax.ShapeDtypeStruct((n_tok, GLA_V_WIDTH), F32),
        jax.ShapeDtypeStruct((n_tok, GLA_K_WIDTH), F32),
        jax.ShapeDtypeStruct((n_tok, GLA_K_WIDTH), F32),
    ]
    out_specs = [pl.BlockSpec((tm, s.shape[1]), row) for s in out_shapes]
    in_specs = [
        pl.BlockSpec((tm, D_MODEL), row),
        pl.BlockSpec((1, 1, N_MOD * D_MODEL), _mod_index_map(is_ctx)),
        pl.BlockSpec((1, D_MODEL), const),
        pl.BlockSpec((NA_WIDTH, NA_WIDTH), const),
        pl.BlockSpec((1, NA_WIDTH), const),
        pl.BlockSpec((1, NA_WIDTH), const),
        pl.BlockSpec(wgate.shape, const),
        pl.BlockSpec((1, 2 * GLA_K_WIDTH), const),
        pl.BlockSpec(memory_space=pl.ANY),
    ]
    args = [x, mod_l, g_attn_l, seg, gq_t, gk_t, wgate, bgate, w_in_t]
    aliases = {}
    if is_ctx:
        kv_shape = (BATCH, DEPTH, NA_HEADS, NA_HEAD_DIM, SEQ)
        if creates_kv:
            kv_spec = pl.BlockSpec((tm // SEQ,) + kv_shape[1:], lambda i: (i, 0, 0, 0, 0))
        else:
            kv_spec = pl.BlockSpec((tm // SEQ, 1) + kv_shape[2:], lambda i: (i, layer, 0, 0, 0))
            aliases = {len(args): len(out_shapes), len(args) + 1: len(out_shapes) + 1}
            in_specs += [pl.BlockSpec(memory_space=pl.ANY)] * 2
            args += list(new_kv)
        out_shapes += [jax.ShapeDtypeStruct(kv_shape, F32)] * 2
        out_specs += [kv_spec, kv_spec]
    return pl.pallas_call(
        functools.partial(_proj_kernel, is_ctx, creates_kv, layer),
        grid=(n_tok // tm,),
        in_specs=in_specs,
        out_specs=out_specs,
        out_shape=out_shapes,
        input_output_aliases=aliases,
        scratch_shapes=[
            pltpu.VMEM((IN_WIDTH, D_MODEL), BF16),
            pltpu.VMEM((2, PROJ_BLOCK, D_MODEL), F32),
            pltpu.VMEM((IN_WIDTH - PROJ_WIDE, D_MODEL), F32),
            pltpu.SemaphoreType.DMA((3,)),
        ],
        compiler_params=_params(1),
        name="proj_ctx" if is_ctx else "proj_smp",
    )(*args)


def _na_window_start(r):
    return min(max(r - NA_WIN_ROWS // 2, 0), GRID_ROWS - NA_WIN_ROWS)


def _na_row_groups():
    groups = []
    for g in range(GRID_ROWS // NA_ROW_GROUP):
        starts = [_na_window_start(r) for r in range(g * NA_ROW_GROUP, (g + 1) * NA_ROW_GROUP)]
        lo = min(starts) // 2 * 2
        count = max(starts) + NA_WIN_ROWS - lo
        groups.append((lo, count + count % 2))
    return groups


def _na_bias_tiles(rpb_ref, hh, tiles_scr):
    n_dr = 2 * NA_WIN_ROWS - 1
    lanes = 2 * GRID_W
    lane = lax.broadcasted_iota(jnp.int32, (GRID_W, lanes), 1)
    col = lax.broadcasted_iota(jnp.int32, (GRID_W, lanes), 0)
    first = lane < GRID_W
    c2 = lane % GRID_W
    cs = jnp.clip(col - NA_WIN_COLS // 2, 0, GRID_W - NA_WIN_COLS)
    valid = (c2 >= cs) & (c2 < cs + NA_WIN_COLS)
    rows = rpb_ref[0, hh]
    base = lanes - (NA_WIN_COLS - 1)

    def skew(dr, shift):
        row = jnp.broadcast_to(rows[dr:dr + 1], (GRID_W, lanes))
        return pltpu.roll(row, shift, axis=1, stride=1, stride_axis=0)

    for e in range(n_dr - 1):
        tile = jnp.where(first, skew(e, base), skew(e + 1, (base + GRID_W) % lanes))
        tiles_scr[e] = jnp.where(valid, tile, MASK_VALUE)


def _na_group_bias(tiles_scr, g, key_lo, key_count, first):
    masked = jnp.full((GRID_W, 2 * GRID_W), MASK_VALUE, F32)
    row_tiles = []
    for r in range(g * NA_ROW_GROUP, (g + 1) * NA_ROW_GROUP):
        rs = _na_window_start(r)
        pieces = []
        for u in range(key_count // 2):
            r0 = key_lo + 2 * u
            in0 = rs <= r0 < rs + NA_WIN_ROWS
            in1 = rs <= r0 + 1 < rs + NA_WIN_ROWS
            if not (in0 or in1):
                pieces.append(masked)
                continue
            tile = tiles_scr[r0 - r + NA_WIN_ROWS - 1]
            if not in1:
                tile = jnp.where(first, tile, MASK_VALUE)
            elif not in0:
                tile = jnp.where(first, MASK_VALUE, tile)
            pieces.append(tile)
        row_tiles.append(jnp.concatenate(pieces, axis=1))
    return jnp.concatenate(row_tiles, axis=0)


def _na_kernel(q_ref, k_ref, v_ref, kc_ref, vc_ref, rpb_ref, o_ref, sc_ref, tiles_scr):
    lane = lax.broadcasted_iota(jnp.int32, (1, 2 * NA_HEAD_DIM), 1)
    first = lane < NA_HEAD_DIM
    q2 = q_ref[...]
    kc2 = jnp.concatenate([kc_ref[0, 0, 0], kc_ref[0, 0, 1]], axis=0).astype(BF16)
    vc2 = jnp.concatenate([vc_ref[0, 0, 0], vc_ref[0, 0, 1]], axis=0).astype(BF16)
    block = NA_ROW_GROUP * GRID_W
    groups = _na_row_groups()
    qms = [jnp.where(first, q2, jnp.zeros_like(q2)), jnp.where(first, jnp.zeros_like(q2), q2)]
    for hh in range(2):
        sc_ref[hh] = _dot(qms[hh], kc2)
        _na_bias_tiles(rpb_ref, hh, tiles_scr.at[hh])
    problems = [(hh, g) for hh in range(2) for g in range(len(groups))]
    scores = []
    for hh, g in problems:
        key_lo, key_count = groups[g]
        keys = slice(key_lo * GRID_W, (key_lo + key_count) * GRID_W)
        s = _dot_nt(qms[hh][g * block:(g + 1) * block], k_ref[keys, :])
        scores.append(s + _na_group_bias(tiles_scr.at[hh], g, key_lo, key_count, first))
    probs = []
    for (hh, g), s in zip(problems, scores):
        sc = sc_ref[hh, g * block:(g + 1) * block, :]
        m = jnp.maximum(jnp.max(s, axis=-1, keepdims=True), jnp.max(sc, axis=-1, keepdims=True))
        p = jnp.exp(s - m)
        pc = jnp.exp(sc - m)
        l = jnp.sum(p, axis=-1, keepdims=True) + jnp.sum(pc, axis=-1, keepdims=True)
        probs.append((p.astype(BF16), pc.astype(BF16), l))
    outs = [[], []]
    for (hh, g), (p, pc, l) in zip(problems, probs):
        key_lo, key_count = groups[g]
        keys = slice(key_lo * GRID_W, (key_lo + key_count) * GRID_W)
        outs[hh].append((_dot(p, v_ref[keys, :]) + _dot_nt(pc, vc2)) / l)
    outs = [jnp.concatenate(o, axis=0) for o in outs]
    o_ref[...] = jnp.where(first, outs[0], outs[1]).astype(BF16)


def _na_call(q, k, v, cache_kt, cache_vt, rpb_rows, layer):
    tok = pl.BlockSpec((DEC_SEQ, 2 * NA_HEAD_DIM), lambda b, hp: (b, hp))
    cache = pl.BlockSpec((1, 1, 2, NA_HEAD_DIM, PAST_LEN), lambda b, hp: (b, layer, hp, 0, 0))
    n_pairs = 2 * NA_WIN_ROWS - 2
    return pl.pallas_call(
        _na_kernel,
        grid=(DEC_BATCH, NA_HEADS // 2),
        in_specs=[
            tok, tok, tok, cache, cache,
            pl.BlockSpec((1, 2) + rpb_rows.shape[2:], lambda b, hp: (layer, hp, 0, 0)),
        ],
        out_specs=tok,
        out_shape=jax.ShapeDtypeStruct((N_SMP, NA_WIDTH), BF16),
        scratch_shapes=[pltpu.VMEM((2, DEC_SEQ, PAST_LEN), F32),
                        pltpu.VMEM((2, n_pairs, GRID_W, 2 * GRID_W), F32)],
        compiler_params=_params(2),
        name="nbr_attn",
    )(q, k, v, cache_kt, cache_vt, rpb_rows)


def _gla_constants():
    C = GLA_CHUNK
    ii, jj = np.meshgrid(np.arange(C), np.arange(C), indexing="ij")
    tri = np.stack([jj <= ii, jj >= ii]).astype(np.float32)
    w = np.concatenate([tri, tri], axis=-1)

    x = ii ^ jj
    hb = np.where(x > 0, np.floor(np.log2(np.maximum(x, 1))), -1).astype(np.int64)
    masks = np.zeros((2, GLA_LEVELS + 1, C, C), np.float32)
    for p in range(GLA_LEVELS):
        masks[0, p] = (jj < ii) & (hb == p)
        masks[1, p] = (jj > ii) & (hb == p)
    masks[:, GLA_LEVELS] = np.eye(C)
    masks = np.tile(masks, (1, 1, 2, 2))
    state_mask = (np.arange(2 * GLA_DV)[:, None] // GLA_DV == np.arange(2 * GLA_DK)[None, :] // GLA_DK)
    upper = np.stack([(np.arange(C) >> p) & 1 for p in range(GLA_LEVELS)]).astype(bool)
    sign = np.stack([np.where(upper, 1.0, -1.0), np.where(upper, -1.0, 1.0)])
    scales = np.broadcast_to((sign * np.log2(np.e))[..., None], (2, GLA_LEVELS, C, GLA_K_WIDTH))
    return w, masks, state_mask.astype(np.float32), np.ascontiguousarray(scales, np.float32)


def _gla_kernel(seq_len, n_seq, is_ctx, creates_states, layer, *refs):
    refs = list(refs)
    gq_ref, gk_ref, gv_ref, lf_ref, lb_ref, w_ref, mask_ref, smask_ref, scale_ref = refs[:9]
    refs = refs[9:]
    if is_ctx:
        if not creates_states:
            refs = refs[2:]
        o_ref, sf_ref, sb_ref = refs[:3]
        refs = refs[3:]
    else:
        cos_ref, sin_ref, s0f_ref, s0b_ref, o_ref = refs[:5]
        refs = refs[5:]
    q_scr, k_scr, of_scr, ob_scr, st_scr, cum_scr = refs

    C = GLA_CHUNK
    n_chunks = seq_len // C
    n_pairs = GLA_HEADS // 2
    pair_k = 2 * GLA_DK
    pair_v = 2 * GLA_DV

    if is_ctx:
        q_src, k_src = gq_ref, gk_ref
    else:
        q_src, k_src = q_scr, k_scr
        lane = lax.broadcasted_iota(jnp.int32, (1, GLA_K_WIDTH), 1)
        quarter = GLA_DK // 4
        low = (lane % (2 * quarter)) < quarter
        cos = jnp.concatenate([cos_ref[...]] * n_seq, axis=0)
        sin = jnp.concatenate([sin_ref[...]] * n_seq, axis=0)

        def rope(t):
            partner = jnp.where(low, pltpu.roll(t, GLA_K_WIDTH - quarter, axis=1), pltpu.roll(t, quarter, axis=1))
            return t * cos + partner * sin

        q_scr[...] = rope(gq_ref[...])
        k_scr[...] = rope(gk_ref[...])

    st_scr[...] = jnp.zeros(st_scr.shape, F32)
    if not is_ctx:
        for d, s0_ref in enumerate((s0f_ref, s0b_ref)):
            for s in range(n_seq):
                for h in range(GLA_HEADS):
                    pr, e = divmod(h, 2)
                    st_scr[s, d, pr, e * GLA_DV:(e + 1) * GLA_DV, e * GLA_DK:(e + 1) * GLA_DK] = s0_ref[s, 0, h].T

    klane = lax.broadcasted_iota(jnp.int32, (1, GLA_K_WIDTH), 1)
    even_head = (klane // GLA_DK) % 2 == 0

    def chunk_decay(s, d, r0, f_ref):
        rows = pl.ds(s * seq_len + r0, C)
        f = f_ref[rows, :]
        f_hi = f.astype(BF16)
        f_lo = (f - f_hi.astype(F32)).astype(BF16)
        cum = _dot(w_ref[d], jnp.concatenate([f_hi, f_lo], axis=0))
        cum_scr[s, d] = cum
        return rows, cum

    def chunk_scores(s, d, rows, cum):
        def cum_rows(r, n):
            return jnp.broadcast_to(cum_scr[s, d, r:r + 1, :], (n, GLA_K_WIDTH))

        def level_factor(p):
            m = 1 << p
            edge = m - 1 if d == 0 else m
            if 2 * m >= 8:
                ref = jnp.concatenate([cum_rows(blk + edge, 2 * m) for blk in range(0, C, 2 * m)], axis=0)
            elif p == 1:
                sub = lax.broadcasted_iota(jnp.int32, (8, GLA_K_WIDTH), 0)
                ref = jnp.concatenate([jnp.where(sub < 4, cum_rows(blk + edge, 8), cum_rows(blk + 4 + edge, 8))
                                       for blk in range(0, C, 8)], axis=0)
            else:
                odd = lax.broadcasted_iota(jnp.int32, (C, GLA_K_WIDTH), 0) % 2 == 1
                if d == 0:
                    ref = jnp.where(odd, pltpu.roll(cum, 1, axis=0), cum)
                else:
                    ref = jnp.where(odd, cum, pltpu.roll(cum, C - 1, axis=0))
            return jnp.exp2((cum - ref) * scale_ref[d, p])

        qc = q_src[rows, :]
        kc = k_src[rows, :]
        vc = gv_ref[rows, :]
        last = C - 1 if d == 0 else 0
        eq = jnp.exp(cum)
        ek = jnp.exp(cum_rows(last, C) - cum)
        total = jnp.exp(cum_scr[s, d, last:last + 1, :])

        a = None
        for p in range(GLA_LEVELS + 1):
            if p < GLA_LEVELS:
                ep = level_factor(p)
                qe = (qc * ep).astype(BF16)
                ke = (kc * ep).astype(BF16)
            else:
                qe = qc.astype(BF16)
                ke = kc.astype(BF16)
            zero = jnp.zeros((C, pair_k), BF16)
            lhs = jnp.concatenate([jnp.concatenate([qe[:, :pair_k], zero], axis=1),
                                   jnp.concatenate([zero, qe[:, pair_k:]], axis=1)], axis=0)
            rhs = jnp.concatenate([jnp.where(even_head, ke, jnp.zeros_like(ke)),
                                   jnp.where(even_head, jnp.zeros_like(ke), ke)], axis=0)
            sc = _dot_nt(lhs, rhs) * mask_ref[d, p]
            a = sc if a is None else a + sc
        q_in = (qc * eq).astype(BF16)
        k_out = (kc * ek).astype(BF16)
        return a.astype(BF16), q_in, k_out, vc, total

    def chunk_output(s, d, rows, o_scr, a, q_in, k_out, vc, total):
        zero_v = jnp.zeros((C, GLA_DV), BF16)
        for pr in range(n_pairs):
            kl = slice(pr * pair_k, (pr + 1) * pair_k)
            vp = vc[:, pr * pair_v:(pr + 1) * pair_v]
            v_diag = jnp.concatenate([jnp.concatenate([vp[:, :GLA_DV], zero_v], axis=1),
                                      jnp.concatenate([zero_v, vp[:, GLA_DV:]], axis=1)], axis=0)
            st = st_scr[s, d, pr]
            o_scr[rows, pr * pair_v:(pr + 1) * pair_v] = (
                _dot(a[pr * C:(pr + 1) * C], v_diag) + _dot_nt(q_in[:, kl], st.astype(BF16)))
            st_scr[s, d, pr] = st * total[:, kl] + _dot_tn(vp, k_out[:, kl]) * smask_ref[...]

    def body(n, carry):
        fwd = pl.multiple_of(n * C, C)
        bwd = pl.multiple_of((n_chunks - 1 - n) * C, C)
        scans = [(s, d) for s in range(n_seq) for d in range(2)]
        decays = [chunk_decay(s, d, bwd if d else fwd, lb_ref if d else lf_ref) for s, d in scans]
        scores = [chunk_scores(s, d, *dec) for (s, d), dec in zip(scans, decays)]
        for (s, d), (rows, _), sc in zip(scans, decays, scores):
            chunk_output(s, d, rows, ob_scr if d else of_scr, *sc)
        return carry

    lax.fori_loop(0, n_chunks, body, 0)
    o_ref[...] = of_scr[...] + ob_scr[...]
    if is_ctx:
        slot = layer if creates_states else 0
        for d, s_ref in enumerate((sf_ref, sb_ref)):
            for s in range(n_seq):
                for h in range(GLA_HEADS):
                    pr, e = divmod(h, 2)
                    s_ref[s, slot, h] = st_scr[s, d, pr, e * GLA_DV:(e + 1) * GLA_DV, e * GLA_DK:(e + 1) * GLA_DK].T
            if creates_states:
                for other in range(DEPTH):
                    if other != layer:
                        s_ref[:, other] = jnp.zeros((n_seq, GLA_HEADS, GLA_DK, GLA_DV), F32)


def _gla_call(gq, gk, gv, lf, lb, consts, new_states=None, rope=None, state_in=None, layer=0):
    is_ctx = state_in is None
    creates_states = is_ctx and new_states is None
    seq_len = SEQ if is_ctx else DEC_SEQ
    n_tok = gq.shape[0]
    n_seq = GLA_CTX_SEQS_PER_STEP if is_ctx else DEC_BATCH
    rows = n_seq * seq_len
    tok = lambda width: pl.BlockSpec((rows, width), lambda b: (b, 0))
    whole = lambda a: pl.BlockSpec(a.shape, lambda b: (0,) * a.ndim)
    state_spec = pl.BlockSpec((n_seq, 1, GLA_HEADS, GLA_DK, GLA_DV), lambda b: (b, layer, 0, 0, 0))
    in_specs = [tok(GLA_K_WIDTH), tok(GLA_K_WIDTH), tok(GLA_V_WIDTH), tok(GLA_K_WIDTH), tok(GLA_K_WIDTH)]
    in_specs += [whole(a) for a in consts]
    args = [gq, gk, gv, lf, lb, *consts]
    out_shape = [jax.ShapeDtypeStruct((n_tok, GLA_V_WIDTH), F32)]
    out_specs = [tok(GLA_V_WIDTH)]
    aliases = {}
    if is_ctx:
        states_shape = (BATCH, DEPTH, GLA_HEADS, GLA_DK, GLA_DV)
        if creates_states:
            new_spec = pl.BlockSpec((n_seq,) + states_shape[1:], lambda b: (b, 0, 0, 0, 0))
        else:
            new_spec = state_spec
            aliases = {len(args): 1, len(args) + 1: 2}
            in_specs += [pl.BlockSpec(memory_space=pl.ANY)] * 2
            args += list(new_states)
        out_shape += [jax.ShapeDtypeStruct(states_shape, F32)] * 2
        out_specs += [new_spec, new_spec]
    else:
        in_specs += [pl.BlockSpec((seq_len, GLA_K_WIDTH), lambda b: (0, 0))] * 2 + [state_spec, state_spec]
        args += list(rope) + list(state_in)
    return pl.pallas_call(
        functools.partial(_gla_kernel, seq_len, n_seq, is_ctx, creates_states, layer),
        grid=(n_tok // rows,),
        in_specs=in_specs,
        out_specs=out_specs,
        out_shape=out_shape,
        input_output_aliases=aliases,
        scratch_shapes=[
            pltpu.VMEM((rows, GLA_K_WIDTH), F32),
            pltpu.VMEM((rows, GLA_K_WIDTH), F32),
            pltpu.VMEM((rows, GLA_V_WIDTH), F32),
            pltpu.VMEM((rows, GLA_V_WIDTH), F32),
            pltpu.VMEM((n_seq, 2, GLA_HEADS // 2, 2 * GLA_DV, 2 * GLA_DK), F32),
            pltpu.VMEM((n_seq, 2, GLA_CHUNK, GLA_K_WIDTH), F32),
        ],
        compiler_params=_params(1),
        name="gla_ctx" if is_ctx else "gla_smp",
    )(*args)


def _rope_tables():
    quarter = GLA_DK // 4
    inv = ROPE_BASE ** (-jnp.arange(quarter, dtype=F32) / quarter)
    t = jnp.arange(DEC_SEQ)
    ang_r = (t // GRID_W).astype(F32)[:, None] * inv
    ang_c = (t % GRID_W).astype(F32)[:, None] * inv
    cos = jnp.concatenate([jnp.cos(ang_r)] * 2 + [jnp.cos(ang_c)] * 2, axis=-1)
    sin = jnp.concatenate([-jnp.sin(ang_r), jnp.sin(ang_r), -jnp.sin(ang_c), jnp.sin(ang_c)], axis=-1)
    return jnp.tile(cos, (1, GLA_HEADS)), jnp.tile(sin, (1, GLA_HEADS))


def _merge_mlp_kernel(layer, x_ref, mod_ref, ona_ref, ogla_ref, gate_ref, gout_ref, gmlp_ref,
                      wo_hbm, wup_hbm, wdown_hbm, o_ref,
                      wo_scr, wup_scr, wdown_scr, stage_cols, stage_rows, sem):
    first_step = pl.program_id(0) == 0
    n_chunks = D_FF // FF_CHUNK
    wo_halves = D_MODEL // FF_CHUNK

    def col_copy(j):
        slot = j % 2
        if j < wo_halves:
            src = wo_hbm.at[layer, :, j * FF_CHUNK:(j + 1) * FF_CHUNK]
        else:
            c = j - wo_halves
            src = wup_hbm.at[layer, :, c * FF_CHUNK:(c + 1) * FF_CHUNK]
        return pltpu.make_async_copy(src, stage_cols.at[slot], sem.at[slot])

    def row_copy(c):
        slot = c % 2
        src = wdown_hbm.at[layer, c * FF_CHUNK:(c + 1) * FF_CHUNK, :]
        return pltpu.make_async_copy(src, stage_rows.at[slot], sem.at[2 + slot])

    n_col_blocks = wo_halves + n_chunks

    def land_col(j):
        col_copy(j).wait()
        block = stage_cols[j % 2].astype(BF16)
        if j < wo_halves:
            wo_scr[:, j * FF_CHUNK:(j + 1) * FF_CHUNK] = block
        else:
            c = j - wo_halves
            wup_scr[:, c * FF_CHUNK:(c + 1) * FF_CHUNK] = block
        if j + 2 < n_col_blocks:
            col_copy(j + 2).start()

    def land_row(c):
        row_copy(c).wait()
        wdown_scr[c * FF_CHUNK:(c + 1) * FF_CHUNK, :] = stage_rows[c % 2].astype(BF16)
        if c + 2 < n_chunks:
            row_copy(c + 2).start()

    def run(streaming):
        if streaming:
            col_copy(0).start()
            col_copy(1).start()
            row_copy(0).start()
            row_copy(1).start()

        mod = mod_ref[0]
        ga1 = mod[:, 2 * D_MODEL:3 * D_MODEL]
        sh2 = mod[:, 3 * D_MODEL:4 * D_MODEL]
        sc2 = mod[:, 4 * D_MODEL:5 * D_MODEL]
        ga2 = mod[:, 5 * D_MODEL:6 * D_MODEL]

        og = ogla_ref[...]
        normed = []
        for h in range(GLA_HEADS):
            oh = og[:, h * GLA_DV:(h + 1) * GLA_DV]
            normed.append(oh * lax.rsqrt(jnp.mean(oh * oh, axis=-1, keepdims=True) + EPS))
        gate = gate_ref[...]
        g = jnp.concatenate(normed, axis=1) * gout_ref[...] * (gate * (1.0 / (1.0 + jnp.exp(-gate))))

        if streaming:
            for j in range(wo_halves):
                land_col(j)
        att = _dot(ona_ref[...], wo_scr[0:NA_WIDTH, :]) + _dot(g.astype(BF16), wo_scr[NA_WIDTH:, :])
        x = x_ref[...] + ga1 * att
        ms = jnp.mean(x * x, axis=-1, keepdims=True)
        h2 = ((x * lax.rsqrt(ms + EPS) * gmlp_ref[...]) * (1.0 + sc2) + sh2).astype(BF16)

        width = FF_CHUNK if streaming else MLP_CHUNK
        acc = None
        for c in range(D_FF // width):
            if streaming:
                land_col(wo_halves + c)
                land_row(c)
            u = jnp.maximum(_dot(h2, wup_scr[:, c * width:(c + 1) * width]), 0.0)
            down = _dot((u * u).astype(BF16), wdown_scr[c * width:(c + 1) * width, :])
            acc = down if acc is None else acc + down
        o_ref[...] = x + ga2 * acc

    pl.when(first_step)(functools.partial(run, True))
    pl.when(jnp.logical_not(first_step))(functools.partial(run, False))


def _merge_mlp_call(x, mod_l, o_na, o_gla, gate, gout_t, g_mlp_l, w_o, w_up, w_down, layer):
    tm = TOKEN_TILE
    n_tok = x.shape[0]
    is_ctx = n_tok == N_CTX
    row = lambda i: (i, 0)
    const = lambda i: (0, 0)
    in_hbm = pl.BlockSpec(memory_space=pl.ANY)
    return pl.pallas_call(
        functools.partial(_merge_mlp_kernel, layer),
        grid=(n_tok // tm,),
        in_specs=[
            pl.BlockSpec((tm, D_MODEL), row),
            pl.BlockSpec((1, 1, N_MOD * D_MODEL), _mod_index_map(is_ctx)),
            pl.BlockSpec((tm, NA_WIDTH), row),
            pl.BlockSpec((tm, GLA_V_WIDTH), row),
            pl.BlockSpec((tm, GLA_V_WIDTH), row),
            pl.BlockSpec((1, GLA_V_WIDTH), const),
            pl.BlockSpec((1, D_MODEL), const),
            in_hbm, in_hbm, in_hbm,
        ],
        out_specs=pl.BlockSpec((tm, D_MODEL), row),
        out_shape=jax.ShapeDtypeStruct((n_tok, D_MODEL), F32),
        scratch_shapes=[
            pltpu.VMEM((D_MODEL, D_MODEL), BF16),
            pltpu.VMEM((D_MODEL, D_FF), BF16),
            pltpu.VMEM((D_FF, D_MODEL), BF16),
            pltpu.VMEM((2, D_MODEL, FF_CHUNK), F32),
            pltpu.VMEM((2, FF_CHUNK, D_MODEL), F32),
            pltpu.SemaphoreType.DMA((4,)),
        ],
        compiler_params=_params(1),
        name="merge_mlp_ctx" if is_ctx else "merge_mlp_smp",
    )(x, mod_l, o_na, o_gla, gate, gout_t, g_mlp_l, w_o, w_up, w_down)


def kernel(x_prompt, x_sample, cache_k, cache_v, state_fwd, state_bwd, c, c_ctx, w_ada, b_ada, g_attn, w_in,
           g_q, g_k, rpb, w_gf, b_gf, w_gb, b_gb, g_gla_out, w_o, g_mlp, w_up, w_down):
    x_c = x_prompt.reshape(N_CTX, D_MODEL)
    x_s = x_sample.reshape(N_SMP, D_MODEL)

    c_rows = jnp.concatenate([c_ctx[None, :], c, jnp.zeros((MOD_ROWS - 1 - DEC_BATCH, D_MODEL), F32)], axis=0)
    mods = _ada_call(c_rows, w_ada, b_ada).reshape(DEPTH, MOD_ROWS, 1, N_MOD * D_MODEL)

    w_in_t = jnp.swapaxes(w_in, 1, 2)
    wgate = jnp.zeros((DEPTH, 2 * GLA_GATE_RANK, 2 * GLA_K_WIDTH), F32)
    wgate = wgate.at[:, :GLA_GATE_RANK, :GLA_K_WIDTH].set(w_gf)
    wgate = wgate.at[:, GLA_GATE_RANK:, GLA_K_WIDTH:].set(w_gb).astype(BF16)
    bgate = jnp.concatenate([b_gf, b_gb], axis=-1).reshape(DEPTH, 1, 2 * GLA_K_WIDTH)
    head_of = np.arange(NA_WIDTH) // NA_HEAD_DIM
    seg = jnp.asarray((head_of[:, None] == head_of[None, :]).astype(np.float32) / NA_HEAD_DIM, BF16)
    gq_t = jnp.tile(g_q, (1, NA_HEADS)).reshape(DEPTH, 1, NA_WIDTH)
    gk_t = jnp.tile(g_k, (1, NA_HEADS)).reshape(DEPTH, 1, NA_WIDTH)
    gout_t = jnp.tile(g_gla_out, (1, GLA_HEADS)).reshape(DEPTH, 1, GLA_V_WIDTH)
    n_dr, n_dc = rpb.shape[2:]
    rpb_rows = jnp.pad(rpb, ((0, 0), (0, 0), (0, -n_dr % 8), (0, 2 * GRID_W - n_dc)))
    gla_w_np, gla_masks_np, gla_smask_np, gla_scales_np = _gla_constants()
    gla_consts = (jnp.asarray(gla_w_np, BF16), jnp.asarray(gla_masks_np, F32), jnp.asarray(gla_smask_np, F32),
                  jnp.asarray(gla_scales_np, F32))
    rope = _rope_tables()

    cache_kt = jnp.swapaxes(cache_k, 3, 4)
    cache_vt = jnp.swapaxes(cache_v, 3, 4)
    new_kv = None
    new_states = None

    for l in range(DEPTH):
        proj_w = (g_attn[l].reshape(1, D_MODEL), w_in_t, seg, gq_t[l], gk_t[l], wgate[l], bgate[l])
        mlp_w = (gout_t[l], g_mlp[l].reshape(1, D_MODEL), w_o, w_up, w_down)

        o_na, gq, gk, gv, gate, lf, lb, *new_kv = _proj_call(x_c, mods[l], *proj_w, True, new_kv=new_kv, layer=l)
        o_gla, *new_states = _gla_call(gq, gk, gv, lf, lb, gla_consts, new_states=new_states, layer=l)
        x_c = _merge_mlp_call(x_c, mods[l], o_na, o_gla, gate, *mlp_w, layer=l)

        q, k, v, gq, gk, gv, gate, lf, lb = _proj_call(x_s, mods[l], *proj_w, False, layer=l)
        o_na = _na_call(q, k, v, cache_kt, cache_vt, rpb_rows, l)
        (o_gla,) = _gla_call(gq, gk, gv, lf, lb, gla_consts, rope=rope,
                             state_in=(state_fwd, state_bwd), layer=l)
        x_s = _merge_mlp_call(x_s, mods[l], o_na, o_gla, gate, *mlp_w, layer=l)

    return (x_c.reshape(BATCH, SEQ, D_MODEL), x_s.reshape(DEC_BATCH, DEC_SEQ, D_MODEL),
            jnp.swapaxes(new_kv[0], 3, 4), jnp.swapaxes(new_kv[1], 3, 4), *new_states)
```

```python
import functools

import numpy as np
import jax
import jax.numpy as jnp
from jax import lax
from jax.experimental import pallas as pl
from jax.experimental.pallas import tpu as pltpu

F32 = jnp.float32
BF16 = jnp.bfloat16

D_MODEL = 1024
BATCH = 16
SEQ = 256
DEPTH = 4
DEC_BATCH = 2
DEC_SEQ = 1024
PAST_LEN = 512
GRID_W = 64
GRID_ROWS = DEC_SEQ // GRID_W
NA_WIDTH = D_MODEL // 2
NA_HEAD_DIM = 64
NA_HEADS = NA_WIDTH // NA_HEAD_DIM
NA_WIN_ROWS = 8
NA_WIN_COLS = 16
GLA_V_WIDTH = D_MODEL - NA_WIDTH
GLA_HEADS = 4
GLA_DV = GLA_V_WIDTH // GLA_HEADS
GLA_DK = GLA_DV // 2
GLA_K_WIDTH = GLA_HEADS * GLA_DK
GLA_GATE_RANK = 16
GLA_GATE_TAU = 16.0
GLA_CHUNK = 64
D_FF = 4 * D_MODEL
ROPE_BASE = 10000.0
N_MOD = 6
EPS = 1e-6

IN_WIDTH = 3 * NA_WIDTH + 2 * GLA_K_WIDTH + 2 * GLA_V_WIDTH + 2 * GLA_GATE_RANK
PROJ_WIDE = IN_WIDTH - 2 * GLA_GATE_RANK

N_CTX = BATCH * SEQ
N_SMP = DEC_BATCH * DEC_SEQ
MOD_ROWS = 8
MASK_VALUE = -1e30

TOKEN_TILE = 512
PROJ_BLOCK = 512
FF_CHUNK = 512
MLP_CHUNK = 1024
ADA_TILE_N = 1536
GLA_LEVELS = 6
NA_ROW_GROUP = 4
GLA_CTX_SEQS_PER_STEP = 4
VMEM_LIMIT = 52 * 1024 * 1024


def _dot(a, b):
    return jnp.dot(a, b, preferred_element_type=F32)


def _dot_nt(a, b):
    return lax.dot_general(a, b, (((1,), (1,)), ((), ())), preferred_element_type=F32)


def _dot_tn(a, b):
    return lax.dot_general(a, b, (((0,), (0,)), ((), ())), preferred_element_type=F32)


def _params(n_grid_dims=1):
    return pltpu.CompilerParams(dimension_semantics=("arbitrary",) * n_grid_dims,
                                vmem_limit_bytes=VMEM_LIMIT)


def _mod_index_map(is_ctx):
    if is_ctx:
        return lambda i: (0, 0, 0)
    return lambda i: (1 + i // (DEC_SEQ // TOKEN_TILE), 0, 0)


def _ada_kernel(c_ref, w_ref, b_ref, o_ref):
    cv = c_ref[...]
    s = cv * (1.0 / (1.0 + jnp.exp(-cv)))
    o_ref[0] = _dot(s.astype(BF16), w_ref[0].astype(BF16)) + b_ref[0]


def _ada_call(c_rows, w_ada, b_ada):
    n_mod = N_MOD * D_MODEL
    return pl.pallas_call(
        _ada_kernel,
        grid=(DEPTH, n_mod // ADA_TILE_N),
        in_specs=[
            pl.BlockSpec((MOD_ROWS, D_MODEL), lambda l, j: (0, 0)),
            pl.BlockSpec((1, D_MODEL, ADA_TILE_N), lambda l, j: (l, 0, j)),
            pl.BlockSpec((1, 1, ADA_TILE_N), lambda l, j: (l, 0, j)),
        ],
        out_specs=pl.BlockSpec((1, MOD_ROWS, ADA_TILE_N), lambda l, j: (l, 0, j)),
        out_shape=jax.ShapeDtypeStruct((DEPTH, MOD_ROWS, n_mod), F32),
        compiler_params=_params(2),
        name="ada_mod",
    )(c_rows, w_ada, b_ada.reshape(DEPTH, 1, n_mod))


def _ctx_attention(q, k, v):
    n_seq = q.shape[0] // SEQ
    lane = lax.broadcasted_iota(jnp.int32, (1, 2 * NA_HEAD_DIM), 1)
    first = lane < NA_HEAD_DIM
    pairs = [(slice(s * SEQ, (s + 1) * SEQ), slice(hp * 2 * NA_HEAD_DIM, (hp + 1) * 2 * NA_HEAD_DIM))
             for s in range(n_seq) for hp in range(NA_HEADS // 2)]
    scores = []
    for rows, cols in pairs:
        q2 = q[rows, cols]
        k2 = k[rows, cols]
        scores.append(_dot_nt(jnp.where(first, q2, jnp.zeros_like(q2)), k2))
        scores.append(_dot_nt(jnp.where(first, jnp.zeros_like(q2), q2), k2))
    probs = []
    for s in scores:
        p = jnp.exp(s - jnp.max(s, axis=-1, keepdims=True))
        probs.append((p.astype(BF16), jnp.sum(p, axis=-1, keepdims=True)))
    outs = []
    for i, (rows, cols) in enumerate(pairs):
        v2 = v[rows, cols]
        (p0, l0), (p1, l1) = probs[2 * i], probs[2 * i + 1]
        outs.append(jnp.where(first, _dot(p0, v2) / l0, _dot(p1, v2) / l1))
    per_seq = NA_HEADS // 2
    return jnp.concatenate([jnp.concatenate(outs[s * per_seq:(s + 1) * per_seq], axis=1) for s in range(n_seq)],
                           axis=0)


def _log_sigmoid(x):
    return jnp.minimum(x, 0.0) - jnp.log(1.0 + jnp.exp(-jnp.abs(x)))


def _proj_kernel(is_ctx, creates_kv, layer, x_ref, mod_ref, g_ref, seg_ref, gq_ref, gk_ref, wgate_ref, bgate_ref,
                 cos_ref, sin_ref, win_hbm, *rest):
    if is_ctx:
        if not creates_kv:
            rest = rest[2:]
        ona_out = rest[0]
        rest = rest[1:]
    else:
        q_out, k_out, v_out = rest[:3]
        rest = rest[3:]
    qkf_out, gv_out, gate_out = rest[:3]
    rest = rest[3:]
    if is_ctx:
        knew_out, vnew_out = rest[:2]
        rest = rest[2:]
    w_scr, stage, tail_stage, sem = rest

    n_blocks = PROJ_WIDE // PROJ_BLOCK
    tail = slice(PROJ_WIDE, IN_WIDTH)
    first_step = pl.program_id(0) == 0

    def block_copy(j):
        src = win_hbm.at[layer, j * PROJ_BLOCK:(j + 1) * PROJ_BLOCK, :]
        return pltpu.make_async_copy(src, stage.at[j % 2], sem.at[j % 2])

    def tail_copy():
        return pltpu.make_async_copy(win_hbm.at[layer, tail, :], tail_stage, sem.at[2])

    def land(j):
        block_copy(j).wait()
        w_scr[j * PROJ_BLOCK:(j + 1) * PROJ_BLOCK, :] = stage[j % 2].astype(BF16)
        if j + 2 < n_blocks:
            block_copy(j + 2).start()

    def run(streaming):
        if streaming:
            block_copy(0).start()
            block_copy(1).start()
            tail_copy().start()

        x = x_ref[...]
        mod = mod_ref[0]
        sh1 = mod[:, 0:D_MODEL]
        sc1 = mod[:, D_MODEL:2 * D_MODEL]
        ms = jnp.mean(x * x, axis=-1, keepdims=True)
        h = (x * lax.rsqrt(ms + EPS) * g_ref[...]) * (1.0 + sc1) + sh1
        hb = h.astype(BF16)

        def projected(j):
            if streaming:
                land(j)
            return _dot_nt(hb, w_scr[j * PROJ_BLOCK:(j + 1) * PROJ_BLOCK, :])

        seg = seg_ref[...]
        q = projected(0)
        q_ms = _dot((q * q).astype(BF16), seg)
        qb = (q * lax.rsqrt(q_ms + EPS) * gq_ref[...] * (NA_HEAD_DIM ** -0.5)).astype(BF16)
        k = projected(1)
        k_ms = _dot((k * k).astype(BF16), seg)
        kn = k * lax.rsqrt(k_ms + EPS) * gk_ref[...]
        v = projected(2)
        if not is_ctx:
            q_out[...] = qb
            k_out[...] = kn.astype(BF16)
            v_out[...] = v.astype(BF16)
        else:
            ona_out[...] = _ctx_attention(qb, kn.astype(BF16), v.astype(BF16)).astype(BF16)
            slot = layer if creates_kv else 0
            for s in range(TOKEN_TILE // SEQ):
                rows = slice(s * SEQ, (s + 1) * SEQ)
                for hp in range(NA_HEADS // 2):
                    cols = slice(hp * 2 * NA_HEAD_DIM, (hp + 1) * 2 * NA_HEAD_DIM)
                    for out, t in ((knew_out, kn[rows, cols].T), (vnew_out, v[rows, cols].T)):
                        out[s, slot, 2 * hp] = t[:NA_HEAD_DIM]
                        out[s, slot, 2 * hp + 1] = t[NA_HEAD_DIM:]
            if creates_kv:
                for out in (knew_out, vnew_out):
                    for other in range(DEPTH):
                        if other != layer:
                            out[:, other] = jnp.zeros((TOKEN_TILE // SEQ, NA_HEADS, NA_HEAD_DIM, SEQ), F32)

        z = projected(3)
        gla_q = z[:, 0:GLA_K_WIDTH] * (GLA_DK ** -0.5)
        gla_k = z[:, GLA_K_WIDTH:]
        if not is_ctx:
            lane = lax.broadcasted_iota(jnp.int32, (1, GLA_K_WIDTH), 1)
            quarter = GLA_DK // 4
            low = (lane % (2 * quarter)) < quarter
            cos = cos_ref[...]
            sin = sin_ref[...]

            def rope(t):
                partner = jnp.where(low, pltpu.roll(t, GLA_K_WIDTH - quarter, axis=1),
                                    pltpu.roll(t, quarter, axis=1))
                return t * cos + partner * sin

            gla_q = rope(gla_q)
            gla_k = rope(gla_k)
        qkf_out[:, 0:GLA_K_WIDTH] = gla_q
        qkf_out[:, GLA_K_WIDTH:2 * GLA_K_WIDTH] = gla_k
        gv_out[...] = projected(4).astype(BF16)
        gate_out[...] = projected(5)

        if streaming:
            tail_copy().wait()
            w_scr[tail, :] = tail_stage[...].astype(BF16)
        zz_t = _dot_nt(w_scr[tail, :], hb)
        pre = _dot_tn(zz_t.astype(BF16), wgate_ref[...]) + bgate_ref[...]
        ls = _log_sigmoid(pre) * (1.0 / GLA_GATE_TAU)
        qkf_out[:, 2 * GLA_K_WIDTH:] = ls

    pl.when(first_step)(functools.partial(run, True))
    pl.when(jnp.logical_not(first_step))(functools.partial(run, False))


def _proj_call(x, mod_l, g_attn_l, w_in_t, seg, gq_t, gk_t, wgate, bgate, rope, is_ctx, new_kv=None, layer=0):
    tm = TOKEN_TILE
    n_tok = x.shape[0]
    creates_kv = is_ctx and new_kv is None
    row = lambda i: (i, 0)
    const = lambda i: (0, 0)
    rope_rows = const if is_ctx else (lambda i: (i % (DEC_SEQ // tm), 0))
    if is_ctx:
        out_shapes = [jax.ShapeDtypeStruct((n_tok, NA_WIDTH), BF16)]
    else:
        out_shapes = [jax.ShapeDtypeStruct((n_tok, NA_WIDTH), BF16)] * 3
    out_shapes += [
        jax.ShapeDtypeStruct((n_tok, 4 * GLA_K_WIDTH), F32),
        jax.ShapeDtypeStruct((n_tok, GLA_V_WIDTH), BF16),
        jax.ShapeDtypeStruct((n_tok, GLA_V_WIDTH), F32),
    ]
    out_specs = [pl.BlockSpec((tm, s.shape[1]), row) for s in out_shapes]
    in_specs = [
        pl.BlockSpec((tm, D_MODEL), row),
        pl.BlockSpec((1, 1, N_MOD * D_MODEL), _mod_index_map(is_ctx)),
        pl.BlockSpec((1, D_MODEL), const),
        pl.BlockSpec((NA_WIDTH, NA_WIDTH), const),
        pl.BlockSpec((1, NA_WIDTH), const),
        pl.BlockSpec((1, NA_WIDTH), const),
        pl.BlockSpec(wgate.shape, const),
        pl.BlockSpec((1, 2 * GLA_K_WIDTH), const),
        pl.BlockSpec((tm, GLA_K_WIDTH), rope_rows),
        pl.BlockSpec((tm, GLA_K_WIDTH), rope_rows),
        pl.BlockSpec(memory_space=pl.ANY),
    ]
    args = [x, mod_l, g_attn_l, seg, gq_t, gk_t, wgate, bgate, *rope, w_in_t]
    aliases = {}
    if is_ctx:
        kv_shape = (BATCH, DEPTH, NA_HEADS, NA_HEAD_DIM, SEQ)
        if creates_kv:
            kv_spec = pl.BlockSpec((tm // SEQ,) + kv_shape[1:], lambda i: (i, 0, 0, 0, 0))
        else:
            kv_spec = pl.BlockSpec((tm // SEQ, 1) + kv_shape[2:], lambda i: (i, layer, 0, 0, 0))
            aliases = {len(args): len(out_shapes), len(args) + 1: len(out_shapes) + 1}
            in_specs += [pl.BlockSpec(memory_space=pl.ANY)] * 2
            args += list(new_kv)
        out_shapes += [jax.ShapeDtypeStruct(kv_shape, F32)] * 2
        out_specs += [kv_spec, kv_spec]
    return pl.pallas_call(
        functools.partial(_proj_kernel, is_ctx, creates_kv, layer),
        grid=(n_tok // tm,),
        in_specs=in_specs,
        out_specs=out_specs,
        out_shape=out_shapes,
        input_output_aliases=aliases,
        scratch_shapes=[
            pltpu.VMEM((IN_WIDTH, D_MODEL), BF16),
            pltpu.VMEM((2, PROJ_BLOCK, D_MODEL), F32),
            pltpu.VMEM((IN_WIDTH - PROJ_WIDE, D_MODEL), F32),
            pltpu.SemaphoreType.DMA((3,)),
        ],
        compiler_params=_params(1),
        name="proj_ctx" if is_ctx else "proj_smp",
    )(*args)


def _na_window_start(r):
    return min(max(r - NA_WIN_ROWS // 2, 0), GRID_ROWS - NA_WIN_ROWS)


def _na_row_groups():
    groups = []
    for g in range(GRID_ROWS // NA_ROW_GROUP):
        starts = [_na_window_start(r) for r in range(g * NA_ROW_GROUP, (g + 1) * NA_ROW_GROUP)]
        lo = min(starts) // 2 * 2
        count = max(starts) + NA_WIN_ROWS - lo
        groups.append((lo, count + count % 2))
    return groups


def _na_bias_tiles(rpb_ref, hh, tiles_scr):
    n_dr = 2 * NA_WIN_ROWS - 1
    lanes = 2 * GRID_W
    lane = lax.broadcasted_iota(jnp.int32, (GRID_W, lanes), 1)
    col = lax.broadcasted_iota(jnp.int32, (GRID_W, lanes), 0)
    first = lane < GRID_W
    c2 = lane % GRID_W
    cs = jnp.clip(col - NA_WIN_COLS // 2, 0, GRID_W - NA_WIN_COLS)
    valid = (c2 >= cs) & (c2 < cs + NA_WIN_COLS)
    rows = rpb_ref[0, hh]
    base = lanes - (NA_WIN_COLS - 1)

    def skew(dr, shift):
        row = jnp.broadcast_to(rows[dr:dr + 1], (GRID_W, lanes))
        return pltpu.roll(row, shift, axis=1, stride=1, stride_axis=0)

    for e in range(n_dr - 1):
        tile = jnp.where(first, skew(e, base), skew(e + 1, (base + GRID_W) % lanes))
        tiles_scr[e] = jnp.where(valid, tile, MASK_VALUE)


def _na_group_bias(tiles_scr, g, key_lo, key_count, first):
    masked = jnp.full((GRID_W, 2 * GRID_W), MASK_VALUE, F32)
    row_tiles = []
    for r in range(g * NA_ROW_GROUP, (g + 1) * NA_ROW_GROUP):
        rs = _na_window_start(r)
        pieces = []
        for u in range(key_count // 2):
            r0 = key_lo + 2 * u
            in0 = rs <= r0 < rs + NA_WIN_ROWS
            in1 = rs <= r0 + 1 < rs + NA_WIN_ROWS
            if not (in0 or in1):
                pieces.append(masked)
                continue
            tile = tiles_scr[r0 - r + NA_WIN_ROWS - 1]
            if not in1:
                tile = jnp.where(first, tile, MASK_VALUE)
            elif not in0:
                tile = jnp.where(first, MASK_VALUE, tile)
            pieces.append(tile)
        row_tiles.append(jnp.concatenate(pieces, axis=1))
    return jnp.concatenate(row_tiles, axis=0)


def _na_kernel(q_ref, k_ref, v_ref, kc_ref, vc_ref, rpb_ref, o_ref, sc_ref, tiles_scr):
    lane = lax.broadcasted_iota(jnp.int32, (1, 2 * NA_HEAD_DIM), 1)
    first = lane < NA_HEAD_DIM
    q2 = q_ref[...]
    kc2 = jnp.concatenate([kc_ref[0, 0, 0], kc_ref[0, 0, 1]], axis=0).astype(BF16)
    vc2 = jnp.concatenate([vc_ref[0, 0, 0], vc_ref[0, 0, 1]], axis=0).astype(BF16)
    block = NA_ROW_GROUP * GRID_W
    groups = _na_row_groups()
    qms = [jnp.where(first, q2, jnp.zeros_like(q2)), jnp.where(first, jnp.zeros_like(q2), q2)]
    for hh in range(2):
        sc_ref[hh] = _dot(qms[hh], kc2)
        _na_bias_tiles(rpb_ref, hh, tiles_scr.at[hh])
    problems = [(hh, g) for hh in range(2) for g in range(len(groups))]
    scores = []
    for hh, g in problems:
        key_lo, key_count = groups[g]
        keys = slice(key_lo * GRID_W, (key_lo + key_count) * GRID_W)
        s = _dot_nt(qms[hh][g * block:(g + 1) * block], k_ref[keys, :])
        scores.append(s + _na_group_bias(tiles_scr.at[hh], g, key_lo, key_count, first))
    probs = []
    for (hh, g), s in zip(problems, scores):
        sc = sc_ref[hh, g * block:(g + 1) * block, :]
        m = jnp.maximum(jnp.max(s, axis=-1, keepdims=True), jnp.max(sc, axis=-1, keepdims=True))
        p = jnp.exp(s - m)
        pc = jnp.exp(sc - m)
        l = jnp.sum(p, axis=-1, keepdims=True) + jnp.sum(pc, axis=-1, keepdims=True)
        probs.append((p.astype(BF16), pc.astype(BF16), l))
    outs = [[], []]
    for (hh, g), (p, pc, l) in zip(problems, probs):
        key_lo, key_count = groups[g]
        keys = slice(key_lo * GRID_W, (key_lo + key_count) * GRID_W)
        outs[hh].append((_dot(p, v_ref[keys, :]) + _dot_nt(pc, vc2)) / l)
    outs = [jnp.concatenate(o, axis=0) for o in outs]
    o_ref[...] = jnp.where(first, outs[0], outs[1]).astype(BF16)


def _na_call(q, k, v, cache_kt, cache_vt, rpb_rows, layer):
    tok = pl.BlockSpec((DEC_SEQ, 2 * NA_HEAD_DIM), lambda b, hp: (b, hp))
    cache = pl.BlockSpec((1, 1, 2, NA_HEAD_DIM, PAST_LEN), lambda b, hp: (b, layer, hp, 0, 0))
    n_pairs = 2 * NA_WIN_ROWS - 2
    return pl.pallas_call(
        _na_kernel,
        grid=(DEC_BATCH, NA_HEADS // 2),
        in_specs=[
            tok, tok, tok, cache, cache,
            pl.BlockSpec((1, 2) + rpb_rows.shape[2:], lambda b, hp: (layer, hp, 0, 0)),
        ],
        out_specs=tok,
        out_shape=jax.ShapeDtypeStruct((N_SMP, NA_WIDTH), BF16),
        scratch_shapes=[pltpu.VMEM((2, DEC_SEQ, PAST_LEN), F32),
                        pltpu.VMEM((2, n_pairs, GRID_W, 2 * GRID_W), F32)],
        compiler_params=_params(2),
        name="nbr_attn",
    )(q, k, v, cache_kt, cache_vt, rpb_rows)


def _gla_constants():
    C = GLA_CHUNK
    ii, jj = np.meshgrid(np.arange(C), np.arange(C), indexing="ij")
    tri = np.stack([jj <= ii, jj >= ii]).astype(np.float32)
    w = np.concatenate([tri, tri], axis=-1)

    x = ii ^ jj
    hb = np.where(x > 0, np.floor(np.log2(np.maximum(x, 1))), -1).astype(np.int64)
    masks = np.zeros((2, GLA_LEVELS + 1, C, C), np.float32)
    for p in range(GLA_LEVELS):
        masks[0, p] = (jj < ii) & (hb == p)
        masks[1, p] = (jj > ii) & (hb == p)
    masks[:, GLA_LEVELS] = np.eye(C)
    masks = np.tile(masks, (1, 1, 2, 2))
    state_mask = (np.arange(2 * GLA_DV)[:, None] // GLA_DV == np.arange(2 * GLA_DK)[None, :] // GLA_DK)
    upper = np.stack([(np.arange(C) >> p) & 1 for p in range(GLA_LEVELS)]).astype(bool)
    sign = np.stack([np.where(upper, 1.0, -1.0), np.where(upper, -1.0, 1.0)])
    scales = np.broadcast_to((sign * np.log2(np.e))[..., None], (2, GLA_LEVELS, C, GLA_K_WIDTH))
    return w, masks, state_mask.astype(np.float32), np.ascontiguousarray(scales, np.float32)


def _gla_kernel(seq_len, n_seq, is_ctx, creates_states, layer, *refs):
    refs = list(refs)
    gq_ref, gk_ref, gv_ref, lf_ref, lb_ref, w_ref, mask_ref, smask_ref, scale_ref = refs[:9]
    refs = refs[9:]
    if is_ctx:
        if not creates_states:
            refs = refs[2:]
        o_ref, sf_ref, sb_ref = refs[:3]
        refs = refs[3:]
    else:
        s0f_ref, s0b_ref, o_ref = refs[:3]
        refs = refs[3:]
    of_scr, ob_scr, st_scr, cum_scr = refs

    C = GLA_CHUNK
    n_chunks = seq_len // C
    n_pairs = GLA_HEADS // 2
    pair_k = 2 * GLA_DK
    pair_v = 2 * GLA_DV

    st_scr[...] = jnp.zeros(st_scr.shape, F32)
    if not is_ctx:
        for d, s0_ref in enumerate((s0f_ref, s0b_ref)):
            for s in range(n_seq):
                for h in range(GLA_HEADS):
                    pr, e = divmod(h, 2)
                    st_scr[s, d, pr, e * GLA_DV:(e + 1) * GLA_DV, e * GLA_DK:(e + 1) * GLA_DK] = s0_ref[s, 0, h].T

    klane = lax.broadcasted_iota(jnp.int32, (1, GLA_K_WIDTH), 1)
    even_head = (klane // GLA_DK) % 2 == 0

    def chunk_decay(s, d, r0, f_ref):
        rows = pl.ds(s * seq_len + r0, C)
        f = f_ref[rows, :]
        f_hi = f.astype(BF16)
        f_lo = (f - f_hi.astype(F32)).astype(BF16)
        cum = _dot(w_ref[d], jnp.concatenate([f_hi, f_lo], axis=0))
        cum_scr[s, d] = cum
        return rows, cum

    def chunk_scores(s, d, rows, cum):
        def cum_rows(r, n):
            return jnp.broadcast_to(cum_scr[s, d, r:r + 1, :], (n, GLA_K_WIDTH))

        def level_factor(p):
            m = 1 << p
            edge = m - 1 if d == 0 else m
            if 2 * m >= 8:
                ref = jnp.concatenate([cum_rows(blk + edge, 2 * m) for blk in range(0, C, 2 * m)], axis=0)
            elif p == 1:
                sub = lax.broadcasted_iota(jnp.int32, (8, GLA_K_WIDTH), 0)
                ref = jnp.concatenate([jnp.where(sub < 4, cum_rows(blk + edge, 8), cum_rows(blk + 4 + edge, 8))
                                       for blk in range(0, C, 8)], axis=0)
            else:
                odd = lax.broadcasted_iota(jnp.int32, (C, GLA_K_WIDTH), 0) % 2 == 1
                if d == 0:
                    ref = jnp.where(odd, pltpu.roll(cum, 1, axis=0), cum)
                else:
                    ref = jnp.where(odd, cum, pltpu.roll(cum, C - 1, axis=0))
            return jnp.exp2((cum - ref) * scale_ref[d, p])

        qc = gq_ref[rows, :]
        kc = gk_ref[rows, :]
        vc = gv_ref[rows, :]
        last = C - 1 if d == 0 else 0
        eq = jnp.exp(cum)
        ek = jnp.exp(cum_rows(last, C) - cum)
        total = jnp.exp(cum_scr[s, d, last:last + 1, :])

        qb = qc.astype(BF16)
        kb = kc.astype(BF16)
        kb_even = jnp.where(even_head, kb, jnp.zeros_like(kb))
        kb_odd = jnp.where(even_head, jnp.zeros_like(kb), kb)
        zero = jnp.zeros((C, pair_k), BF16)
        a = None
        for p in range(GLA_LEVELS + 1):
            if p < GLA_LEVELS:
                ep = level_factor(p).astype(BF16)
                qe, ke_even, ke_odd = qb * ep, kb_even * ep, kb_odd * ep
            else:
                qe, ke_even, ke_odd = qb, kb_even, kb_odd
            lhs = jnp.concatenate([jnp.concatenate([qe[:, :pair_k], zero], axis=1),
                                   jnp.concatenate([zero, qe[:, pair_k:]], axis=1)], axis=0)
            sc = _dot_nt(lhs, jnp.concatenate([ke_even, ke_odd], axis=0)) * mask_ref[d, p]
            a = sc if a is None else a + sc
        q_in = (qc * eq).astype(BF16)
        k_out = (kc * ek).astype(BF16)
        return a.astype(BF16), q_in, k_out, vc, total

    def chunk_output(s, d, rows, o_scr, a, q_in, k_out, vc, total):
        zero_v = jnp.zeros((C, GLA_DV), BF16)
        for pr in range(n_pairs):
            kl = slice(pr * pair_k, (pr + 1) * pair_k)
            vp = vc[:, pr * pair_v:(pr + 1) * pair_v]
            v_diag = jnp.concatenate([jnp.concatenate([vp[:, :GLA_DV], zero_v], axis=1),
                                      jnp.concatenate([zero_v, vp[:, GLA_DV:]], axis=1)], axis=0)
            st = st_scr[s, d, pr]
            o_scr[rows, pr * pair_v:(pr + 1) * pair_v] = (
                _dot(a[pr * C:(pr + 1) * C], v_diag) + _dot_nt(q_in[:, kl], st.astype(BF16)))
            st_scr[s, d, pr] = st * total[:, kl] + _dot_tn(vp, k_out[:, kl]) * smask_ref[...]

    def body(n, carry):
        fwd = pl.multiple_of(n * C, C)
        bwd = pl.multiple_of((n_chunks - 1 - n) * C, C)
        scans = [(s, d) for s in range(n_seq) for d in range(2)]
        decays = [chunk_decay(s, d, bwd if d else fwd, lb_ref if d else lf_ref) for s, d in scans]
        scores = [chunk_scores(s, d, *dec) for (s, d), dec in zip(scans, decays)]
        for (s, d), (rows, _), sc in zip(scans, decays, scores):
            chunk_output(s, d, rows, ob_scr if d else of_scr, *sc)
        return carry

    lax.fori_loop(0, n_chunks, body, 0)
    o_ref[...] = of_scr[...] + ob_scr[...]
    if is_ctx:
        slot = layer if creates_states else 0
        for d, s_ref in enumerate((sf_ref, sb_ref)):
            for s in range(n_seq):
                for h in range(GLA_HEADS):
                    pr, e = divmod(h, 2)
                    s_ref[s, slot, h] = st_scr[s, d, pr, e * GLA_DV:(e + 1) * GLA_DV, e * GLA_DK:(e + 1) * GLA_DK].T
            if creates_states:
                for other in range(DEPTH):
                    if other != layer:
                        s_ref[:, other] = jnp.zeros((n_seq, GLA_HEADS, GLA_DK, GLA_DV), F32)


def _gla_call(qkf, gv, consts, new_states=None, state_in=None, layer=0):
    is_ctx = state_in is None
    creates_states = is_ctx and new_states is None
    seq_len = SEQ if is_ctx else DEC_SEQ
    n_tok = qkf.shape[0]
    n_seq = GLA_CTX_SEQS_PER_STEP if is_ctx else DEC_BATCH
    rows = n_seq * seq_len
    tok = lambda width: pl.BlockSpec((rows, width), lambda b: (b, 0))
    whole = lambda a: pl.BlockSpec(a.shape, lambda b: (0,) * a.ndim)
    state_spec = pl.BlockSpec((n_seq, 1, GLA_HEADS, GLA_DK, GLA_DV), lambda b: (b, layer, 0, 0, 0))
    qkf_block = lambda j: pl.BlockSpec((rows, GLA_K_WIDTH), lambda b: (b, j))
    in_specs = [qkf_block(0), qkf_block(1), tok(GLA_V_WIDTH), qkf_block(2), qkf_block(3)]
    in_specs += [whole(a) for a in consts]
    args = [qkf, qkf, gv, qkf, qkf, *consts]
    out_shape = [jax.ShapeDtypeStruct((n_tok, GLA_V_WIDTH), F32)]
    out_specs = [tok(GLA_V_WIDTH)]
    aliases = {}
    if is_ctx:
        states_shape = (BATCH, DEPTH, GLA_HEADS, GLA_DK, GLA_DV)
        if creates_states:
            new_spec = pl.BlockSpec((n_seq,) + states_shape[1:], lambda b: (b, 0, 0, 0, 0))
        else:
            new_spec = state_spec
            aliases = {len(args): 1, len(args) + 1: 2}
            in_specs += [pl.BlockSpec(memory_space=pl.ANY)] * 2
            args += list(new_states)
        out_shape += [jax.ShapeDtypeStruct(states_shape, F32)] * 2
        out_specs += [new_spec, new_spec]
    else:
        in_specs += [state_spec, state_spec]
        args += list(state_in)
    return pl.pallas_call(
        functools.partial(_gla_kernel, seq_len, n_seq, is_ctx, creates_states, layer),
        grid=(n_tok // rows,),
        in_specs=in_specs,
        out_specs=out_specs,
        out_shape=out_shape,
        input_output_aliases=aliases,
        scratch_shapes=[
            pltpu.VMEM((rows, GLA_V_WIDTH), F32),
            pltpu.VMEM((rows, GLA_V_WIDTH), F32),
            pltpu.VMEM((n_seq, 2, GLA_HEADS // 2, 2 * GLA_DV, 2 * GLA_DK), F32),
            pltpu.VMEM((n_seq, 2, GLA_CHUNK, GLA_K_WIDTH), F32),
        ],
        compiler_params=_params(1),
        name="gla_ctx" if is_ctx else "gla_smp",
    )(*args)


def _rope_tables():
    quarter = GLA_DK // 4
    inv = ROPE_BASE ** (-jnp.arange(quarter, dtype=F32) / quarter)
    t = jnp.arange(DEC_SEQ)
    ang_r = (t // GRID_W).astype(F32)[:, None] * inv
    ang_c = (t % GRID_W).astype(F32)[:, None] * inv
    cos = jnp.concatenate([jnp.cos(ang_r)] * 2 + [jnp.cos(ang_c)] * 2, axis=-1)
    sin = jnp.concatenate([-jnp.sin(ang_r), jnp.sin(ang_r), -jnp.sin(ang_c), jnp.sin(ang_c)], axis=-1)
    return jnp.tile(cos, (1, GLA_HEADS)), jnp.tile(sin, (1, GLA_HEADS))


def _merge_mlp_kernel(layer, x_ref, mod_ref, ona_ref, ogla_ref, gate_ref, gout_ref, gmlp_ref,
                      wo_hbm, wup_hbm, wdown_hbm, o_ref,
                      wo_scr, wup_scr, wdown_scr, stage_cols, stage_rows, sem):
    first_step = pl.program_id(0) == 0
    n_chunks = D_FF // FF_CHUNK
    wo_halves = D_MODEL // FF_CHUNK

    def col_copy(j):
        slot = j % 2
        if j < wo_halves:
            src = wo_hbm.at[layer, :, j * FF_CHUNK:(j + 1) * FF_CHUNK]
        else:
            c = j - wo_halves
            src = wup_hbm.at[layer, :, c * FF_CHUNK:(c + 1) * FF_CHUNK]
        return pltpu.make_async_copy(src, stage_cols.at[slot], sem.at[slot])

    def row_copy(c):
        slot = c % 2
        src = wdown_hbm.at[layer, c * FF_CHUNK:(c + 1) * FF_CHUNK, :]
        return pltpu.make_async_copy(src, stage_rows.at[slot], sem.at[2 + slot])

    n_col_blocks = wo_halves + n_chunks

    def land_col(j):
        col_copy(j).wait()
        block = stage_cols[j % 2].astype(BF16)
        if j < wo_halves:
            wo_scr[:, j * FF_CHUNK:(j + 1) * FF_CHUNK] = block
        else:
            c = j - wo_halves
            wup_scr[:, c * FF_CHUNK:(c + 1) * FF_CHUNK] = block
        if j + 2 < n_col_blocks:
            col_copy(j + 2).start()

    def land_row(c):
        row_copy(c).wait()
        wdown_scr[c * FF_CHUNK:(c + 1) * FF_CHUNK, :] = stage_rows[c % 2].astype(BF16)
        if c + 2 < n_chunks:
            row_copy(c + 2).start()

    def run(streaming):
        if streaming:
            col_copy(0).start()
            col_copy(1).start()
            row_copy(0).start()
            row_copy(1).start()

        mod = mod_ref[0]
        ga1 = mod[:, 2 * D_MODEL:3 * D_MODEL]
        sh2 = mod[:, 3 * D_MODEL:4 * D_MODEL]
        sc2 = mod[:, 4 * D_MODEL:5 * D_MODEL]
        ga2 = mod[:, 5 * D_MODEL:6 * D_MODEL]

        og = ogla_ref[...]
        normed = []
        for h in range(GLA_HEADS):
            oh = og[:, h * GLA_DV:(h + 1) * GLA_DV]
            normed.append(oh * lax.rsqrt(jnp.mean(oh * oh, axis=-1, keepdims=True) + EPS))
        gate = gate_ref[...]
        g = jnp.concatenate(normed, axis=1) * gout_ref[...] * (gate * (1.0 / (1.0 + jnp.exp(-gate))))

        if streaming:
            for j in range(wo_halves):
                land_col(j)
        att = _dot(ona_ref[...], wo_scr[0:NA_WIDTH, :]) + _dot(g.astype(BF16), wo_scr[NA_WIDTH:, :])
        x = x_ref[...] + ga1 * att
        ms = jnp.mean(x * x, axis=-1, keepdims=True)
        h2 = ((x * lax.rsqrt(ms + EPS) * gmlp_ref[...]) * (1.0 + sc2) + sh2).astype(BF16)

        width = FF_CHUNK if streaming else MLP_CHUNK
        acc = None
        for c in range(D_FF // width):
            if streaming:
                land_col(wo_halves + c)
                land_row(c)
            u = jnp.maximum(_dot(h2, wup_scr[:, c * width:(c + 1) * width]), 0.0)
            down = _dot((u * u).astype(BF16), wdown_scr[c * width:(c + 1) * width, :])
            acc = down if acc is None else acc + down
        o_ref[...] = x + ga2 * acc

    pl.when(first_step)(functools.partial(run, True))
    pl.when(jnp.logical_not(first_step))(functools.partial(run, False))


def _merge_mlp_call(x, mod_l, o_na, o_gla, gate, gout_t, g_mlp_l, w_o, w_up, w_down, layer):
    tm = TOKEN_TILE
    n_tok = x.shape[0]
    is_ctx = n_tok == N_CTX
    row = lambda i: (i, 0)
    const = lambda i: (0, 0)
    in_hbm = pl.BlockSpec(memory_space=pl.ANY)
    return pl.pallas_call(
        functools.partial(_merge_mlp_kernel, layer),
        grid=(n_tok // tm,),
        in_specs=[
            pl.BlockSpec((tm, D_MODEL), row),
            pl.BlockSpec((1, 1, N_MOD * D_MODEL), _mod_index_map(is_ctx)),
            pl.BlockSpec((tm, NA_WIDTH), row),
            pl.BlockSpec((tm, GLA_V_WIDTH), row),
            pl.BlockSpec((tm, GLA_V_WIDTH), row),
            pl.BlockSpec((1, GLA_V_WIDTH), const),
            pl.BlockSpec((1, D_MODEL), const),
            in_hbm, in_hbm, in_hbm,
        ],
        out_specs=pl.BlockSpec((tm, D_MODEL), row),
        out_shape=jax.ShapeDtypeStruct((n_tok, D_MODEL), F32),
        scratch_shapes=[
            pltpu.VMEM((D_MODEL, D_MODEL), BF16),
            pltpu.VMEM((D_MODEL, D_FF), BF16),
            pltpu.VMEM((D_FF, D_MODEL), BF16),
            pltpu.VMEM((2, D_MODEL, FF_CHUNK), F32),
            pltpu.VMEM((2, FF_CHUNK, D_MODEL), F32),
            pltpu.SemaphoreType.DMA((4,)),
        ],
        compiler_params=_params(1),
        name="merge_mlp_ctx" if is_ctx else "merge_mlp_smp",
    )(x, mod_l, o_na, o_gla, gate, gout_t, g_mlp_l, w_o, w_up, w_down)


def kernel(x_prompt, x_sample, cache_k, cache_v, state_fwd, state_bwd, c, c_ctx, w_ada, b_ada, g_attn, w_in,
           g_q, g_k, rpb, w_gf, b_gf, w_gb, b_gb, g_gla_out, w_o, g_mlp, w_up, w_down):
    x_c = x_prompt.reshape(N_CTX, D_MODEL)
    x_s = x_sample.reshape(N_SMP, D_MODEL)

    c_rows = jnp.concatenate([c_ctx[None, :], c, jnp.zeros((MOD_ROWS - 1 - DEC_BATCH, D_MODEL), F32)], axis=0)
    mods = _ada_call(c_rows, w_ada, b_ada).reshape(DEPTH, MOD_ROWS, 1, N_MOD * D_MODEL)

    w_in_t = jnp.swapaxes(w_in, 1, 2)
    wgate = jnp.zeros((DEPTH, 2 * GLA_GATE_RANK, 2 * GLA_K_WIDTH), F32)
    wgate = wgate.at[:, :GLA_GATE_RANK, :GLA_K_WIDTH].set(w_gf)
    wgate = wgate.at[:, GLA_GATE_RANK:, GLA_K_WIDTH:].set(w_gb).astype(BF16)
    bgate = jnp.concatenate([b_gf, b_gb], axis=-1).reshape(DEPTH, 1, 2 * GLA_K_WIDTH)
    head_of = np.arange(NA_WIDTH) // NA_HEAD_DIM
    seg = jnp.asarray((head_of[:, None] == head_of[None, :]).astype(np.float32) / NA_HEAD_DIM, BF16)
    gq_t = jnp.tile(g_q, (1, NA_HEADS)).reshape(DEPTH, 1, NA_WIDTH)
    gk_t = jnp.tile(g_k, (1, NA_HEADS)).reshape(DEPTH, 1, NA_WIDTH)
    gout_t = jnp.tile(g_gla_out, (1, GLA_HEADS)).reshape(DEPTH, 1, GLA_V_WIDTH)
    n_dr, n_dc = rpb.shape[2:]
    rpb_rows = jnp.pad(rpb, ((0, 0), (0, 0), (0, -n_dr % 8), (0, 2 * GRID_W - n_dc)))
    gla_w_np, gla_masks_np, gla_smask_np, gla_scales_np = _gla_constants()
    gla_consts = (jnp.asarray(gla_w_np, BF16), jnp.asarray(gla_masks_np, F32), jnp.asarray(gla_smask_np, F32),
                  jnp.asarray(gla_scales_np, F32))
    rope = _rope_tables()

    cache_kt = jnp.swapaxes(cache_k, 3, 4)
    cache_vt = jnp.swapaxes(cache_v, 3, 4)
    new_kv = None
    new_states = None

    for l in range(DEPTH):
        proj_w = (g_attn[l].reshape(1, D_MODEL), w_in_t, seg, gq_t[l], gk_t[l], wgate[l], bgate[l], rope)
        mlp_w = (gout_t[l], g_mlp[l].reshape(1, D_MODEL), w_o, w_up, w_down)

        o_na, qkf, gv, gate, *new_kv = _proj_call(x_c, mods[l], *proj_w, True, new_kv=new_kv, layer=l)
        o_gla, *new_states = _gla_call(qkf, gv, gla_consts, new_states=new_states, layer=l)
        x_c = _merge_mlp_call(x_c, mods[l], o_na, o_gla, gate, *mlp_w, layer=l)

        q, k, v, qkf, gv, gate = _proj_call(x_s, mods[l], *proj_w, False, layer=l)
        o_na = _na_call(q, k, v, cache_kt, cache_vt, rpb_rows, l)
        (o_gla,) = _gla_call(qkf, gv, gla_consts, state_in=(state_fwd, state_bwd), layer=l)
        x_s = _merge_mlp_call(x_s, mods[l], o_na, o_gla, gate, *mlp_w, layer=l)

    return (x_c.reshape(BATCH, SEQ, D_MODEL), x_s.reshape(DEC_BATCH, DEC_SEQ, D_MODEL),
            jnp.swapaxes(new_kv[0], 3, 4), jnp.swapaxes(new_kv[1], 3, 4), *new_states)
```

```python
import functools

import numpy as np
import jax
import jax.numpy as jnp
from jax import lax
from jax.experimental import pallas as pl
from jax.experimental.pallas import tpu as pltpu

F32 = jnp.float32
BF16 = jnp.bfloat16

D_MODEL = 1024
BATCH = 16
SEQ = 256
DEPTH = 4
DEC_BATCH = 2
DEC_SEQ = 1024
PAST_LEN = 512
GRID_W = 64
GRID_ROWS = DEC_SEQ // GRID_W
NA_WIDTH = D_MODEL // 2
NA_HEAD_DIM = 64
NA_HEADS = NA_WIDTH // NA_HEAD_DIM
NA_WIN_ROWS = 8
NA_WIN_COLS = 16
GLA_V_WIDTH = D_MODEL - NA_WIDTH
GLA_HEADS = 4
GLA_DV = GLA_V_WIDTH // GLA_HEADS
GLA_DK = GLA_DV // 2
GLA_K_WIDTH = GLA_HEADS * GLA_DK
GLA_GATE_RANK = 16
GLA_GATE_TAU = 16.0
GLA_CHUNK = 64
D_FF = 4 * D_MODEL
ROPE_BASE = 10000.0
N_MOD = 6
EPS = 1e-6

IN_WIDTH = 3 * NA_WIDTH + 2 * GLA_K_WIDTH + 2 * GLA_V_WIDTH + 2 * GLA_GATE_RANK
PROJ_WIDE = IN_WIDTH - 2 * GLA_GATE_RANK

N_CTX = BATCH * SEQ
N_SMP = DEC_BATCH * DEC_SEQ
MOD_ROWS = 8
MASK_VALUE = -1e30

TOKEN_TILE = 512
PROJ_BLOCK = 512
FF_CHUNK = 512
MLP_CHUNK = 1024
ADA_TILE_N = 1536
GLA_LEVELS = 6
NA_ROW_GROUP = 2
GLA_CTX_SEQS_PER_STEP = 4
VMEM_LIMIT = 52 * 1024 * 1024


def _dot(a, b):
    return jnp.dot(a, b, preferred_element_type=F32)


def _dot_nt(a, b):
    return lax.dot_general(a, b, (((1,), (1,)), ((), ())), preferred_element_type=F32)


def _dot_tn(a, b):
    return lax.dot_general(a, b, (((0,), (0,)), ((), ())), preferred_element_type=F32)


def _params(n_grid_dims=1):
    return pltpu.CompilerParams(dimension_semantics=("arbitrary",) * n_grid_dims,
                                vmem_limit_bytes=VMEM_LIMIT)


def _mod_index_map(is_ctx):
    if is_ctx:
        return lambda i: (0, 0, 0)
    return lambda i: (1 + i // (DEC_SEQ // TOKEN_TILE), 0, 0)


def _ada_kernel(c_ref, w_ref, b_ref, o_ref):
    cv = c_ref[...]
    s = cv * (1.0 / (1.0 + jnp.exp(-cv)))
    o_ref[0] = _dot(s.astype(BF16), w_ref[0].astype(BF16)) + b_ref[0]


def _ada_call(c_rows, w_ada, b_ada):
    n_mod = N_MOD * D_MODEL
    return pl.pallas_call(
        _ada_kernel,
        grid=(DEPTH, n_mod // ADA_TILE_N),
        in_specs=[
            pl.BlockSpec((MOD_ROWS, D_MODEL), lambda l, j: (0, 0)),
            pl.BlockSpec((1, D_MODEL, ADA_TILE_N), lambda l, j: (l, 0, j)),
            pl.BlockSpec((1, 1, ADA_TILE_N), lambda l, j: (l, 0, j)),
        ],
        out_specs=pl.BlockSpec((1, MOD_ROWS, ADA_TILE_N), lambda l, j: (l, 0, j)),
        out_shape=jax.ShapeDtypeStruct((DEPTH, MOD_ROWS, n_mod), F32),
        compiler_params=_params(2),
        name="ada_mod",
    )(c_rows, w_ada, b_ada.reshape(DEPTH, 1, n_mod))


def _ctx_attention(q, k, v):
    n_seq = q.shape[0] // SEQ
    lane = lax.broadcasted_iota(jnp.int32, (1, 2 * NA_HEAD_DIM), 1)
    first = lane < NA_HEAD_DIM
    pairs = [(slice(s * SEQ, (s + 1) * SEQ), slice(hp * 2 * NA_HEAD_DIM, (hp + 1) * 2 * NA_HEAD_DIM))
             for s in range(n_seq) for hp in range(NA_HEADS // 2)]
    scores = []
    for rows, cols in pairs:
        q2 = q[rows, cols]
        k2 = k[rows, cols]
        scores.append(_dot_nt(jnp.where(first, q2, jnp.zeros_like(q2)), k2))
        scores.append(_dot_nt(jnp.where(first, jnp.zeros_like(q2), q2), k2))
    probs = []
    for s in scores:
        p = jnp.exp(s - jnp.max(s, axis=-1, keepdims=True))
        probs.append((p.astype(BF16), jnp.sum(p, axis=-1, keepdims=True)))
    outs = []
    for i, (rows, cols) in enumerate(pairs):
        v2 = v[rows, cols]
        (p0, l0), (p1, l1) = probs[2 * i], probs[2 * i + 1]
        outs.append(jnp.where(first, _dot(p0, v2) / l0, _dot(p1, v2) / l1))
    per_seq = NA_HEADS // 2
    return jnp.concatenate([jnp.concatenate(outs[s * per_seq:(s + 1) * per_seq], axis=1) for s in range(n_seq)],
                           axis=0)


def _log_sigmoid(x):
    return jnp.minimum(x, 0.0) - jnp.log(1.0 + jnp.exp(-jnp.abs(x)))


def _proj_kernel(is_ctx, creates_kv, layer, x_ref, mod_ref, g_ref, seg_ref, gq_ref, gk_ref, wgate_ref, bgate_ref,
                 cos_ref, sin_ref, win_hbm, *rest):
    if is_ctx:
        if not creates_kv:
            rest = rest[2:]
        ona_out = rest[0]
        rest = rest[1:]
    else:
        q_out, k_out, v_out = rest[:3]
        rest = rest[3:]
    qkf_out, gv_out, gate_out = rest[:3]
    rest = rest[3:]
    if is_ctx:
        knew_out, vnew_out = rest[:2]
        rest = rest[2:]
    w_scr, stage, tail_stage, sem = rest

    n_blocks = PROJ_WIDE // PROJ_BLOCK
    tail = slice(PROJ_WIDE, IN_WIDTH)
    first_step = pl.program_id(0) == 0

    def block_copy(j):
        src = win_hbm.at[layer, j * PROJ_BLOCK:(j + 1) * PROJ_BLOCK, :]
        return pltpu.make_async_copy(src, stage.at[j % 2], sem.at[j % 2])

    def tail_copy():
        return pltpu.make_async_copy(win_hbm.at[layer, tail, :], tail_stage, sem.at[2])

    def land(j):
        block_copy(j).wait()
        w_scr[j * PROJ_BLOCK:(j + 1) * PROJ_BLOCK, :] = stage[j % 2].astype(BF16)
        if j + 2 < n_blocks:
            block_copy(j + 2).start()

    def run(streaming):
        if streaming:
            block_copy(0).start()
            block_copy(1).start()
            tail_copy().start()

        x = x_ref[...]
        mod = mod_ref[0]
        sh1 = mod[:, 0:D_MODEL]
        sc1 = mod[:, D_MODEL:2 * D_MODEL]
        ms = jnp.mean(x * x, axis=-1, keepdims=True)
        h = (x * lax.rsqrt(ms + EPS) * g_ref[...]) * (1.0 + sc1) + sh1
        hb = h.astype(BF16)

        def projected(j):
            if streaming:
                land(j)
            return _dot_nt(hb, w_scr[j * PROJ_BLOCK:(j + 1) * PROJ_BLOCK, :])

        seg = seg_ref[...]
        q = projected(0)
        q_ms = _dot((q * q).astype(BF16), seg)
        qb = (q * lax.rsqrt(q_ms + EPS) * gq_ref[...] * (NA_HEAD_DIM ** -0.5)).astype(BF16)
        k = projected(1)
        k_ms = _dot((k * k).astype(BF16), seg)
        kn = k * lax.rsqrt(k_ms + EPS) * gk_ref[...]
        v = projected(2)
        if not is_ctx:
            q_out[...] = qb
            k_out[...] = kn.astype(BF16)
            v_out[...] = v.astype(BF16)
        else:
            ona_out[...] = _ctx_attention(qb, kn.astype(BF16), v.astype(BF16)).astype(BF16)
            slot = layer if creates_kv else 0
            for s in range(TOKEN_TILE // SEQ):
                rows = slice(s * SEQ, (s + 1) * SEQ)
                for hp in range(NA_HEADS // 2):
                    cols = slice(hp * 2 * NA_HEAD_DIM, (hp + 1) * 2 * NA_HEAD_DIM)
                    for out, t in ((knew_out, kn[rows, cols].T), (vnew_out, v[rows, cols].T)):
                        out[s, slot, 2 * hp] = t[:NA_HEAD_DIM]
                        out[s, slot, 2 * hp + 1] = t[NA_HEAD_DIM:]
            if creates_kv:
                for out in (knew_out, vnew_out):
                    for other in range(DEPTH):
                        if other != layer:
                            out[:, other] = jnp.zeros((TOKEN_TILE // SEQ, NA_HEADS, NA_HEAD_DIM, SEQ), F32)

        z = projected(3)
        gla_q = z[:, 0:GLA_K_WIDTH] * (GLA_DK ** -0.5)
        gla_k = z[:, GLA_K_WIDTH:]
        if not is_ctx:
            lane = lax.broadcasted_iota(jnp.int32, (1, GLA_K_WIDTH), 1)
            quarter = GLA_DK // 4
            low = (lane % (2 * quarter)) < quarter
            cos = cos_ref[...]
            sin = sin_ref[...]

            def rope(t):
                partner = jnp.where(low, pltpu.roll(t, GLA_K_WIDTH - quarter, axis=1),
                                    pltpu.roll(t, quarter, axis=1))
                return t * cos + partner * sin

            gla_q = rope(gla_q)
            gla_k = rope(gla_k)
        qkf_out[:, 0:GLA_K_WIDTH] = gla_q
        qkf_out[:, GLA_K_WIDTH:2 * GLA_K_WIDTH] = gla_k
        gv_out[...] = projected(4).astype(BF16)
        gate_out[...] = projected(5)

        if streaming:
            tail_copy().wait()
            w_scr[tail, :] = tail_stage[...].astype(BF16)
        zz_t = _dot_nt(w_scr[tail, :], hb)
        pre = _dot_tn(zz_t.astype(BF16), wgate_ref[...]) + bgate_ref[...]
        ls = _log_sigmoid(pre) * (1.0 / GLA_GATE_TAU)
        qkf_out[:, 2 * GLA_K_WIDTH:] = ls

    pl.when(first_step)(functools.partial(run, True))
    pl.when(jnp.logical_not(first_step))(functools.partial(run, False))


def _proj_call(x, mod_l, g_attn_l, w_in_t, seg, gq_t, gk_t, wgate, bgate, rope, is_ctx, new_kv=None, layer=0):
    tm = TOKEN_TILE
    n_tok = x.shape[0]
    creates_kv = is_ctx and new_kv is None
    row = lambda i: (i, 0)
    const = lambda i: (0, 0)
    rope_rows = const if is_ctx else (lambda i: (i % (DEC_SEQ // tm), 0))
    if is_ctx:
        out_shapes = [jax.ShapeDtypeStruct((n_tok, NA_WIDTH), BF16)]
    else:
        out_shapes = [jax.ShapeDtypeStruct((n_tok, NA_WIDTH), BF16)] * 3
    out_shapes += [
        jax.ShapeDtypeStruct((n_tok, 4 * GLA_K_WIDTH), F32),
        jax.ShapeDtypeStruct((n_tok, GLA_V_WIDTH), BF16),
        jax.ShapeDtypeStruct((n_tok, GLA_V_WIDTH), F32),
    ]
    out_specs = [pl.BlockSpec((tm, s.shape[1]), row) for s in out_shapes]
    in_specs = [
        pl.BlockSpec((tm, D_MODEL), row),
        pl.BlockSpec((1, 1, N_MOD * D_MODEL), _mod_index_map(is_ctx)),
        pl.BlockSpec((1, D_MODEL), const),
        pl.BlockSpec((NA_WIDTH, NA_WIDTH), const),
        pl.BlockSpec((1, NA_WIDTH), const),
        pl.BlockSpec((1, NA_WIDTH), const),
        pl.BlockSpec(wgate.shape, const),
        pl.BlockSpec((1, 2 * GLA_K_WIDTH), const),
        pl.BlockSpec((tm, GLA_K_WIDTH), rope_rows),
        pl.BlockSpec((tm, GLA_K_WIDTH), rope_rows),
        pl.BlockSpec(memory_space=pl.ANY),
    ]
    args = [x, mod_l, g_attn_l, seg, gq_t, gk_t, wgate, bgate, *rope, w_in_t]
    aliases = {}
    if is_ctx:
        kv_shape = (BATCH, DEPTH, NA_HEADS, NA_HEAD_DIM, SEQ)
        if creates_kv:
            kv_spec = pl.BlockSpec((tm // SEQ,) + kv_shape[1:], lambda i: (i, 0, 0, 0, 0))
        else:
            kv_spec = pl.BlockSpec((tm // SEQ, 1) + kv_shape[2:], lambda i: (i, layer, 0, 0, 0))
            aliases = {len(args): len(out_shapes), len(args) + 1: len(out_shapes) + 1}
            in_specs += [pl.BlockSpec(memory_space=pl.ANY)] * 2
            args += list(new_kv)
        out_shapes += [jax.ShapeDtypeStruct(kv_shape, F32)] * 2
        out_specs += [kv_spec, kv_spec]
    return pl.pallas_call(
        functools.partial(_proj_kernel, is_ctx, creates_kv, layer),
        grid=(n_tok // tm,),
        in_specs=in_specs,
        out_specs=out_specs,
        out_shape=out_shapes,
        input_output_aliases=aliases,
        scratch_shapes=[
            pltpu.VMEM((IN_WIDTH, D_MODEL), BF16),
            pltpu.VMEM((2, PROJ_BLOCK, D_MODEL), F32),
            pltpu.VMEM((IN_WIDTH - PROJ_WIDE, D_MODEL), F32),
            pltpu.SemaphoreType.DMA((3,)),
        ],
        compiler_params=_params(1),
        name="proj_ctx" if is_ctx else "proj_smp",
    )(*args)


def _na_window_start(r):
    return min(max(r - NA_WIN_ROWS // 2, 0), GRID_ROWS - NA_WIN_ROWS)


def _na_row_groups():
    groups = []
    for g in range(GRID_ROWS // NA_ROW_GROUP):
        starts = [_na_window_start(r) for r in range(g * NA_ROW_GROUP, (g + 1) * NA_ROW_GROUP)]
        lo = min(starts) // 2 * 2
        count = max(starts) + NA_WIN_ROWS - lo
        groups.append((lo, count + count % 2))
    return groups


def _na_bias_tiles(rpb_ref, hh, tiles_scr):
    n_dr = 2 * NA_WIN_ROWS - 1
    lanes = 2 * GRID_W
    lane = lax.broadcasted_iota(jnp.int32, (GRID_W, lanes), 1)
    col = lax.broadcasted_iota(jnp.int32, (GRID_W, lanes), 0)
    first = lane < GRID_W
    c2 = lane % GRID_W
    cs = jnp.clip(col - NA_WIN_COLS // 2, 0, GRID_W - NA_WIN_COLS)
    valid = (c2 >= cs) & (c2 < cs + NA_WIN_COLS)
    rows = rpb_ref[0, hh]
    base = lanes - (NA_WIN_COLS - 1)

    def skew(dr, shift):
        row = jnp.broadcast_to(rows[dr:dr + 1], (GRID_W, lanes))
        return pltpu.roll(row, shift, axis=1, stride=1, stride_axis=0)

    for e in range(n_dr - 1):
        tile = jnp.where(first, skew(e, base), skew(e + 1, (base + GRID_W) % lanes))
        tiles_scr[e] = jnp.where(valid, tile, MASK_VALUE)


def _na_group_bias(tiles_scr, g, key_lo, key_count, first):
    masked = jnp.full((GRID_W, 2 * GRID_W), MASK_VALUE, F32)
    row_tiles = []
    for r in range(g * NA_ROW_GROUP, (g + 1) * NA_ROW_GROUP):
        rs = _na_window_start(r)
        pieces = []
        for u in range(key_count // 2):
            r0 = key_lo + 2 * u
            in0 = rs <= r0 < rs + NA_WIN_ROWS
            in1 = rs <= r0 + 1 < rs + NA_WIN_ROWS
            if not (in0 or in1):
                pieces.append(masked)
                continue
            tile = tiles_scr[r0 - r + NA_WIN_ROWS - 1]
            if not in1:
                tile = jnp.where(first, tile, MASK_VALUE)
            elif not in0:
                tile = jnp.where(first, MASK_VALUE, tile)
            pieces.append(tile)
        row_tiles.append(jnp.concatenate(pieces, axis=1))
    return jnp.concatenate(row_tiles, axis=0)


def _na_kernel(q_ref, k_ref, v_ref, kc_ref, vc_ref, rpb_ref, o_ref, sc_ref, tiles_scr):
    lane = lax.broadcasted_iota(jnp.int32, (1, 2 * NA_HEAD_DIM), 1)
    first = lane < NA_HEAD_DIM
    q2 = q_ref[...]
    kc2 = jnp.concatenate([kc_ref[0, 0, 0], kc_ref[0, 0, 1]], axis=0).astype(BF16)
    vc2 = jnp.concatenate([vc_ref[0, 0, 0], vc_ref[0, 0, 1]], axis=0).astype(BF16)
    block = NA_ROW_GROUP * GRID_W
    groups = _na_row_groups()
    qms = [jnp.where(first, q2, jnp.zeros_like(q2)), jnp.where(first, jnp.zeros_like(q2), q2)]
    for hh in range(2):
        sc_ref[hh] = _dot(qms[hh], kc2)
        _na_bias_tiles(rpb_ref, hh, tiles_scr.at[hh])
    problems = [(hh, g) for hh in range(2) for g in range(len(groups))]
    scores = []
    for hh, g in problems:
        key_lo, key_count = groups[g]
        keys = slice(key_lo * GRID_W, (key_lo + key_count) * GRID_W)
        s = _dot_nt(qms[hh][g * block:(g + 1) * block], k_ref[keys, :])
        scores.append(s + _na_group_bias(tiles_scr.at[hh], g, key_lo, key_count, first))
    probs = []
    for (hh, g), s in zip(problems, scores):
        sc = sc_ref[hh, g * block:(g + 1) * block, :]
        m = jnp.maximum(jnp.max(s, axis=-1, keepdims=True), jnp.max(sc, axis=-1, keepdims=True))
        p = jnp.exp(s - m)
        pc = jnp.exp(sc - m)
        l = jnp.sum(p, axis=-1, keepdims=True) + jnp.sum(pc, axis=-1, keepdims=True)
        probs.append((p.astype(BF16), pc.astype(BF16), l))
    outs = [[], []]
    for (hh, g), (p, pc, l) in zip(problems, probs):
        key_lo, key_count = groups[g]
        keys = slice(key_lo * GRID_W, (key_lo + key_count) * GRID_W)
        outs[hh].append((_dot(p, v_ref[keys, :]) + _dot_nt(pc, vc2)) / l)
    outs = [jnp.concatenate(o, axis=0) for o in outs]
    o_ref[...] = jnp.where(first, outs[0], outs[1]).astype(BF16)


def _na_call(q, k, v, cache_kt, cache_vt, rpb_rows, layer):
    tok = pl.BlockSpec((DEC_SEQ, 2 * NA_HEAD_DIM), lambda b, hp: (b, hp))
    cache = pl.BlockSpec((1, 1, 2, NA_HEAD_DIM, PAST_LEN), lambda b, hp: (b, layer, hp, 0, 0))
    n_pairs = 2 * NA_WIN_ROWS - 2
    return pl.pallas_call(
        _na_kernel,
        grid=(DEC_BATCH, NA_HEADS // 2),
        in_specs=[
            tok, tok, tok, cache, cache,
            pl.BlockSpec((1, 2) + rpb_rows.shape[2:], lambda b, hp: (layer, hp, 0, 0)),
        ],
        out_specs=tok,
        out_shape=jax.ShapeDtypeStruct((N_SMP, NA_WIDTH), BF16),
        scratch_shapes=[pltpu.VMEM((2, DEC_SEQ, PAST_LEN), F32),
                        pltpu.VMEM((2, n_pairs, GRID_W, 2 * GRID_W), F32)],
        compiler_params=_params(2),
        name="nbr_attn",
    )(q, k, v, cache_kt, cache_vt, rpb_rows)


def _gla_constants():
    C = GLA_CHUNK
    ii, jj = np.meshgrid(np.arange(C), np.arange(C), indexing="ij")
    tri = np.stack([jj <= ii, jj >= ii]).astype(np.float32)
    w = np.concatenate([tri, tri], axis=-1)

    x = ii ^ jj
    hb = np.where(x > 0, np.floor(np.log2(np.maximum(x, 1))), -1).astype(np.int64)
    masks = np.zeros((2, GLA_LEVELS + 1, C, C), np.float32)
    for p in range(GLA_LEVELS):
        masks[0, p] = (jj < ii) & (hb == p)
        masks[1, p] = (jj > ii) & (hb == p)
    masks[:, GLA_LEVELS] = np.eye(C)
    masks = np.tile(masks, (1, 1, 2, 2))
    state_mask = (np.arange(2 * GLA_DV)[:, None] // GLA_DV == np.arange(2 * GLA_DK)[None, :] // GLA_DK)
    upper = np.stack([(np.arange(C) >> p) & 1 for p in range(GLA_LEVELS)]).astype(bool)
    sign = np.stack([np.where(upper, 1.0, -1.0), np.where(upper, -1.0, 1.0)])
    scales = np.broadcast_to((sign * np.log2(np.e))[..., None], (2, GLA_LEVELS, C, GLA_K_WIDTH))
    return w, masks, state_mask.astype(np.float32), np.ascontiguousarray(scales, np.float32)


def _gla_kernel(seq_len, n_seq, is_ctx, creates_states, layer, *refs):
    refs = list(refs)
    gq_ref, gk_ref, gv_ref, lf_ref, lb_ref, w_ref, mask_ref, smask_ref, scale_ref = refs[:9]
    refs = refs[9:]
    if is_ctx:
        if not creates_states:
            refs = refs[2:]
        o_ref, sf_ref, sb_ref = refs[:3]
        refs = refs[3:]
    else:
        s0f_ref, s0b_ref, o_ref = refs[:3]
        refs = refs[3:]
    st_scr, cum_scr = refs

    C = GLA_CHUNK
    n_chunks = seq_len // C
    n_pairs = GLA_HEADS // 2
    pair_k = 2 * GLA_DK
    pair_v = 2 * GLA_DV

    st_scr[...] = jnp.zeros(st_scr.shape, F32)
    if not is_ctx:
        for d, s0_ref in enumerate((s0f_ref, s0b_ref)):
            for s in range(n_seq):
                for h in range(GLA_HEADS):
                    pr, e = divmod(h, 2)
                    st_scr[s, d, pr, e * GLA_DV:(e + 1) * GLA_DV, e * GLA_DK:(e + 1) * GLA_DK] = s0_ref[s, 0, h].T

    klane = lax.broadcasted_iota(jnp.int32, (1, GLA_K_WIDTH), 1)
    even_head = (klane // GLA_DK) % 2 == 0

    def chunk_decay(s, d, r0, f_ref):
        rows = pl.ds(s * seq_len + r0, C)
        f = f_ref[rows, :]
        f_hi = f.astype(BF16)
        f_lo = (f - f_hi.astype(F32)).astype(BF16)
        cum = _dot(w_ref[d], jnp.concatenate([f_hi, f_lo], axis=0))
        cum_scr[s, d] = cum
        return rows, cum

    def chunk_scores(s, d, rows, cum):
        def cum_rows(r, n):
            return jnp.broadcast_to(cum_scr[s, d, r:r + 1, :], (n, GLA_K_WIDTH))

        def level_factor(p):
            m = 1 << p
            edge = m - 1 if d == 0 else m
            if 2 * m >= 8:
                ref = jnp.concatenate([cum_rows(blk + edge, 2 * m) for blk in range(0, C, 2 * m)], axis=0)
            elif p == 1:
                sub = lax.broadcasted_iota(jnp.int32, (8, GLA_K_WIDTH), 0)
                ref = jnp.concatenate([jnp.where(sub < 4, cum_rows(blk + edge, 8), cum_rows(blk + 4 + edge, 8))
                                       for blk in range(0, C, 8)], axis=0)
            else:
                odd = lax.broadcasted_iota(jnp.int32, (C, GLA_K_WIDTH), 0) % 2 == 1
                if d == 0:
                    ref = jnp.where(odd, pltpu.roll(cum, 1, axis=0), cum)
                else:
                    ref = jnp.where(odd, cum, pltpu.roll(cum, C - 1, axis=0))
            return jnp.exp2((cum - ref) * scale_ref[d, p])

        qc = gq_ref[rows, :]
        kc = gk_ref[rows, :]
        vc = gv_ref[rows, :]
        last = C - 1 if d == 0 else 0
        eq = jnp.exp(cum)
        ek = jnp.exp(cum_rows(last, C) - cum)
        total = jnp.exp(cum_scr[s, d, last:last + 1, :])

        qb = qc.astype(BF16)
        kb = kc.astype(BF16)
        kb_even = jnp.where(even_head, kb, jnp.zeros_like(kb))
        kb_odd = jnp.where(even_head, jnp.zeros_like(kb), kb)
        zero = jnp.zeros((C, pair_k), BF16)
        a = None
        for p in range(GLA_LEVELS + 1):
            if p < GLA_LEVELS:
                ep = level_factor(p).astype(BF16)
                qe, ke_even, ke_odd = qb * ep, kb_even * ep, kb_odd * ep
            else:
                qe, ke_even, ke_odd = qb, kb_even, kb_odd
            lhs = jnp.concatenate([jnp.concatenate([qe[:, :pair_k], zero], axis=1),
                                   jnp.concatenate([zero, qe[:, pair_k:]], axis=1)], axis=0)
            sc = _dot_nt(lhs, jnp.concatenate([ke_even, ke_odd], axis=0)) * mask_ref[d, p]
            a = sc if a is None else a + sc
        q_in = (qc * eq).astype(BF16)
        k_out = (kc * ek).astype(BF16)
        return a.astype(BF16), q_in, k_out, vc, total

    def chunk_output(s, d, rows, accumulate, a, q_in, k_out, vc, total):
        zero_v = jnp.zeros((C, GLA_DV), BF16)
        for pr in range(n_pairs):
            kl = slice(pr * pair_k, (pr + 1) * pair_k)
            vp = vc[:, pr * pair_v:(pr + 1) * pair_v]
            v_diag = jnp.concatenate([jnp.concatenate([vp[:, :GLA_DV], zero_v], axis=1),
                                      jnp.concatenate([zero_v, vp[:, GLA_DV:]], axis=1)], axis=0)
            st = st_scr[s, d, pr]
            o = _dot(a[pr * C:(pr + 1) * C], v_diag) + _dot_nt(q_in[:, kl], st.astype(BF16))
            if accumulate:
                o_ref[rows, pr * pair_v:(pr + 1) * pair_v] += o
            else:
                o_ref[rows, pr * pair_v:(pr + 1) * pair_v] = o
            st_scr[s, d, pr] = st * total[:, kl] + _dot_tn(vp, k_out[:, kl]) * smask_ref[...]

    def body(accumulate, n, carry):
        fwd = pl.multiple_of(n * C, C)
        bwd = pl.multiple_of((n_chunks - 1 - n) * C, C)
        scans = [(s, d) for s in range(n_seq) for d in range(2)]
        decays = [chunk_decay(s, d, bwd if d else fwd, lb_ref if d else lf_ref) for s, d in scans]
        scores = [chunk_scores(s, d, *dec) for (s, d), dec in zip(scans, decays)]
        for (s, d), (rows, _), sc in zip(scans, decays, scores):
            chunk_output(s, d, rows, accumulate, *sc)
        return carry

    half = n_chunks // 2
    lax.fori_loop(0, half, functools.partial(body, False), 0)
    lax.fori_loop(half, n_chunks, functools.partial(body, True), 0)
    if is_ctx:
        slot = layer if creates_states else 0
        for d, s_ref in enumerate((sf_ref, sb_ref)):
            for s in range(n_seq):
                for h in range(GLA_HEADS):
                    pr, e = divmod(h, 2)
                    s_ref[s, slot, h] = st_scr[s, d, pr, e * GLA_DV:(e + 1) * GLA_DV, e * GLA_DK:(e + 1) * GLA_DK].T
            if creates_states:
                for other in range(DEPTH):
                    if other != layer:
                        s_ref[:, other] = jnp.zeros((n_seq, GLA_HEADS, GLA_DK, GLA_DV), F32)


def _gla_call(qkf, gv, consts, new_states=None, state_in=None, layer=0):
    is_ctx = state_in is None
    creates_states = is_ctx and new_states is None
    seq_len = SEQ if is_ctx else DEC_SEQ
    n_tok = qkf.shape[0]
    n_seq = GLA_CTX_SEQS_PER_STEP if is_ctx else DEC_BATCH
    rows = n_seq * seq_len
    tok = lambda width: pl.BlockSpec((rows, width), lambda b: (b, 0))
    whole = lambda a: pl.BlockSpec(a.shape, lambda b: (0,) * a.ndim)
    state_spec = pl.BlockSpec((n_seq, 1, GLA_HEADS, GLA_DK, GLA_DV), lambda b: (b, layer, 0, 0, 0))
    qkf_block = lambda j: pl.BlockSpec((rows, GLA_K_WIDTH), lambda b: (b, j))
    in_specs = [qkf_block(0), qkf_block(1), tok(GLA_V_WIDTH), qkf_block(2), qkf_block(3)]
    in_specs += [whole(a) for a in consts]
    args = [qkf, qkf, gv, qkf, qkf, *consts]
    out_shape = [jax.ShapeDtypeStruct((n_tok, GLA_V_WIDTH), F32)]
    out_specs = [tok(GLA_V_WIDTH)]
    aliases = {}
    if is_ctx:
        states_shape = (BATCH, DEPTH, GLA_HEADS, GLA_DK, GLA_DV)
        if creates_states:
            new_spec = pl.BlockSpec((n_seq,) + states_shape[1:], lambda b: (b, 0, 0, 0, 0))
        else:
            new_spec = state_spec
            aliases = {len(args): 1, len(args) + 1: 2}
            in_specs += [pl.BlockSpec(memory_space=pl.ANY)] * 2
            args += list(new_states)
        out_shape += [jax.ShapeDtypeStruct(states_shape, F32)] * 2
        out_specs += [new_spec, new_spec]
    else:
        in_specs += [state_spec, state_spec]
        args += list(state_in)
    return pl.pallas_call(
        functools.partial(_gla_kernel, seq_len, n_seq, is_ctx, creates_states, layer),
        grid=(n_tok // rows,),
        in_specs=in_specs,
        out_specs=out_specs,
        out_shape=out_shape,
        input_output_aliases=aliases,
        scratch_shapes=[
            pltpu.VMEM((n_seq, 2, GLA_HEADS // 2, 2 * GLA_DV, 2 * GLA_DK), F32),
            pltpu.VMEM((n_seq, 2, GLA_CHUNK, GLA_K_WIDTH), F32),
        ],
        compiler_params=_params(1),
        name="gla_ctx" if is_ctx else "gla_smp",
    )(*args)


def _rope_tables():
    quarter = GLA_DK // 4
    inv = ROPE_BASE ** (-jnp.arange(quarter, dtype=F32) / quarter)
    t = jnp.arange(DEC_SEQ)
    ang_r = (t // GRID_W).astype(F32)[:, None] * inv
    ang_c = (t % GRID_W).astype(F32)[:, None] * inv
    cos = jnp.concatenate([jnp.cos(ang_r)] * 2 + [jnp.cos(ang_c)] * 2, axis=-1)
    sin = jnp.concatenate([-jnp.sin(ang_r), jnp.sin(ang_r), -jnp.sin(ang_c), jnp.sin(ang_c)], axis=-1)
    return jnp.tile(cos, (1, GLA_HEADS)), jnp.tile(sin, (1, GLA_HEADS))


def _merge_mlp_kernel(layer, x_ref, mod_ref, ona_ref, ogla_ref, gate_ref, gout_ref, gmlp_ref,
                      wo_hbm, wup_hbm, wdown_hbm, o_ref,
                      wo_scr, wup_scr, wdown_scr, stage_cols, stage_rows, sem):
    first_step = pl.program_id(0) == 0
    n_chunks = D_FF // FF_CHUNK
    wo_halves = D_MODEL // FF_CHUNK

    def col_copy(j):
        slot = j % 2
        if j < wo_halves:
            src = wo_hbm.at[layer, :, j * FF_CHUNK:(j + 1) * FF_CHUNK]
        else:
            c = j - wo_halves
            src = wup_hbm.at[layer, :, c * FF_CHUNK:(c + 1) * FF_CHUNK]
        return pltpu.make_async_copy(src, stage_cols.at[slot], sem.at[slot])

    def row_copy(c):
        slot = c % 2
        src = wdown_hbm.at[layer, c * FF_CHUNK:(c + 1) * FF_CHUNK, :]
        return pltpu.make_async_copy(src, stage_rows.at[slot], sem.at[2 + slot])

    n_col_blocks = wo_halves + n_chunks

    def land_col(j):
        col_copy(j).wait()
        block = stage_cols[j % 2].astype(BF16)
        if j < wo_halves:
            wo_scr[:, j * FF_CHUNK:(j + 1) * FF_CHUNK] = block
        else:
            c = j - wo_halves
            wup_scr[:, c * FF_CHUNK:(c + 1) * FF_CHUNK] = block
        if j + 2 < n_col_blocks:
            col_copy(j + 2).start()

    def land_row(c):
        row_copy(c).wait()
        wdown_scr[c * FF_CHUNK:(c + 1) * FF_CHUNK, :] = stage_rows[c % 2].astype(BF16)
        if c + 2 < n_chunks:
            row_copy(c + 2).start()

    def run(streaming):
        if streaming:
            col_copy(0).start()
            col_copy(1).start()
            row_copy(0).start()
            row_copy(1).start()

        mod = mod_ref[0]
        ga1 = mod[:, 2 * D_MODEL:3 * D_MODEL]
        sh2 = mod[:, 3 * D_MODEL:4 * D_MODEL]
        sc2 = mod[:, 4 * D_MODEL:5 * D_MODEL]
        ga2 = mod[:, 5 * D_MODEL:6 * D_MODEL]

        og = ogla_ref[...]
        normed = []
        for h in range(GLA_HEADS):
            oh = og[:, h * GLA_DV:(h + 1) * GLA_DV]
            normed.append(oh * lax.rsqrt(jnp.mean(oh * oh, axis=-1, keepdims=True) + EPS))
        gate = gate_ref[...]
        g = jnp.concatenate(normed, axis=1) * gout_ref[...] * (gate * (1.0 / (1.0 + jnp.exp(-gate))))

        if streaming:
            for j in range(wo_halves):
                land_col(j)
        att = _dot(ona_ref[...], wo_scr[0:NA_WIDTH, :]) + _dot(g.astype(BF16), wo_scr[NA_WIDTH:, :])
        x = x_ref[...] + ga1 * att
        ms = jnp.mean(x * x, axis=-1, keepdims=True)
        h2 = ((x * lax.rsqrt(ms + EPS) * gmlp_ref[...]) * (1.0 + sc2) + sh2).astype(BF16)

        width = FF_CHUNK if streaming else MLP_CHUNK
        acc = None
        for c in range(D_FF // width):
            if streaming:
                land_col(wo_halves + c)
                land_row(c)
            u = jnp.maximum(_dot(h2, wup_scr[:, c * width:(c + 1) * width]), 0.0)
            down = _dot((u * u).astype(BF16), wdown_scr[c * width:(c + 1) * width, :])
            acc = down if acc is None else acc + down
        o_ref[...] = x + ga2 * acc

    pl.when(first_step)(functools.partial(run, True))
    pl.when(jnp.logical_not(first_step))(functools.partial(run, False))


def _merge_mlp_call(x, mod_l, o_na, o_gla, gate, gout_t, g_mlp_l, w_o, w_up, w_down, layer):
    tm = TOKEN_TILE
    n_tok = x.shape[0]
    is_ctx = n_tok == N_CTX
    row = lambda i: (i, 0)
    const = lambda i: (0, 0)
    in_hbm = pl.BlockSpec(memory_space=pl.ANY)
    return pl.pallas_call(
        functools.partial(_merge_mlp_kernel, layer),
        grid=(n_tok // tm,),
        in_specs=[
            pl.BlockSpec((tm, D_MODEL), row),
            pl.BlockSpec((1, 1, N_MOD * D_MODEL), _mod_index_map(is_ctx)),
            pl.BlockSpec((tm, NA_WIDTH), row),
            pl.BlockSpec((tm, GLA_V_WIDTH), row),
            pl.BlockSpec((tm, GLA_V_WIDTH), row),
            pl.BlockSpec((1, GLA_V_WIDTH), const),
            pl.BlockSpec((1, D_MODEL), const),
            in_hbm, in_hbm, in_hbm,
        ],
        out_specs=pl.BlockSpec((tm, D_MODEL), row),
        out_shape=jax.ShapeDtypeStruct((n_tok, D_MODEL), F32),
        scratch_shapes=[
            pltpu.VMEM((D_MODEL, D_MODEL), BF16),
            pltpu.VMEM((D_MODEL, D_FF), BF16),
            pltpu.VMEM((D_FF, D_MODEL), BF16),
            pltpu.VMEM((2, D_MODEL, FF_CHUNK), F32),
            pltpu.VMEM((2, FF_CHUNK, D_MODEL), F32),
            pltpu.SemaphoreType.DMA((4,)),
        ],
        compiler_params=_params(1),
        name="merge_mlp_ctx" if is_ctx else "merge_mlp_smp",
    )(x, mod_l, o_na, o_gla, gate, gout_t, g_mlp_l, w_o, w_up, w_down)


def kernel(x_prompt, x_sample, cache_k, cache_v, state_fwd, state_bwd, c, c_ctx, w_ada, b_ada, g_attn, w_in,
           g_q, g_k, rpb, w_gf, b_gf, w_gb, b_gb, g_gla_out, w_o, g_mlp, w_up, w_down):
    x_c = x_prompt.reshape(N_CTX, D_MODEL)
    x_s = x_sample.reshape(N_SMP, D_MODEL)

    c_rows = jnp.concatenate([c_ctx[None, :], c, jnp.zeros((MOD_ROWS - 1 - DEC_BATCH, D_MODEL), F32)], axis=0)
    mods = _ada_call(c_rows, w_ada, b_ada).reshape(DEPTH, MOD_ROWS, 1, N_MOD * D_MODEL)

    w_in_t = jnp.swapaxes(w_in, 1, 2)
    wgate = jnp.zeros((DEPTH, 2 * GLA_GATE_RANK, 2 * GLA_K_WIDTH), F32)
    wgate = wgate.at[:, :GLA_GATE_RANK, :GLA_K_WIDTH].set(w_gf)
    wgate = wgate.at[:, GLA_GATE_RANK:, GLA_K_WIDTH:].set(w_gb).astype(BF16)
    bgate = jnp.concatenate([b_gf, b_gb], axis=-1).reshape(DEPTH, 1, 2 * GLA_K_WIDTH)
    head_of = np.arange(NA_WIDTH) // NA_HEAD_DIM
    seg = jnp.asarray((head_of[:, None] == head_of[None, :]).astype(np.float32) / NA_HEAD_DIM, BF16)
    gq_t = jnp.tile(g_q, (1, NA_HEADS)).reshape(DEPTH, 1, NA_WIDTH)
    gk_t = jnp.tile(g_k, (1, NA_HEADS)).reshape(DEPTH, 1, NA_WIDTH)
    gout_t = jnp.tile(g_gla_out, (1, GLA_HEADS)).reshape(DEPTH, 1, GLA_V_WIDTH)
    n_dr, n_dc = rpb.shape[2:]
    rpb_rows = jnp.pad(rpb, ((0, 0), (0, 0), (0, -n_dr % 8), (0, 2 * GRID_W - n_dc)))
    gla_w_np, gla_masks_np, gla_smask_np, gla_scales_np = _gla_constants()
    gla_consts = (jnp.asarray(gla_w_np, BF16), jnp.asarray(gla_masks_np, F32), jnp.asarray(gla_smask_np, F32),
                  jnp.asarray(gla_scales_np, F32))
    rope = _rope_tables()

    cache_kt = jnp.swapaxes(cache_k, 3, 4)
    cache_vt = jnp.swapaxes(cache_v, 3, 4)
    new_kv = None
    new_states = None

    for l in range(DEPTH):
        proj_w = (g_attn[l].reshape(1, D_MODEL), w_in_t, seg, gq_t[l], gk_t[l], wgate[l], bgate[l], rope)
        mlp_w = (gout_t[l], g_mlp[l].reshape(1, D_MODEL), w_o, w_up, w_down)

        o_na, qkf, gv, gate, *new_kv = _proj_call(x_c, mods[l], *proj_w, True, new_kv=new_kv, layer=l)
        o_gla, *new_states = _gla_call(qkf, gv, gla_consts, new_states=new_states, layer=l)
        x_c = _merge_mlp_call(x_c, mods[l], o_na, o_gla, gate, *mlp_w, layer=l)

        q, k, v, qkf, gv, gate = _proj_call(x_s, mods[l], *proj_w, False, layer=l)
        o_na = _na_call(q, k, v, cache_kt, cache_vt, rpb_rows, l)
        (o_gla,) = _gla_call(qkf, gv, gla_consts, state_in=(state_fwd, state_bwd), layer=l)
        x_s = _merge_mlp_call(x_s, mods[l], o_na, o_gla, gate, *mlp_w, layer=l)

    return (x_c.reshape(BATCH, SEQ, D_MODEL), x_s.reshape(DEC_BATCH, DEC_SEQ, D_MODEL),
            jnp.swapaxes(new_kv[0], 3, 4), jnp.swapaxes(new_kv[1], 3, 4), *new_states)
```

```python
import functools

import numpy as np
import jax
import jax.numpy as jnp
from jax import lax
from jax.experimental import pallas as pl
from jax.experimental.pallas import tpu as pltpu

F32 = jnp.float32
BF16 = jnp.bfloat16

D_MODEL = 1024
BATCH = 16
SEQ = 256
DEPTH = 4
DEC_BATCH = 2
DEC_SEQ = 1024
PAST_LEN = 512
GRID_W = 64
GRID_ROWS = DEC_SEQ // GRID_W
NA_WIDTH = D_MODEL // 2
NA_HEAD_DIM = 64
NA_HEADS = NA_WIDTH // NA_HEAD_DIM
NA_WIN_ROWS = 8
NA_WIN_COLS = 16
GLA_V_WIDTH = D_MODEL - NA_WIDTH
GLA_HEADS = 4
GLA_DV = GLA_V_WIDTH // GLA_HEADS
GLA_DK = GLA_DV // 2
GLA_K_WIDTH = GLA_HEADS * GLA_DK
GLA_GATE_RANK = 16
GLA_GATE_TAU = 16.0
GLA_CHUNK = 64
D_FF = 4 * D_MODEL
ROPE_BASE = 10000.0
N_MOD = 6
EPS = 1e-6

IN_WIDTH = 3 * NA_WIDTH + 2 * GLA_K_WIDTH + 2 * GLA_V_WIDTH + 2 * GLA_GATE_RANK
PROJ_WIDE = IN_WIDTH - 2 * GLA_GATE_RANK

N_CTX = BATCH * SEQ
N_SMP = DEC_BATCH * DEC_SEQ
SUBLANES = 8
MOD_ROWS = SUBLANES
MASK_VALUE = -1e30

TOKEN_TILE = 512
PROJ_BLOCK = 512
FF_CHUNK = 512
MLP_CHUNK = 1024
ADA_TILE_N = 1536
GLA_LEVELS = 6
NA_ROW_GROUP = 4
GLA_CTX_SEQS_PER_STEP = 4
VMEM_LIMIT = 52 * 1024 * 1024


def _dot(a, b):
    return jnp.dot(a, b, preferred_element_type=F32)


def _dot_nt(a, b):
    return lax.dot_general(a, b, (((1,), (1,)), ((), ())), preferred_element_type=F32)


def _dot_tn(a, b):
    return lax.dot_general(a, b, (((0,), (0,)), ((), ())), preferred_element_type=F32)


def _params(n_grid_dims=1):
    return pltpu.CompilerParams(dimension_semantics=("arbitrary",) * n_grid_dims,
                                vmem_limit_bytes=VMEM_LIMIT)


def _mod_index_map(is_ctx):
    if is_ctx:
        return lambda i: (0, 0, 0)
    return lambda i: (1 + i // (DEC_SEQ // TOKEN_TILE), 0, 0)


def _ada_kernel(c_ref, w_ref, b_ref, o_ref):
    cv = c_ref[...]
    s = cv * (1.0 / (1.0 + jnp.exp(-cv)))
    o_ref[0] = _dot(s.astype(BF16), w_ref[0].astype(BF16)) + b_ref[0]


def _ada_call(c_rows, w_ada, b_ada):
    n_mod = N_MOD * D_MODEL
    return pl.pallas_call(
        _ada_kernel,
        grid=(DEPTH, n_mod // ADA_TILE_N),
        in_specs=[
            pl.BlockSpec((MOD_ROWS, D_MODEL), lambda l, j: (0, 0)),
            pl.BlockSpec((1, D_MODEL, ADA_TILE_N), lambda l, j: (l, 0, j)),
            pl.BlockSpec((1, 1, ADA_TILE_N), lambda l, j: (l, 0, j)),
        ],
        out_specs=pl.BlockSpec((1, MOD_ROWS, ADA_TILE_N), lambda l, j: (l, 0, j)),
        out_shape=jax.ShapeDtypeStruct((DEPTH, MOD_ROWS, n_mod), F32),
        compiler_params=_params(2),
        name="ada_mod",
    )(c_rows, w_ada, b_ada.reshape(DEPTH, 1, n_mod))


def _ctx_attention(q, k, v):
    n_seq = q.shape[0] // SEQ
    lane = lax.broadcasted_iota(jnp.int32, (1, 2 * NA_HEAD_DIM), 1)
    first = lane < NA_HEAD_DIM
    pairs = [(slice(s * SEQ, (s + 1) * SEQ), slice(hp * 2 * NA_HEAD_DIM, (hp + 1) * 2 * NA_HEAD_DIM))
             for s in range(n_seq) for hp in range(NA_HEADS // 2)]
    scores = []
    for rows, cols in pairs:
        q2 = q[rows, cols]
        k2 = k[rows, cols]
        scores.append(_dot_nt(jnp.where(first, q2, jnp.zeros_like(q2)), k2))
        scores.append(_dot_nt(jnp.where(first, jnp.zeros_like(q2), q2), k2))
    probs = []
    for s in scores:
        p = jnp.exp(s - jnp.max(s, axis=-1, keepdims=True))
        probs.append((p.astype(BF16), jnp.sum(p, axis=-1, keepdims=True)))
    outs = []
    for i, (rows, cols) in enumerate(pairs):
        v2 = v[rows, cols]
        (p0, l0), (p1, l1) = probs[2 * i], probs[2 * i + 1]
        outs.append(jnp.where(first, _dot(p0, v2) / l0, _dot(p1, v2) / l1))
    per_seq = NA_HEADS // 2
    return jnp.concatenate([jnp.concatenate(outs[s * per_seq:(s + 1) * per_seq], axis=1) for s in range(n_seq)],
                           axis=0)


def _log_sigmoid(x):
    return jnp.minimum(x, 0.0) - jnp.log(1.0 + jnp.exp(-jnp.abs(x)))


def _proj_kernel(is_ctx, creates_kv, layer, x_ref, mod_ref, g_ref, seg_ref, gq_ref, gk_ref, wgate_ref, bgate_ref,
                 cos_ref, sin_ref, win_hbm, *rest):
    if is_ctx:
        if not creates_kv:
            rest = rest[2:]
        ona_out = rest[0]
        rest = rest[1:]
    else:
        q_out, k_out, v_out = rest[:3]
        rest = rest[3:]
    qkf_out, gv_out, gate_out = rest[:3]
    rest = rest[3:]
    if is_ctx:
        knew_out, vnew_out = rest[:2]
        rest = rest[2:]
    w_scr, stage, tail_stage, sem = rest

    n_blocks = PROJ_WIDE // PROJ_BLOCK
    tail = slice(PROJ_WIDE, IN_WIDTH)
    first_step = pl.program_id(0) == 0

    def block_copy(j):
        src = win_hbm.at[layer, j * PROJ_BLOCK:(j + 1) * PROJ_BLOCK, :]
        return pltpu.make_async_copy(src, stage.at[j % 2], sem.at[j % 2])

    def tail_copy():
        return pltpu.make_async_copy(win_hbm.at[layer, tail, :], tail_stage, sem.at[2])

    def land(j):
        block_copy(j).wait()
        w_scr[j * PROJ_BLOCK:(j + 1) * PROJ_BLOCK, :] = stage[j % 2].astype(BF16)
        if j + 2 < n_blocks:
            block_copy(j + 2).start()

    def run(streaming):
        if streaming:
            block_copy(0).start()
            block_copy(1).start()
            tail_copy().start()

        x = x_ref[...]
        mod = mod_ref[0]
        sh1 = mod[:, 0:D_MODEL]
        sc1 = mod[:, D_MODEL:2 * D_MODEL]
        ms = jnp.mean(x * x, axis=-1, keepdims=True)
        h = (x * lax.rsqrt(ms + EPS) * g_ref[...]) * (1.0 + sc1) + sh1
        hb = h.astype(BF16)

        def projected(j):
            if streaming:
                land(j)
            return _dot_nt(hb, w_scr[j * PROJ_BLOCK:(j + 1) * PROJ_BLOCK, :])

        seg = seg_ref[...]
        q = projected(0)
        q_ms = _dot((q * q).astype(BF16), seg)
        qb = (q * lax.rsqrt(q_ms + EPS) * gq_ref[...] * (NA_HEAD_DIM ** -0.5)).astype(BF16)
        k = projected(1)
        k_ms = _dot((k * k).astype(BF16), seg)
        kn = k * lax.rsqrt(k_ms + EPS) * gk_ref[...]
        v = projected(2)
        if not is_ctx:
            q_out[...] = qb
            k_out[...] = kn.astype(BF16)
            v_out[...] = v.astype(BF16)
        else:
            ona_out[...] = _ctx_attention(qb, kn.astype(BF16), v.astype(BF16)).astype(BF16)
            slot = layer if creates_kv else 0
            for s in range(TOKEN_TILE // SEQ):
                rows = slice(s * SEQ, (s + 1) * SEQ)
                for hp in range(NA_HEADS // 2):
                    cols = slice(hp * 2 * NA_HEAD_DIM, (hp + 1) * 2 * NA_HEAD_DIM)
                    for out, t in ((knew_out, kn[rows, cols].T), (vnew_out, v[rows, cols].T)):
                        out[s, slot, 2 * hp] = t[:NA_HEAD_DIM]
                        out[s, slot, 2 * hp + 1] = t[NA_HEAD_DIM:]
            if creates_kv:
                for out in (knew_out, vnew_out):
                    for other in range(DEPTH):
                        if other != layer:
                            out[:, other] = jnp.zeros((TOKEN_TILE // SEQ, NA_HEADS, NA_HEAD_DIM, SEQ), F32)

        z = projected(3)
        gla_q = z[:, 0:GLA_K_WIDTH] * (GLA_DK ** -0.5)
        gla_k = z[:, GLA_K_WIDTH:]
        if not is_ctx:
            lane = lax.broadcasted_iota(jnp.int32, (1, GLA_K_WIDTH), 1)
            quarter = GLA_DK // 4
            low = (lane % (2 * quarter)) < quarter
            cos = cos_ref[...]
            sin = sin_ref[...]

            def rope(t):
                partner = jnp.where(low, pltpu.roll(t, GLA_K_WIDTH - quarter, axis=1),
                                    pltpu.roll(t, quarter, axis=1))
                return t * cos + partner * sin

            gla_q = rope(gla_q)
            gla_k = rope(gla_k)
        qkf_out[:, 0:GLA_K_WIDTH] = gla_q
        qkf_out[:, GLA_K_WIDTH:2 * GLA_K_WIDTH] = gla_k
        gv_out[...] = projected(4).astype(BF16)
        gate_out[...] = projected(5)

        if streaming:
            tail_copy().wait()
            w_scr[tail, :] = tail_stage[...].astype(BF16)
        zz_t = _dot_nt(w_scr[tail, :], hb)
        pre = _dot_tn(zz_t.astype(BF16), wgate_ref[...]) + bgate_ref[...]
        ls = _log_sigmoid(pre) * (1.0 / GLA_GATE_TAU)
        qkf_out[:, 2 * GLA_K_WIDTH:] = ls

    pl.when(first_step)(functools.partial(run, True))
    pl.when(jnp.logical_not(first_step))(functools.partial(run, False))


def _proj_call(x, mod_l, g_attn_l, w_in_t, seg, gq_t, gk_t, wgate, bgate, rope, is_ctx, new_kv=None, layer=0):
    tm = TOKEN_TILE
    n_tok = x.shape[0]
    creates_kv = is_ctx and new_kv is None
    row = lambda i: (i, 0)
    const = lambda i: (0, 0)
    rope_rows = const if is_ctx else (lambda i: (i % (DEC_SEQ // tm), 0))
    if is_ctx:
        out_shapes = [jax.ShapeDtypeStruct((n_tok, NA_WIDTH), BF16)]
    else:
        out_shapes = [jax.ShapeDtypeStruct((n_tok, NA_WIDTH), BF16)] * 3
    out_shapes += [
        jax.ShapeDtypeStruct((n_tok, 4 * GLA_K_WIDTH), F32),
        jax.ShapeDtypeStruct((n_tok, GLA_V_WIDTH), BF16),
        jax.ShapeDtypeStruct((n_tok, GLA_V_WIDTH), F32),
    ]
    out_specs = [pl.BlockSpec((tm, s.shape[1]), row) for s in out_shapes]
    in_specs = [
        pl.BlockSpec((tm, D_MODEL), row),
        pl.BlockSpec((1, 1, N_MOD * D_MODEL), _mod_index_map(is_ctx)),
        pl.BlockSpec((1, D_MODEL), const),
        pl.BlockSpec((NA_WIDTH, NA_WIDTH), const),
        pl.BlockSpec((1, NA_WIDTH), const),
        pl.BlockSpec((1, NA_WIDTH), const),
        pl.BlockSpec(wgate.shape, const),
        pl.BlockSpec((1, 2 * GLA_K_WIDTH), const),
        pl.BlockSpec((tm, GLA_K_WIDTH), rope_rows),
        pl.BlockSpec((tm, GLA_K_WIDTH), rope_rows),
        pl.BlockSpec(memory_space=pl.ANY),
    ]
    args = [x, mod_l, g_attn_l, seg, gq_t, gk_t, wgate, bgate, *rope, w_in_t]
    aliases = {}
    if is_ctx:
        kv_shape = (BATCH, DEPTH, NA_HEADS, NA_HEAD_DIM, SEQ)
        if creates_kv:
            kv_spec = pl.BlockSpec((tm // SEQ,) + kv_shape[1:], lambda i: (i, 0, 0, 0, 0))
        else:
            kv_spec = pl.BlockSpec((tm // SEQ, 1) + kv_shape[2:], lambda i: (i, layer, 0, 0, 0))
            aliases = {len(args): len(out_shapes), len(args) + 1: len(out_shapes) + 1}
            in_specs += [pl.BlockSpec(memory_space=pl.ANY)] * 2
            args += list(new_kv)
        out_shapes += [jax.ShapeDtypeStruct(kv_shape, F32)] * 2
        out_specs += [kv_spec, kv_spec]
    return pl.pallas_call(
        functools.partial(_proj_kernel, is_ctx, creates_kv, layer),
        grid=(n_tok // tm,),
        in_specs=in_specs,
        out_specs=out_specs,
        out_shape=out_shapes,
        input_output_aliases=aliases,
        scratch_shapes=[
            pltpu.VMEM((IN_WIDTH, D_MODEL), BF16),
            pltpu.VMEM((2, PROJ_BLOCK, D_MODEL), F32),
            pltpu.VMEM((IN_WIDTH - PROJ_WIDE, D_MODEL), F32),
            pltpu.SemaphoreType.DMA((3,)),
        ],
        compiler_params=_params(1),
        name="proj_ctx" if is_ctx else "proj_smp",
    )(*args)


def _na_window_start(r):
    return min(max(r - NA_WIN_ROWS // 2, 0), GRID_ROWS - NA_WIN_ROWS)


def _na_row_groups():
    groups = []
    for g in range(GRID_ROWS // NA_ROW_GROUP):
        starts = [_na_window_start(r) for r in range(g * NA_ROW_GROUP, (g + 1) * NA_ROW_GROUP)]
        lo = min(starts) // 2 * 2
        count = max(starts) + NA_WIN_ROWS - lo
        groups.append((lo, count + count % 2))
    return groups


def _na_bias_tiles(rpb_ref, hh, tiles_scr):
    n_dr = 2 * NA_WIN_ROWS - 1
    lanes = 2 * GRID_W
    lane = lax.broadcasted_iota(jnp.int32, (GRID_W, lanes), 1)
    col = lax.broadcasted_iota(jnp.int32, (GRID_W, lanes), 0)
    first = lane < GRID_W
    c2 = lane % GRID_W
    cs = jnp.clip(col - NA_WIN_COLS // 2, 0, GRID_W - NA_WIN_COLS)
    valid = (c2 >= cs) & (c2 < cs + NA_WIN_COLS)
    rows = rpb_ref[0, hh]
    base = lanes - (NA_WIN_COLS - 1)

    def skew(dr, shift):
        row = jnp.broadcast_to(rows[dr:dr + 1], (GRID_W, lanes))
        return pltpu.roll(row, shift, axis=1, stride=1, stride_axis=0)

    for e in range(n_dr - 1):
        tile = jnp.where(first, skew(e, base), skew(e + 1, (base + GRID_W) % lanes))
        tiles_scr[e] = jnp.where(valid, tile, MASK_VALUE)


def _na_group_bias(tiles_scr, g, key_lo, key_count, first):
    masked = jnp.full((GRID_W, 2 * GRID_W), MASK_VALUE, F32)
    row_tiles = []
    for r in range(g * NA_ROW_GROUP, (g + 1) * NA_ROW_GROUP):
        rs = _na_window_start(r)
        pieces = []
        for u in range(key_count // 2):
            r0 = key_lo + 2 * u
            in0 = rs <= r0 < rs + NA_WIN_ROWS
            in1 = rs <= r0 + 1 < rs + NA_WIN_ROWS
            if not (in0 or in1):
                pieces.append(masked)
                continue
            tile = tiles_scr[r0 - r + NA_WIN_ROWS - 1]
            if not in1:
                tile = jnp.where(first, tile, MASK_VALUE)
            elif not in0:
                tile = jnp.where(first, MASK_VALUE, tile)
            pieces.append(tile)
        row_tiles.append(jnp.concatenate(pieces, axis=1))
    return jnp.concatenate(row_tiles, axis=0)


def _na_kernel(q_ref, k_ref, v_ref, kc_ref, vc_ref, rpb_ref, o_ref, sc_ref, tiles_scr):
    lane = lax.broadcasted_iota(jnp.int32, (1, 2 * NA_HEAD_DIM), 1)
    first = lane < NA_HEAD_DIM
    q2 = q_ref[...]
    kc2 = jnp.concatenate([kc_ref[0, 0, 0], kc_ref[0, 0, 1]], axis=0).astype(BF16)
    vc2 = jnp.concatenate([vc_ref[0, 0, 0], vc_ref[0, 0, 1]], axis=0).astype(BF16)
    block = NA_ROW_GROUP * GRID_W
    groups = _na_row_groups()
    qms = [jnp.where(first, q2, jnp.zeros_like(q2)), jnp.where(first, jnp.zeros_like(q2), q2)]
    for hh in range(2):
        sc_ref[hh] = _dot(qms[hh], kc2)
        _na_bias_tiles(rpb_ref, hh, tiles_scr.at[hh])
    problems = [(hh, g) for hh in range(2) for g in range(len(groups))]
    scores = []
    for hh, g in problems:
        key_lo, key_count = groups[g]
        keys = slice(key_lo * GRID_W, (key_lo + key_count) * GRID_W)
        s = _dot_nt(qms[hh][g * block:(g + 1) * block], k_ref[keys, :])
        scores.append(s + _na_group_bias(tiles_scr.at[hh], g, key_lo, key_count, first))
    probs = []
    for (hh, g), s in zip(problems, scores):
        sc = sc_ref[hh, g * block:(g + 1) * block, :]
        m = jnp.maximum(jnp.max(s, axis=-1, keepdims=True), jnp.max(sc, axis=-1, keepdims=True))
        p = jnp.exp(s - m)
        pc = jnp.exp(sc - m)
        l = jnp.sum(p, axis=-1, keepdims=True) + jnp.sum(pc, axis=-1, keepdims=True)
        probs.append((p.astype(BF16), pc.astype(BF16), l))
    outs = [[], []]
    for (hh, g), (p, pc, l) in zip(problems, probs):
        key_lo, key_count = groups[g]
        keys = slice(key_lo * GRID_W, (key_lo + key_count) * GRID_W)
        outs[hh].append((_dot(p, v_ref[keys, :]) + _dot_nt(pc, vc2)) / l)
    outs = [jnp.concatenate(o, axis=0) for o in outs]
    o_ref[...] = jnp.where(first, outs[0], outs[1]).astype(BF16)


def _na_call(q, k, v, cache_kt, cache_vt, rpb_rows, layer):
    tok = pl.BlockSpec((DEC_SEQ, 2 * NA_HEAD_DIM), lambda b, hp: (b, hp))
    cache = pl.BlockSpec((1, 1, 2, NA_HEAD_DIM, PAST_LEN), lambda b, hp: (b, layer, hp, 0, 0))
    n_pairs = 2 * NA_WIN_ROWS - 2
    return pl.pallas_call(
        _na_kernel,
        grid=(DEC_BATCH, NA_HEADS // 2),
        in_specs=[
            tok, tok, tok, cache, cache,
            pl.BlockSpec((1, 2) + rpb_rows.shape[2:], lambda b, hp: (layer, hp, 0, 0)),
        ],
        out_specs=tok,
        out_shape=jax.ShapeDtypeStruct((N_SMP, NA_WIDTH), BF16),
        scratch_shapes=[pltpu.VMEM((2, DEC_SEQ, PAST_LEN), F32),
                        pltpu.VMEM((2, n_pairs, GRID_W, 2 * GRID_W), F32)],
        compiler_params=_params(2),
        name="nbr_attn",
    )(q, k, v, cache_kt, cache_vt, rpb_rows)


def _gla_constants():
    C = GLA_CHUNK
    ii, jj = np.meshgrid(np.arange(C), np.arange(C), indexing="ij")
    tri = np.stack([jj <= ii, jj >= ii]).astype(np.float32)
    w = np.concatenate([tri, tri], axis=-1)

    x = ii ^ jj
    hb = np.where(x > 0, np.floor(np.log2(np.maximum(x, 1))), -1).astype(np.int64)
    masks = np.zeros((2, GLA_LEVELS + 1, C, C), np.float32)
    for p in range(GLA_LEVELS):
        masks[0, p] = (jj < ii) & (hb == p)
        masks[1, p] = (jj > ii) & (hb == p)
    masks[:, GLA_LEVELS] = np.eye(C)
    masks = np.tile(masks, (1, 1, 2, 2))
    state_mask = (np.arange(2 * GLA_DV)[:, None] // GLA_DV == np.arange(2 * GLA_DK)[None, :] // GLA_DK)
    upper = np.stack([(np.arange(C) >> p) & 1 for p in range(GLA_LEVELS)]).astype(bool)
    sign = np.stack([np.where(upper, 1.0, -1.0), np.where(upper, -1.0, 1.0)])
    scales = np.broadcast_to((sign * np.log2(np.e))[..., None], (2, GLA_LEVELS, C, GLA_K_WIDTH))
    return w, masks, state_mask.astype(np.float32), np.ascontiguousarray(scales, np.float32)


def _gla_kernel(seq_len, n_seq, is_ctx, creates_states, layer, *refs):
    refs = list(refs)
    gq_ref, gk_ref, gv_ref, lf_ref, lb_ref, w_ref, mask_ref, smask_ref, scale_ref = refs[:9]
    refs = refs[9:]
    if is_ctx:
        if not creates_states:
            refs = refs[2:]
        o_ref, sf_ref, sb_ref = refs[:3]
        refs = refs[3:]
    else:
        s0f_ref, s0b_ref, o_ref = refs[:3]
        refs = refs[3:]
    st_scr, cum_scr = refs

    C = GLA_CHUNK
    n_chunks = seq_len // C
    n_pairs = GLA_HEADS // 2
    pair_k = 2 * GLA_DK
    pair_v = 2 * GLA_DV

    st_scr[...] = jnp.zeros(st_scr.shape, F32)
    if not is_ctx:
        for d, s0_ref in enumerate((s0f_ref, s0b_ref)):
            for s in range(n_seq):
                for h in range(GLA_HEADS):
                    pr, e = divmod(h, 2)
                    st_scr[s, d, pr, e * GLA_DV:(e + 1) * GLA_DV, e * GLA_DK:(e + 1) * GLA_DK] = s0_ref[s, 0, h].T

    klane = lax.broadcasted_iota(jnp.int32, (1, GLA_K_WIDTH), 1)
    even_head = (klane // GLA_DK) % 2 == 0

    def chunk_decay(s, d, r0, f_ref):
        rows = pl.ds(s * seq_len + r0, C)
        f = f_ref[rows, :]
        f_hi = f.astype(BF16)
        f_lo = (f - f_hi.astype(F32)).astype(BF16)
        cum = _dot(w_ref[d], jnp.concatenate([f_hi, f_lo], axis=0))
        cum_scr[s, d] = cum
        return rows, cum

    def chunk_scores(s, d, rows, cum):
        def cum_rows(r, n):
            return jnp.broadcast_to(cum_scr[s, d, r:r + 1, :], (n, GLA_K_WIDTH))

        def level_factor(p):
            m = 1 << p
            edge = m - 1 if d == 0 else m
            if 2 * m >= SUBLANES:
                ref = jnp.concatenate([cum_rows(blk + edge, 2 * m) for blk in range(0, C, 2 * m)], axis=0)
            elif 2 * m == SUBLANES // 2:
                sub = lax.broadcasted_iota(jnp.int32, (SUBLANES, GLA_K_WIDTH), 0)
                ref = jnp.concatenate([jnp.where(sub < 2 * m, cum_rows(blk + edge, SUBLANES),
                                                 cum_rows(blk + 2 * m + edge, SUBLANES))
                                       for blk in range(0, C, SUBLANES)], axis=0)
            else:
                odd = lax.broadcasted_iota(jnp.int32, (C, GLA_K_WIDTH), 0) % 2 == 1
                if d == 0:
                    ref = jnp.where(odd, pltpu.roll(cum, 1, axis=0), cum)
                else:
                    ref = jnp.where(odd, cum, pltpu.roll(cum, C - 1, axis=0))
            return jnp.exp2((cum - ref) * scale_ref[d, p])

        qc = gq_ref[rows, :]
        kc = gk_ref[rows, :]
        vc = gv_ref[rows, :]
        last = C - 1 if d == 0 else 0
        eq = jnp.exp(cum)
        ek = jnp.exp(cum_rows(last, C) - cum)
        total = jnp.exp(cum_scr[s, d, last:last + 1, :])

        qb = qc.astype(BF16)
        kb = kc.astype(BF16)
        kb_even = jnp.where(even_head, kb, jnp.zeros_like(kb))
        kb_odd = jnp.where(even_head, jnp.zeros_like(kb), kb)
        zero = jnp.zeros((C, pair_k), BF16)
        a = None
        for p in range(GLA_LEVELS + 1):
            if p < GLA_LEVELS:
                ep = level_factor(p).astype(BF16)
                qe, ke_even, ke_odd = qb * ep, kb_even * ep, kb_odd * ep
            else:
                qe, ke_even, ke_odd = qb, kb_even, kb_odd
            lhs = jnp.concatenate([jnp.concatenate([qe[:, :pair_k], zero], axis=1),
                                   jnp.concatenate([zero, qe[:, pair_k:]], axis=1)], axis=0)
            sc = _dot_nt(lhs, jnp.concatenate([ke_even, ke_odd], axis=0)) * mask_ref[d, p]
            a = sc if a is None else a + sc
        q_in = (qc * eq).astype(BF16)
        k_out = (kc * ek).astype(BF16)
        return a.astype(BF16), q_in, k_out, vc, total

    def chunk_output(s, d, rows, accumulate, a, q_in, k_out, vc, total):
        zero_v = jnp.zeros((C, GLA_DV), BF16)
        for pr in range(n_pairs):
            kl = slice(pr * pair_k, (pr + 1) * pair_k)
            vp = vc[:, pr * pair_v:(pr + 1) * pair_v]
            v_diag = jnp.concatenate([jnp.concatenate([vp[:, :GLA_DV], zero_v], axis=1),
                                      jnp.concatenate([zero_v, vp[:, GLA_DV:]], axis=1)], axis=0)
            st = st_scr[s, d, pr]
            o = _dot(a[pr * C:(pr + 1) * C], v_diag) + _dot_nt(q_in[:, kl], st.astype(BF16))
            if accumulate:
                o_ref[rows, pr * pair_v:(pr + 1) * pair_v] += o
            else:
                o_ref[rows, pr * pair_v:(pr + 1) * pair_v] = o
            st_scr[s, d, pr] = st * total[:, kl] + _dot_tn(vp, k_out[:, kl]) * smask_ref[...]

    def body(accumulate, n, carry):
        fwd = pl.multiple_of(n * C, C)
        bwd = pl.multiple_of((n_chunks - 1 - n) * C, C)
        scans = [(s, d) for s in range(n_seq) for d in range(2)]
        decays = [chunk_decay(s, d, bwd if d else fwd, lb_ref if d else lf_ref) for s, d in scans]
        scores = [chunk_scores(s, d, *dec) for (s, d), dec in zip(scans, decays)]
        for (s, d), (rows, _), sc in zip(scans, decays, scores):
            chunk_output(s, d, rows, accumulate, *sc)
        return carry

    half = n_chunks // 2
    lax.fori_loop(0, half, functools.partial(body, False), 0)
    lax.fori_loop(half, n_chunks, functools.partial(body, True), 0)
    if is_ctx:
        slot = layer if creates_states else 0
        for d, s_ref in enumerate((sf_ref, sb_ref)):
            for s in range(n_seq):
                for h in range(GLA_HEADS):
                    pr, e = divmod(h, 2)
                    s_ref[s, slot, h] = st_scr[s, d, pr, e * GLA_DV:(e + 1) * GLA_DV, e * GLA_DK:(e + 1) * GLA_DK].T
            if creates_states:
                for other in range(DEPTH):
                    if other != layer:
                        s_ref[:, other] = jnp.zeros((n_seq, GLA_HEADS, GLA_DK, GLA_DV), F32)


def _gla_call(qkf, gv, consts, new_states=None, state_in=None, layer=0):
    is_ctx = state_in is None
    creates_states = is_ctx and new_states is None
    seq_len = SEQ if is_ctx else DEC_SEQ
    n_tok = qkf.shape[0]
    n_seq = GLA_CTX_SEQS_PER_STEP if is_ctx else DEC_BATCH
    rows = n_seq * seq_len
    tok = lambda width: pl.BlockSpec((rows, width), lambda b: (b, 0))
    whole = lambda a: pl.BlockSpec(a.shape, lambda b: (0,) * a.ndim)
    state_spec = pl.BlockSpec((n_seq, 1, GLA_HEADS, GLA_DK, GLA_DV), lambda b: (b, layer, 0, 0, 0))
    qkf_block = lambda j: pl.BlockSpec((rows, GLA_K_WIDTH), lambda b: (b, j))
    in_specs = [qkf_block(0), qkf_block(1), tok(GLA_V_WIDTH), qkf_block(2), qkf_block(3)]
    in_specs += [whole(a) for a in consts]
    args = [qkf, qkf, gv, qkf, qkf, *consts]
    out_shape = [jax.ShapeDtypeStruct((n_tok, GLA_V_WIDTH), F32)]
    out_specs = [tok(GLA_V_WIDTH)]
    aliases = {}
    if is_ctx:
        states_shape = (BATCH, DEPTH, GLA_HEADS, GLA_DK, GLA_DV)
        if creates_states:
            new_spec = pl.BlockSpec((n_seq,) + states_shape[1:], lambda b: (b, 0, 0, 0, 0))
        else:
            new_spec = state_spec
            aliases = {len(args): 1, len(args) + 1: 2}
            in_specs += [pl.BlockSpec(memory_space=pl.ANY)] * 2
            args += list(new_states)
        out_shape += [jax.ShapeDtypeStruct(states_shape, F32)] * 2
        out_specs += [new_spec, new_spec]
    else:
        in_specs += [state_spec, state_spec]
        args += list(state_in)
    return pl.pallas_call(
        functools.partial(_gla_kernel, seq_len, n_seq, is_ctx, creates_states, layer),
        grid=(n_tok // rows,),
        in_specs=in_specs,
        out_specs=out_specs,
        out_shape=out_shape,
        input_output_aliases=aliases,
        scratch_shapes=[
            pltpu.VMEM((n_seq, 2, GLA_HEADS // 2, 2 * GLA_DV, 2 * GLA_DK), F32),
            pltpu.VMEM((n_seq, 2, GLA_CHUNK, GLA_K_WIDTH), F32),
        ],
        compiler_params=_params(1),
        name="gla_ctx" if is_ctx else "gla_smp",
    )(*args)


def _rope_tables():
    quarter = GLA_DK // 4
    inv = ROPE_BASE ** (-jnp.arange(quarter, dtype=F32) / quarter)
    t = jnp.arange(DEC_SEQ)
    ang_r = (t // GRID_W).astype(F32)[:, None] * inv
    ang_c = (t % GRID_W).astype(F32)[:, None] * inv
    cos = jnp.concatenate([jnp.cos(ang_r)] * 2 + [jnp.cos(ang_c)] * 2, axis=-1)
    sin = jnp.concatenate([-jnp.sin(ang_r), jnp.sin(ang_r), -jnp.sin(ang_c), jnp.sin(ang_c)], axis=-1)
    return jnp.tile(cos, (1, GLA_HEADS)), jnp.tile(sin, (1, GLA_HEADS))


def _merge_mlp_kernel(layer, x_ref, mod_ref, ona_ref, ogla_ref, gate_ref, gout_ref, gmlp_ref,
                      wo_hbm, wup_hbm, wdown_hbm, o_ref,
                      wo_scr, wup_scr, wdown_scr, stage_cols, stage_rows, sem):
    first_step = pl.program_id(0) == 0
    n_chunks = D_FF // FF_CHUNK
    wo_halves = D_MODEL // FF_CHUNK

    def col_copy(j):
        slot = j % 2
        if j < wo_halves:
            src = wo_hbm.at[layer, :, j * FF_CHUNK:(j + 1) * FF_CHUNK]
        else:
            c = j - wo_halves
            src = wup_hbm.at[layer, :, c * FF_CHUNK:(c + 1) * FF_CHUNK]
        return pltpu.make_async_copy(src, stage_cols.at[slot], sem.at[slot])

    def row_copy(c):
        slot = c % 2
        src = wdown_hbm.at[layer, c * FF_CHUNK:(c + 1) * FF_CHUNK, :]
        return pltpu.make_async_copy(src, stage_rows.at[slot], sem.at[2 + slot])

    n_col_blocks = wo_halves + n_chunks

    def land_col(j):
        col_copy(j).wait()
        block = stage_cols[j % 2].astype(BF16)
        if j < wo_halves:
            wo_scr[:, j * FF_CHUNK:(j + 1) * FF_CHUNK] = block
        else:
            c = j - wo_halves
            wup_scr[:, c * FF_CHUNK:(c + 1) * FF_CHUNK] = block
        if j + 2 < n_col_blocks:
            col_copy(j + 2).start()

    def land_row(c):
        row_copy(c).wait()
        wdown_scr[c * FF_CHUNK:(c + 1) * FF_CHUNK, :] = stage_rows[c % 2].astype(BF16)
        if c + 2 < n_chunks:
            row_copy(c + 2).start()

    def run(streaming):
        if streaming:
            col_copy(0).start()
            col_copy(1).start()
            row_copy(0).start()
            row_copy(1).start()

        mod = mod_ref[0]
        ga1 = mod[:, 2 * D_MODEL:3 * D_MODEL]
        sh2 = mod[:, 3 * D_MODEL:4 * D_MODEL]
        sc2 = mod[:, 4 * D_MODEL:5 * D_MODEL]
        ga2 = mod[:, 5 * D_MODEL:6 * D_MODEL]

        og = ogla_ref[...]
        normed = []
        for h in range(GLA_HEADS):
            oh = og[:, h * GLA_DV:(h + 1) * GLA_DV]
            normed.append(oh * lax.rsqrt(jnp.mean(oh * oh, axis=-1, keepdims=True) + EPS))
        gate = gate_ref[...]
        g = jnp.concatenate(normed, axis=1) * gout_ref[...] * (gate * (1.0 / (1.0 + jnp.exp(-gate))))

        if streaming:
            for j in range(wo_halves):
                land_col(j)
        att = _dot(ona_ref[...], wo_scr[0:NA_WIDTH, :]) + _dot(g.astype(BF16), wo_scr[NA_WIDTH:, :])
        x = x_ref[...] + ga1 * att
        ms = jnp.mean(x * x, axis=-1, keepdims=True)
        h2 = ((x * lax.rsqrt(ms + EPS) * gmlp_ref[...]) * (1.0 + sc2) + sh2).astype(BF16)

        width = FF_CHUNK if streaming else MLP_CHUNK
        acc = None
        for c in range(D_FF // width):
            if streaming:
                land_col(wo_halves + c)
                land_row(c)
            u = jnp.maximum(_dot(h2, wup_scr[:, c * width:(c + 1) * width]), 0.0)
            down = _dot((u * u).astype(BF16), wdown_scr[c * width:(c + 1) * width, :])
            acc = down if acc is None else acc + down
        o_ref[...] = x + ga2 * acc

    pl.when(first_step)(functools.partial(run, True))
    pl.when(jnp.logical_not(first_step))(functools.partial(run, False))


def _merge_mlp_call(x, mod_l, o_na, o_gla, gate, gout_t, g_mlp_l, w_o, w_up, w_down, layer):
    tm = TOKEN_TILE
    n_tok = x.shape[0]
    is_ctx = n_tok == N_CTX
    row = lambda i: (i, 0)
    const = lambda i: (0, 0)
    in_hbm = pl.BlockSpec(memory_space=pl.ANY)
    return pl.pallas_call(
        functools.partial(_merge_mlp_kernel, layer),
        grid=(n_tok // tm,),
        in_specs=[
            pl.BlockSpec((tm, D_MODEL), row),
            pl.BlockSpec((1, 1, N_MOD * D_MODEL), _mod_index_map(is_ctx)),
            pl.BlockSpec((tm, NA_WIDTH), row),
            pl.BlockSpec((tm, GLA_V_WIDTH), row),
            pl.BlockSpec((tm, GLA_V_WIDTH), row),
            pl.BlockSpec((1, GLA_V_WIDTH), const),
            pl.BlockSpec((1, D_MODEL), const),
            in_hbm, in_hbm, in_hbm,
        ],
        out_specs=pl.BlockSpec((tm, D_MODEL), row),
        out_shape=jax.ShapeDtypeStruct((n_tok, D_MODEL), F32),
        scratch_shapes=[
            pltpu.VMEM((D_MODEL, D_MODEL), BF16),
            pltpu.VMEM((D_MODEL, D_FF), BF16),
            pltpu.VMEM((D_FF, D_MODEL), BF16),
            pltpu.VMEM((2, D_MODEL, FF_CHUNK), F32),
            pltpu.VMEM((2, FF_CHUNK, D_MODEL), F32),
            pltpu.SemaphoreType.DMA((4,)),
        ],
        compiler_params=_params(1),
        name="merge_mlp_ctx" if is_ctx else "merge_mlp_smp",
    )(x, mod_l, o_na, o_gla, gate, gout_t, g_mlp_l, w_o, w_up, w_down)


def kernel(x_prompt, x_sample, cache_k, cache_v, state_fwd, state_bwd, c, c_ctx, w_ada, b_ada, g_attn, w_in,
           g_q, g_k, rpb, w_gf, b_gf, w_gb, b_gb, g_gla_out, w_o, g_mlp, w_up, w_down):
    x_c = x_prompt.reshape(N_CTX, D_MODEL)
    x_s = x_sample.reshape(N_SMP, D_MODEL)

    c_rows = jnp.concatenate([c_ctx[None, :], c, jnp.zeros((MOD_ROWS - 1 - DEC_BATCH, D_MODEL), F32)], axis=0)
    mods = _ada_call(c_rows, w_ada, b_ada).reshape(DEPTH, MOD_ROWS, 1, N_MOD * D_MODEL)

    w_in_t = jnp.swapaxes(w_in, 1, 2)
    wgate = jnp.zeros((DEPTH, 2 * GLA_GATE_RANK, 2 * GLA_K_WIDTH), F32)
    wgate = wgate.at[:, :GLA_GATE_RANK, :GLA_K_WIDTH].set(w_gf)
    wgate = wgate.at[:, GLA_GATE_RANK:, GLA_K_WIDTH:].set(w_gb).astype(BF16)
    bgate = jnp.concatenate([b_gf, b_gb], axis=-1).reshape(DEPTH, 1, 2 * GLA_K_WIDTH)
    head_of = np.arange(NA_WIDTH) // NA_HEAD_DIM
    seg = jnp.asarray((head_of[:, None] == head_of[None, :]).astype(np.float32) / NA_HEAD_DIM, BF16)
    gq_t = jnp.tile(g_q, (1, NA_HEADS)).reshape(DEPTH, 1, NA_WIDTH)
    gk_t = jnp.tile(g_k, (1, NA_HEADS)).reshape(DEPTH, 1, NA_WIDTH)
    gout_t = jnp.tile(g_gla_out, (1, GLA_HEADS)).reshape(DEPTH, 1, GLA_V_WIDTH)
    n_dr, n_dc = rpb.shape[2:]
    rpb_rows = jnp.pad(rpb, ((0, 0), (0, 0), (0, -n_dr % SUBLANES), (0, 2 * GRID_W - n_dc)))
    gla_w_np, gla_masks_np, gla_smask_np, gla_scales_np = _gla_constants()
    gla_consts = (jnp.asarray(gla_w_np, BF16), jnp.asarray(gla_masks_np, F32), jnp.asarray(gla_smask_np, F32),
                  jnp.asarray(gla_scales_np, F32))
    rope = _rope_tables()

    cache_kt = jnp.swapaxes(cache_k, 3, 4)
    cache_vt = jnp.swapaxes(cache_v, 3, 4)
    new_kv = None
    new_states = None

    for l in range(DEPTH):
        proj_w = (g_attn[l].reshape(1, D_MODEL), w_in_t, seg, gq_t[l], gk_t[l], wgate[l], bgate[l], rope)
        mlp_w = (gout_t[l], g_mlp[l].reshape(1, D_MODEL), w_o, w_up, w_down)

        o_na, qkf, gv, gate, *new_kv = _proj_call(x_c, mods[l], *proj_w, True, new_kv=new_kv, layer=l)
        o_gla, *new_states = _gla_call(qkf, gv, gla_consts, new_states=new_states, layer=l)
        x_c = _merge_mlp_call(x_c, mods[l], o_na, o_gla, gate, *mlp_w, layer=l)

        q, k, v, qkf, gv, gate = _proj_call(x_s, mods[l], *proj_w, False, layer=l)
        o_na = _na_call(q, k, v, cache_kt, cache_vt, rpb_rows, l)
        (o_gla,) = _gla_call(qkf, gv, gla_consts, state_in=(state_fwd, state_bwd), layer=l)
        x_s = _merge_mlp_call(x_s, mods[l], o_na, o_gla, gate, *mlp_w, layer=l)

    return (x_c.reshape(BATCH, SEQ, D_MODEL), x_s.reshape(DEC_BATCH, DEC_SEQ, D_MODEL),
            jnp.swapaxes(new_kv[0], 3, 4), jnp.swapaxes(new_kv[1], 3, 4), *new_states)
```

```python
import functools

import numpy as np
import jax
import jax.numpy as jnp
from jax import lax
from jax.experimental import pallas as pl
from jax.experimental.pallas import tpu as pltpu

F32 = jnp.float32
BF16 = jnp.bfloat16

D_MODEL = 1024
BATCH = 16
SEQ = 256
DEPTH = 4
DEC_BATCH = 2
DEC_SEQ = 1024
PAST_LEN = 512
GRID_W = 64
GRID_ROWS = DEC_SEQ // GRID_W
NA_WIDTH = D_MODEL // 2
NA_HEAD_DIM = 64
NA_HEADS = NA_WIDTH // NA_HEAD_DIM
NA_WIN_ROWS = 8
NA_WIN_COLS = 16
GLA_V_WIDTH = D_MODEL - NA_WIDTH
GLA_HEADS = 4
GLA_DV = GLA_V_WIDTH // GLA_HEADS
GLA_DK = GLA_DV // 2
GLA_K_WIDTH = GLA_HEADS * GLA_DK
GLA_GATE_RANK = 16
GLA_GATE_TAU = 16.0
GLA_CHUNK = 64
D_FF = 4 * D_MODEL
ROPE_BASE = 10000.0
N_MOD = 6
EPS = 1e-6

IN_WIDTH = 3 * NA_WIDTH + 2 * GLA_K_WIDTH + 2 * GLA_V_WIDTH + 2 * GLA_GATE_RANK
PROJ_WIDE = IN_WIDTH - 2 * GLA_GATE_RANK

N_CTX = BATCH * SEQ
N_SMP = DEC_BATCH * DEC_SEQ
SUBLANES = 8
MXU_WIDTH = 256
MOD_ROWS = SUBLANES
MASK_VALUE = -1e30

TOKEN_TILE = 512
PROJ_BLOCK = 512
FF_CHUNK = 512
MLP_CHUNK = 1024
ADA_TILE_N = 1536
GLA_LEVELS = 6
NA_ROW_GROUP = 4
GLA_CTX_SEQS_PER_STEP = 4
VMEM_LIMIT = 52 * 1024 * 1024


def _dot(a, b):
    return jnp.dot(a, b, preferred_element_type=F32)


def _dot_nt(a, b):
    return lax.dot_general(a, b, (((1,), (1,)), ((), ())), preferred_element_type=F32)


def _dot_tn(a, b):
    return lax.dot_general(a, b, (((0,), (0,)), ((), ())), preferred_element_type=F32)


def _params(n_grid_dims=1):
    return pltpu.CompilerParams(dimension_semantics=("arbitrary",) * n_grid_dims,
                                vmem_limit_bytes=VMEM_LIMIT)


def _mod_index_map(is_ctx):
    if is_ctx:
        return lambda i: (0, 0, 0)
    return lambda i: (1 + i // (DEC_SEQ // TOKEN_TILE), 0, 0)


def _ada_kernel(c_ref, w_ref, b_ref, o_ref):
    cv = c_ref[...]
    s = cv * (1.0 / (1.0 + jnp.exp(-cv)))
    o_ref[0] = _dot(s.astype(BF16), w_ref[0].astype(BF16)) + b_ref[0]


def _ada_call(c_rows, w_ada, b_ada):
    n_mod = N_MOD * D_MODEL
    return pl.pallas_call(
        _ada_kernel,
        grid=(DEPTH, n_mod // ADA_TILE_N),
        in_specs=[
            pl.BlockSpec((MOD_ROWS, D_MODEL), lambda l, j: (0, 0)),
            pl.BlockSpec((1, D_MODEL, ADA_TILE_N), lambda l, j: (l, 0, j)),
            pl.BlockSpec((1, 1, ADA_TILE_N), lambda l, j: (l, 0, j)),
        ],
        out_specs=pl.BlockSpec((1, MOD_ROWS, ADA_TILE_N), lambda l, j: (l, 0, j)),
        out_shape=jax.ShapeDtypeStruct((DEPTH, MOD_ROWS, n_mod), F32),
        compiler_params=_params(2),
        name="ada_mod",
    )(c_rows, w_ada, b_ada.reshape(DEPTH, 1, n_mod))


def _ctx_attention(q, k, v):
    n_seq = q.shape[0] // SEQ
    lane = lax.broadcasted_iota(jnp.int32, (1, 2 * NA_HEAD_DIM), 1)
    first = lane < NA_HEAD_DIM
    pairs = [(slice(s * SEQ, (s + 1) * SEQ), slice(hp * 2 * NA_HEAD_DIM, (hp + 1) * 2 * NA_HEAD_DIM))
             for s in range(n_seq) for hp in range(NA_HEADS // 2)]
    scores = []
    for rows, cols in pairs:
        q2 = q[rows, cols]
        k2 = k[rows, cols]
        scores.append(_dot_nt(jnp.where(first, q2, jnp.zeros_like(q2)), k2))
        scores.append(_dot_nt(jnp.where(first, jnp.zeros_like(q2), q2), k2))
    probs = []
    for s in scores:
        p = jnp.exp(s - jnp.max(s, axis=-1, keepdims=True))
        probs.append((p.astype(BF16), jnp.sum(p, axis=-1, keepdims=True)))
    outs = []
    for i, (rows, cols) in enumerate(pairs):
        v2 = v[rows, cols]
        (p0, l0), (p1, l1) = probs[2 * i], probs[2 * i + 1]
        outs.append(jnp.where(first, _dot(p0, v2) / l0, _dot(p1, v2) / l1))
    per_seq = NA_HEADS // 2
    return jnp.concatenate([jnp.concatenate(outs[s * per_seq:(s + 1) * per_seq], axis=1) for s in range(n_seq)],
                           axis=0)


def _log_sigmoid(x):
    return jnp.minimum(x, 0.0) - jnp.log(1.0 + jnp.exp(-jnp.abs(x)))


def _proj_kernel(is_ctx, creates_kv, layer, x_ref, mod_ref, g_ref, seg_ref, gq_ref, gk_ref, wgate_ref, bgate_ref,
                 cos_ref, sin_ref, win_hbm, *rest):
    if is_ctx:
        if not creates_kv:
            rest = rest[2:]
        ona_out = rest[0]
        rest = rest[1:]
    else:
        q_out, k_out, v_out = rest[:3]
        rest = rest[3:]
    qkf_out, gv_out, gate_out = rest[:3]
    rest = rest[3:]
    if is_ctx:
        knew_out, vnew_out = rest[:2]
        rest = rest[2:]
    w_scr, stage, tail_stage, sem = rest

    n_blocks = PROJ_WIDE // PROJ_BLOCK
    tail = slice(PROJ_WIDE, IN_WIDTH)
    first_step = pl.program_id(0) == 0

    def block_copy(j):
        src = win_hbm.at[layer, j * PROJ_BLOCK:(j + 1) * PROJ_BLOCK, :]
        return pltpu.make_async_copy(src, stage.at[j % 2], sem.at[j % 2])

    def tail_copy():
        return pltpu.make_async_copy(win_hbm.at[layer, tail, :], tail_stage, sem.at[2])

    def land(j):
        block_copy(j).wait()
        w_scr[j * PROJ_BLOCK:(j + 1) * PROJ_BLOCK, :] = stage[j % 2].astype(BF16)
        if j + 2 < n_blocks:
            block_copy(j + 2).start()

    def run(streaming):
        if streaming:
            block_copy(0).start()
            block_copy(1).start()
            tail_copy().start()

        x = x_ref[...]
        mod = mod_ref[0]
        sh1 = mod[:, 0:D_MODEL]
        sc1 = mod[:, D_MODEL:2 * D_MODEL]
        ms = jnp.mean(x * x, axis=-1, keepdims=True)
        h = (x * lax.rsqrt(ms + EPS) * g_ref[...]) * (1.0 + sc1) + sh1
        hb = h.astype(BF16)

        def projected(j):
            if streaming:
                land(j)
            return _dot_nt(hb, w_scr[j * PROJ_BLOCK:(j + 1) * PROJ_BLOCK, :])

        seg = seg_ref[...]

        def head_mean_square(t):
            sq = (t * t).astype(BF16)
            return jnp.concatenate([_dot(sq[:, c:c + MXU_WIDTH], seg) for c in range(0, NA_WIDTH, MXU_WIDTH)],
                                   axis=1)

        q = projected(0)
        q_ms = head_mean_square(q)
        qb = (q * lax.rsqrt(q_ms + EPS) * gq_ref[...] * (NA_HEAD_DIM ** -0.5)).astype(BF16)
        k = projected(1)
        k_ms = head_mean_square(k)
        kn = k * lax.rsqrt(k_ms + EPS) * gk_ref[...]
        v = projected(2)
        if not is_ctx:
            q_out[...] = qb
            k_out[...] = kn.astype(BF16)
            v_out[...] = v.astype(BF16)
        else:
            ona_out[...] = _ctx_attention(qb, kn.astype(BF16), v.astype(BF16)).astype(BF16)
            slot = layer if creates_kv else 0
            for s in range(TOKEN_TILE // SEQ):
                rows = slice(s * SEQ, (s + 1) * SEQ)
                for hp in range(NA_HEADS // 2):
                    cols = slice(hp * 2 * NA_HEAD_DIM, (hp + 1) * 2 * NA_HEAD_DIM)
                    for out, t in ((knew_out, kn[rows, cols].T), (vnew_out, v[rows, cols].T)):
                        out[s, slot, 2 * hp] = t[:NA_HEAD_DIM]
                        out[s, slot, 2 * hp + 1] = t[NA_HEAD_DIM:]
            if creates_kv:
                for out in (knew_out, vnew_out):
                    for other in range(DEPTH):
                        if other != layer:
                            out[:, other] = jnp.zeros((TOKEN_TILE // SEQ, NA_HEADS, NA_HEAD_DIM, SEQ), F32)

        z = projected(3)
        gla_q = z[:, 0:GLA_K_WIDTH] * (GLA_DK ** -0.5)
        gla_k = z[:, GLA_K_WIDTH:]
        if not is_ctx:
            lane = lax.broadcasted_iota(jnp.int32, (1, GLA_K_WIDTH), 1)
            quarter = GLA_DK // 4
            low = (lane % (2 * quarter)) < quarter
            cos = cos_ref[...]
            sin = sin_ref[...]

            def rope(t):
                partner = jnp.where(low, pltpu.roll(t, GLA_K_WIDTH - quarter, axis=1),
                                    pltpu.roll(t, quarter, axis=1))
                return t * cos + partner * sin

            gla_q = rope(gla_q)
            gla_k = rope(gla_k)
        qkf_out[:, 0:GLA_K_WIDTH] = gla_q
        qkf_out[:, GLA_K_WIDTH:2 * GLA_K_WIDTH] = gla_k
        gv_out[...] = projected(4).astype(BF16)
        gate_out[...] = projected(5)

        if streaming:
            tail_copy().wait()
            w_scr[tail, :] = tail_stage[...].astype(BF16)
        zz_t = _dot_nt(w_scr[tail, :], hb)
        pre = _dot_tn(zz_t.astype(BF16), wgate_ref[...]) + bgate_ref[...]
        ls = _log_sigmoid(pre) * (1.0 / GLA_GATE_TAU)
        qkf_out[:, 2 * GLA_K_WIDTH:] = ls

    pl.when(first_step)(functools.partial(run, True))
    pl.when(jnp.logical_not(first_step))(functools.partial(run, False))


def _proj_call(x, mod_l, g_attn_l, w_in_t, seg, gq_t, gk_t, wgate, bgate, rope, is_ctx, new_kv=None, layer=0):
    tm = TOKEN_TILE
    n_tok = x.shape[0]
    creates_kv = is_ctx and new_kv is None
    row = lambda i: (i, 0)
    const = lambda i: (0, 0)
    rope_rows = const if is_ctx else (lambda i: (i % (DEC_SEQ // tm), 0))
    if is_ctx:
        out_shapes = [jax.ShapeDtypeStruct((n_tok, NA_WIDTH), BF16)]
    else:
        out_shapes = [jax.ShapeDtypeStruct((n_tok, NA_WIDTH), BF16)] * 3
    out_shapes += [
        jax.ShapeDtypeStruct((n_tok, 4 * GLA_K_WIDTH), F32),
        jax.ShapeDtypeStruct((n_tok, GLA_V_WIDTH), BF16),
        jax.ShapeDtypeStruct((n_tok, GLA_V_WIDTH), F32),
    ]
    out_specs = [pl.BlockSpec((tm, s.shape[1]), row) for s in out_shapes]
    in_specs = [
        pl.BlockSpec((tm, D_MODEL), row),
        pl.BlockSpec((1, 1, N_MOD * D_MODEL), _mod_index_map(is_ctx)),
        pl.BlockSpec((1, D_MODEL), const),
        pl.BlockSpec((MXU_WIDTH, MXU_WIDTH), const),
        pl.BlockSpec((1, NA_WIDTH), const),
        pl.BlockSpec((1, NA_WIDTH), const),
        pl.BlockSpec(wgate.shape, const),
        pl.BlockSpec((1, 2 * GLA_K_WIDTH), const),
        pl.BlockSpec((tm, GLA_K_WIDTH), rope_rows),
        pl.BlockSpec((tm, GLA_K_WIDTH), rope_rows),
        pl.BlockSpec(memory_space=pl.ANY),
    ]
    args = [x, mod_l, g_attn_l, seg, gq_t, gk_t, wgate, bgate, *rope, w_in_t]
    aliases = {}
    if is_ctx:
        kv_shape = (BATCH, DEPTH, NA_HEADS, NA_HEAD_DIM, SEQ)
        if creates_kv:
            kv_spec = pl.BlockSpec((tm // SEQ,) + kv_shape[1:], lambda i: (i, 0, 0, 0, 0))
        else:
            kv_spec = pl.BlockSpec((tm // SEQ, 1) + kv_shape[2:], lambda i: (i, layer, 0, 0, 0))
            aliases = {len(args): len(out_shapes), len(args) + 1: len(out_shapes) + 1}
            in_specs += [pl.BlockSpec(memory_space=pl.ANY)] * 2
            args += list(new_kv)
        out_shapes += [jax.ShapeDtypeStruct(kv_shape, F32)] * 2
        out_specs += [kv_spec, kv_spec]
    return pl.pallas_call(
        functools.partial(_proj_kernel, is_ctx, creates_kv, layer),
        grid=(n_tok // tm,),
        in_specs=in_specs,
        out_specs=out_specs,
        out_shape=out_shapes,
        input_output_aliases=aliases,
        scratch_shapes=[
            pltpu.VMEM((IN_WIDTH, D_MODEL), BF16),
            pltpu.VMEM((2, PROJ_BLOCK, D_MODEL), F32),
            pltpu.VMEM((IN_WIDTH - PROJ_WIDE, D_MODEL), F32),
            pltpu.SemaphoreType.DMA((3,)),
        ],
        compiler_params=_params(1),
        name="proj_ctx" if is_ctx else "proj_smp",
    )(*args)


def _na_window_start(r):
    return min(max(r - NA_WIN_ROWS // 2, 0), GRID_ROWS - NA_WIN_ROWS)


def _na_row_groups():
    groups = []
    for g in range(GRID_ROWS // NA_ROW_GROUP):
        starts = [_na_window_start(r) for r in range(g * NA_ROW_GROUP, (g + 1) * NA_ROW_GROUP)]
        lo = min(starts) // 2 * 2
        count = max(starts) + NA_WIN_ROWS - lo
        groups.append((lo, count + count % 2))
    return groups


def _na_bias_tiles(rpb_ref, hh, tiles_scr):
    n_dr = 2 * NA_WIN_ROWS - 1
    lanes = 2 * GRID_W
    lane = lax.broadcasted_iota(jnp.int32, (GRID_W, lanes), 1)
    col = lax.broadcasted_iota(jnp.int32, (GRID_W, lanes), 0)
    first = lane < GRID_W
    c2 = lane % GRID_W
    cs = jnp.clip(col - NA_WIN_COLS // 2, 0, GRID_W - NA_WIN_COLS)
    valid = (c2 >= cs) & (c2 < cs + NA_WIN_COLS)
    rows = rpb_ref[0, hh]
    base = lanes - (NA_WIN_COLS - 1)

    def skew(dr, shift):
        row = jnp.broadcast_to(rows[dr:dr + 1], (GRID_W, lanes))
        return pltpu.roll(row, shift, axis=1, stride=1, stride_axis=0)

    for e in range(n_dr - 1):
        tile = jnp.where(first, skew(e, base), skew(e + 1, (base + GRID_W) % lanes))
        tiles_scr[e] = jnp.where(valid, tile, MASK_VALUE)


def _na_group_bias(tiles_scr, g, key_lo, key_count, first):
    masked = jnp.full((GRID_W, 2 * GRID_W), MASK_VALUE, F32)
    row_tiles = []
    for r in range(g * NA_ROW_GROUP, (g + 1) * NA_ROW_GROUP):
        rs = _na_window_start(r)
        pieces = []
        for u in range(key_count // 2):
            r0 = key_lo + 2 * u
            in0 = rs <= r0 < rs + NA_WIN_ROWS
            in1 = rs <= r0 + 1 < rs + NA_WIN_ROWS
            if not (in0 or in1):
                pieces.append(masked)
                continue
            tile = tiles_scr[r0 - r + NA_WIN_ROWS - 1]
            if not in1:
                tile = jnp.where(first, tile, MASK_VALUE)
            elif not in0:
                tile = jnp.where(first, MASK_VALUE, tile)
            pieces.append(tile)
        row_tiles.append(jnp.concatenate(pieces, axis=1))
    return jnp.concatenate(row_tiles, axis=0)


def _na_kernel(q_ref, k_ref, v_ref, kc_ref, vc_ref, rpb_ref, o_ref, sc_ref, tiles_scr):
    lane = lax.broadcasted_iota(jnp.int32, (1, 2 * NA_HEAD_DIM), 1)
    first = lane < NA_HEAD_DIM
    q2 = q_ref[...]
    kc2 = jnp.concatenate([kc_ref[0, 0, 0], kc_ref[0, 0, 1]], axis=0).astype(BF16)
    vc2 = jnp.concatenate([vc_ref[0, 0, 0], vc_ref[0, 0, 1]], axis=0).astype(BF16)
    block = NA_ROW_GROUP * GRID_W
    groups = _na_row_groups()
    qms = [jnp.where(first, q2, jnp.zeros_like(q2)), jnp.where(first, jnp.zeros_like(q2), q2)]
    for hh in range(2):
        sc_ref[hh] = _dot(qms[hh], kc2)
        _na_bias_tiles(rpb_ref, hh, tiles_scr.at[hh])
    problems = [(hh, g) for hh in range(2) for g in range(len(groups))]
    scores = []
    for hh, g in problems:
        key_lo, key_count = groups[g]
        keys = slice(key_lo * GRID_W, (key_lo + key_count) * GRID_W)
        s = _dot_nt(qms[hh][g * block:(g + 1) * block], k_ref[keys, :])
        scores.append(s + _na_group_bias(tiles_scr.at[hh], g, key_lo, key_count, first))
    probs = []
    for (hh, g), s in zip(problems, scores):
        sc = sc_ref[hh, g * block:(g + 1) * block, :]
        m = jnp.maximum(jnp.max(s, axis=-1, keepdims=True), jnp.max(sc, axis=-1, keepdims=True))
        p = jnp.exp(s - m)
        pc = jnp.exp(sc - m)
        l = jnp.sum(p, axis=-1, keepdims=True) + jnp.sum(pc, axis=-1, keepdims=True)
        probs.append((p.astype(BF16), pc.astype(BF16), l))
    outs = [[], []]
    for (hh, g), (p, pc, l) in zip(problems, probs):
        key_lo, key_count = groups[g]
        keys = slice(key_lo * GRID_W, (key_lo + key_count) * GRID_W)
        outs[hh].append((_dot(p, v_ref[keys, :]) + _dot_nt(pc, vc2)) / l)
    outs = [jnp.concatenate(o, axis=0) for o in outs]
    o_ref[...] = jnp.where(first, outs[0], outs[1]).astype(BF16)


def _na_call(q, k, v, cache_kt, cache_vt, rpb_rows, layer):
    tok = pl.BlockSpec((DEC_SEQ, 2 * NA_HEAD_DIM), lambda b, hp: (b, hp))
    cache = pl.BlockSpec((1, 1, 2, NA_HEAD_DIM, PAST_LEN), lambda b, hp: (b, layer, hp, 0, 0))
    n_pairs = 2 * NA_WIN_ROWS - 2
    return pl.pallas_call(
        _na_kernel,
        grid=(DEC_BATCH, NA_HEADS // 2),
        in_specs=[
            tok, tok, tok, cache, cache,
            pl.BlockSpec((1, 2) + rpb_rows.shape[2:], lambda b, hp: (layer, hp, 0, 0)),
        ],
        out_specs=tok,
        out_shape=jax.ShapeDtypeStruct((N_SMP, NA_WIDTH), BF16),
        scratch_shapes=[pltpu.VMEM((2, DEC_SEQ, PAST_LEN), F32),
                        pltpu.VMEM((2, n_pairs, GRID_W, 2 * GRID_W), F32)],
        compiler_params=_params(2),
        name="nbr_attn",
    )(q, k, v, cache_kt, cache_vt, rpb_rows)


def _gla_constants():
    C = GLA_CHUNK
    ii, jj = np.meshgrid(np.arange(C), np.arange(C), indexing="ij")
    tri = np.stack([jj <= ii, jj >= ii]).astype(np.float32)
    w = np.concatenate([tri, tri], axis=-1)

    x = ii ^ jj
    hb = np.where(x > 0, np.floor(np.log2(np.maximum(x, 1))), -1).astype(np.int64)
    masks = np.zeros((2, GLA_LEVELS + 1, C, C), np.float32)
    for p in range(GLA_LEVELS):
        masks[0, p] = (jj < ii) & (hb == p)
        masks[1, p] = (jj > ii) & (hb == p)
    masks[:, GLA_LEVELS] = np.eye(C)
    masks = np.tile(masks, (1, 1, 2, 2))
    state_mask = (np.arange(2 * GLA_DV)[:, None] // GLA_DV == np.arange(2 * GLA_DK)[None, :] // GLA_DK)
    upper = np.stack([(np.arange(C) >> p) & 1 for p in range(GLA_LEVELS)]).astype(bool)
    sign = np.stack([np.where(upper, 1.0, -1.0), np.where(upper, -1.0, 1.0)])
    scales = np.broadcast_to((sign * np.log2(np.e))[..., None], (2, GLA_LEVELS, C, GLA_K_WIDTH))
    return w, masks, state_mask.astype(np.float32), np.ascontiguousarray(scales, np.float32)


def _gla_kernel(seq_len, n_seq, is_ctx, creates_states, layer, *refs):
    refs = list(refs)
    gq_ref, gk_ref, gv_ref, lf_ref, lb_ref, w_ref, mask_ref, smask_ref, scale_ref = refs[:9]
    refs = refs[9:]
    if is_ctx:
        if not creates_states:
            refs = refs[2:]
        o_ref, sf_ref, sb_ref = refs[:3]
        refs = refs[3:]
    else:
        s0f_ref, s0b_ref, o_ref = refs[:3]
        refs = refs[3:]
    st_scr, cum_scr = refs

    C = GLA_CHUNK
    n_chunks = seq_len // C
    n_pairs = GLA_HEADS // 2
    pair_k = 2 * GLA_DK
    pair_v = 2 * GLA_DV

    st_scr[...] = jnp.zeros(st_scr.shape, F32)
    if not is_ctx:
        for d, s0_ref in enumerate((s0f_ref, s0b_ref)):
            for s in range(n_seq):
                for h in range(GLA_HEADS):
                    pr, e = divmod(h, 2)
                    st_scr[s, d, pr, e * GLA_DV:(e + 1) * GLA_DV, e * GLA_DK:(e + 1) * GLA_DK] = s0_ref[s, 0, h].T

    klane = lax.broadcasted_iota(jnp.int32, (1, GLA_K_WIDTH), 1)
    even_head = (klane // GLA_DK) % 2 == 0

    def chunk_decay(s, d, r0, f_ref):
        rows = pl.ds(s * seq_len + r0, C)
        f = f_ref[rows, :]
        f_hi = f.astype(BF16)
        f_lo = (f - f_hi.astype(F32)).astype(BF16)
        cum = _dot(w_ref[d], jnp.concatenate([f_hi, f_lo], axis=0))
        cum_scr[s, d] = cum
        return rows, cum

    def chunk_scores(s, d, rows, cum):
        def cum_rows(r, n):
            return jnp.broadcast_to(cum_scr[s, d, r:r + 1, :], (n, GLA_K_WIDTH))

        def level_factor(p):
            m = 1 << p
            edge = m - 1 if d == 0 else m
            if 2 * m >= SUBLANES:
                ref = jnp.concatenate([cum_rows(blk + edge, 2 * m) for blk in range(0, C, 2 * m)], axis=0)
            elif 2 * m == SUBLANES // 2:
                sub = lax.broadcasted_iota(jnp.int32, (SUBLANES, GLA_K_WIDTH), 0)
                ref = jnp.concatenate([jnp.where(sub < 2 * m, cum_rows(blk + edge, SUBLANES),
                                                 cum_rows(blk + 2 * m + edge, SUBLANES))
                                       for blk in range(0, C, SUBLANES)], axis=0)
            else:
                odd = lax.broadcasted_iota(jnp.int32, (C, GLA_K_WIDTH), 0) % 2 == 1
                if d == 0:
                    ref = jnp.where(odd, pltpu.roll(cum, 1, axis=0), cum)
                else:
                    ref = jnp.where(odd, cum, pltpu.roll(cum, C - 1, axis=0))
            return jnp.exp2((cum - ref) * scale_ref[d, p])

        qc = gq_ref[rows, :]
        kc = gk_ref[rows, :]
        vc = gv_ref[rows, :]
        last = C - 1 if d == 0 else 0
        eq = jnp.exp(cum)
        ek = jnp.exp(cum_rows(last, C) - cum)
        total = jnp.exp(cum_scr[s, d, last:last + 1, :])

        qb = qc.astype(BF16)
        kb = kc.astype(BF16)
        kb_even = jnp.where(even_head, kb, jnp.zeros_like(kb))
        kb_odd = jnp.where(even_head, jnp.zeros_like(kb), kb)
        zero = jnp.zeros((C, pair_k), BF16)
        a = None
        for p in range(GLA_LEVELS + 1):
            if p < GLA_LEVELS:
                ep = level_factor(p).astype(BF16)
                qe, ke_even, ke_odd = qb * ep, kb_even * ep, kb_odd * ep
            else:
                qe, ke_even, ke_odd = qb, kb_even, kb_odd
            lhs = jnp.concatenate([jnp.concatenate([qe[:, :pair_k], zero], axis=1),
                                   jnp.concatenate([zero, qe[:, pair_k:]], axis=1)], axis=0)
            sc = _dot_nt(lhs, jnp.concatenate([ke_even, ke_odd], axis=0)) * mask_ref[d, p]
            a = sc if a is None else a + sc
        q_in = (qc * eq).astype(BF16)
        k_out = (kc * ek).astype(BF16)
        return a.astype(BF16), q_in, k_out, vc, total

    def chunk_output(s, d, rows, accumulate, a, q_in, k_out, vc, total):
        zero_v = jnp.zeros((C, GLA_DV), BF16)
        for pr in range(n_pairs):
            kl = slice(pr * pair_k, (pr + 1) * pair_k)
            vp = vc[:, pr * pair_v:(pr + 1) * pair_v]
            v_diag = jnp.concatenate([jnp.concatenate([vp[:, :GLA_DV], zero_v], axis=1),
                                      jnp.concatenate([zero_v, vp[:, GLA_DV:]], axis=1)], axis=0)
            st = st_scr[s, d, pr]
            o = _dot(a[pr * C:(pr + 1) * C], v_diag) + _dot_nt(q_in[:, kl], st.astype(BF16))
            if accumulate:
                o_ref[rows, pr * pair_v:(pr + 1) * pair_v] += o
            else:
                o_ref[rows, pr * pair_v:(pr + 1) * pair_v] = o
            st_scr[s, d, pr] = st * total[:, kl] + _dot_tn(vp, k_out[:, kl]) * smask_ref[...]

    def body(accumulate, n, carry):
        fwd = pl.multiple_of(n * C, C)
        bwd = pl.multiple_of((n_chunks - 1 - n) * C, C)
        scans = [(s, d) for s in range(n_seq) for d in range(2)]
        decays = [chunk_decay(s, d, bwd if d else fwd, lb_ref if d else lf_ref) for s, d in scans]
        scores = [chunk_scores(s, d, *dec) for (s, d), dec in zip(scans, decays)]
        for (s, d), (rows, _), sc in zip(scans, decays, scores):
            chunk_output(s, d, rows, accumulate, *sc)
        return carry

    half = n_chunks // 2
    lax.fori_loop(0, half, functools.partial(body, False), 0)
    lax.fori_loop(half, n_chunks, functools.partial(body, True), 0)
    if is_ctx:
        slot = layer if creates_states else 0
        for d, s_ref in enumerate((sf_ref, sb_ref)):
            for s in range(n_seq):
                for h in range(GLA_HEADS):
                    pr, e = divmod(h, 2)
                    s_ref[s, slot, h] = st_scr[s, d, pr, e * GLA_DV:(e + 1) * GLA_DV, e * GLA_DK:(e + 1) * GLA_DK].T
            if creates_states:
                for other in range(DEPTH):
                    if other != layer:
                        s_ref[:, other] = jnp.zeros((n_seq, GLA_HEADS, GLA_DK, GLA_DV), F32)


def _gla_call(qkf, gv, consts, new_states=None, state_in=None, layer=0):
    is_ctx = state_in is None
    creates_states = is_ctx and new_states is None
    seq_len = SEQ if is_ctx else DEC_SEQ
    n_tok = qkf.shape[0]
    n_seq = GLA_CTX_SEQS_PER_STEP if is_ctx else DEC_BATCH
    rows = n_seq * seq_len
    tok = lambda width: pl.BlockSpec((rows, width), lambda b: (b, 0))
    whole = lambda a: pl.BlockSpec(a.shape, lambda b: (0,) * a.ndim)
    state_spec = pl.BlockSpec((n_seq, 1, GLA_HEADS, GLA_DK, GLA_DV), lambda b: (b, layer, 0, 0, 0))
    qkf_block = lambda j: pl.BlockSpec((rows, GLA_K_WIDTH), lambda b: (b, j))
    in_specs = [qkf_block(0), qkf_block(1), tok(GLA_V_WIDTH), qkf_block(2), qkf_block(3)]
    in_specs += [whole(a) for a in consts]
    args = [qkf, qkf, gv, qkf, qkf, *consts]
    out_shape = [jax.ShapeDtypeStruct((n_tok, GLA_V_WIDTH), F32)]
    out_specs = [tok(GLA_V_WIDTH)]
    aliases = {}
    if is_ctx:
        states_shape = (BATCH, DEPTH, GLA_HEADS, GLA_DK, GLA_DV)
        if creates_states:
            new_spec = pl.BlockSpec((n_seq,) + states_shape[1:], lambda b: (b, 0, 0, 0, 0))
        else:
            new_spec = state_spec
            aliases = {len(args): 1, len(args) + 1: 2}
            in_specs += [pl.BlockSpec(memory_space=pl.ANY)] * 2
            args += list(new_states)
        out_shape += [jax.ShapeDtypeStruct(states_shape, F32)] * 2
        out_specs += [new_spec, new_spec]
    else:
        in_specs += [state_spec, state_spec]
        args += list(state_in)
    return pl.pallas_call(
        functools.partial(_gla_kernel, seq_len, n_seq, is_ctx, creates_states, layer),
        grid=(n_tok // rows,),
        in_specs=in_specs,
        out_specs=out_specs,
        out_shape=out_shape,
        input_output_aliases=aliases,
        scratch_shapes=[
            pltpu.VMEM((n_seq, 2, GLA_HEADS // 2, 2 * GLA_DV, 2 * GLA_DK), F32),
            pltpu.VMEM((n_seq, 2, GLA_CHUNK, GLA_K_WIDTH), F32),
        ],
        compiler_params=_params(1),
        name="gla_ctx" if is_ctx else "gla_smp",
    )(*args)


def _rope_tables():
    quarter = GLA_DK // 4
    inv = ROPE_BASE ** (-jnp.arange(quarter, dtype=F32) / quarter)
    t = jnp.arange(DEC_SEQ)
    ang_r = (t // GRID_W).astype(F32)[:, None] * inv
    ang_c = (t % GRID_W).astype(F32)[:, None] * inv
    cos = jnp.concatenate([jnp.cos(ang_r)] * 2 + [jnp.cos(ang_c)] * 2, axis=-1)
    sin = jnp.concatenate([-jnp.sin(ang_r), jnp.sin(ang_r), -jnp.sin(ang_c), jnp.sin(ang_c)], axis=-1)
    return jnp.tile(cos, (1, GLA_HEADS)), jnp.tile(sin, (1, GLA_HEADS))


def _merge_mlp_kernel(layer, x_ref, mod_ref, ona_ref, ogla_ref, gate_ref, gout_ref, gmlp_ref,
                      wo_hbm, wup_hbm, wdown_hbm, o_ref,
                      wo_scr, wup_scr, wdown_scr, stage_cols, stage_rows, sem):
    first_step = pl.program_id(0) == 0
    n_chunks = D_FF // FF_CHUNK
    wo_halves = D_MODEL // FF_CHUNK

    def col_copy(j):
        slot = j % 2
        if j < wo_halves:
            src = wo_hbm.at[layer, :, j * FF_CHUNK:(j + 1) * FF_CHUNK]
        else:
            c = j - wo_halves
            src = wup_hbm.at[layer, :, c * FF_CHUNK:(c + 1) * FF_CHUNK]
        return pltpu.make_async_copy(src, stage_cols.at[slot], sem.at[slot])

    def row_copy(c):
        slot = c % 2
        src = wdown_hbm.at[layer, c * FF_CHUNK:(c + 1) * FF_CHUNK, :]
        return pltpu.make_async_copy(src, stage_rows.at[slot], sem.at[2 + slot])

    n_col_blocks = wo_halves + n_chunks

    def land_col(j):
        col_copy(j).wait()
        block = stage_cols[j % 2].astype(BF16)
        if j < wo_halves:
            wo_scr[:, j * FF_CHUNK:(j + 1) * FF_CHUNK] = block
        else:
            c = j - wo_halves
            wup_scr[:, c * FF_CHUNK:(c + 1) * FF_CHUNK] = block
        if j + 2 < n_col_blocks:
            col_copy(j + 2).start()

    def land_row(c):
        row_copy(c).wait()
        wdown_scr[c * FF_CHUNK:(c + 1) * FF_CHUNK, :] = stage_rows[c % 2].astype(BF16)
        if c + 2 < n_chunks:
            row_copy(c + 2).start()

    def run(streaming):
        if streaming:
            col_copy(0).start()
            col_copy(1).start()
            row_copy(0).start()
            row_copy(1).start()

        mod = mod_ref[0]
        ga1 = mod[:, 2 * D_MODEL:3 * D_MODEL]
        sh2 = mod[:, 3 * D_MODEL:4 * D_MODEL]
        sc2 = mod[:, 4 * D_MODEL:5 * D_MODEL]
        ga2 = mod[:, 5 * D_MODEL:6 * D_MODEL]

        og = ogla_ref[...]
        normed = []
        for h in range(GLA_HEADS):
            oh = og[:, h * GLA_DV:(h + 1) * GLA_DV]
            normed.append(oh * lax.rsqrt(jnp.mean(oh * oh, axis=-1, keepdims=True) + EPS))
        gate = gate_ref[...]
        g = jnp.concatenate(normed, axis=1) * gout_ref[...] * (gate * (1.0 / (1.0 + jnp.exp(-gate))))

        if streaming:
            for j in range(wo_halves):
                land_col(j)
        att = _dot(ona_ref[...], wo_scr[0:NA_WIDTH, :]) + _dot(g.astype(BF16), wo_scr[NA_WIDTH:, :])
        x = x_ref[...] + ga1 * att
        ms = jnp.mean(x * x, axis=-1, keepdims=True)
        h2 = ((x * lax.rsqrt(ms + EPS) * gmlp_ref[...]) * (1.0 + sc2) + sh2).astype(BF16)

        width = FF_CHUNK if streaming else MLP_CHUNK
        acc = None
        for c in range(D_FF // width):
            if streaming:
                land_col(wo_halves + c)
                land_row(c)
            u = jnp.maximum(_dot(h2, wup_scr[:, c * width:(c + 1) * width]), 0.0)
            down = _dot((u * u).astype(BF16), wdown_scr[c * width:(c + 1) * width, :])
            acc = down if acc is None else acc + down
        o_ref[...] = x + ga2 * acc

    pl.when(first_step)(functools.partial(run, True))
    pl.when(jnp.logical_not(first_step))(functools.partial(run, False))


def _merge_mlp_call(x, mod_l, o_na, o_gla, gate, gout_t, g_mlp_l, w_o, w_up, w_down, layer):
    tm = TOKEN_TILE
    n_tok = x.shape[0]
    is_ctx = n_tok == N_CTX
    row = lambda i: (i, 0)
    const = lambda i: (0, 0)
    in_hbm = pl.BlockSpec(memory_space=pl.ANY)
    return pl.pallas_call(
        functools.partial(_merge_mlp_kernel, layer),
        grid=(n_tok // tm,),
        in_specs=[
            pl.BlockSpec((tm, D_MODEL), row),
            pl.BlockSpec((1, 1, N_MOD * D_MODEL), _mod_index_map(is_ctx)),
            pl.BlockSpec((tm, NA_WIDTH), row),
            pl.BlockSpec((tm, GLA_V_WIDTH), row),
            pl.BlockSpec((tm, GLA_V_WIDTH), row),
            pl.BlockSpec((1, GLA_V_WIDTH), const),
            pl.BlockSpec((1, D_MODEL), const),
            in_hbm, in_hbm, in_hbm,
        ],
        out_specs=pl.BlockSpec((tm, D_MODEL), row),
        out_shape=jax.ShapeDtypeStruct((n_tok, D_MODEL), F32),
        scratch_shapes=[
            pltpu.VMEM((D_MODEL, D_MODEL), BF16),
            pltpu.VMEM((D_MODEL, D_FF), BF16),
            pltpu.VMEM((D_FF, D_MODEL), BF16),
            pltpu.VMEM((2, D_MODEL, FF_CHUNK), F32),
            pltpu.VMEM((2, FF_CHUNK, D_MODEL), F32),
            pltpu.SemaphoreType.DMA((4,)),
        ],
        compiler_params=_params(1),
        name="merge_mlp_ctx" if is_ctx else "merge_mlp_smp",
    )(x, mod_l, o_na, o_gla, gate, gout_t, g_mlp_l, w_o, w_up, w_down)


def kernel(x_prompt, x_sample, cache_k, cache_v, state_fwd, state_bwd, c, c_ctx, w_ada, b_ada, g_attn, w_in,
           g_q, g_k, rpb, w_gf, b_gf, w_gb, b_gb, g_gla_out, w_o, g_mlp, w_up, w_down):
    x_c = x_prompt.reshape(N_CTX, D_MODEL)
    x_s = x_sample.reshape(N_SMP, D_MODEL)

    c_rows = jnp.concatenate([c_ctx[None, :], c, jnp.zeros((MOD_ROWS - 1 - DEC_BATCH, D_MODEL), F32)], axis=0)
    mods = _ada_call(c_rows, w_ada, b_ada).reshape(DEPTH, MOD_ROWS, 1, N_MOD * D_MODEL)

    w_in_t = jnp.swapaxes(w_in, 1, 2)
    wgate = jnp.zeros((DEPTH, 2 * GLA_GATE_RANK, 2 * GLA_K_WIDTH), F32)
    wgate = wgate.at[:, :GLA_GATE_RANK, :GLA_K_WIDTH].set(w_gf)
    wgate = wgate.at[:, GLA_GATE_RANK:, GLA_K_WIDTH:].set(w_gb).astype(BF16)
    bgate = jnp.concatenate([b_gf, b_gb], axis=-1).reshape(DEPTH, 1, 2 * GLA_K_WIDTH)
    head_of = np.arange(MXU_WIDTH) // NA_HEAD_DIM
    seg = jnp.asarray((head_of[:, None] == head_of[None, :]).astype(np.float32) / NA_HEAD_DIM, BF16)
    gq_t = jnp.tile(g_q, (1, NA_HEADS)).reshape(DEPTH, 1, NA_WIDTH)
    gk_t = jnp.tile(g_k, (1, NA_HEADS)).reshape(DEPTH, 1, NA_WIDTH)
    gout_t = jnp.tile(g_gla_out, (1, GLA_HEADS)).reshape(DEPTH, 1, GLA_V_WIDTH)
    n_dr, n_dc = rpb.shape[2:]
    rpb_rows = jnp.pad(rpb, ((0, 0), (0, 0), (0, -n_dr % SUBLANES), (0, 2 * GRID_W - n_dc)))
    gla_w_np, gla_masks_np, gla_smask_np, gla_scales_np = _gla_constants()
    gla_consts = (jnp.asarray(gla_w_np, BF16), jnp.asarray(gla_masks_np, F32), jnp.asarray(gla_smask_np, F32),
                  jnp.asarray(gla_scales_np, F32))
    rope = _rope_tables()

    cache_kt = jnp.swapaxes(cache_k, 3, 4)
    cache_vt = jnp.swapaxes(cache_v, 3, 4)
    new_kv = None
    new_states = None

    for l in range(DEPTH):
        proj_w = (g_attn[l].reshape(1, D_MODEL), w_in_t, seg, gq_t[l], gk_t[l], wgate[l], bgate[l], rope)
        mlp_w = (gout_t[l], g_mlp[l].reshape(1, D_MODEL), w_o, w_up, w_down)

        o_na, qkf, gv, gate, *new_kv = _proj_call(x_c, mods[l], *proj_w, True, new_kv=new_kv, layer=l)
        o_gla, *new_states = _gla_call(qkf, gv, gla_consts, new_states=new_states, layer=l)
        x_c = _merge_mlp_call(x_c, mods[l], o_na, o_gla, gate, *mlp_w, layer=l)

        q, k, v, qkf, gv, gate = _proj_call(x_s, mods[l], *proj_w, False, layer=l)
        o_na = _na_call(q, k, v, cache_kt, cache_vt, rpb_rows, l)
        (o_gla,) = _gla_call(qkf, gv, gla_consts, state_in=(state_fwd, state_bwd), layer=l)
        x_s = _merge_mlp_call(x_s, mods[l], o_na, o_gla, gate, *mlp_w, layer=l)

    return (x_c.reshape(BATCH, SEQ, D_MODEL), x_s.reshape(DEC_BATCH, DEC_SEQ, D_MODEL),
            jnp.swapaxes(new_kv[0], 3, 4), jnp.swapaxes(new_kv[1], 3, 4), *new_states)
```

```python
import functools

import numpy as np
import jax
import jax.numpy as jnp
from jax import lax
from jax.experimental import pallas as pl
from jax.experimental.pallas import tpu as pltpu

F32 = jnp.float32
BF16 = jnp.bfloat16

D_MODEL = 1024
BATCH = 16
SEQ = 256
DEPTH = 4
DEC_BATCH = 2
DEC_SEQ = 1024
PAST_LEN = 512
GRID_W = 64
GRID_ROWS = DEC_SEQ // GRID_W
NA_WIDTH = D_MODEL // 2
NA_HEAD_DIM = 64
NA_HEADS = NA_WIDTH // NA_HEAD_DIM
NA_WIN_ROWS = 8
NA_WIN_COLS = 16
GLA_V_WIDTH = D_MODEL - NA_WIDTH
GLA_HEADS = 4
GLA_DV = GLA_V_WIDTH // GLA_HEADS
GLA_DK = GLA_DV // 2
GLA_K_WIDTH = GLA_HEADS * GLA_DK
GLA_GATE_RANK = 16
GLA_GATE_TAU = 16.0
GLA_CHUNK = 64
D_FF = 4 * D_MODEL
ROPE_BASE = 10000.0
N_MOD = 6
EPS = 1e-6

IN_WIDTH = 3 * NA_WIDTH + 2 * GLA_K_WIDTH + 2 * GLA_V_WIDTH + 2 * GLA_GATE_RANK
PROJ_WIDE = IN_WIDTH - 2 * GLA_GATE_RANK

N_CTX = BATCH * SEQ
N_SMP = DEC_BATCH * DEC_SEQ
SUBLANES = 8
MXU_WIDTH = 256
MOD_ROWS = SUBLANES
MASK_VALUE = -1e30

TOKEN_TILE = 512
PROJ_BLOCK = 512
PROJ_STREAM_BLOCK = 2 * PROJ_BLOCK
FF_CHUNK = 512
MLP_CHUNK = 1024
ADA_TILE_N = 3072
GLA_LEVELS = 6
NA_ROW_GROUP = 4
GLA_CTX_SEQS_PER_STEP = 4
VMEM_LIMIT = 52 * 1024 * 1024


def _dot(a, b):
    return jnp.dot(a, b, preferred_element_type=F32)


def _dot_nt(a, b):
    return lax.dot_general(a, b, (((1,), (1,)), ((), ())), preferred_element_type=F32)


def _dot_tn(a, b):
    return lax.dot_general(a, b, (((0,), (0,)), ((), ())), preferred_element_type=F32)


def _params(n_grid_dims=1):
    return pltpu.CompilerParams(dimension_semantics=("arbitrary",) * n_grid_dims,
                                vmem_limit_bytes=VMEM_LIMIT)


def _mod_index_map(is_ctx):
    if is_ctx:
        return lambda i: (0, 0, 0)
    return lambda i: (1 + i // (DEC_SEQ // TOKEN_TILE), 0, 0)


def _ada_kernel(c_ref, w_ref, b_ref, o_ref):
    cv = c_ref[...]
    s = cv * (1.0 / (1.0 + jnp.exp(-cv)))
    o_ref[0] = _dot(s.astype(BF16), w_ref[0].astype(BF16)) + b_ref[0]


def _ada_call(c_rows, w_ada, b_ada):
    n_mod = N_MOD * D_MODEL
    return pl.pallas_call(
        _ada_kernel,
        grid=(DEPTH, n_mod // ADA_TILE_N),
        in_specs=[
            pl.BlockSpec((MOD_ROWS, D_MODEL), lambda l, j: (0, 0)),
            pl.BlockSpec((1, D_MODEL, ADA_TILE_N), lambda l, j: (l, 0, j)),
            pl.BlockSpec((1, 1, ADA_TILE_N), lambda l, j: (l, 0, j)),
        ],
        out_specs=pl.BlockSpec((1, MOD_ROWS, ADA_TILE_N), lambda l, j: (l, 0, j)),
        out_shape=jax.ShapeDtypeStruct((DEPTH, MOD_ROWS, n_mod), F32),
        compiler_params=_params(2),
        name="ada_mod",
    )(c_rows, w_ada, b_ada.reshape(DEPTH, 1, n_mod))


def _ctx_attention(q, k, v):
    n_seq = q.shape[0] // SEQ
    lane = lax.broadcasted_iota(jnp.int32, (1, 2 * NA_HEAD_DIM), 1)
    first = lane < NA_HEAD_DIM
    pairs = [(slice(s * SEQ, (s + 1) * SEQ), slice(hp * 2 * NA_HEAD_DIM, (hp + 1) * 2 * NA_HEAD_DIM))
             for s in range(n_seq) for hp in range(NA_HEADS // 2)]
    scores = []
    for rows, cols in pairs:
        q2 = q[rows, cols]
        k2 = k[rows, cols]
        scores.append(_dot_nt(jnp.where(first, q2, jnp.zeros_like(q2)), k2))
        scores.append(_dot_nt(jnp.where(first, jnp.zeros_like(q2), q2), k2))
    probs = []
    for s in scores:
        p = jnp.exp(s - jnp.max(s, axis=-1, keepdims=True))
        probs.append((p.astype(BF16), jnp.sum(p, axis=-1, keepdims=True)))
    outs = []
    for i, (rows, cols) in enumerate(pairs):
        v2 = v[rows, cols]
        (p0, l0), (p1, l1) = probs[2 * i], probs[2 * i + 1]
        outs.append(jnp.where(first, _dot(p0, v2) / l0, _dot(p1, v2) / l1))
    per_seq = NA_HEADS // 2
    return jnp.concatenate([jnp.concatenate(outs[s * per_seq:(s + 1) * per_seq], axis=1) for s in range(n_seq)],
                           axis=0)


def _log_sigmoid(x):
    return jnp.minimum(x, 0.0) - jnp.log(1.0 + jnp.exp(-jnp.abs(x)))


def _proj_kernel(is_ctx, creates_kv, layer, x_ref, mod_ref, g_ref, seg_ref, gq_ref, gk_ref, wgate_ref, bgate_ref,
                 cos_ref, sin_ref, win_hbm, *rest):
    if is_ctx:
        if not creates_kv:
            rest = rest[2:]
        ona_out = rest[0]
        rest = rest[1:]
    else:
        q_out, k_out, v_out = rest[:3]
        rest = rest[3:]
    qkf_out, gv_out, gate_out = rest[:3]
    rest = rest[3:]
    if is_ctx:
        knew_out, vnew_out = rest[:2]
        rest = rest[2:]
    w_scr, stage, tail_stage, sem = rest

    n_stream_blocks = PROJ_WIDE // PROJ_STREAM_BLOCK
    per_stream_block = PROJ_STREAM_BLOCK // PROJ_BLOCK
    tail = slice(PROJ_WIDE, IN_WIDTH)
    first_step = pl.program_id(0) == 0

    def block_copy(j):
        src = win_hbm.at[layer, j * PROJ_STREAM_BLOCK:(j + 1) * PROJ_STREAM_BLOCK, :]
        return pltpu.make_async_copy(src, stage.at[j % 2], sem.at[j % 2])

    def tail_copy():
        return pltpu.make_async_copy(win_hbm.at[layer, tail, :], tail_stage, sem.at[2])

    def land(j):
        block_copy(j).wait()
        w_scr[j * PROJ_STREAM_BLOCK:(j + 1) * PROJ_STREAM_BLOCK, :] = stage[j % 2].astype(BF16)
        if j + 2 < n_stream_blocks:
            block_copy(j + 2).start()

    def run(streaming):
        if streaming:
            block_copy(0).start()
            block_copy(1).start()
            tail_copy().start()

        x = x_ref[...]
        mod = mod_ref[0]
        sh1 = mod[:, 0:D_MODEL]
        sc1 = mod[:, D_MODEL:2 * D_MODEL]
        ms = jnp.mean(x * x, axis=-1, keepdims=True)
        h = (x * lax.rsqrt(ms + EPS) * g_ref[...]) * (1.0 + sc1) + sh1
        hb = h.astype(BF16)

        def projected(j):
            if streaming and j % per_stream_block == 0:
                land(j // per_stream_block)
            return _dot_nt(hb, w_scr[j * PROJ_BLOCK:(j + 1) * PROJ_BLOCK, :])

        seg = seg_ref[...]

        def head_mean_square(t):
            sq = (t * t).astype(BF16)
            return jnp.concatenate([_dot(sq[:, c:c + MXU_WIDTH], seg) for c in range(0, NA_WIDTH, MXU_WIDTH)],
                                   axis=1)

        q = projected(0)
        q_ms = head_mean_square(q)
        qb = (q * lax.rsqrt(q_ms + EPS) * gq_ref[...] * (NA_HEAD_DIM ** -0.5)).astype(BF16)
        k = projected(1)
        k_ms = head_mean_square(k)
        kn = k * lax.rsqrt(k_ms + EPS) * gk_ref[...]
        v = projected(2)
        if not is_ctx:
            q_out[...] = qb
            k_out[...] = kn.astype(BF16)
            v_out[...] = v.astype(BF16)
        else:
            ona_out[...] = _ctx_attention(qb, kn.astype(BF16), v.astype(BF16)).astype(BF16)
            slot = layer if creates_kv else 0
            for s in range(TOKEN_TILE // SEQ):
                rows = slice(s * SEQ, (s + 1) * SEQ)
                for hp in range(NA_HEADS // 2):
                    cols = slice(hp * 2 * NA_HEAD_DIM, (hp + 1) * 2 * NA_HEAD_DIM)
                    for out, t in ((knew_out, kn[rows, cols].T), (vnew_out, v[rows, cols].T)):
                        out[s, slot, 2 * hp] = t[:NA_HEAD_DIM]
                        out[s, slot, 2 * hp + 1] = t[NA_HEAD_DIM:]
            if creates_kv:
                for out in (knew_out, vnew_out):
                    for other in range(DEPTH):
                        if other != layer:
                            out[:, other] = jnp.zeros((TOKEN_TILE // SEQ, NA_HEADS, NA_HEAD_DIM, SEQ), F32)

        z = projected(3)
        gla_q = z[:, 0:GLA_K_WIDTH] * (GLA_DK ** -0.5)
        gla_k = z[:, GLA_K_WIDTH:]
        if not is_ctx:
            lane = lax.broadcasted_iota(jnp.int32, (1, GLA_K_WIDTH), 1)
            quarter = GLA_DK // 4
            low = (lane % (2 * quarter)) < quarter
            cos = cos_ref[...]
            sin = sin_ref[...]

            def rope(t):
                partner = jnp.where(low, pltpu.roll(t, GLA_K_WIDTH - quarter, axis=1),
                                    pltpu.roll(t, quarter, axis=1))
                return t * cos + partner * sin

            gla_q = rope(gla_q)
            gla_k = rope(gla_k)
        qkf_out[:, 0:GLA_K_WIDTH] = gla_q
        qkf_out[:, GLA_K_WIDTH:2 * GLA_K_WIDTH] = gla_k

        if streaming:
            tail_copy().wait()
            w_scr[tail, :] = tail_stage[...].astype(BF16)
        zz_t = _dot_nt(w_scr[tail, :], hb)
        pre = _dot_tn(zz_t.astype(BF16), wgate_ref[...]) + bgate_ref[...]
        gla_v = projected(4)
        qkf_out[:, 2 * GLA_K_WIDTH:] = _log_sigmoid(pre) * (1.0 / GLA_GATE_TAU)
        gv_out[...] = gla_v.astype(BF16)
        gate_out[...] = projected(5)

    pl.when(first_step)(functools.partial(run, True))
    pl.when(jnp.logical_not(first_step))(functools.partial(run, False))


def _proj_call(x, mod_l, g_attn_l, w_in_t, seg, gq_t, gk_t, wgate, bgate, rope, is_ctx, new_kv=None, layer=0):
    tm = TOKEN_TILE
    n_tok = x.shape[0]
    creates_kv = is_ctx and new_kv is None
    row = lambda i: (i, 0)
    const = lambda i: (0, 0)
    rope_rows = const if is_ctx else (lambda i: (i % (DEC_SEQ // tm), 0))
    if is_ctx:
        out_shapes = [jax.ShapeDtypeStruct((n_tok, NA_WIDTH), BF16)]
    else:
        out_shapes = [jax.ShapeDtypeStruct((n_tok, NA_WIDTH), BF16)] * 3
    out_shapes += [
        jax.ShapeDtypeStruct((n_tok, 4 * GLA_K_WIDTH), F32),
        jax.ShapeDtypeStruct((n_tok, GLA_V_WIDTH), BF16),
        jax.ShapeDtypeStruct((n_tok, GLA_V_WIDTH), F32),
    ]
    out_specs = [pl.BlockSpec((tm, s.shape[1]), row) for s in out_shapes]
    in_specs = [
        pl.BlockSpec((tm, D_MODEL), row),
        pl.BlockSpec((1, 1, N_MOD * D_MODEL), _mod_index_map(is_ctx)),
        pl.BlockSpec((1, D_MODEL), const),
        pl.BlockSpec((MXU_WIDTH, MXU_WIDTH), const),
        pl.BlockSpec((1, NA_WIDTH), const),
        pl.BlockSpec((1, NA_WIDTH), const),
        pl.BlockSpec(wgate.shape, const),
        pl.BlockSpec((1, 2 * GLA_K_WIDTH), const),
        pl.BlockSpec((tm, GLA_K_WIDTH), rope_rows),
        pl.BlockSpec((tm, GLA_K_WIDTH), rope_rows),
        pl.BlockSpec(memory_space=pl.ANY),
    ]
    args = [x, mod_l, g_attn_l, seg, gq_t, gk_t, wgate, bgate, *rope, w_in_t]
    aliases = {}
    if is_ctx:
        kv_shape = (BATCH, DEPTH, NA_HEADS, NA_HEAD_DIM, SEQ)
        if creates_kv:
            kv_spec = pl.BlockSpec((tm // SEQ,) + kv_shape[1:], lambda i: (i, 0, 0, 0, 0))
        else:
            kv_spec = pl.BlockSpec((tm // SEQ, 1) + kv_shape[2:], lambda i: (i, layer, 0, 0, 0))
            aliases = {len(args): len(out_shapes), len(args) + 1: len(out_shapes) + 1}
            in_specs += [pl.BlockSpec(memory_space=pl.ANY)] * 2
            args += list(new_kv)
        out_shapes += [jax.ShapeDtypeStruct(kv_shape, F32)] * 2
        out_specs += [kv_spec, kv_spec]
    return pl.pallas_call(
        functools.partial(_proj_kernel, is_ctx, creates_kv, layer),
        grid=(n_tok // tm,),
        in_specs=in_specs,
        out_specs=out_specs,
        out_shape=out_shapes,
        input_output_aliases=aliases,
        scratch_shapes=[
            pltpu.VMEM((IN_WIDTH, D_MODEL), BF16),
            pltpu.VMEM((2, PROJ_STREAM_BLOCK, D_MODEL), F32),
            pltpu.VMEM((IN_WIDTH - PROJ_WIDE, D_MODEL), F32),
            pltpu.SemaphoreType.DMA((3,)),
        ],
        compiler_params=_params(1),
        name="proj_ctx" if is_ctx else "proj_smp",
    )(*args)


def _na_window_start(r):
    return min(max(r - NA_WIN_ROWS // 2, 0), GRID_ROWS - NA_WIN_ROWS)


def _na_row_groups():
    groups = []
    for g in range(GRID_ROWS // NA_ROW_GROUP):
        starts = [_na_window_start(r) for r in range(g * NA_ROW_GROUP, (g + 1) * NA_ROW_GROUP)]
        lo = min(starts) // 2 * 2
        count = max(starts) + NA_WIN_ROWS - lo
        groups.append((lo, count + count % 2))
    return groups


def _na_bias_tiles(rpb_ref, hh, tiles_scr):
    n_dr = 2 * NA_WIN_ROWS - 1
    lanes = 2 * GRID_W
    lane = lax.broadcasted_iota(jnp.int32, (GRID_W, lanes), 1)
    col = lax.broadcasted_iota(jnp.int32, (GRID_W, lanes), 0)
    first = lane < GRID_W
    c2 = lane % GRID_W
    cs = jnp.clip(col - NA_WIN_COLS // 2, 0, GRID_W - NA_WIN_COLS)
    valid = (c2 >= cs) & (c2 < cs + NA_WIN_COLS)
    rows = rpb_ref[0, hh]
    base = lanes - (NA_WIN_COLS - 1)

    def skew(dr, shift):
        row = jnp.broadcast_to(rows[dr:dr + 1], (GRID_W, lanes))
        return pltpu.roll(row, shift, axis=1, stride=1, stride_axis=0)

    for e in range(n_dr - 1):
        tile = jnp.where(first, skew(e, base), skew(e + 1, (base + GRID_W) % lanes))
        tiles_scr[e] = jnp.where(valid, tile, MASK_VALUE)


def _na_group_bias(tiles_scr, g, key_lo, key_count, first):
    masked = jnp.full((GRID_W, 2 * GRID_W), MASK_VALUE, F32)
    row_tiles = []
    for r in range(g * NA_ROW_GROUP, (g + 1) * NA_ROW_GROUP):
        rs = _na_window_start(r)
        pieces = []
        for u in range(key_count // 2):
            r0 = key_lo + 2 * u
            in0 = rs <= r0 < rs + NA_WIN_ROWS
            in1 = rs <= r0 + 1 < rs + NA_WIN_ROWS
            if not (in0 or in1):
                pieces.append(masked)
                continue
            tile = tiles_scr[r0 - r + NA_WIN_ROWS - 1]
            if not in1:
                tile = jnp.where(first, tile, MASK_VALUE)
            elif not in0:
                tile = jnp.where(first, MASK_VALUE, tile)
            pieces.append(tile)
        row_tiles.append(jnp.concatenate(pieces, axis=1))
    return jnp.concatenate(row_tiles, axis=0)


def _na_kernel(q_ref, k_ref, v_ref, kc_ref, vc_ref, rpb_ref, o_ref, sc_ref, tiles_scr):
    lane = lax.broadcasted_iota(jnp.int32, (1, 2 * NA_HEAD_DIM), 1)
    first = lane < NA_HEAD_DIM
    q2 = q_ref[...]
    kc2 = jnp.concatenate([kc_ref[0, 0, 0], kc_ref[0, 0, 1]], axis=0).astype(BF16)
    vc2 = jnp.concatenate([vc_ref[0, 0, 0], vc_ref[0, 0, 1]], axis=0).astype(BF16)
    block = NA_ROW_GROUP * GRID_W
    groups = _na_row_groups()
    qms = [jnp.where(first, q2, jnp.zeros_like(q2)), jnp.where(first, jnp.zeros_like(q2), q2)]
    for hh in range(2):
        sc_ref[hh] = _dot(qms[hh], kc2)
        _na_bias_tiles(rpb_ref, hh, tiles_scr.at[hh])
    problems = [(hh, g) for hh in range(2) for g in range(len(groups))]
    scores = []
    for hh, g in problems:
        key_lo, key_count = groups[g]
        keys = slice(key_lo * GRID_W, (key_lo + key_count) * GRID_W)
        s = _dot_nt(qms[hh][g * block:(g + 1) * block], k_ref[keys, :])
        scores.append(s + _na_group_bias(tiles_scr.at[hh], g, key_lo, key_count, first))
    probs = []
    for (hh, g), s in zip(problems, scores):
        sc = sc_ref[hh, g * block:(g + 1) * block, :]
        m = jnp.maximum(jnp.max(s, axis=-1, keepdims=True), jnp.max(sc, axis=-1, keepdims=True))
        p = jnp.exp(s - m)
        pc = jnp.exp(sc - m)
        l = jnp.sum(p, axis=-1, keepdims=True) + jnp.sum(pc, axis=-1, keepdims=True)
        probs.append((p.astype(BF16), pc.astype(BF16), l))
    outs = [[], []]
    for (hh, g), (p, pc, l) in zip(problems, probs):
        key_lo, key_count = groups[g]
        keys = slice(key_lo * GRID_W, (key_lo + key_count) * GRID_W)
        outs[hh].append((_dot(p, v_ref[keys, :]) + _dot_nt(pc, vc2)) / l)
    outs = [jnp.concatenate(o, axis=0) for o in outs]
    o_ref[...] = jnp.where(first, outs[0], outs[1]).astype(BF16)


def _na_call(q, k, v, cache_kt, cache_vt, rpb_rows, layer):
    tok = pl.BlockSpec((DEC_SEQ, 2 * NA_HEAD_DIM), lambda b, hp: (b, hp))
    cache = pl.BlockSpec((1, 1, 2, NA_HEAD_DIM, PAST_LEN), lambda b, hp: (b, layer, hp, 0, 0))
    n_pairs = 2 * NA_WIN_ROWS - 2
    return pl.pallas_call(
        _na_kernel,
        grid=(DEC_BATCH, NA_HEADS // 2),
        in_specs=[
            tok, tok, tok, cache, cache,
            pl.BlockSpec((1, 2) + rpb_rows.shape[2:], lambda b, hp: (layer, hp, 0, 0)),
        ],
        out_specs=tok,
        out_shape=jax.ShapeDtypeStruct((N_SMP, NA_WIDTH), BF16),
        scratch_shapes=[pltpu.VMEM((2, DEC_SEQ, PAST_LEN), F32),
                        pltpu.VMEM((2, n_pairs, GRID_W, 2 * GRID_W), F32)],
        compiler_params=_params(2),
        name="nbr_attn",
    )(q, k, v, cache_kt, cache_vt, rpb_rows)


def _gla_constants():
    C = GLA_CHUNK
    ii, jj = np.meshgrid(np.arange(C), np.arange(C), indexing="ij")
    tri = np.stack([jj <= ii, jj >= ii]).astype(np.float32)
    w = np.concatenate([tri, tri], axis=-1)

    x = ii ^ jj
    hb = np.where(x > 0, np.floor(np.log2(np.maximum(x, 1))), -1).astype(np.int64)
    masks = np.zeros((2, GLA_LEVELS + 1, C, C), np.float32)
    for p in range(GLA_LEVELS):
        masks[0, p] = (jj < ii) & (hb == p)
        masks[1, p] = (jj > ii) & (hb == p)
    masks[:, GLA_LEVELS] = np.eye(C)
    masks = np.tile(masks, (1, 1, 2, 2))
    state_mask = (np.arange(2 * GLA_DV)[:, None] // GLA_DV == np.arange(2 * GLA_DK)[None, :] // GLA_DK)
    upper = np.stack([(np.arange(C) >> p) & 1 for p in range(GLA_LEVELS)]).astype(bool)
    sign = np.stack([np.where(upper, 1.0, -1.0), np.where(upper, -1.0, 1.0)])
    scales = np.broadcast_to((sign * np.log2(np.e))[..., None], (2, GLA_LEVELS, C, GLA_K_WIDTH))
    return w, masks, state_mask.astype(np.float32), np.ascontiguousarray(scales, np.float32)


def _gla_kernel(seq_len, n_seq, is_ctx, creates_states, layer, *refs):
    refs = list(refs)
    gq_ref, gk_ref, gv_ref, lf_ref, lb_ref, w_ref, mask_ref, smask_ref, scale_ref = refs[:9]
    refs = refs[9:]
    if is_ctx:
        if not creates_states:
            refs = refs[2:]
        o_ref, sf_ref, sb_ref = refs[:3]
        refs = refs[3:]
    else:
        s0f_ref, s0b_ref, o_ref = refs[:3]
        refs = refs[3:]
    st_scr, cum_scr = refs

    C = GLA_CHUNK
    n_chunks = seq_len // C
    n_pairs = GLA_HEADS // 2
    pair_k = 2 * GLA_DK
    pair_v = 2 * GLA_DV

    st_scr[...] = jnp.zeros(st_scr.shape, F32)
    if not is_ctx:
        for d, s0_ref in enumerate((s0f_ref, s0b_ref)):
            for s in range(n_seq):
                for h in range(GLA_HEADS):
                    pr, e = divmod(h, 2)
                    st_scr[s, d, pr, e * GLA_DV:(e + 1) * GLA_DV, e * GLA_DK:(e + 1) * GLA_DK] = s0_ref[s, 0, h].T

    klane = lax.broadcasted_iota(jnp.int32, (1, GLA_K_WIDTH), 1)
    even_head = (klane // GLA_DK) % 2 == 0

    def chunk_decay(s, d, r0, f_ref):
        rows = pl.ds(s * seq_len + r0, C)
        f = f_ref[rows, :]
        f_hi = f.astype(BF16)
        f_lo = (f - f_hi.astype(F32)).astype(BF16)
        cum = _dot(w_ref[d], jnp.concatenate([f_hi, f_lo], axis=0))
        cum_scr[s, d] = cum
        return rows, cum

    def chunk_scores(s, d, rows, cum):
        def cum_rows(r, n):
            return jnp.broadcast_to(cum_scr[s, d, r:r + 1, :], (n, GLA_K_WIDTH))

        def level_factor(p):
            m = 1 << p
            edge = m - 1 if d == 0 else m
            if 2 * m >= SUBLANES:
                ref = jnp.concatenate([cum_rows(blk + edge, 2 * m) for blk in range(0, C, 2 * m)], axis=0)
            elif 2 * m == SUBLANES // 2:
                sub = lax.broadcasted_iota(jnp.int32, (SUBLANES, GLA_K_WIDTH), 0)
                ref = jnp.concatenate([jnp.where(sub < 2 * m, cum_rows(blk + edge, SUBLANES),
                                                 cum_rows(blk + 2 * m + edge, SUBLANES))
                                       for blk in range(0, C, SUBLANES)], axis=0)
            else:
                odd = lax.broadcasted_iota(jnp.int32, (C, GLA_K_WIDTH), 0) % 2 == 1
                if d == 0:
                    ref = jnp.where(odd, pltpu.roll(cum, 1, axis=0), cum)
                else:
                    ref = jnp.where(odd, cum, pltpu.roll(cum, C - 1, axis=0))
            return jnp.exp2((cum - ref) * scale_ref[d, p])

        qc = gq_ref[rows, :]
        kc = gk_ref[rows, :]
        vc = gv_ref[rows, :]
        last = C - 1 if d == 0 else 0
        eq = jnp.exp(cum)
        ek = jnp.exp(cum_rows(last, C) - cum)
        total = jnp.exp(cum_scr[s, d, last:last + 1, :])

        qb = qc.astype(BF16)
        kb = kc.astype(BF16)
        kb_even = jnp.where(even_head, kb, jnp.zeros_like(kb))
        kb_odd = jnp.where(even_head, jnp.zeros_like(kb), kb)
        zero = jnp.zeros((C, pair_k), BF16)
        a = None
        for p in range(GLA_LEVELS + 1):
            if p < GLA_LEVELS:
                ep = level_factor(p).astype(BF16)
                qe, ke_even, ke_odd = qb * ep, kb_even * ep, kb_odd * ep
            else:
                qe, ke_even, ke_odd = qb, kb_even, kb_odd
            lhs = jnp.concatenate([jnp.concatenate([qe[:, :pair_k], zero], axis=1),
                                   jnp.concatenate([zero, qe[:, pair_k:]], axis=1)], axis=0)
            sc = _dot_nt(lhs, jnp.concatenate([ke_even, ke_odd], axis=0)) * mask_ref[d, p]
            a = sc if a is None else a + sc
        q_in = (qc * eq).astype(BF16)
        k_out = (kc * ek).astype(BF16)
        return a.astype(BF16), q_in, k_out, vc, total

    def chunk_output(s, d, rows, accumulate, a, q_in, k_out, vc, total):
        zero_v = jnp.zeros((C, GLA_DV), BF16)
        for pr in range(n_pairs):
            kl = slice(pr * pair_k, (pr + 1) * pair_k)
            vp = vc[:, pr * pair_v:(pr + 1) * pair_v]
            v_diag = jnp.concatenate([jnp.concatenate([vp[:, :GLA_DV], zero_v], axis=1),
                                      jnp.concatenate([zero_v, vp[:, GLA_DV:]], axis=1)], axis=0)
            st = st_scr[s, d, pr]
            o = _dot(a[pr * C:(pr + 1) * C], v_diag) + _dot_nt(q_in[:, kl], st.astype(BF16))
            if accumulate:
                o_ref[rows, pr * pair_v:(pr + 1) * pair_v] += o
            else:
                o_ref[rows, pr * pair_v:(pr + 1) * pair_v] = o
            st_scr[s, d, pr] = st * total[:, kl] + _dot_tn(vp, k_out[:, kl]) * smask_ref[...]

    def body(accumulate, n, carry):
        fwd = pl.multiple_of(n * C, C)
        bwd = pl.multiple_of((n_chunks - 1 - n) * C, C)
        scans = [(s, d) for s in range(n_seq) for d in range(2)]
        decays = [chunk_decay(s, d, bwd if d else fwd, lb_ref if d else lf_ref) for s, d in scans]
        scores = [chunk_scores(s, d, *dec) for (s, d), dec in zip(scans, decays)]
        for (s, d), (rows, _), sc in zip(scans, decays, scores):
            chunk_output(s, d, rows, accumulate, *sc)
        return carry

    half = n_chunks // 2
    lax.fori_loop(0, half, functools.partial(body, False), 0)
    lax.fori_loop(half, n_chunks, functools.partial(body, True), 0)
    if is_ctx:
        slot = layer if creates_states else 0
        for d, s_ref in enumerate((sf_ref, sb_ref)):
            for s in range(n_seq):
                for h in range(GLA_HEADS):
                    pr, e = divmod(h, 2)
                    s_ref[s, slot, h] = st_scr[s, d, pr, e * GLA_DV:(e + 1) * GLA_DV, e * GLA_DK:(e + 1) * GLA_DK].T
            if creates_states:
                for other in range(DEPTH):
                    if other != layer:
                        s_ref[:, other] = jnp.zeros((n_seq, GLA_HEADS, GLA_DK, GLA_DV), F32)


def _gla_call(qkf, gv, consts, new_states=None, state_in=None, layer=0):
    is_ctx = state_in is None
    creates_states = is_ctx and new_states is None
    seq_len = SEQ if is_ctx else DEC_SEQ
    n_tok = qkf.shape[0]
    n_seq = GLA_CTX_SEQS_PER_STEP if is_ctx else DEC_BATCH
    rows = n_seq * seq_len
    tok = lambda width: pl.BlockSpec((rows, width), lambda b: (b, 0))
    whole = lambda a: pl.BlockSpec(a.shape, lambda b: (0,) * a.ndim)
    state_spec = pl.BlockSpec((n_seq, 1, GLA_HEADS, GLA_DK, GLA_DV), lambda b: (b, layer, 0, 0, 0))
    qkf_block = lambda j: pl.BlockSpec((rows, GLA_K_WIDTH), lambda b: (b, j))
    in_specs = [qkf_block(0), qkf_block(1), tok(GLA_V_WIDTH), qkf_block(2), qkf_block(3)]
    in_specs += [whole(a) for a in consts]
    args = [qkf, qkf, gv, qkf, qkf, *consts]
    out_shape = [jax.ShapeDtypeStruct((n_tok, GLA_V_WIDTH), F32)]
    out_specs = [tok(GLA_V_WIDTH)]
    aliases = {}
    if is_ctx:
        states_shape = (BATCH, DEPTH, GLA_HEADS, GLA_DK, GLA_DV)
        if creates_states:
            new_spec = pl.BlockSpec((n_seq,) + states_shape[1:], lambda b: (b, 0, 0, 0, 0))
        else:
            new_spec = state_spec
            aliases = {len(args): 1, len(args) + 1: 2}
            in_specs += [pl.BlockSpec(memory_space=pl.ANY)] * 2
            args += list(new_states)
        out_shape += [jax.ShapeDtypeStruct(states_shape, F32)] * 2
        out_specs += [new_spec, new_spec]
    else:
        in_specs += [state_spec, state_spec]
        args += list(state_in)
    return pl.pallas_call(
        functools.partial(_gla_kernel, seq_len, n_seq, is_ctx, creates_states, layer),
        grid=(n_tok // rows,),
        in_specs=in_specs,
        out_specs=out_specs,
        out_shape=out_shape,
        input_output_aliases=aliases,
        scratch_shapes=[
            pltpu.VMEM((n_seq, 2, GLA_HEADS // 2, 2 * GLA_DV, 2 * GLA_DK), F32),
            pltpu.VMEM((n_seq, 2, GLA_CHUNK, GLA_K_WIDTH), F32),
        ],
        compiler_params=_params(1),
        name="gla_ctx" if is_ctx else "gla_smp",
    )(*args)


def _rope_tables():
    quarter = GLA_DK // 4
    inv = ROPE_BASE ** (-jnp.arange(quarter, dtype=F32) / quarter)
    t = jnp.arange(DEC_SEQ)
    ang_r = (t // GRID_W).astype(F32)[:, None] * inv
    ang_c = (t % GRID_W).astype(F32)[:, None] * inv
    cos = jnp.concatenate([jnp.cos(ang_r)] * 2 + [jnp.cos(ang_c)] * 2, axis=-1)
    sin = jnp.concatenate([-jnp.sin(ang_r), jnp.sin(ang_r), -jnp.sin(ang_c), jnp.sin(ang_c)], axis=-1)
    return jnp.tile(cos, (1, GLA_HEADS)), jnp.tile(sin, (1, GLA_HEADS))


def _merge_mlp_kernel(layer, x_ref, mod_ref, ona_ref, ogla_ref, gate_ref, gout_ref, gmlp_ref,
                      wo_hbm, wup_hbm, wdown_hbm, o_ref,
                      wo_scr, wup_scr, wdown_scr, stage_cols, stage_rows, sem):
    first_step = pl.program_id(0) == 0
    n_chunks = D_FF // FF_CHUNK
    wo_halves = D_MODEL // FF_CHUNK

    def col_copy(j):
        slot = j % 2
        if j < wo_halves:
            src = wo_hbm.at[layer, :, j * FF_CHUNK:(j + 1) * FF_CHUNK]
        else:
            c = j - wo_halves
            src = wup_hbm.at[layer, :, c * FF_CHUNK:(c + 1) * FF_CHUNK]
        return pltpu.make_async_copy(src, stage_cols.at[slot], sem.at[slot])

    def row_copy(c):
        slot = c % 2
        src = wdown_hbm.at[layer, c * FF_CHUNK:(c + 1) * FF_CHUNK, :]
        return pltpu.make_async_copy(src, stage_rows.at[slot], sem.at[2 + slot])

    n_col_blocks = wo_halves + n_chunks

    def land_col(j):
        col_copy(j).wait()
        block = stage_cols[j % 2].astype(BF16)
        if j < wo_halves:
            wo_scr[:, j * FF_CHUNK:(j + 1) * FF_CHUNK] = block
        else:
            c = j - wo_halves
            wup_scr[:, c * FF_CHUNK:(c + 1) * FF_CHUNK] = block
        if j + 2 < n_col_blocks:
            col_copy(j + 2).start()

    def land_row(c):
        row_copy(c).wait()
        wdown_scr[c * FF_CHUNK:(c + 1) * FF_CHUNK, :] = stage_rows[c % 2].astype(BF16)
        if c + 2 < n_chunks:
            row_copy(c + 2).start()

    def run(streaming):
        if streaming:
            col_copy(0).start()
            col_copy(1).start()
            row_copy(0).start()
            row_copy(1).start()

        mod = mod_ref[0]
        ga1 = mod[:, 2 * D_MODEL:3 * D_MODEL]
        sh2 = mod[:, 3 * D_MODEL:4 * D_MODEL]
        sc2 = mod[:, 4 * D_MODEL:5 * D_MODEL]
        ga2 = mod[:, 5 * D_MODEL:6 * D_MODEL]

        og = ogla_ref[...]
        normed = []
        for h in range(GLA_HEADS):
            oh = og[:, h * GLA_DV:(h + 1) * GLA_DV]
            normed.append(oh * lax.rsqrt(jnp.mean(oh * oh, axis=-1, keepdims=True) + EPS))
        gate = gate_ref[...]
        g = jnp.concatenate(normed, axis=1) * gout_ref[...] * (gate * (1.0 / (1.0 + jnp.exp(-gate))))

        if streaming:
            for j in range(wo_halves):
                land_col(j)
        att = _dot(ona_ref[...], wo_scr[0:NA_WIDTH, :]) + _dot(g.astype(BF16), wo_scr[NA_WIDTH:, :])
        x = x_ref[...] + ga1 * att
        ms = jnp.mean(x * x, axis=-1, keepdims=True)
        h2 = ((x * lax.rsqrt(ms + EPS) * gmlp_ref[...]) * (1.0 + sc2) + sh2).astype(BF16)

        width = FF_CHUNK if streaming else MLP_CHUNK
        acc = None
        for c in range(D_FF // width):
            if streaming:
                land_col(wo_halves + c)
                land_row(c)
            u = jnp.maximum(_dot(h2, wup_scr[:, c * width:(c + 1) * width]), 0.0)
            down = _dot((u * u).astype(BF16), wdown_scr[c * width:(c + 1) * width, :])
            acc = down if acc is None else acc + down
        o_ref[...] = x + ga2 * acc

    pl.when(first_step)(functools.partial(run, True))
    pl.when(jnp.logical_not(first_step))(functools.partial(run, False))


def _merge_mlp_call(x, mod_l, o_na, o_gla, gate, gout_t, g_mlp_l, w_o, w_up, w_down, layer):
    tm = TOKEN_TILE
    n_tok = x.shape[0]
    is_ctx = n_tok == N_CTX
    row = lambda i: (i, 0)
    const = lambda i: (0, 0)
    in_hbm = pl.BlockSpec(memory_space=pl.ANY)
    return pl.pallas_call(
        functools.partial(_merge_mlp_kernel, layer),
        grid=(n_tok // tm,),
        in_specs=[
            pl.BlockSpec((tm, D_MODEL), row),
            pl.BlockSpec((1, 1, N_MOD * D_MODEL), _mod_index_map(is_ctx)),
            pl.BlockSpec((tm, NA_WIDTH), row),
            pl.BlockSpec((tm, GLA_V_WIDTH), row),
            pl.BlockSpec((tm, GLA_V_WIDTH), row),
            pl.BlockSpec((1, GLA_V_WIDTH), const),
            pl.BlockSpec((1, D_MODEL), const),
            in_hbm, in_hbm, in_hbm,
        ],
        out_specs=pl.BlockSpec((tm, D_MODEL), row),
        out_shape=jax.ShapeDtypeStruct((n_tok, D_MODEL), F32),
        scratch_shapes=[
            pltpu.VMEM((D_MODEL, D_MODEL), BF16),
            pltpu.VMEM((D_MODEL, D_FF), BF16),
            pltpu.VMEM((D_FF, D_MODEL), BF16),
            pltpu.VMEM((2, D_MODEL, FF_CHUNK), F32),
            pltpu.VMEM((2, FF_CHUNK, D_MODEL), F32),
            pltpu.SemaphoreType.DMA((4,)),
        ],
        compiler_params=_params(1),
        name="merge_mlp_ctx" if is_ctx else "merge_mlp_smp",
    )(x, mod_l, o_na, o_gla, gate, gout_t, g_mlp_l, w_o, w_up, w_down)


def kernel(x_prompt, x_sample, cache_k, cache_v, state_fwd, state_bwd, c, c_ctx, w_ada, b_ada, g_attn, w_in,
           g_q, g_k, rpb, w_gf, b_gf, w_gb, b_gb, g_gla_out, w_o, g_mlp, w_up, w_down):
    x_c = x_prompt.reshape(N_CTX, D_MODEL)
    x_s = x_sample.reshape(N_SMP, D_MODEL)

    c_rows = jnp.concatenate([c_ctx[None, :], c, jnp.zeros((MOD_ROWS - 1 - DEC_BATCH, D_MODEL), F32)], axis=0)
    mods = _ada_call(c_rows, w_ada, b_ada).reshape(DEPTH, MOD_ROWS, 1, N_MOD * D_MODEL)

    w_in_t = jnp.swapaxes(w_in, 1, 2)
    wgate = jnp.zeros((DEPTH, 2 * GLA_GATE_RANK, 2 * GLA_K_WIDTH), F32)
    wgate = wgate.at[:, :GLA_GATE_RANK, :GLA_K_WIDTH].set(w_gf)
    wgate = wgate.at[:, GLA_GATE_RANK:, GLA_K_WIDTH:].set(w_gb).astype(BF16)
    bgate = jnp.concatenate([b_gf, b_gb], axis=-1).reshape(DEPTH, 1, 2 * GLA_K_WIDTH)
    head_of = np.arange(MXU_WIDTH) // NA_HEAD_DIM
    seg = jnp.asarray((head_of[:, None] == head_of[None, :]).astype(np.float32) / NA_HEAD_DIM, BF16)
    gq_t = jnp.tile(g_q, (1, NA_HEADS)).reshape(DEPTH, 1, NA_WIDTH)
    gk_t = jnp.tile(g_k, (1, NA_HEADS)).reshape(DEPTH, 1, NA_WIDTH)
    gout_t = jnp.tile(g_gla_out, (1, GLA_HEADS)).reshape(DEPTH, 1, GLA_V_WIDTH)
    n_dr, n_dc = rpb.shape[2:]
    rpb_rows = jnp.pad(rpb, ((0, 0), (0, 0), (0, -n_dr % SUBLANES), (0, 2 * GRID_W - n_dc)))
    gla_w_np, gla_masks_np, gla_smask_np, gla_scales_np = _gla_constants()
    gla_consts = (jnp.asarray(gla_w_np, BF16), jnp.asarray(gla_masks_np, F32), jnp.asarray(gla_smask_np, F32),
                  jnp.asarray(gla_scales_np, F32))
    rope = _rope_tables()

    cache_kt = jnp.swapaxes(cache_k, 3, 4)
    cache_vt = jnp.swapaxes(cache_v, 3, 4)
    new_kv = None
    new_states = None

    for l in range(DEPTH):
        proj_w = (g_attn[l].reshape(1, D_MODEL), w_in_t, seg, gq_t[l], gk_t[l], wgate[l], bgate[l], rope)
        mlp_w = (gout_t[l], g_mlp[l].reshape(1, D_MODEL), w_o, w_up, w_down)

        o_na, qkf, gv, gate, *new_kv = _proj_call(x_c, mods[l], *proj_w, True, new_kv=new_kv, layer=l)
        o_gla, *new_states = _gla_call(qkf, gv, gla_consts, new_states=new_states, layer=l)
        x_c = _merge_mlp_call(x_c, mods[l], o_na, o_gla, gate, *mlp_w, layer=l)

        q, k, v, qkf, gv, gate = _proj_call(x_s, mods[l], *proj_w, False, layer=l)
        o_na = _na_call(q, k, v, cache_kt, cache_vt, rpb_rows, l)
        (o_gla,) = _gla_call(qkf, gv, gla_consts, state_in=(state_fwd, state_bwd), layer=l)
        x_s = _merge_mlp_call(x_s, mods[l], o_na, o_gla, gate, *mlp_w, layer=l)

    return (x_c.reshape(BATCH, SEQ, D_MODEL), x_s.reshape(DEC_BATCH, DEC_SEQ, D_MODEL),
            jnp.swapaxes(new_kv[0], 3, 4), jnp.swapaxes(new_kv[1], 3, 4), *new_states)
```

```python
import functools

import numpy as np
import jax
import jax.numpy as jnp
from jax import lax
from jax.experimental import pallas as pl
from jax.experimental.pallas import tpu as pltpu

F32 = jnp.float32
BF16 = jnp.bfloat16

D_MODEL = 1024
BATCH = 16
SEQ = 256
DEPTH = 4
DEC_BATCH = 2
DEC_SEQ = 1024
PAST_LEN = 512
GRID_W = 64
GRID_ROWS = DEC_SEQ // GRID_W
NA_WIDTH = D_MODEL // 2
NA_HEAD_DIM = 64
NA_HEADS = NA_WIDTH // NA_HEAD_DIM
NA_WIN_ROWS = 8
NA_WIN_COLS = 16
GLA_V_WIDTH = D_MODEL - NA_WIDTH
GLA_HEADS = 4
GLA_DV = GLA_V_WIDTH // GLA_HEADS
GLA_DK = GLA_DV // 2
GLA_K_WIDTH = GLA_HEADS * GLA_DK
GLA_GATE_RANK = 16
GLA_GATE_TAU = 16.0
GLA_CHUNK = 64
D_FF = 4 * D_MODEL
ROPE_BASE = 10000.0
N_MOD = 6
EPS = 1e-6

IN_WIDTH = 3 * NA_WIDTH + 2 * GLA_K_WIDTH + 2 * GLA_V_WIDTH + 2 * GLA_GATE_RANK
PROJ_WIDE = IN_WIDTH - 2 * GLA_GATE_RANK

N_CTX = BATCH * SEQ
N_SMP = DEC_BATCH * DEC_SEQ
SUBLANES = 8
MXU_WIDTH = 256
MOD_ROWS = SUBLANES
MASK_VALUE = -1e30

TOKEN_TILE = 512
PROJ_BLOCK = 512
FF_CHUNK = 512
MLP_CHUNK = 1024
ADA_TILE_N = 1536
GLA_LEVELS = 6
NA_ROW_GROUP = 4
GLA_CTX_SEQS_PER_STEP = 4
VMEM_LIMIT = 52 * 1024 * 1024


def _dot(a, b):
    return jnp.dot(a, b, preferred_element_type=F32)


def _dot_nt(a, b):
    return lax.dot_general(a, b, (((1,), (1,)), ((), ())), preferred_element_type=F32)


def _dot_tn(a, b):
    return lax.dot_general(a, b, (((0,), (0,)), ((), ())), preferred_element_type=F32)


def _params(n_grid_dims=1):
    return pltpu.CompilerParams(dimension_semantics=("arbitrary",) * n_grid_dims,
                                vmem_limit_bytes=VMEM_LIMIT)


def _mod_index_map(is_ctx):
    if is_ctx:
        return lambda i: (0, 0, 0)
    return lambda i: (1 + i // (DEC_SEQ // TOKEN_TILE), 0, 0)


def _ada_kernel(c_ref, w_ref, b_ref, o_ref):
    cv = c_ref[...]
    s = cv * (1.0 / (1.0 + jnp.exp(-cv)))
    o_ref[0] = _dot(s.astype(BF16), w_ref[0].astype(BF16)) + b_ref[0]


def _ada_call(c_rows, w_ada, b_ada):
    n_mod = N_MOD * D_MODEL
    return pl.pallas_call(
        _ada_kernel,
        grid=(DEPTH, n_mod // ADA_TILE_N),
        in_specs=[
            pl.BlockSpec((MOD_ROWS, D_MODEL), lambda l, j: (0, 0)),
            pl.BlockSpec((1, D_MODEL, ADA_TILE_N), lambda l, j: (l, 0, j)),
            pl.BlockSpec((1, 1, ADA_TILE_N), lambda l, j: (l, 0, j)),
        ],
        out_specs=pl.BlockSpec((1, MOD_ROWS, ADA_TILE_N), lambda l, j: (l, 0, j)),
        out_shape=jax.ShapeDtypeStruct((DEPTH, MOD_ROWS, n_mod), F32),
        compiler_params=_params(2),
        name="ada_mod",
    )(c_rows, w_ada, b_ada.reshape(DEPTH, 1, n_mod))


def _ctx_attention(q, k, v):
    n_seq = q.shape[0] // SEQ
    lane = lax.broadcasted_iota(jnp.int32, (1, 2 * NA_HEAD_DIM), 1)
    first = lane < NA_HEAD_DIM
    pairs = [(slice(s * SEQ, (s + 1) * SEQ), slice(hp * 2 * NA_HEAD_DIM, (hp + 1) * 2 * NA_HEAD_DIM))
             for s in range(n_seq) for hp in range(NA_HEADS // 2)]
    scores = []
    for rows, cols in pairs:
        q2 = q[rows, cols]
        k2 = k[rows, cols]
        scores.append(_dot_nt(jnp.where(first, q2, jnp.zeros_like(q2)), k2))
        scores.append(_dot_nt(jnp.where(first, jnp.zeros_like(q2), q2), k2))
    probs = []
    for s in scores:
        p = jnp.exp(s - jnp.max(s, axis=-1, keepdims=True))
        probs.append((p.astype(BF16), jnp.sum(p, axis=-1, keepdims=True)))
    outs = []
    for i, (rows, cols) in enumerate(pairs):
        v2 = v[rows, cols]
        (p0, l0), (p1, l1) = probs[2 * i], probs[2 * i + 1]
        outs.append(jnp.where(first, _dot(p0, v2) / l0, _dot(p1, v2) / l1))
    per_seq = NA_HEADS // 2
    return jnp.concatenate([jnp.concatenate(outs[s * per_seq:(s + 1) * per_seq], axis=1) for s in range(n_seq)],
                           axis=0)


def _log_sigmoid(x):
    return jnp.minimum(x, 0.0) - jnp.log(1.0 + jnp.exp(-jnp.abs(x)))


def _proj_kernel(is_ctx, creates_kv, layer, x_ref, mod_ref, g_ref, seg_ref, gq_ref, gk_ref, wgate_ref, bgate_ref,
                 cos_ref, sin_ref, win_hbm, *rest):
    if is_ctx:
        if not creates_kv:
            rest = rest[2:]
        ona_out = rest[0]
        rest = rest[1:]
    else:
        q_out, k_out, v_out = rest[:3]
        rest = rest[3:]
    qkf_out, gv_out, gate_out = rest[:3]
    rest = rest[3:]
    if is_ctx:
        knew_out, vnew_out = rest[:2]
        rest = rest[2:]
    w_scr, stage, tail_stage, sem = rest

    n_blocks = PROJ_WIDE // PROJ_BLOCK
    tail = slice(PROJ_WIDE, IN_WIDTH)
    first_step = pl.program_id(0) == 0

    def block_copy(j):
        src = win_hbm.at[layer, j * PROJ_BLOCK:(j + 1) * PROJ_BLOCK, :]
        return pltpu.make_async_copy(src, stage.at[j % 2], sem.at[j % 2])

    def tail_copy():
        return pltpu.make_async_copy(win_hbm.at[layer, tail, :], tail_stage, sem.at[2])

    def land(j):
        block_copy(j).wait()
        w_scr[j * PROJ_BLOCK:(j + 1) * PROJ_BLOCK, :] = stage[j % 2].astype(BF16)
        if j + 2 < n_blocks:
            block_copy(j + 2).start()

    def run(streaming):
        if streaming:
            block_copy(0).start()
            block_copy(1).start()
            tail_copy().start()

        x = x_ref[...]
        mod = mod_ref[0]
        sh1 = mod[:, 0:D_MODEL]
        sc1 = mod[:, D_MODEL:2 * D_MODEL]
        ms = jnp.mean(x * x, axis=-1, keepdims=True)
        h = (x * lax.rsqrt(ms + EPS) * g_ref[...]) * (1.0 + sc1) + sh1
        hb = h.astype(BF16)

        def projected(j):
            if streaming:
                land(j)
            return _dot_nt(hb, w_scr[j * PROJ_BLOCK:(j + 1) * PROJ_BLOCK, :])

        seg = seg_ref[...]

        def head_mean_square(t):
            sq = (t * t).astype(BF16)
            return jnp.concatenate([_dot(sq[:, c:c + MXU_WIDTH], seg) for c in range(0, NA_WIDTH, MXU_WIDTH)],
                                   axis=1)

        q = projected(0)
        q_ms = head_mean_square(q)
        qb = (q * lax.rsqrt(q_ms + EPS) * gq_ref[...] * (NA_HEAD_DIM ** -0.5)).astype(BF16)
        k = projected(1)
        k_ms = head_mean_square(k)
        kn = k * lax.rsqrt(k_ms + EPS) * gk_ref[...]

        if streaming:
            tail_copy().wait()
            w_scr[tail, :] = tail_stage[...].astype(BF16)
        zz_t = _dot_nt(w_scr[tail, :], hb)
        pre = _dot_tn(zz_t.astype(BF16), wgate_ref[...]) + bgate_ref[...]
        qkf_out[:, 2 * GLA_K_WIDTH:] = _log_sigmoid(pre) * (1.0 / GLA_GATE_TAU)

        v = projected(2)
        if not is_ctx:
            q_out[...] = qb
            k_out[...] = kn.astype(BF16)
            v_out[...] = v.astype(BF16)
        else:
            ona_out[...] = _ctx_attention(qb, kn.astype(BF16), v.astype(BF16)).astype(BF16)
            slot = layer if creates_kv else 0
            for s in range(TOKEN_TILE // SEQ):
                rows = slice(s * SEQ, (s + 1) * SEQ)
                for hp in range(NA_HEADS // 2):
                    cols = slice(hp * 2 * NA_HEAD_DIM, (hp + 1) * 2 * NA_HEAD_DIM)
                    for out, t in ((knew_out, kn[rows, cols].T), (vnew_out, v[rows, cols].T)):
                        out[s, slot, 2 * hp] = t[:NA_HEAD_DIM]
                        out[s, slot, 2 * hp + 1] = t[NA_HEAD_DIM:]
            if creates_kv:
                for out in (knew_out, vnew_out):
                    for other in range(DEPTH):
                        if other != layer:
                            out[:, other] = jnp.zeros((TOKEN_TILE // SEQ, NA_HEADS, NA_HEAD_DIM, SEQ), F32)

        z = projected(3)
        gla_q = z[:, 0:GLA_K_WIDTH] * (GLA_DK ** -0.5)
        gla_k = z[:, GLA_K_WIDTH:]
        if not is_ctx:
            lane = lax.broadcasted_iota(jnp.int32, (1, GLA_K_WIDTH), 1)
            quarter = GLA_DK // 4
            low = (lane % (2 * quarter)) < quarter
            cos = cos_ref[...]
            sin = sin_ref[...]

            def rope(t):
                partner = jnp.where(low, pltpu.roll(t, GLA_K_WIDTH - quarter, axis=1),
                                    pltpu.roll(t, quarter, axis=1))
                return t * cos + partner * sin

            gla_q = rope(gla_q)
            gla_k = rope(gla_k)
        qkf_out[:, 0:GLA_K_WIDTH] = gla_q
        qkf_out[:, GLA_K_WIDTH:2 * GLA_K_WIDTH] = gla_k
        gv_out[...] = projected(4).astype(BF16)
        gate_out[...] = projected(5)

    pl.when(first_step)(functools.partial(run, True))
    pl.when(jnp.logical_not(first_step))(functools.partial(run, False))


def _proj_call(x, mod_l, g_attn_l, w_in_t, seg, gq_t, gk_t, wgate, bgate, rope, is_ctx, new_kv=None, layer=0):
    tm = TOKEN_TILE
    n_tok = x.shape[0]
    creates_kv = is_ctx and new_kv is None
    row = lambda i: (i, 0)
    const = lambda i: (0, 0)
    rope_rows = const if is_ctx else (lambda i: (i % (DEC_SEQ // tm), 0))
    if is_ctx:
        out_shapes = [jax.ShapeDtypeStruct((n_tok, NA_WIDTH), BF16)]
    else:
        out_shapes = [jax.ShapeDtypeStruct((n_tok, NA_WIDTH), BF16)] * 3
    out_shapes += [
        jax.ShapeDtypeStruct((n_tok, 4 * GLA_K_WIDTH), F32),
        jax.ShapeDtypeStruct((n_tok, GLA_V_WIDTH), BF16),
        jax.ShapeDtypeStruct((n_tok, GLA_V_WIDTH), F32),
    ]
    out_specs = [pl.BlockSpec((tm, s.shape[1]), row) for s in out_shapes]
    in_specs = [
        pl.BlockSpec((tm, D_MODEL), row),
        pl.BlockSpec((1, 1, N_MOD * D_MODEL), _mod_index_map(is_ctx)),
        pl.BlockSpec((1, D_MODEL), const),
        pl.BlockSpec((MXU_WIDTH, MXU_WIDTH), const),
        pl.BlockSpec((1, NA_WIDTH), const),
        pl.BlockSpec((1, NA_WIDTH), const),
        pl.BlockSpec(wgate.shape, const),
        pl.BlockSpec((1, 2 * GLA_K_WIDTH), const),
        pl.BlockSpec((tm, GLA_K_WIDTH), rope_rows),
        pl.BlockSpec((tm, GLA_K_WIDTH), rope_rows),
        pl.BlockSpec(memory_space=pl.ANY),
    ]
    args = [x, mod_l, g_attn_l, seg, gq_t, gk_t, wgate, bgate, *rope, w_in_t]
    aliases = {}
    if is_ctx:
        kv_shape = (BATCH, DEPTH, NA_HEADS, NA_HEAD_DIM, SEQ)
        if creates_kv:
            kv_spec = pl.BlockSpec((tm // SEQ,) + kv_shape[1:], lambda i: (i, 0, 0, 0, 0))
        else:
            kv_spec = pl.BlockSpec((tm // SEQ, 1) + kv_shape[2:], lambda i: (i, layer, 0, 0, 0))
            aliases = {len(args): len(out_shapes), len(args) + 1: len(out_shapes) + 1}
            in_specs += [pl.BlockSpec(memory_space=pl.ANY)] * 2
            args += list(new_kv)
        out_shapes += [jax.ShapeDtypeStruct(kv_shape, F32)] * 2
        out_specs += [kv_spec, kv_spec]
    return pl.pallas_call(
        functools.partial(_proj_kernel, is_ctx, creates_kv, layer),
        grid=(n_tok // tm,),
        in_specs=in_specs,
        out_specs=out_specs,
        out_shape=out_shapes,
        input_output_aliases=aliases,
        scratch_shapes=[
            pltpu.VMEM((IN_WIDTH, D_MODEL), BF16),
            pltpu.VMEM((2, PROJ_BLOCK, D_MODEL), F32),
            pltpu.VMEM((IN_WIDTH - PROJ_WIDE, D_MODEL), F32),
            pltpu.SemaphoreType.DMA((3,)),
        ],
        compiler_params=_params(1),
        name="proj_ctx" if is_ctx else "proj_smp",
    )(*args)


def _na_window_start(r):
    return min(max(r - NA_WIN_ROWS // 2, 0), GRID_ROWS - NA_WIN_ROWS)


def _na_row_groups():
    groups = []
    for g in range(GRID_ROWS // NA_ROW_GROUP):
        starts = [_na_window_start(r) for r in range(g * NA_ROW_GROUP, (g + 1) * NA_ROW_GROUP)]
        lo = min(starts) // 2 * 2
        count = max(starts) + NA_WIN_ROWS - lo
        groups.append((lo, count + count % 2))
    return groups


def _na_bias_tiles(rpb_ref, hh, tiles_scr):
    n_dr = 2 * NA_WIN_ROWS - 1
    lanes = 2 * GRID_W
    lane = lax.broadcasted_iota(jnp.int32, (GRID_W, lanes), 1)
    col = lax.broadcasted_iota(jnp.int32, (GRID_W, lanes), 0)
    first = lane < GRID_W
    c2 = lane % GRID_W
    cs = jnp.clip(col - NA_WIN_COLS // 2, 0, GRID_W - NA_WIN_COLS)
    valid = (c2 >= cs) & (c2 < cs + NA_WIN_COLS)
    rows = rpb_ref[0, hh]
    base = lanes - (NA_WIN_COLS - 1)

    def skew(dr, shift):
        row = jnp.broadcast_to(rows[dr:dr + 1], (GRID_W, lanes))
        return pltpu.roll(row, shift, axis=1, stride=1, stride_axis=0)

    for e in range(n_dr - 1):
        tile = jnp.where(first, skew(e, base), skew(e + 1, (base + GRID_W) % lanes))
        tiles_scr[e] = jnp.where(valid, tile, MASK_VALUE)


def _na_group_bias(tiles_scr, g, key_lo, key_count, first):
    masked = jnp.full((GRID_W, 2 * GRID_W), MASK_VALUE, F32)
    row_tiles = []
    for r in range(g * NA_ROW_GROUP, (g + 1) * NA_ROW_GROUP):
        rs = _na_window_start(r)
        pieces = []
        for u in range(key_count // 2):
            r0 = key_lo + 2 * u
            in0 = rs <= r0 < rs + NA_WIN_ROWS
            in1 = rs <= r0 + 1 < rs + NA_WIN_ROWS
            if not (in0 or in1):
                pieces.append(masked)
                continue
            tile = tiles_scr[r0 - r + NA_WIN_ROWS - 1]
            if not in1:
                tile = jnp.where(first, tile, MASK_VALUE)
            elif not in0:
                tile = jnp.where(first, MASK_VALUE, tile)
            pieces.append(tile)
        row_tiles.append(jnp.concatenate(pieces, axis=1))
    return jnp.concatenate(row_tiles, axis=0)


def _na_kernel(q_ref, k_ref, v_ref, kc_ref, vc_ref, rpb_ref, o_ref, sc_ref, tiles_scr):
    lane = lax.broadcasted_iota(jnp.int32, (1, 2 * NA_HEAD_DIM), 1)
    first = lane < NA_HEAD_DIM
    q2 = q_ref[...]
    kc2 = jnp.concatenate([kc_ref[0, 0, 0], kc_ref[0, 0, 1]], axis=0).astype(BF16)
    vc2 = jnp.concatenate([vc_ref[0, 0, 0], vc_ref[0, 0, 1]], axis=0).astype(BF16)
    block = NA_ROW_GROUP * GRID_W
    groups = _na_row_groups()
    qms = [jnp.where(first, q2, jnp.zeros_like(q2)), jnp.where(first, jnp.zeros_like(q2), q2)]
    for hh in range(2):
        sc_ref[hh] = _dot(qms[hh], kc2)
        _na_bias_tiles(rpb_ref, hh, tiles_scr.at[hh])
    problems = [(hh, g) for hh in range(2) for g in range(len(groups))]
    scores = []
    for hh, g in problems:
        key_lo, key_count = groups[g]
        keys = slice(key_lo * GRID_W, (key_lo + key_count) * GRID_W)
        s = _dot_nt(qms[hh][g * block:(g + 1) * block], k_ref[keys, :])
        scores.append(s + _na_group_bias(tiles_scr.at[hh], g, key_lo, key_count, first))
    probs = []
    for (hh, g), s in zip(problems, scores):
        sc = sc_ref[hh, g * block:(g + 1) * block, :]
        m = jnp.maximum(jnp.max(s, axis=-1, keepdims=True), jnp.max(sc, axis=-1, keepdims=True))
        p = jnp.exp(s - m)
        pc = jnp.exp(sc - m)
        l = jnp.sum(p, axis=-1, keepdims=True) + jnp.sum(pc, axis=-1, keepdims=True)
        probs.append((p.astype(BF16), pc.astype(BF16), l))
    outs = [[], []]
    for (hh, g), (p, pc, l) in zip(problems, probs):
        key_lo, key_count = groups[g]
        keys = slice(key_lo * GRID_W, (key_lo + key_count) * GRID_W)
        outs[hh].append((_dot(p, v_ref[keys, :]) + _dot_nt(pc, vc2)) / l)
    outs = [jnp.concatenate(o, axis=0) for o in outs]
    o_ref[...] = jnp.where(first, outs[0], outs[1]).astype(BF16)


def _na_call(q, k, v, cache_kt, cache_vt, rpb_rows, layer):
    tok = pl.BlockSpec((DEC_SEQ, 2 * NA_HEAD_DIM), lambda b, hp: (b, hp))
    cache = pl.BlockSpec((1, 1, 2, NA_HEAD_DIM, PAST_LEN), lambda b, hp: (b, layer, hp, 0, 0))
    n_pairs = 2 * NA_WIN_ROWS - 2
    return pl.pallas_call(
        _na_kernel,
        grid=(DEC_BATCH, NA_HEADS // 2),
        in_specs=[
            tok, tok, tok, cache, cache,
            pl.BlockSpec((1, 2) + rpb_rows.shape[2:], lambda b, hp: (layer, hp, 0, 0)),
        ],
        out_specs=tok,
        out_shape=jax.ShapeDtypeStruct((N_SMP, NA_WIDTH), BF16),
        scratch_shapes=[pltpu.VMEM((2, DEC_SEQ, PAST_LEN), F32),
                        pltpu.VMEM((2, n_pairs, GRID_W, 2 * GRID_W), F32)],
        compiler_params=_params(2),
        name="nbr_attn",
    )(q, k, v, cache_kt, cache_vt, rpb_rows)


def _gla_constants():
    C = GLA_CHUNK
    ii, jj = np.meshgrid(np.arange(C), np.arange(C), indexing="ij")
    tri = np.stack([jj <= ii, jj >= ii]).astype(np.float32)
    w = np.concatenate([tri, tri], axis=-1)

    x = ii ^ jj
    hb = np.where(x > 0, np.floor(np.log2(np.maximum(x, 1))), -1).astype(np.int64)
    masks = np.zeros((2, GLA_LEVELS + 1, C, C), np.float32)
    for p in range(GLA_LEVELS):
        masks[0, p] = (jj < ii) & (hb == p)
        masks[1, p] = (jj > ii) & (hb == p)
    masks[:, GLA_LEVELS] = np.eye(C)
    masks = np.tile(masks, (1, 1, 2, 2))
    state_mask = (np.arange(2 * GLA_DV)[:, None] // GLA_DV == np.arange(2 * GLA_DK)[None, :] // GLA_DK)
    upper = np.stack([(np.arange(C) >> p) & 1 for p in range(GLA_LEVELS)]).astype(bool)
    sign = np.stack([np.where(upper, 1.0, -1.0), np.where(upper, -1.0, 1.0)])
    scales = np.broadcast_to((sign * np.log2(np.e))[..., None], (2, GLA_LEVELS, C, GLA_K_WIDTH))
    return w, masks, state_mask.astype(np.float32), np.ascontiguousarray(scales, np.float32)


def _gla_kernel(seq_len, n_seq, is_ctx, creates_states, layer, *refs):
    refs = list(refs)
    gq_ref, gk_ref, gv_ref, lf_ref, lb_ref, w_ref, mask_ref, smask_ref, scale_ref = refs[:9]
    refs = refs[9:]
    if is_ctx:
        if not creates_states:
            refs = refs[2:]
        o_ref, sf_ref, sb_ref = refs[:3]
        refs = refs[3:]
    else:
        s0f_ref, s0b_ref, o_ref = refs[:3]
        refs = refs[3:]
    st_scr, cum_scr = refs

    C = GLA_CHUNK
    n_chunks = seq_len // C
    n_pairs = GLA_HEADS // 2
    pair_k = 2 * GLA_DK
    pair_v = 2 * GLA_DV

    st_scr[...] = jnp.zeros(st_scr.shape, F32)
    if not is_ctx:
        for d, s0_ref in enumerate((s0f_ref, s0b_ref)):
            for s in range(n_seq):
                for h in range(GLA_HEADS):
                    pr, e = divmod(h, 2)
                    st_scr[s, d, pr, e * GLA_DV:(e + 1) * GLA_DV, e * GLA_DK:(e + 1) * GLA_DK] = s0_ref[s, 0, h].T

    klane = lax.broadcasted_iota(jnp.int32, (1, GLA_K_WIDTH), 1)
    even_head = (klane // GLA_DK) % 2 == 0

    def chunk_decay(s, d, r0, f_ref):
        rows = pl.ds(s * seq_len + r0, C)
        f = f_ref[rows, :]
        f_hi = f.astype(BF16)
        f_lo = (f - f_hi.astype(F32)).astype(BF16)
        cum = _dot(w_ref[d], jnp.concatenate([f_hi, f_lo], axis=0))
        cum_scr[s, d] = cum
        return rows, cum

    def chunk_scores(s, d, rows, cum):
        def cum_rows(r, n):
            return jnp.broadcast_to(cum_scr[s, d, r:r + 1, :], (n, GLA_K_WIDTH))

        def level_factor(p):
            m = 1 << p
            edge = m - 1 if d == 0 else m
            if 2 * m >= SUBLANES:
                ref = jnp.concatenate([cum_rows(blk + edge, 2 * m) for blk in range(0, C, 2 * m)], axis=0)
            elif 2 * m == SUBLANES // 2:
                sub = lax.broadcasted_iota(jnp.int32, (SUBLANES, GLA_K_WIDTH), 0)
                ref = jnp.concatenate([jnp.where(sub < 2 * m, cum_rows(blk + edge, SUBLANES),
                                                 cum_rows(blk + 2 * m + edge, SUBLANES))
                                       for blk in range(0, C, SUBLANES)], axis=0)
            else:
                odd = lax.broadcasted_iota(jnp.int32, (C, GLA_K_WIDTH), 0) % 2 == 1
                if d == 0:
                    ref = jnp.where(odd, pltpu.roll(cum, 1, axis=0), cum)
                else:
                    ref = jnp.where(odd, cum, pltpu.roll(cum, C - 1, axis=0))
            return jnp.exp2((cum - ref) * scale_ref[d, p])

        qc = gq_ref[rows, :]
        kc = gk_ref[rows, :]
        vc = gv_ref[rows, :]
        last = C - 1 if d == 0 else 0
        eq = jnp.exp(cum)
        ek = jnp.exp(cum_rows(last, C) - cum)
        total = jnp.exp(cum_scr[s, d, last:last + 1, :])

        qb = qc.astype(BF16)
        kb = kc.astype(BF16)
        kb_even = jnp.where(even_head, kb, jnp.zeros_like(kb))
        kb_odd = jnp.where(even_head, jnp.zeros_like(kb), kb)
        zero = jnp.zeros((C, pair_k), BF16)
        a = None
        for p in range(GLA_LEVELS + 1):
            if p < GLA_LEVELS:
                ep = level_factor(p).astype(BF16)
                qe, ke_even, ke_odd = qb * ep, kb_even * ep, kb_odd * ep
            else:
                qe, ke_even, ke_odd = qb, kb_even, kb_odd
            lhs = jnp.concatenate([jnp.concatenate([qe[:, :pair_k], zero], axis=1),
                                   jnp.concatenate([zero, qe[:, pair_k:]], axis=1)], axis=0)
            sc = _dot_nt(lhs, jnp.concatenate([ke_even, ke_odd], axis=0)) * mask_ref[d, p]
            a = sc if a is None else a + sc
        q_in = (qc * eq).astype(BF16)
        k_out = (kc * ek).astype(BF16)
        return a.astype(BF16), q_in, k_out, vc, total

    def chunk_output(s, d, rows, accumulate, a, q_in, k_out, vc, total):
        zero_v = jnp.zeros((C, GLA_DV), BF16)
        for pr in range(n_pairs):
            kl = slice(pr * pair_k, (pr + 1) * pair_k)
            vp = vc[:, pr * pair_v:(pr + 1) * pair_v]
            v_diag = jnp.concatenate([jnp.concatenate([vp[:, :GLA_DV], zero_v], axis=1),
                                      jnp.concatenate([zero_v, vp[:, GLA_DV:]], axis=1)], axis=0)
            st = st_scr[s, d, pr]
            o = _dot(a[pr * C:(pr + 1) * C], v_diag) + _dot_nt(q_in[:, kl], st.astype(BF16))
            if accumulate:
                o_ref[rows, pr * pair_v:(pr + 1) * pair_v] += o
            else:
                o_ref[rows, pr * pair_v:(pr + 1) * pair_v] = o
            st_scr[s, d, pr] = st * total[:, kl] + _dot_tn(vp, k_out[:, kl]) * smask_ref[...]

    def body(accumulate, n, carry):
        fwd = pl.multiple_of(n * C, C)
        bwd = pl.multiple_of((n_chunks - 1 - n) * C, C)
        scans = [(s, d) for s in range(n_seq) for d in range(2)]
        decays = [chunk_decay(s, d, bwd if d else fwd, lb_ref if d else lf_ref) for s, d in scans]
        scores = [chunk_scores(s, d, *dec) for (s, d), dec in zip(scans, decays)]
        for (s, d), (rows, _), sc in zip(scans, decays, scores):
            chunk_output(s, d, rows, accumulate, *sc)
        return carry

    half = n_chunks // 2
    lax.fori_loop(0, half, functools.partial(body, False), 0)
    lax.fori_loop(half, n_chunks, functools.partial(body, True), 0)
    if is_ctx:
        slot = layer if creates_states else 0
        for d, s_ref in enumerate((sf_ref, sb_ref)):
            for s in range(n_seq):
                for h in range(GLA_HEADS):
                    pr, e = divmod(h, 2)
                    s_ref[s, slot, h] = st_scr[s, d, pr, e * GLA_DV:(e + 1) * GLA_DV, e * GLA_DK:(e + 1) * GLA_DK].T
            if creates_states:
                for other in range(DEPTH):
                    if other != layer:
                        s_ref[:, other] = jnp.zeros((n_seq, GLA_HEADS, GLA_DK, GLA_DV), F32)


def _gla_call(qkf, gv, consts, new_states=None, state_in=None, layer=0):
    is_ctx = state_in is None
    creates_states = is_ctx and new_states is None
    seq_len = SEQ if is_ctx else DEC_SEQ
    n_tok = qkf.shape[0]
    n_seq = GLA_CTX_SEQS_PER_STEP if is_ctx else DEC_BATCH
    rows = n_seq * seq_len
    tok = lambda width: pl.BlockSpec((rows, width), lambda b: (b, 0))
    whole = lambda a: pl.BlockSpec(a.shape, lambda b: (0,) * a.ndim)
    state_spec = pl.BlockSpec((n_seq, 1, GLA_HEADS, GLA_DK, GLA_DV), lambda b: (b, layer, 0, 0, 0))
    qkf_block = lambda j: pl.BlockSpec((rows, GLA_K_WIDTH), lambda b: (b, j))
    in_specs = [qkf_block(0), qkf_block(1), tok(GLA_V_WIDTH), qkf_block(2), qkf_block(3)]
    in_specs += [whole(a) for a in consts]
    args = [qkf, qkf, gv, qkf, qkf, *consts]
    out_shape = [jax.ShapeDtypeStruct((n_tok, GLA_V_WIDTH), F32)]
    out_specs = [tok(GLA_V_WIDTH)]
    aliases = {}
    if is_ctx:
        states_shape = (BATCH, DEPTH, GLA_HEADS, GLA_DK, GLA_DV)
        if creates_states:
            new_spec = pl.BlockSpec((n_seq,) + states_shape[1:], lambda b: (b, 0, 0, 0, 0))
        else:
            new_spec = state_spec
            aliases = {len(args): 1, len(args) + 1: 2}
            in_specs += [pl.BlockSpec(memory_space=pl.ANY)] * 2
            args += list(new_states)
        out_shape += [jax.ShapeDtypeStruct(states_shape, F32)] * 2
        out_specs += [new_spec, new_spec]
    else:
        in_specs += [state_spec, state_spec]
        args += list(state_in)
    return pl.pallas_call(
        functools.partial(_gla_kernel, seq_len, n_seq, is_ctx, creates_states, layer),
        grid=(n_tok // rows,),
        in_specs=in_specs,
        out_specs=out_specs,
        out_shape=out_shape,
        input_output_aliases=aliases,
        scratch_shapes=[
            pltpu.VMEM((n_seq, 2, GLA_HEADS // 2, 2 * GLA_DV, 2 * GLA_DK), F32),
            pltpu.VMEM((n_seq, 2, GLA_CHUNK, GLA_K_WIDTH), F32),
        ],
        compiler_params=_params(1),
        name="gla_ctx" if is_ctx else "gla_smp",
    )(*args)


def _rope_tables():
    quarter = GLA_DK // 4
    inv = ROPE_BASE ** (-jnp.arange(quarter, dtype=F32) / quarter)
    t = jnp.arange(DEC_SEQ)
    ang_r = (t // GRID_W).astype(F32)[:, None] * inv
    ang_c = (t % GRID_W).astype(F32)[:, None] * inv
    cos = jnp.concatenate([jnp.cos(ang_r)] * 2 + [jnp.cos(ang_c)] * 2, axis=-1)
    sin = jnp.concatenate([-jnp.sin(ang_r), jnp.sin(ang_r), -jnp.sin(ang_c), jnp.sin(ang_c)], axis=-1)
    return jnp.tile(cos, (1, GLA_HEADS)), jnp.tile(sin, (1, GLA_HEADS))


def _merge_mlp_kernel(layer, x_ref, mod_ref, ona_ref, ogla_ref, gate_ref, gout_ref, gmlp_ref,
                      wo_hbm, wup_hbm, wdown_hbm, o_ref,
                      wo_scr, wup_scr, wdown_scr, stage_cols, stage_rows, sem):
    first_step = pl.program_id(0) == 0
    n_chunks = D_FF // FF_CHUNK
    wo_halves = D_MODEL // FF_CHUNK

    def col_copy(j):
        slot = j % 2
        if j < wo_halves:
            src = wo_hbm.at[layer, :, j * FF_CHUNK:(j + 1) * FF_CHUNK]
        else:
            c = j - wo_halves
            src = wup_hbm.at[layer, :, c * FF_CHUNK:(c + 1) * FF_CHUNK]
        return pltpu.make_async_copy(src, stage_cols.at[slot], sem.at[slot])

    def row_copy(c):
        slot = c % 2
        src = wdown_hbm.at[layer, c * FF_CHUNK:(c + 1) * FF_CHUNK, :]
        return pltpu.make_async_copy(src, stage_rows.at[slot], sem.at[2 + slot])

    n_col_blocks = wo_halves + n_chunks

    def land_col(j):
        col_copy(j).wait()
        block = stage_cols[j % 2].astype(BF16)
        if j < wo_halves:
            wo_scr[:, j * FF_CHUNK:(j + 1) * FF_CHUNK] = block
        else:
            c = j - wo_halves
            wup_scr[:, c * FF_CHUNK:(c + 1) * FF_CHUNK] = block
        if j + 2 < n_col_blocks:
            col_copy(j + 2).start()

    def land_row(c):
        row_copy(c).wait()
        wdown_scr[c * FF_CHUNK:(c + 1) * FF_CHUNK, :] = stage_rows[c % 2].astype(BF16)
        if c + 2 < n_chunks:
            row_copy(c + 2).start()

    def run(streaming):
        if streaming:
            col_copy(0).start()
            col_copy(1).start()
            row_copy(0).start()
            row_copy(1).start()

        mod = mod_ref[0]
        ga1 = mod[:, 2 * D_MODEL:3 * D_MODEL]
        sh2 = mod[:, 3 * D_MODEL:4 * D_MODEL]
        sc2 = mod[:, 4 * D_MODEL:5 * D_MODEL]
        ga2 = mod[:, 5 * D_MODEL:6 * D_MODEL]

        og = ogla_ref[...]
        normed = []
        for h in range(GLA_HEADS):
            oh = og[:, h * GLA_DV:(h + 1) * GLA_DV]
            normed.append(oh * lax.rsqrt(jnp.mean(oh * oh, axis=-1, keepdims=True) + EPS))
        gate = gate_ref[...]
        g = jnp.concatenate(normed, axis=1) * gout_ref[...] * (gate * (1.0 / (1.0 + jnp.exp(-gate))))

        if streaming:
            for j in range(wo_halves):
                land_col(j)
        att = _dot(ona_ref[...], wo_scr[0:NA_WIDTH, :]) + _dot(g.astype(BF16), wo_scr[NA_WIDTH:, :])
        x = x_ref[...] + ga1 * att
        ms = jnp.mean(x * x, axis=-1, keepdims=True)
        h2 = ((x * lax.rsqrt(ms + EPS) * gmlp_ref[...]) * (1.0 + sc2) + sh2).astype(BF16)

        width = FF_CHUNK if streaming else MLP_CHUNK
        acc = None
        for c in range(D_FF // width):
            if streaming:
                land_col(wo_halves + c)
                land_row(c)
            u = jnp.maximum(_dot(h2, wup_scr[:, c * width:(c + 1) * width]), 0.0)
            down = _dot((u * u).astype(BF16), wdown_scr[c * width:(c + 1) * width, :])
            acc = down if acc is None else acc + down
        o_ref[...] = x + ga2 * acc

    pl.when(first_step)(functools.partial(run, True))
    pl.when(jnp.logical_not(first_step))(functools.partial(run, False))


def _merge_mlp_call(x, mod_l, o_na, o_gla, gate, gout_t, g_mlp_l, w_o, w_up, w_down, layer):
    tm = TOKEN_TILE
    n_tok = x.shape[0]
    is_ctx = n_tok == N_CTX
    row = lambda i: (i, 0)
    const = lambda i: (0, 0)
    in_hbm = pl.BlockSpec(memory_space=pl.ANY)
    return pl.pallas_call(
        functools.partial(_merge_mlp_kernel, layer),
        grid=(n_tok // tm,),
        in_specs=[
            pl.BlockSpec((tm, D_MODEL), row),
            pl.BlockSpec((1, 1, N_MOD * D_MODEL), _mod_index_map(is_ctx)),
            pl.BlockSpec((tm, NA_WIDTH), row),
            pl.BlockSpec((tm, GLA_V_WIDTH), row),
            pl.BlockSpec((tm, GLA_V_WIDTH), row),
            pl.BlockSpec((1, GLA_V_WIDTH), const),
            pl.BlockSpec((1, D_MODEL), const),
            in_hbm, in_hbm, in_hbm,
        ],
        out_specs=pl.BlockSpec((tm, D_MODEL), row),
        out_shape=jax.ShapeDtypeStruct((n_tok, D_MODEL), F32),
        scratch_shapes=[
            pltpu.VMEM((D_MODEL, D_MODEL), BF16),
            pltpu.VMEM((D_MODEL, D_FF), BF16),
            pltpu.VMEM((D_FF, D_MODEL), BF16),
            pltpu.VMEM((2, D_MODEL, FF_CHUNK), F32),
            pltpu.VMEM((2, FF_CHUNK, D_MODEL), F32),
            pltpu.SemaphoreType.DMA((4,)),
        ],
        compiler_params=_params(1),
        name="merge_mlp_ctx" if is_ctx else "merge_mlp_smp",
    )(x, mod_l, o_na, o_gla, gate, gout_t, g_mlp_l, w_o, w_up, w_down)


def kernel(x_prompt, x_sample, cache_k, cache_v, state_fwd, state_bwd, c, c_ctx, w_ada, b_ada, g_attn, w_in,
           g_q, g_k, rpb, w_gf, b_gf, w_gb, b_gb, g_gla_out, w_o, g_mlp, w_up, w_down):
    x_c = x_prompt.reshape(N_CTX, D_MODEL)
    x_s = x_sample.reshape(N_SMP, D_MODEL)

    c_rows = jnp.concatenate([c_ctx[None, :], c, jnp.zeros((MOD_ROWS - 1 - DEC_BATCH, D_MODEL), F32)], axis=0)
    mods = _ada_call(c_rows, w_ada, b_ada).reshape(DEPTH, MOD_ROWS, 1, N_MOD * D_MODEL)

    w_in_t = jnp.swapaxes(w_in, 1, 2)
    wgate = jnp.zeros((DEPTH, 2 * GLA_GATE_RANK, 2 * GLA_K_WIDTH), F32)
    wgate = wgate.at[:, :GLA_GATE_RANK, :GLA_K_WIDTH].set(w_gf)
    wgate = wgate.at[:, GLA_GATE_RANK:, GLA_K_WIDTH:].set(w_gb).astype(BF16)
    bgate = jnp.concatenate([b_gf, b_gb], axis=-1).reshape(DEPTH, 1, 2 * GLA_K_WIDTH)
    head_of = np.arange(MXU_WIDTH) // NA_HEAD_DIM
    seg = jnp.asarray((head_of[:, None] == head_of[None, :]).astype(np.float32) / NA_HEAD_DIM, BF16)
    gq_t = jnp.tile(g_q, (1, NA_HEADS)).reshape(DEPTH, 1, NA_WIDTH)
    gk_t = jnp.tile(g_k, (1, NA_HEADS)).reshape(DEPTH, 1, NA_WIDTH)
    gout_t = jnp.tile(g_gla_out, (1, GLA_HEADS)).reshape(DEPTH, 1, GLA_V_WIDTH)
    n_dr, n_dc = rpb.shape[2:]
    rpb_rows = jnp.pad(rpb, ((0, 0), (0, 0), (0, -n_dr % SUBLANES), (0, 2 * GRID_W - n_dc)))
    gla_w_np, gla_masks_np, gla_smask_np, gla_scales_np = _gla_constants()
    gla_consts = (jnp.asarray(gla_w_np, BF16), jnp.asarray(gla_masks_np, F32), jnp.asarray(gla_smask_np, F32),
                  jnp.asarray(gla_scales_np, F32))
    rope = _rope_tables()

    cache_kt = jnp.swapaxes(cache_k, 3, 4)
    cache_vt = jnp.swapaxes(cache_v, 3, 4)
    new_kv = None
    new_states = None

    for l in range(DEPTH):
        proj_w = (g_attn[l].reshape(1, D_MODEL), w_in_t, seg, gq_t[l], gk_t[l], wgate[l], bgate[l], rope)
        mlp_w = (gout_t[l], g_mlp[l].reshape(1, D_MODEL), w_o, w_up, w_down)

        o_na, qkf, gv, gate, *new_kv = _proj_call(x_c, mods[l], *proj_w, True, new_kv=new_kv, layer=l)
        o_gla, *new_states = _gla_call(qkf, gv, gla_consts, new_states=new_states, layer=l)
        x_c = _merge_mlp_call(x_c, mods[l], o_na, o_gla, gate, *mlp_w, layer=l)

        q, k, v, qkf, gv, gate = _proj_call(x_s, mods[l], *proj_w, False, layer=l)
        o_na = _na_call(q, k, v, cache_kt, cache_vt, rpb_rows, l)
        (o_gla,) = _gla_call(qkf, gv, gla_consts, state_in=(state_fwd, state_bwd), layer=l)
        x_s = _merge_mlp_call(x_s, mods[l], o_na, o_gla, gate, *mlp_w, layer=l)

    return (x_c.reshape(BATCH, SEQ, D_MODEL), x_s.reshape(DEC_BATCH, DEC_SEQ, D_MODEL),
            jnp.swapaxes(new_kv[0], 3, 4), jnp.swapaxes(new_kv[1], 3, 4), *new_states)
```

```python
import functools

import numpy as np
import jax
import jax.numpy as jnp
from jax import lax
from jax.experimental import pallas as pl
from jax.experimental.pallas import tpu as pltpu

F32 = jnp.float32
BF16 = jnp.bfloat16

D_MODEL = 1024
BATCH = 16
SEQ = 256
DEPTH = 4
DEC_BATCH = 2
DEC_SEQ = 1024
PAST_LEN = 512
GRID_W = 64
GRID_ROWS = DEC_SEQ // GRID_W
NA_WIDTH = D_MODEL // 2
NA_HEAD_DIM = 64
NA_HEADS = NA_WIDTH // NA_HEAD_DIM
NA_WIN_ROWS = 8
NA_WIN_COLS = 16
GLA_V_WIDTH = D_MODEL - NA_WIDTH
GLA_HEADS = 4
GLA_DV = GLA_V_WIDTH // GLA_HEADS
GLA_DK = GLA_DV // 2
GLA_K_WIDTH = GLA_HEADS * GLA_DK
GLA_GATE_RANK = 16
GLA_GATE_TAU = 16.0
GLA_CHUNK = 64
D_FF = 4 * D_MODEL
ROPE_BASE = 10000.0
N_MOD = 6
EPS = 1e-6

IN_WIDTH = 3 * NA_WIDTH + 2 * GLA_K_WIDTH + 2 * GLA_V_WIDTH + 2 * GLA_GATE_RANK
PROJ_WIDE = IN_WIDTH - 2 * GLA_GATE_RANK

N_CTX = BATCH * SEQ
N_SMP = DEC_BATCH * DEC_SEQ
SUBLANES = 8
MXU_WIDTH = 256
MOD_ROWS = SUBLANES
MASK_VALUE = -1e30

TOKEN_TILE = 512
PROJ_BLOCK = 512
FF_CHUNK = 512
STREAM_SLOTS = 3
MLP_CHUNK = 1024
ADA_TILE_N = 1536
GLA_LEVELS = 6
NA_ROW_GROUP = 4
GLA_CTX_SEQS_PER_STEP = 4
VMEM_LIMIT = 52 * 1024 * 1024


def _dot(a, b):
    return jnp.dot(a, b, preferred_element_type=F32)


def _dot_nt(a, b):
    return lax.dot_general(a, b, (((1,), (1,)), ((), ())), preferred_element_type=F32)


def _dot_tn(a, b):
    return lax.dot_general(a, b, (((0,), (0,)), ((), ())), preferred_element_type=F32)


def _params(n_grid_dims=1):
    return pltpu.CompilerParams(dimension_semantics=("arbitrary",) * n_grid_dims,
                                vmem_limit_bytes=VMEM_LIMIT)


def _mod_index_map(is_ctx):
    if is_ctx:
        return lambda i: (0, 0, 0)
    return lambda i: (1 + i // (DEC_SEQ // TOKEN_TILE), 0, 0)


def _ada_kernel(c_ref, w_ref, b_ref, o_ref):
    cv = c_ref[...]
    s = cv * (1.0 / (1.0 + jnp.exp(-cv)))
    o_ref[0] = _dot(s.astype(BF16), w_ref[0].astype(BF16)) + b_ref[0]


def _ada_call(c_rows, w_ada, b_ada):
    n_mod = N_MOD * D_MODEL
    return pl.pallas_call(
        _ada_kernel,
        grid=(DEPTH, n_mod // ADA_TILE_N),
        in_specs=[
            pl.BlockSpec((MOD_ROWS, D_MODEL), lambda l, j: (0, 0)),
            pl.BlockSpec((1, D_MODEL, ADA_TILE_N), lambda l, j: (l, 0, j)),
            pl.BlockSpec((1, 1, ADA_TILE_N), lambda l, j: (l, 0, j)),
        ],
        out_specs=pl.BlockSpec((1, MOD_ROWS, ADA_TILE_N), lambda l, j: (l, 0, j)),
        out_shape=jax.ShapeDtypeStruct((DEPTH, MOD_ROWS, n_mod), F32),
        compiler_params=_params(2),
        name="ada_mod",
    )(c_rows, w_ada, b_ada.reshape(DEPTH, 1, n_mod))


def _ctx_attention(q, k, v):
    n_seq = q.shape[0] // SEQ
    lane = lax.broadcasted_iota(jnp.int32, (1, 2 * NA_HEAD_DIM), 1)
    first = lane < NA_HEAD_DIM
    pairs = [(slice(s * SEQ, (s + 1) * SEQ), slice(hp * 2 * NA_HEAD_DIM, (hp + 1) * 2 * NA_HEAD_DIM))
             for s in range(n_seq) for hp in range(NA_HEADS // 2)]
    scores = []
    for rows, cols in pairs:
        q2 = q[rows, cols]
        k2 = k[rows, cols]
        scores.append(_dot_nt(jnp.where(first, q2, jnp.zeros_like(q2)), k2))
        scores.append(_dot_nt(jnp.where(first, jnp.zeros_like(q2), q2), k2))
    probs = []
    for s in scores:
        p = jnp.exp(s - jnp.max(s, axis=-1, keepdims=True))
        probs.append((p.astype(BF16), jnp.sum(p, axis=-1, keepdims=True)))
    outs = []
    for i, (rows, cols) in enumerate(pairs):
        v2 = v[rows, cols]
        (p0, l0), (p1, l1) = probs[2 * i], probs[2 * i + 1]
        outs.append(jnp.where(first, _dot(p0, v2) / l0, _dot(p1, v2) / l1))
    per_seq = NA_HEADS // 2
    return jnp.concatenate([jnp.concatenate(outs[s * per_seq:(s + 1) * per_seq], axis=1) for s in range(n_seq)],
                           axis=0)


def _log_sigmoid(x):
    return jnp.minimum(x, 0.0) - jnp.log(1.0 + jnp.exp(-jnp.abs(x)))


def _proj_kernel(is_ctx, creates_kv, layer, x_ref, mod_ref, g_ref, seg_ref, gq_ref, gk_ref, wgate_ref, bgate_ref,
                 cos_ref, sin_ref, win_hbm, *rest):
    if is_ctx:
        if not creates_kv:
            rest = rest[2:]
        ona_out = rest[0]
        rest = rest[1:]
    else:
        q_out, k_out, v_out = rest[:3]
        rest = rest[3:]
    qkf_out, gv_out, gate_out = rest[:3]
    rest = rest[3:]
    if is_ctx:
        knew_out, vnew_out = rest[:2]
        rest = rest[2:]
    w_scr, stage, tail_stage, sem = rest

    n_blocks = PROJ_WIDE // PROJ_BLOCK
    tail = slice(PROJ_WIDE, IN_WIDTH)
    first_step = pl.program_id(0) == 0

    def block_copy(j):
        src = win_hbm.at[layer, j * PROJ_BLOCK:(j + 1) * PROJ_BLOCK, :]
        return pltpu.make_async_copy(src, stage.at[j % 2], sem.at[j % 2])

    def tail_copy():
        return pltpu.make_async_copy(win_hbm.at[layer, tail, :], tail_stage, sem.at[2])

    def land(j):
        block_copy(j).wait()
        w_scr[j * PROJ_BLOCK:(j + 1) * PROJ_BLOCK, :] = stage[j % 2].astype(BF16)
        if j + 2 < n_blocks:
            block_copy(j + 2).start()

    def run(streaming):
        if streaming:
            block_copy(0).start()
            block_copy(1).start()
            tail_copy().start()

        x = x_ref[...]
        mod = mod_ref[0]
        sh1 = mod[:, 0:D_MODEL]
        sc1 = mod[:, D_MODEL:2 * D_MODEL]
        ms = jnp.mean(x * x, axis=-1, keepdims=True)
        h = (x * lax.rsqrt(ms + EPS) * g_ref[...]) * (1.0 + sc1) + sh1
        hb = h.astype(BF16)

        def projected(j):
            if streaming:
                land(j)
            return _dot_nt(hb, w_scr[j * PROJ_BLOCK:(j + 1) * PROJ_BLOCK, :])

        seg = seg_ref[...]

        def head_mean_square(t):
            sq = (t * t).astype(BF16)
            return jnp.concatenate([_dot(sq[:, c:c + MXU_WIDTH], seg) for c in range(0, NA_WIDTH, MXU_WIDTH)],
                                   axis=1)

        q = projected(0)
        q_ms = head_mean_square(q)
        qb = (q * lax.rsqrt(q_ms + EPS) * gq_ref[...] * (NA_HEAD_DIM ** -0.5)).astype(BF16)
        k = projected(1)
        k_ms = head_mean_square(k)
        kn = k * lax.rsqrt(k_ms + EPS) * gk_ref[...]
        v = projected(2)
        if not is_ctx:
            q_out[...] = qb
            k_out[...] = kn.astype(BF16)
            v_out[...] = v.astype(BF16)
        else:
            ona_out[...] = _ctx_attention(qb, kn.astype(BF16), v.astype(BF16)).astype(BF16)
            slot = layer if creates_kv else 0
            for s in range(TOKEN_TILE // SEQ):
                rows = slice(s * SEQ, (s + 1) * SEQ)
                for hp in range(NA_HEADS // 2):
                    cols = slice(hp * 2 * NA_HEAD_DIM, (hp + 1) * 2 * NA_HEAD_DIM)
                    for out, t in ((knew_out, kn[rows, cols].T), (vnew_out, v[rows, cols].T)):
                        out[s, slot, 2 * hp] = t[:NA_HEAD_DIM]
                        out[s, slot, 2 * hp + 1] = t[NA_HEAD_DIM:]
            if creates_kv:
                for out in (knew_out, vnew_out):
                    for other in range(DEPTH):
                        if other != layer:
                            out[:, other] = jnp.zeros((TOKEN_TILE // SEQ, NA_HEADS, NA_HEAD_DIM, SEQ), F32)

        z = projected(3)
        gla_q = z[:, 0:GLA_K_WIDTH] * (GLA_DK ** -0.5)
        gla_k = z[:, GLA_K_WIDTH:]
        if not is_ctx:
            lane = lax.broadcasted_iota(jnp.int32, (1, GLA_K_WIDTH), 1)
            quarter = GLA_DK // 4
            low = (lane % (2 * quarter)) < quarter
            cos = cos_ref[...]
            sin = sin_ref[...]

            def rope(t):
                partner = jnp.where(low, pltpu.roll(t, GLA_K_WIDTH - quarter, axis=1),
                                    pltpu.roll(t, quarter, axis=1))
                return t * cos + partner * sin

            gla_q = rope(gla_q)
            gla_k = rope(gla_k)
        qkf_out[:, 0:GLA_K_WIDTH] = gla_q
        qkf_out[:, GLA_K_WIDTH:2 * GLA_K_WIDTH] = gla_k
        gv_out[...] = projected(4).astype(BF16)
        gate_out[...] = projected(5)

        if streaming:
            tail_copy().wait()
            w_scr[tail, :] = tail_stage[...].astype(BF16)
        zz_t = _dot_nt(w_scr[tail, :], hb)
        pre = _dot_tn(zz_t.astype(BF16), wgate_ref[...]) + bgate_ref[...]
        ls = _log_sigmoid(pre) * (1.0 / GLA_GATE_TAU)
        qkf_out[:, 2 * GLA_K_WIDTH:] = ls

    pl.when(first_step)(functools.partial(run, True))
    pl.when(jnp.logical_not(first_step))(functools.partial(run, False))


def _proj_call(x, mod_l, g_attn_l, w_in_t, seg, gq_t, gk_t, wgate, bgate, rope, is_ctx, new_kv=None, layer=0):
    tm = TOKEN_TILE
    n_tok = x.shape[0]
    creates_kv = is_ctx and new_kv is None
    row = lambda i: (i, 0)
    const = lambda i: (0, 0)
    rope_rows = const if is_ctx else (lambda i: (i % (DEC_SEQ // tm), 0))
    if is_ctx:
        out_shapes = [jax.ShapeDtypeStruct((n_tok, NA_WIDTH), BF16)]
    else:
        out_shapes = [jax.ShapeDtypeStruct((n_tok, NA_WIDTH), BF16)] * 3
    out_shapes += [
        jax.ShapeDtypeStruct((n_tok, 4 * GLA_K_WIDTH), F32),
        jax.ShapeDtypeStruct((n_tok, GLA_V_WIDTH), BF16),
        jax.ShapeDtypeStruct((n_tok, GLA_V_WIDTH), F32),
    ]
    out_specs = [pl.BlockSpec((tm, s.shape[1]), row) for s in out_shapes]
    in_specs = [
        pl.BlockSpec((tm, D_MODEL), row),
        pl.BlockSpec((1, 1, N_MOD * D_MODEL), _mod_index_map(is_ctx)),
        pl.BlockSpec((1, D_MODEL), const),
        pl.BlockSpec((MXU_WIDTH, MXU_WIDTH), const),
        pl.BlockSpec((1, NA_WIDTH), const),
        pl.BlockSpec((1, NA_WIDTH), const),
        pl.BlockSpec(wgate.shape, const),
        pl.BlockSpec((1, 2 * GLA_K_WIDTH), const),
        pl.BlockSpec((tm, GLA_K_WIDTH), rope_rows),
        pl.BlockSpec((tm, GLA_K_WIDTH), rope_rows),
        pl.BlockSpec(memory_space=pl.ANY),
    ]
    args = [x, mod_l, g_attn_l, seg, gq_t, gk_t, wgate, bgate, *rope, w_in_t]
    aliases = {}
    if is_ctx:
        kv_shape = (BATCH, DEPTH, NA_HEADS, NA_HEAD_DIM, SEQ)
        if creates_kv:
            kv_spec = pl.BlockSpec((tm // SEQ,) + kv_shape[1:], lambda i: (i, 0, 0, 0, 0))
        else:
            kv_spec = pl.BlockSpec((tm // SEQ, 1) + kv_shape[2:], lambda i: (i, layer, 0, 0, 0))
            aliases = {len(args): len(out_shapes), len(args) + 1: len(out_shapes) + 1}
            in_specs += [pl.BlockSpec(memory_space=pl.ANY)] * 2
            args += list(new_kv)
        out_shapes += [jax.ShapeDtypeStruct(kv_shape, F32)] * 2
        out_specs += [kv_spec, kv_spec]
    return pl.pallas_call(
        functools.partial(_proj_kernel, is_ctx, creates_kv, layer),
        grid=(n_tok // tm,),
        in_specs=in_specs,
        out_specs=out_specs,
        out_shape=out_shapes,
        input_output_aliases=aliases,
        scratch_shapes=[
            pltpu.VMEM((IN_WIDTH, D_MODEL), BF16),
            pltpu.VMEM((2, PROJ_BLOCK, D_MODEL), F32),
            pltpu.VMEM((IN_WIDTH - PROJ_WIDE, D_MODEL), F32),
            pltpu.SemaphoreType.DMA((3,)),
        ],
        compiler_params=_params(1),
        name="proj_ctx" if is_ctx else "proj_smp",
    )(*args)


def _na_window_start(r):
    return min(max(r - NA_WIN_ROWS // 2, 0), GRID_ROWS - NA_WIN_ROWS)


def _na_row_groups():
    groups = []
    for g in range(GRID_ROWS // NA_ROW_GROUP):
        starts = [_na_window_start(r) for r in range(g * NA_ROW_GROUP, (g + 1) * NA_ROW_GROUP)]
        lo = min(starts) // 2 * 2
        count = max(starts) + NA_WIN_ROWS - lo
        groups.append((lo, count + count % 2))
    return groups


def _na_bias_tiles(rpb_ref, hh, tiles_scr):
    n_dr = 2 * NA_WIN_ROWS - 1
    lanes = 2 * GRID_W
    lane = lax.broadcasted_iota(jnp.int32, (GRID_W, lanes), 1)
    col = lax.broadcasted_iota(jnp.int32, (GRID_W, lanes), 0)
    first = lane < GRID_W
    c2 = lane % GRID_W
    cs = jnp.clip(col - NA_WIN_COLS // 2, 0, GRID_W - NA_WIN_COLS)
    valid = (c2 >= cs) & (c2 < cs + NA_WIN_COLS)
    rows = rpb_ref[0, hh]
    base = lanes - (NA_WIN_COLS - 1)

    def skew(dr, shift):
        row = jnp.broadcast_to(rows[dr:dr + 1], (GRID_W, lanes))
        return pltpu.roll(row, shift, axis=1, stride=1, stride_axis=0)

    for e in range(n_dr - 1):
        tile = jnp.where(first, skew(e, base), skew(e + 1, (base + GRID_W) % lanes))
        tiles_scr[e] = jnp.where(valid, tile, MASK_VALUE)


def _na_group_bias(tiles_scr, g, key_lo, key_count, first):
    masked = jnp.full((GRID_W, 2 * GRID_W), MASK_VALUE, F32)
    row_tiles = []
    for r in range(g * NA_ROW_GROUP, (g + 1) * NA_ROW_GROUP):
        rs = _na_window_start(r)
        pieces = []
        for u in range(key_count // 2):
            r0 = key_lo + 2 * u
            in0 = rs <= r0 < rs + NA_WIN_ROWS
            in1 = rs <= r0 + 1 < rs + NA_WIN_ROWS
            if not (in0 or in1):
                pieces.append(masked)
                continue
            tile = tiles_scr[r0 - r + NA_WIN_ROWS - 1]
            if not in1:
                tile = jnp.where(first, tile, MASK_VALUE)
            elif not in0:
                tile = jnp.where(first, MASK_VALUE, tile)
            pieces.append(tile)
        row_tiles.append(jnp.concatenate(pieces, axis=1))
    return jnp.concatenate(row_tiles, axis=0)


def _na_kernel(q_ref, k_ref, v_ref, kc_ref, vc_ref, rpb_ref, o_ref, sc_ref, tiles_scr):
    lane = lax.broadcasted_iota(jnp.int32, (1, 2 * NA_HEAD_DIM), 1)
    first = lane < NA_HEAD_DIM
    q2 = q_ref[...]
    kc2 = jnp.concatenate([kc_ref[0, 0, 0], kc_ref[0, 0, 1]], axis=0).astype(BF16)
    vc2 = jnp.concatenate([vc_ref[0, 0, 0], vc_ref[0, 0, 1]], axis=0).astype(BF16)
    block = NA_ROW_GROUP * GRID_W
    groups = _na_row_groups()
    qms = [jnp.where(first, q2, jnp.zeros_like(q2)), jnp.where(first, jnp.zeros_like(q2), q2)]
    for hh in range(2):
        sc_ref[hh] = _dot(qms[hh], kc2)
        _na_bias_tiles(rpb_ref, hh, tiles_scr.at[hh])
    problems = [(hh, g) for hh in range(2) for g in range(len(groups))]
    scores = []
    for hh, g in problems:
        key_lo, key_count = groups[g]
        keys = slice(key_lo * GRID_W, (key_lo + key_count) * GRID_W)
        s = _dot_nt(qms[hh][g * block:(g + 1) * block], k_ref[keys, :])
        scores.append(s + _na_group_bias(tiles_scr.at[hh], g, key_lo, key_count, first))
    probs = []
    for (hh, g), s in zip(problems, scores):
        sc = sc_ref[hh, g * block:(g + 1) * block, :]
        m = jnp.maximum(jnp.max(s, axis=-1, keepdims=True), jnp.max(sc, axis=-1, keepdims=True))
        p = jnp.exp(s - m)
        pc = jnp.exp(sc - m)
        l = jnp.sum(p, axis=-1, keepdims=True) + jnp.sum(pc, axis=-1, keepdims=True)
        probs.append((p.astype(BF16), pc.astype(BF16), l))
    outs = [[], []]
    for (hh, g), (p, pc, l) in zip(problems, probs):
        key_lo, key_count = groups[g]
        keys = slice(key_lo * GRID_W, (key_lo + key_count) * GRID_W)
        outs[hh].append((_dot(p, v_ref[keys, :]) + _dot_nt(pc, vc2)) / l)
    outs = [jnp.concatenate(o, axis=0) for o in outs]
    o_ref[...] = jnp.where(first, outs[0], outs[1]).astype(BF16)


def _na_call(q, k, v, cache_kt, cache_vt, rpb_rows, layer):
    tok = pl.BlockSpec((DEC_SEQ, 2 * NA_HEAD_DIM), lambda b, hp: (b, hp))
    cache = pl.BlockSpec((1, 1, 2, NA_HEAD_DIM, PAST_LEN), lambda b, hp: (b, layer, hp, 0, 0))
    n_pairs = 2 * NA_WIN_ROWS - 2
    return pl.pallas_call(
        _na_kernel,
        grid=(DEC_BATCH, NA_HEADS // 2),
        in_specs=[
            tok, tok, tok, cache, cache,
            pl.BlockSpec((1, 2) + rpb_rows.shape[2:], lambda b, hp: (layer, hp, 0, 0)),
        ],
        out_specs=tok,
        out_shape=jax.ShapeDtypeStruct((N_SMP, NA_WIDTH), BF16),
        scratch_shapes=[pltpu.VMEM((2, DEC_SEQ, PAST_LEN), F32),
                        pltpu.VMEM((2, n_pairs, GRID_W, 2 * GRID_W), F32)],
        compiler_params=_params(2),
        name="nbr_attn",
    )(q, k, v, cache_kt, cache_vt, rpb_rows)


def _gla_constants():
    C = GLA_CHUNK
    ii, jj = np.meshgrid(np.arange(C), np.arange(C), indexing="ij")
    tri = np.stack([jj <= ii, jj >= ii]).astype(np.float32)
    w = np.concatenate([tri, tri], axis=-1)

    x = ii ^ jj
    hb = np.where(x > 0, np.floor(np.log2(np.maximum(x, 1))), -1).astype(np.int64)
    masks = np.zeros((2, GLA_LEVELS + 1, C, C), np.float32)
    for p in range(GLA_LEVELS):
        masks[0, p] = (jj < ii) & (hb == p)
        masks[1, p] = (jj > ii) & (hb == p)
    masks[:, GLA_LEVELS] = np.eye(C)
    masks = np.tile(masks, (1, 1, 2, 2))
    state_mask = (np.arange(2 * GLA_DV)[:, None] // GLA_DV == np.arange(2 * GLA_DK)[None, :] // GLA_DK)
    upper = np.stack([(np.arange(C) >> p) & 1 for p in range(GLA_LEVELS)]).astype(bool)
    sign = np.stack([np.where(upper, 1.0, -1.0), np.where(upper, -1.0, 1.0)])
    scales = np.broadcast_to((sign * np.log2(np.e))[..., None], (2, GLA_LEVELS, C, GLA_K_WIDTH))
    return w, masks, state_mask.astype(np.float32), np.ascontiguousarray(scales, np.float32)


def _gla_kernel(seq_len, n_seq, is_ctx, creates_states, layer, *refs):
    refs = list(refs)
    gq_ref, gk_ref, gv_ref, lf_ref, lb_ref, w_ref, mask_ref, smask_ref, scale_ref = refs[:9]
    refs = refs[9:]
    if is_ctx:
        if not creates_states:
            refs = refs[2:]
        o_ref, sf_ref, sb_ref = refs[:3]
        refs = refs[3:]
    else:
        s0f_ref, s0b_ref, o_ref = refs[:3]
        refs = refs[3:]
    st_scr, cum_scr = refs

    C = GLA_CHUNK
    n_chunks = seq_len // C
    n_pairs = GLA_HEADS // 2
    pair_k = 2 * GLA_DK
    pair_v = 2 * GLA_DV

    st_scr[...] = jnp.zeros(st_scr.shape, F32)
    if not is_ctx:
        for d, s0_ref in enumerate((s0f_ref, s0b_ref)):
            for s in range(n_seq):
                for h in range(GLA_HEADS):
                    pr, e = divmod(h, 2)
                    st_scr[s, d, pr, e * GLA_DV:(e + 1) * GLA_DV, e * GLA_DK:(e + 1) * GLA_DK] = s0_ref[s, 0, h].T

    klane = lax.broadcasted_iota(jnp.int32, (1, GLA_K_WIDTH), 1)
    even_head = (klane // GLA_DK) % 2 == 0

    def chunk_decay(s, d, r0, f_ref):
        rows = pl.ds(s * seq_len + r0, C)
        f = f_ref[rows, :]
        f_hi = f.astype(BF16)
        f_lo = (f - f_hi.astype(F32)).astype(BF16)
        cum = _dot(w_ref[d], jnp.concatenate([f_hi, f_lo], axis=0))
        cum_scr[s, d] = cum
        return rows, cum

    def chunk_scores(s, d, rows, cum):
        def cum_rows(r, n):
            return jnp.broadcast_to(cum_scr[s, d, r:r + 1, :], (n, GLA_K_WIDTH))

        def level_factor(p):
            m = 1 << p
            edge = m - 1 if d == 0 else m
            if 2 * m >= SUBLANES:
                ref = jnp.concatenate([cum_rows(blk + edge, 2 * m) for blk in range(0, C, 2 * m)], axis=0)
            elif 2 * m == SUBLANES // 2:
                sub = lax.broadcasted_iota(jnp.int32, (SUBLANES, GLA_K_WIDTH), 0)
                ref = jnp.concatenate([jnp.where(sub < 2 * m, cum_rows(blk + edge, SUBLANES),
                                                 cum_rows(blk + 2 * m + edge, SUBLANES))
                                       for blk in range(0, C, SUBLANES)], axis=0)
            else:
                odd = lax.broadcasted_iota(jnp.int32, (C, GLA_K_WIDTH), 0) % 2 == 1
                if d == 0:
                    ref = jnp.where(odd, pltpu.roll(cum, 1, axis=0), cum)
                else:
                    ref = jnp.where(odd, cum, pltpu.roll(cum, C - 1, axis=0))
            return jnp.exp2((cum - ref) * scale_ref[d, p])

        qc = gq_ref[rows, :]
        kc = gk_ref[rows, :]
        vc = gv_ref[rows, :]
        last = C - 1 if d == 0 else 0
        eq = jnp.exp(cum)
        ek = jnp.exp(cum_rows(last, C) - cum)
        total = jnp.exp(cum_scr[s, d, last:last + 1, :])

        qb = qc.astype(BF16)
        kb = kc.astype(BF16)
        kb_even = jnp.where(even_head, kb, jnp.zeros_like(kb))
        kb_odd = jnp.where(even_head, jnp.zeros_like(kb), kb)
        zero = jnp.zeros((C, pair_k), BF16)
        a = None
        for p in range(GLA_LEVELS + 1):
            if p < GLA_LEVELS:
                ep = level_factor(p).astype(BF16)
                qe, ke_even, ke_odd = qb * ep, kb_even * ep, kb_odd * ep
            else:
                qe, ke_even, ke_odd = qb, kb_even, kb_odd
            lhs = jnp.concatenate([jnp.concatenate([qe[:, :pair_k], zero], axis=1),
                                   jnp.concatenate([zero, qe[:, pair_k:]], axis=1)], axis=0)
            sc = _dot_nt(lhs, jnp.concatenate([ke_even, ke_odd], axis=0)) * mask_ref[d, p]
            a = sc if a is None else a + sc
        q_in = (qc * eq).astype(BF16)
        k_out = (kc * ek).astype(BF16)
        return a.astype(BF16), q_in, k_out, vc, total

    def chunk_output(s, d, rows, accumulate, a, q_in, k_out, vc, total):
        zero_v = jnp.zeros((C, GLA_DV), BF16)
        for pr in range(n_pairs):
            kl = slice(pr * pair_k, (pr + 1) * pair_k)
            vp = vc[:, pr * pair_v:(pr + 1) * pair_v]
            v_diag = jnp.concatenate([jnp.concatenate([vp[:, :GLA_DV], zero_v], axis=1),
                                      jnp.concatenate([zero_v, vp[:, GLA_DV:]], axis=1)], axis=0)
            st = st_scr[s, d, pr]
            o = _dot(a[pr * C:(pr + 1) * C], v_diag) + _dot_nt(q_in[:, kl], st.astype(BF16))
            if accumulate:
                o_ref[rows, pr * pair_v:(pr + 1) * pair_v] += o
            else:
                o_ref[rows, pr * pair_v:(pr + 1) * pair_v] = o
            st_scr[s, d, pr] = st * total[:, kl] + _dot_tn(vp, k_out[:, kl]) * smask_ref[...]

    def body(accumulate, n, carry):
        fwd = pl.multiple_of(n * C, C)
        bwd = pl.multiple_of((n_chunks - 1 - n) * C, C)
        scans = [(s, d) for s in range(n_seq) for d in range(2)]
        decays = [chunk_decay(s, d, bwd if d else fwd, lb_ref if d else lf_ref) for s, d in scans]
        scores = [chunk_scores(s, d, *dec) for (s, d), dec in zip(scans, decays)]
        for (s, d), (rows, _), sc in zip(scans, decays, scores):
            chunk_output(s, d, rows, accumulate, *sc)
        return carry

    half = n_chunks // 2
    lax.fori_loop(0, half, functools.partial(body, False), 0)
    lax.fori_loop(half, n_chunks, functools.partial(body, True), 0)
    if is_ctx:
        slot = layer if creates_states else 0
        for d, s_ref in enumerate((sf_ref, sb_ref)):
            for s in range(n_seq):
                for h in range(GLA_HEADS):
                    pr, e = divmod(h, 2)
                    s_ref[s, slot, h] = st_scr[s, d, pr, e * GLA_DV:(e + 1) * GLA_DV, e * GLA_DK:(e + 1) * GLA_DK].T
            if creates_states:
                for other in range(DEPTH):
                    if other != layer:
                        s_ref[:, other] = jnp.zeros((n_seq, GLA_HEADS, GLA_DK, GLA_DV), F32)


def _gla_call(qkf, gv, consts, new_states=None, state_in=None, layer=0):
    is_ctx = state_in is None
    creates_states = is_ctx and new_states is None
    seq_len = SEQ if is_ctx else DEC_SEQ
    n_tok = qkf.shape[0]
    n_seq = GLA_CTX_SEQS_PER_STEP if is_ctx else DEC_BATCH
    rows = n_seq * seq_len
    tok = lambda width: pl.BlockSpec((rows, width), lambda b: (b, 0))
    whole = lambda a: pl.BlockSpec(a.shape, lambda b: (0,) * a.ndim)
    state_spec = pl.BlockSpec((n_seq, 1, GLA_HEADS, GLA_DK, GLA_DV), lambda b: (b, layer, 0, 0, 0))
    qkf_block = lambda j: pl.BlockSpec((rows, GLA_K_WIDTH), lambda b: (b, j))
    in_specs = [qkf_block(0), qkf_block(1), tok(GLA_V_WIDTH), qkf_block(2), qkf_block(3)]
    in_specs += [whole(a) for a in consts]
    args = [qkf, qkf, gv, qkf, qkf, *consts]
    out_shape = [jax.ShapeDtypeStruct((n_tok, GLA_V_WIDTH), F32)]
    out_specs = [tok(GLA_V_WIDTH)]
    aliases = {}
    if is_ctx:
        states_shape = (BATCH, DEPTH, GLA_HEADS, GLA_DK, GLA_DV)
        if creates_states:
            new_spec = pl.BlockSpec((n_seq,) + states_shape[1:], lambda b: (b, 0, 0, 0, 0))
        else:
            new_spec = state_spec
            aliases = {len(args): 1, len(args) + 1: 2}
            in_specs += [pl.BlockSpec(memory_space=pl.ANY)] * 2
            args += list(new_states)
        out_shape += [jax.ShapeDtypeStruct(states_shape, F32)] * 2
        out_specs += [new_spec, new_spec]
    else:
        in_specs += [state_spec, state_spec]
        args += list(state_in)
    return pl.pallas_call(
        functools.partial(_gla_kernel, seq_len, n_seq, is_ctx, creates_states, layer),
        grid=(n_tok // rows,),
        in_specs=in_specs,
        out_specs=out_specs,
        out_shape=out_shape,
        input_output_aliases=aliases,
        scratch_shapes=[
            pltpu.VMEM((n_seq, 2, GLA_HEADS // 2, 2 * GLA_DV, 2 * GLA_DK), F32),
            pltpu.VMEM((n_seq, 2, GLA_CHUNK, GLA_K_WIDTH), F32),
        ],
        compiler_params=_params(1),
        name="gla_ctx" if is_ctx else "gla_smp",
    )(*args)


def _rope_tables():
    quarter = GLA_DK // 4
    inv = ROPE_BASE ** (-jnp.arange(quarter, dtype=F32) / quarter)
    t = jnp.arange(DEC_SEQ)
    ang_r = (t // GRID_W).astype(F32)[:, None] * inv
    ang_c = (t % GRID_W).astype(F32)[:, None] * inv
    cos = jnp.concatenate([jnp.cos(ang_r)] * 2 + [jnp.cos(ang_c)] * 2, axis=-1)
    sin = jnp.concatenate([-jnp.sin(ang_r), jnp.sin(ang_r), -jnp.sin(ang_c), jnp.sin(ang_c)], axis=-1)
    return jnp.tile(cos, (1, GLA_HEADS)), jnp.tile(sin, (1, GLA_HEADS))


def _merge_mlp_kernel(layer, x_ref, mod_ref, ona_ref, ogla_ref, gate_ref, gout_ref, gmlp_ref,
                      wo_hbm, wup_hbm, wdown_hbm, o_ref,
                      wo_scr, wup_scr, wdown_scr, stage_cols, stage_rows, sem):
    first_step = pl.program_id(0) == 0
    n_chunks = D_FF // FF_CHUNK
    wo_halves = D_MODEL // FF_CHUNK

    def col_copy(j):
        slot = j % STREAM_SLOTS
        if j < wo_halves:
            src = wo_hbm.at[layer, :, j * FF_CHUNK:(j + 1) * FF_CHUNK]
        else:
            c = j - wo_halves
            src = wup_hbm.at[layer, :, c * FF_CHUNK:(c + 1) * FF_CHUNK]
        return pltpu.make_async_copy(src, stage_cols.at[slot], sem.at[slot])

    def row_copy(c):
        slot = c % STREAM_SLOTS
        src = wdown_hbm.at[layer, c * FF_CHUNK:(c + 1) * FF_CHUNK, :]
        return pltpu.make_async_copy(src, stage_rows.at[slot], sem.at[STREAM_SLOTS + slot])

    n_col_blocks = wo_halves + n_chunks

    def cast_col(j):
        block = stage_cols[j % STREAM_SLOTS].astype(BF16)
        if j < wo_halves:
            wo_scr[:, j * FF_CHUNK:(j + 1) * FF_CHUNK] = block
        else:
            c = j - wo_halves
            wup_scr[:, c * FF_CHUNK:(c + 1) * FF_CHUNK] = block

    def cast_row(c):
        wdown_scr[c * FF_CHUNK:(c + 1) * FF_CHUNK, :] = stage_rows[c % STREAM_SLOTS].astype(BF16)

    def refill_col(j):
        if j + STREAM_SLOTS < n_col_blocks:
            col_copy(j + STREAM_SLOTS).start()

    def refill_row(c):
        if c + STREAM_SLOTS < n_chunks:
            row_copy(c + STREAM_SLOTS).start()

    def run(streaming):
        if streaming:
            for j in range(STREAM_SLOTS):
                col_copy(j).start()
            for c in range(STREAM_SLOTS):
                row_copy(c).start()

        mod = mod_ref[0]
        ga1 = mod[:, 2 * D_MODEL:3 * D_MODEL]
        sh2 = mod[:, 3 * D_MODEL:4 * D_MODEL]
        sc2 = mod[:, 4 * D_MODEL:5 * D_MODEL]
        ga2 = mod[:, 5 * D_MODEL:6 * D_MODEL]

        og = ogla_ref[...]
        normed = []
        for h in range(GLA_HEADS):
            oh = og[:, h * GLA_DV:(h + 1) * GLA_DV]
            normed.append(oh * lax.rsqrt(jnp.mean(oh * oh, axis=-1, keepdims=True) + EPS))
        gate = gate_ref[...]
        g = jnp.concatenate(normed, axis=1) * gout_ref[...] * (gate * (1.0 / (1.0 + jnp.exp(-gate))))

        if streaming:
            for j in range(wo_halves):
                col_copy(j).wait()
                cast_col(j)
                refill_col(j)
        att = _dot(ona_ref[...], wo_scr[0:NA_WIDTH, :]) + _dot(g.astype(BF16), wo_scr[NA_WIDTH:, :])
        x = x_ref[...] + ga1 * att
        ms = jnp.mean(x * x, axis=-1, keepdims=True)
        h2 = ((x * lax.rsqrt(ms + EPS) * gmlp_ref[...]) * (1.0 + sc2) + sh2).astype(BF16)

        width = FF_CHUNK if streaming else MLP_CHUNK
        acc = None
        for c in range(D_FF // width):
            if streaming:
                ahead = [a for a in ((0, 1) if c == 0 else (c + 1,)) if a < n_chunks]
                for a in ahead:
                    col_copy(wo_halves + a).wait()
                    row_copy(a).wait()
                for a in ahead:
                    cast_col(wo_halves + a)
                    cast_row(a)
            u = jnp.maximum(_dot(h2, wup_scr[:, c * width:(c + 1) * width]), 0.0)
            down = _dot((u * u).astype(BF16), wdown_scr[c * width:(c + 1) * width, :])
            acc = down if acc is None else acc + down
            if streaming:
                for a in ahead:
                    refill_col(wo_halves + a)
                    refill_row(a)
        o_ref[...] = x + ga2 * acc

    pl.when(first_step)(functools.partial(run, True))
    pl.when(jnp.logical_not(first_step))(functools.partial(run, False))


def _merge_mlp_call(x, mod_l, o_na, o_gla, gate, gout_t, g_mlp_l, w_o, w_up, w_down, layer):
    tm = TOKEN_TILE
    n_tok = x.shape[0]
    is_ctx = n_tok == N_CTX
    row = lambda i: (i, 0)
    const = lambda i: (0, 0)
    in_hbm = pl.BlockSpec(memory_space=pl.ANY)
    return pl.pallas_call(
        functools.partial(_merge_mlp_kernel, layer),
        grid=(n_tok // tm,),
        in_specs=[
            pl.BlockSpec((tm, D_MODEL), row),
            pl.BlockSpec((1, 1, N_MOD * D_MODEL), _mod_index_map(is_ctx)),
            pl.BlockSpec((tm, NA_WIDTH), row),
            pl.BlockSpec((tm, GLA_V_WIDTH), row),
            pl.BlockSpec((tm, GLA_V_WIDTH), row),
            pl.BlockSpec((1, GLA_V_WIDTH), const),
            pl.BlockSpec((1, D_MODEL), const),
            in_hbm, in_hbm, in_hbm,
        ],
        out_specs=pl.BlockSpec((tm, D_MODEL), row),
        out_shape=jax.ShapeDtypeStruct((n_tok, D_MODEL), F32),
        scratch_shapes=[
            pltpu.VMEM((D_MODEL, D_MODEL), BF16),
            pltpu.VMEM((D_MODEL, D_FF), BF16),
            pltpu.VMEM((D_FF, D_MODEL), BF16),
            pltpu.VMEM((STREAM_SLOTS, D_MODEL, FF_CHUNK), F32),
            pltpu.VMEM((STREAM_SLOTS, FF_CHUNK, D_MODEL), F32),
            pltpu.SemaphoreType.DMA((2 * STREAM_SLOTS,)),
        ],
        compiler_params=_params(1),
        name="merge_mlp_ctx" if is_ctx else "merge_mlp_smp",
    )(x, mod_l, o_na, o_gla, gate, gout_t, g_mlp_l, w_o, w_up, w_down)


def kernel(x_prompt, x_sample, cache_k, cache_v, state_fwd, state_bwd, c, c_ctx, w_ada, b_ada, g_attn, w_in,
           g_q, g_k, rpb, w_gf, b_gf, w_gb, b_gb, g_gla_out, w_o, g_mlp, w_up, w_down):
    x_c = x_prompt.reshape(N_CTX, D_MODEL)
    x_s = x_sample.reshape(N_SMP, D_MODEL)

    c_rows = jnp.concatenate([c_ctx[None, :], c, jnp.zeros((MOD_ROWS - 1 - DEC_BATCH, D_MODEL), F32)], axis=0)
    mods = _ada_call(c_rows, w_ada, b_ada).reshape(DEPTH, MOD_ROWS, 1, N_MOD * D_MODEL)

    w_in_t = jnp.swapaxes(w_in, 1, 2)
    wgate = jnp.zeros((DEPTH, 2 * GLA_GATE_RANK, 2 * GLA_K_WIDTH), F32)
    wgate = wgate.at[:, :GLA_GATE_RANK, :GLA_K_WIDTH].set(w_gf)
    wgate = wgate.at[:, GLA_GATE_RANK:, GLA_K_WIDTH:].set(w_gb).astype(BF16)
    bgate = jnp.concatenate([b_gf, b_gb], axis=-1).reshape(DEPTH, 1, 2 * GLA_K_WIDTH)
    head_of = np.arange(MXU_WIDTH) // NA_HEAD_DIM
    seg = jnp.asarray((head_of[:, None] == head_of[None, :]).astype(np.float32) / NA_HEAD_DIM, BF16)
    gq_t = jnp.tile(g_q, (1, NA_HEADS)).reshape(DEPTH, 1, NA_WIDTH)
    gk_t = jnp.tile(g_k, (1, NA_HEADS)).reshape(DEPTH, 1, NA_WIDTH)
    gout_t = jnp.tile(g_gla_out, (1, GLA_HEADS)).reshape(DEPTH, 1, GLA_V_WIDTH)
    n_dr, n_dc = rpb.shape[2:]
    rpb_rows = jnp.pad(rpb, ((0, 0), (0, 0), (0, -n_dr % SUBLANES), (0, 2 * GRID_W - n_dc)))
    gla_w_np, gla_masks_np, gla_smask_np, gla_scales_np = _gla_constants()
    gla_consts = (jnp.asarray(gla_w_np, BF16), jnp.asarray(gla_masks_np, F32), jnp.asarray(gla_smask_np, F32),
                  jnp.asarray(gla_scales_np, F32))
    rope = _rope_tables()

    cache_kt = jnp.swapaxes(cache_k, 3, 4)
    cache_vt = jnp.swapaxes(cache_v, 3, 4)
    new_kv = None
    new_states = None

    for l in range(DEPTH):
        proj_w = (g_attn[l].reshape(1, D_MODEL), w_in_t, seg, gq_t[l], gk_t[l], wgate[l], bgate[l], rope)
        mlp_w = (gout_t[l], g_mlp[l].reshape(1, D_MODEL), w_o, w_up, w_down)

        o_na, qkf, gv, gate, *new_kv = _proj_call(x_c, mods[l], *proj_w, True, new_kv=new_kv, layer=l)
        o_gla, *new_states = _gla_call(qkf, gv, gla_consts, new_states=new_states, layer=l)
        x_c = _merge_mlp_call(x_c, mods[l], o_na, o_gla, gate, *mlp_w, layer=l)

        q, k, v, qkf, gv, gate = _proj_call(x_s, mods[l], *proj_w, False, layer=l)
        o_na = _na_call(q, k, v, cache_kt, cache_vt, rpb_rows, l)
        (o_gla,) = _gla_call(qkf, gv, gla_consts, state_in=(state_fwd, state_bwd), layer=l)
        x_s = _merge_mlp_call(x_s, mods[l], o_na, o_gla, gate, *mlp_w, layer=l)

    return (x_c.reshape(BATCH, SEQ, D_MODEL), x_s.reshape(DEC_BATCH, DEC_SEQ, D_MODEL),
            jnp.swapaxes(new_kv[0], 3, 4), jnp.swapaxes(new_kv[1], 3, 4), *new_states)
```

```python
import functools

import numpy as np
import jax
import jax.numpy as jnp
from jax import lax
from jax.experimental import pallas as pl
from jax.experimental.pallas import tpu as pltpu

F32 = jnp.float32
BF16 = jnp.bfloat16

D_MODEL = 1024
BATCH = 16
SEQ = 256
DEPTH = 4
DEC_BATCH = 2
DEC_SEQ = 1024
PAST_LEN = 512
GRID_W = 64
GRID_ROWS = DEC_SEQ // GRID_W
NA_WIDTH = D_MODEL // 2
NA_HEAD_DIM = 64
NA_HEADS = NA_WIDTH // NA_HEAD_DIM
NA_WIN_ROWS = 8
NA_WIN_COLS = 16
GLA_V_WIDTH = D_MODEL - NA_WIDTH
GLA_HEADS = 4
GLA_DV = GLA_V_WIDTH // GLA_HEADS
GLA_DK = GLA_DV // 2
GLA_K_WIDTH = GLA_HEADS * GLA_DK
GLA_GATE_RANK = 16
GLA_GATE_TAU = 16.0
GLA_CHUNK = 64
D_FF = 4 * D_MODEL
ROPE_BASE = 10000.0
N_MOD = 6
EPS = 1e-6

IN_WIDTH = 3 * NA_WIDTH + 2 * GLA_K_WIDTH + 2 * GLA_V_WIDTH + 2 * GLA_GATE_RANK
PROJ_WIDE = IN_WIDTH - 2 * GLA_GATE_RANK

N_CTX = BATCH * SEQ
N_SMP = DEC_BATCH * DEC_SEQ
SUBLANES = 8
MXU_WIDTH = 256
MOD_ROWS = SUBLANES
MASK_VALUE = -1e30

TOKEN_TILE = 512
PROJ_BLOCK = 512
FF_CHUNK = 512
MLP_CHUNK = 1024
ADA_TILE_N = 1536
GLA_LEVELS = 6
NA_ROW_GROUP = 4
GLA_CTX_SEQS_PER_STEP = 4
WEIGHT_DMA_PRIORITY = 1
VMEM_LIMIT = 52 * 1024 * 1024


def _dot(a, b):
    return jnp.dot(a, b, preferred_element_type=F32)


def _dot_nt(a, b):
    return lax.dot_general(a, b, (((1,), (1,)), ((), ())), preferred_element_type=F32)


def _dot_tn(a, b):
    return lax.dot_general(a, b, (((0,), (0,)), ((), ())), preferred_element_type=F32)


def _params(n_grid_dims=1):
    return pltpu.CompilerParams(dimension_semantics=("arbitrary",) * n_grid_dims,
                                vmem_limit_bytes=VMEM_LIMIT)


def _mod_index_map(is_ctx):
    if is_ctx:
        return lambda i: (0, 0, 0)
    return lambda i: (1 + i // (DEC_SEQ // TOKEN_TILE), 0, 0)


def _ada_kernel(c_ref, w_ref, b_ref, o_ref):
    cv = c_ref[...]
    s = cv * (1.0 / (1.0 + jnp.exp(-cv)))
    o_ref[0] = _dot(s.astype(BF16), w_ref[0].astype(BF16)) + b_ref[0]


def _ada_call(c_rows, w_ada, b_ada):
    n_mod = N_MOD * D_MODEL
    return pl.pallas_call(
        _ada_kernel,
        grid=(DEPTH, n_mod // ADA_TILE_N),
        in_specs=[
            pl.BlockSpec((MOD_ROWS, D_MODEL), lambda l, j: (0, 0)),
            pl.BlockSpec((1, D_MODEL, ADA_TILE_N), lambda l, j: (l, 0, j)),
            pl.BlockSpec((1, 1, ADA_TILE_N), lambda l, j: (l, 0, j)),
        ],
        out_specs=pl.BlockSpec((1, MOD_ROWS, ADA_TILE_N), lambda l, j: (l, 0, j)),
        out_shape=jax.ShapeDtypeStruct((DEPTH, MOD_ROWS, n_mod), F32),
        compiler_params=_params(2),
        name="ada_mod",
    )(c_rows, w_ada, b_ada.reshape(DEPTH, 1, n_mod))


def _ctx_attention(q, k, v):
    n_seq = q.shape[0] // SEQ
    lane = lax.broadcasted_iota(jnp.int32, (1, 2 * NA_HEAD_DIM), 1)
    first = lane < NA_HEAD_DIM
    pairs = [(slice(s * SEQ, (s + 1) * SEQ), slice(hp * 2 * NA_HEAD_DIM, (hp + 1) * 2 * NA_HEAD_DIM))
             for s in range(n_seq) for hp in range(NA_HEADS // 2)]
    scores = []
    for rows, cols in pairs:
        q2 = q[rows, cols]
        k2 = k[rows, cols]
        scores.append(_dot_nt(jnp.where(first, q2, jnp.zeros_like(q2)), k2))
        scores.append(_dot_nt(jnp.where(first, jnp.zeros_like(q2), q2), k2))
    probs = []
    for s in scores:
        p = jnp.exp(s - jnp.max(s, axis=-1, keepdims=True))
        probs.append((p.astype(BF16), jnp.sum(p, axis=-1, keepdims=True)))
    outs = []
    for i, (rows, cols) in enumerate(pairs):
        v2 = v[rows, cols]
        (p0, l0), (p1, l1) = probs[2 * i], probs[2 * i + 1]
        outs.append(jnp.where(first, _dot(p0, v2) / l0, _dot(p1, v2) / l1))
    per_seq = NA_HEADS // 2
    return jnp.concatenate([jnp.concatenate(outs[s * per_seq:(s + 1) * per_seq], axis=1) for s in range(n_seq)],
                           axis=0)


def _log_sigmoid(x):
    return jnp.minimum(x, 0.0) - jnp.log(1.0 + jnp.exp(-jnp.abs(x)))


def _proj_kernel(is_ctx, creates_kv, layer, x_ref, mod_ref, g_ref, seg_ref, gq_ref, gk_ref, wgate_ref, bgate_ref,
                 cos_ref, sin_ref, win_hbm, *rest):
    if is_ctx:
        if not creates_kv:
            rest = rest[2:]
        ona_out = rest[0]
        rest = rest[1:]
    else:
        q_out, k_out, v_out = rest[:3]
        rest = rest[3:]
    qkf_out, gv_out, gate_out = rest[:3]
    rest = rest[3:]
    if is_ctx:
        knew_out, vnew_out = rest[:2]
        rest = rest[2:]
    w_scr, stage, tail_stage, sem = rest

    n_blocks = PROJ_WIDE // PROJ_BLOCK
    tail = slice(PROJ_WIDE, IN_WIDTH)
    first_step = pl.program_id(0) == 0

    def block_copy(j):
        src = win_hbm.at[layer, j * PROJ_BLOCK:(j + 1) * PROJ_BLOCK, :]
        return pltpu.make_async_copy(src, stage.at[j % 2], sem.at[j % 2])

    def tail_copy():
        return pltpu.make_async_copy(win_hbm.at[layer, tail, :], tail_stage, sem.at[2])

    def land(j):
        block_copy(j).wait()
        w_scr[j * PROJ_BLOCK:(j + 1) * PROJ_BLOCK, :] = stage[j % 2].astype(BF16)
        if j + 2 < n_blocks:
            block_copy(j + 2).start(priority=WEIGHT_DMA_PRIORITY)

    def run(streaming):
        if streaming:
            block_copy(0).start(priority=WEIGHT_DMA_PRIORITY)
            block_copy(1).start(priority=WEIGHT_DMA_PRIORITY)
            tail_copy().start(priority=WEIGHT_DMA_PRIORITY)

        x = x_ref[...]
        mod = mod_ref[0]
        sh1 = mod[:, 0:D_MODEL]
        sc1 = mod[:, D_MODEL:2 * D_MODEL]
        ms = jnp.mean(x * x, axis=-1, keepdims=True)
        h = (x * lax.rsqrt(ms + EPS) * g_ref[...]) * (1.0 + sc1) + sh1
        hb = h.astype(BF16)

        def projected(j):
            if streaming:
                land(j)
            return _dot_nt(hb, w_scr[j * PROJ_BLOCK:(j + 1) * PROJ_BLOCK, :])

        seg = seg_ref[...]

        def head_mean_square(t):
            sq = (t * t).astype(BF16)
            return jnp.concatenate([_dot(sq[:, c:c + MXU_WIDTH], seg) for c in range(0, NA_WIDTH, MXU_WIDTH)],
                                   axis=1)

        q = projected(0)
        q_ms = head_mean_square(q)
        qb = (q * lax.rsqrt(q_ms + EPS) * gq_ref[...] * (NA_HEAD_DIM ** -0.5)).astype(BF16)
        k = projected(1)
        k_ms = head_mean_square(k)
        kn = k * lax.rsqrt(k_ms + EPS) * gk_ref[...]
        v = projected(2)
        if not is_ctx:
            q_out[...] = qb
            k_out[...] = kn.astype(BF16)
            v_out[...] = v.astype(BF16)
        else:
            ona_out[...] = _ctx_attention(qb, kn.astype(BF16), v.astype(BF16)).astype(BF16)
            slot = layer if creates_kv else 0
            for s in range(TOKEN_TILE // SEQ):
                rows = slice(s * SEQ, (s + 1) * SEQ)
                for hp in range(NA_HEADS // 2):
                    cols = slice(hp * 2 * NA_HEAD_DIM, (hp + 1) * 2 * NA_HEAD_DIM)
                    for out, t in ((knew_out, kn[rows, cols].T), (vnew_out, v[rows, cols].T)):
                        out[s, slot, 2 * hp] = t[:NA_HEAD_DIM]
                        out[s, slot, 2 * hp + 1] = t[NA_HEAD_DIM:]
            if creates_kv:
                for out in (knew_out, vnew_out):
                    for other in range(DEPTH):
                        if other != layer:
                            out[:, other] = jnp.zeros((TOKEN_TILE // SEQ, NA_HEADS, NA_HEAD_DIM, SEQ), F32)

        z = projected(3)
        gla_q = z[:, 0:GLA_K_WIDTH] * (GLA_DK ** -0.5)
        gla_k = z[:, GLA_K_WIDTH:]
        if not is_ctx:
            lane = lax.broadcasted_iota(jnp.int32, (1, GLA_K_WIDTH), 1)
            quarter = GLA_DK // 4
            low = (lane % (2 * quarter)) < quarter
            cos = cos_ref[...]
            sin = sin_ref[...]

            def rope(t):
                partner = jnp.where(low, pltpu.roll(t, GLA_K_WIDTH - quarter, axis=1),
                                    pltpu.roll(t, quarter, axis=1))
                return t * cos + partner * sin

            gla_q = rope(gla_q)
            gla_k = rope(gla_k)
        qkf_out[:, 0:GLA_K_WIDTH] = gla_q
        qkf_out[:, GLA_K_WIDTH:2 * GLA_K_WIDTH] = gla_k
        gv_out[...] = projected(4).astype(BF16)
        gate_out[...] = projected(5)

        if streaming:
            tail_copy().wait()
            w_scr[tail, :] = tail_stage[...].astype(BF16)
        zz_t = _dot_nt(w_scr[tail, :], hb)
        pre = _dot_tn(zz_t.astype(BF16), wgate_ref[...]) + bgate_ref[...]
        ls = _log_sigmoid(pre) * (1.0 / GLA_GATE_TAU)
        qkf_out[:, 2 * GLA_K_WIDTH:] = ls

    pl.when(first_step)(functools.partial(run, True))
    pl.when(jnp.logical_not(first_step))(functools.partial(run, False))


def _proj_call(x, mod_l, g_attn_l, w_in_t, seg, gq_t, gk_t, wgate, bgate, rope, is_ctx, new_kv=None, layer=0):
    tm = TOKEN_TILE
    n_tok = x.shape[0]
    creates_kv = is_ctx and new_kv is None
    row = lambda i: (i, 0)
    const = lambda i: (0, 0)
    rope_rows = const if is_ctx else (lambda i: (i % (DEC_SEQ // tm), 0))
    if is_ctx:
        out_shapes = [jax.ShapeDtypeStruct((n_tok, NA_WIDTH), BF16)]
    else:
        out_shapes = [jax.ShapeDtypeStruct((n_tok, NA_WIDTH), BF16)] * 3
    out_shapes += [
        jax.ShapeDtypeStruct((n_tok, 4 * GLA_K_WIDTH), F32),
        jax.ShapeDtypeStruct((n_tok, GLA_V_WIDTH), BF16),
        jax.ShapeDtypeStruct((n_tok, GLA_V_WIDTH), F32),
    ]
    out_specs = [pl.BlockSpec((tm, s.shape[1]), row) for s in out_shapes]
    in_specs = [
        pl.BlockSpec((tm, D_MODEL), row),
        pl.BlockSpec((1, 1, N_MOD * D_MODEL), _mod_index_map(is_ctx)),
        pl.BlockSpec((1, D_MODEL), const),
        pl.BlockSpec((MXU_WIDTH, MXU_WIDTH), const),
        pl.BlockSpec((1, NA_WIDTH), const),
        pl.BlockSpec((1, NA_WIDTH), const),
        pl.BlockSpec(wgate.shape, const),
        pl.BlockSpec((1, 2 * GLA_K_WIDTH), const),
        pl.BlockSpec((tm, GLA_K_WIDTH), rope_rows),
        pl.BlockSpec((tm, GLA_K_WIDTH), rope_rows),
        pl.BlockSpec(memory_space=pl.ANY),
    ]
    args = [x, mod_l, g_attn_l, seg, gq_t, gk_t, wgate, bgate, *rope, w_in_t]
    aliases = {}
    if is_ctx:
        kv_shape = (BATCH, DEPTH, NA_HEADS, NA_HEAD_DIM, SEQ)
        if creates_kv:
            kv_spec = pl.BlockSpec((tm // SEQ,) + kv_shape[1:], lambda i: (i, 0, 0, 0, 0))
        else:
            kv_spec = pl.BlockSpec((tm // SEQ, 1) + kv_shape[2:], lambda i: (i, layer, 0, 0, 0))
            aliases = {len(args): len(out_shapes), len(args) + 1: len(out_shapes) + 1}
            in_specs += [pl.BlockSpec(memory_space=pl.ANY)] * 2
            args += list(new_kv)
        out_shapes += [jax.ShapeDtypeStruct(kv_shape, F32)] * 2
        out_specs += [kv_spec, kv_spec]
    return pl.pallas_call(
        functools.partial(_proj_kernel, is_ctx, creates_kv, layer),
        grid=(n_tok // tm,),
        in_specs=in_specs,
        out_specs=out_specs,
        out_shape=out_shapes,
        input_output_aliases=aliases,
        scratch_shapes=[
            pltpu.VMEM((IN_WIDTH, D_MODEL), BF16),
            pltpu.VMEM((2, PROJ_BLOCK, D_MODEL), F32),
            pltpu.VMEM((IN_WIDTH - PROJ_WIDE, D_MODEL), F32),
            pltpu.SemaphoreType.DMA((3,)),
        ],
        compiler_params=_params(1),
        name="proj_ctx" if is_ctx else "proj_smp",
    )(*args)


def _na_window_start(r):
    return min(max(r - NA_WIN_ROWS // 2, 0), GRID_ROWS - NA_WIN_ROWS)


def _na_row_groups():
    groups = []
    for g in range(GRID_ROWS // NA_ROW_GROUP):
        starts = [_na_window_start(r) for r in range(g * NA_ROW_GROUP, (g + 1) * NA_ROW_GROUP)]
        lo = min(starts) // 2 * 2
        count = max(starts) + NA_WIN_ROWS - lo
        groups.append((lo, count + count % 2))
    return groups


def _na_bias_tiles(rpb_ref, hh, tiles_scr):
    n_dr = 2 * NA_WIN_ROWS - 1
    lanes = 2 * GRID_W
    lane = lax.broadcasted_iota(jnp.int32, (GRID_W, lanes), 1)
    col = lax.broadcasted_iota(jnp.int32, (GRID_W, lanes), 0)
    first = lane < GRID_W
    c2 = lane % GRID_W
    cs = jnp.clip(col - NA_WIN_COLS // 2, 0, GRID_W - NA_WIN_COLS)
    valid = (c2 >= cs) & (c2 < cs + NA_WIN_COLS)
    rows = rpb_ref[0, hh]
    base = lanes - (NA_WIN_COLS - 1)

    def skew(dr, shift):
        row = jnp.broadcast_to(rows[dr:dr + 1], (GRID_W, lanes))
        return pltpu.roll(row, shift, axis=1, stride=1, stride_axis=0)

    for e in range(n_dr - 1):
        tile = jnp.where(first, skew(e, base), skew(e + 1, (base + GRID_W) % lanes))
        tiles_scr[e] = jnp.where(valid, tile, MASK_VALUE)


def _na_group_bias(tiles_scr, g, key_lo, key_count, first):
    masked = jnp.full((GRID_W, 2 * GRID_W), MASK_VALUE, F32)
    row_tiles = []
    for r in range(g * NA_ROW_GROUP, (g + 1) * NA_ROW_GROUP):
        rs = _na_window_start(r)
        pieces = []
        for u in range(key_count // 2):
            r0 = key_lo + 2 * u
            in0 = rs <= r0 < rs + NA_WIN_ROWS
            in1 = rs <= r0 + 1 < rs + NA_WIN_ROWS
            if not (in0 or in1):
                pieces.append(masked)
                continue
            tile = tiles_scr[r0 - r + NA_WIN_ROWS - 1]
            if not in1:
                tile = jnp.where(first, tile, MASK_VALUE)
            elif not in0:
                tile = jnp.where(first, MASK_VALUE, tile)
            pieces.append(tile)
        row_tiles.append(jnp.concatenate(pieces, axis=1))
    return jnp.concatenate(row_tiles, axis=0)


def _na_kernel(q_ref, k_ref, v_ref, kc_ref, vc_ref, rpb_ref, o_ref, sc_ref, tiles_scr):
    lane = lax.broadcasted_iota(jnp.int32, (1, 2 * NA_HEAD_DIM), 1)
    first = lane < NA_HEAD_DIM
    q2 = q_ref[...]
    kc2 = jnp.concatenate([kc_ref[0, 0, 0], kc_ref[0, 0, 1]], axis=0).astype(BF16)
    vc2 = jnp.concatenate([vc_ref[0, 0, 0], vc_ref[0, 0, 1]], axis=0).astype(BF16)
    block = NA_ROW_GROUP * GRID_W
    groups = _na_row_groups()
    qms = [jnp.where(first, q2, jnp.zeros_like(q2)), jnp.where(first, jnp.zeros_like(q2), q2)]
    for hh in range(2):
        sc_ref[hh] = _dot(qms[hh], kc2)
        _na_bias_tiles(rpb_ref, hh, tiles_scr.at[hh])
    problems = [(hh, g) for hh in range(2) for g in range(len(groups))]
    scores = []
    for hh, g in problems:
        key_lo, key_count = groups[g]
        keys = slice(key_lo * GRID_W, (key_lo + key_count) * GRID_W)
        s = _dot_nt(qms[hh][g * block:(g + 1) * block], k_ref[keys, :])
        scores.append(s + _na_group_bias(tiles_scr.at[hh], g, key_lo, key_count, first))
    probs = []
    for (hh, g), s in zip(problems, scores):
        sc = sc_ref[hh, g * block:(g + 1) * block, :]
        m = jnp.maximum(jnp.max(s, axis=-1, keepdims=True), jnp.max(sc, axis=-1, keepdims=True))
        p = jnp.exp(s - m)
        pc = jnp.exp(sc - m)
        l = jnp.sum(p, axis=-1, keepdims=True) + jnp.sum(pc, axis=-1, keepdims=True)
        probs.append((p.astype(BF16), pc.astype(BF16), l))
    outs = [[], []]
    for (hh, g), (p, pc, l) in zip(problems, probs):
        key_lo, key_count = groups[g]
        keys = slice(key_lo * GRID_W, (key_lo + key_count) * GRID_W)
        outs[hh].append((_dot(p, v_ref[keys, :]) + _dot_nt(pc, vc2)) / l)
    outs = [jnp.concatenate(o, axis=0) for o in outs]
    o_ref[...] = jnp.where(first, outs[0], outs[1]).astype(BF16)


def _na_call(q, k, v, cache_kt, cache_vt, rpb_rows, layer):
    tok = pl.BlockSpec((DEC_SEQ, 2 * NA_HEAD_DIM), lambda b, hp: (b, hp))
    cache = pl.BlockSpec((1, 1, 2, NA_HEAD_DIM, PAST_LEN), lambda b, hp: (b, layer, hp, 0, 0))
    n_pairs = 2 * NA_WIN_ROWS - 2
    return pl.pallas_call(
        _na_kernel,
        grid=(DEC_BATCH, NA_HEADS // 2),
        in_specs=[
            tok, tok, tok, cache, cache,
            pl.BlockSpec((1, 2) + rpb_rows.shape[2:], lambda b, hp: (layer, hp, 0, 0)),
        ],
        out_specs=tok,
        out_shape=jax.ShapeDtypeStruct((N_SMP, NA_WIDTH), BF16),
        scratch_shapes=[pltpu.VMEM((2, DEC_SEQ, PAST_LEN), F32),
                        pltpu.VMEM((2, n_pairs, GRID_W, 2 * GRID_W), F32)],
        compiler_params=_params(2),
        name="nbr_attn",
    )(q, k, v, cache_kt, cache_vt, rpb_rows)


def _gla_constants():
    C = GLA_CHUNK
    ii, jj = np.meshgrid(np.arange(C), np.arange(C), indexing="ij")
    tri = np.stack([jj <= ii, jj >= ii]).astype(np.float32)
    w = np.concatenate([tri, tri], axis=-1)

    x = ii ^ jj
    hb = np.where(x > 0, np.floor(np.log2(np.maximum(x, 1))), -1).astype(np.int64)
    masks = np.zeros((2, GLA_LEVELS + 1, C, C), np.float32)
    for p in range(GLA_LEVELS):
        masks[0, p] = (jj < ii) & (hb == p)
        masks[1, p] = (jj > ii) & (hb == p)
    masks[:, GLA_LEVELS] = np.eye(C)
    masks = np.tile(masks, (1, 1, 2, 2))
    state_mask = (np.arange(2 * GLA_DV)[:, None] // GLA_DV == np.arange(2 * GLA_DK)[None, :] // GLA_DK)
    upper = np.stack([(np.arange(C) >> p) & 1 for p in range(GLA_LEVELS)]).astype(bool)
    sign = np.stack([np.where(upper, 1.0, -1.0), np.where(upper, -1.0, 1.0)])
    scales = np.broadcast_to((sign * np.log2(np.e))[..., None], (2, GLA_LEVELS, C, GLA_K_WIDTH))
    return w, masks, state_mask.astype(np.float32), np.ascontiguousarray(scales, np.float32)


def _gla_kernel(seq_len, n_seq, is_ctx, creates_states, layer, *refs):
    refs = list(refs)
    gq_ref, gk_ref, gv_ref, lf_ref, lb_ref, w_ref, mask_ref, smask_ref, scale_ref = refs[:9]
    refs = refs[9:]
    if is_ctx:
        if not creates_states:
            refs = refs[2:]
        o_ref, sf_ref, sb_ref = refs[:3]
        refs = refs[3:]
    else:
        s0f_ref, s0b_ref, o_ref = refs[:3]
        refs = refs[3:]
    st_scr, cum_scr = refs

    C = GLA_CHUNK
    n_chunks = seq_len // C
    n_pairs = GLA_HEADS // 2
    pair_k = 2 * GLA_DK
    pair_v = 2 * GLA_DV

    st_scr[...] = jnp.zeros(st_scr.shape, F32)
    if not is_ctx:
        for d, s0_ref in enumerate((s0f_ref, s0b_ref)):
            for s in range(n_seq):
                for h in range(GLA_HEADS):
                    pr, e = divmod(h, 2)
                    st_scr[s, d, pr, e * GLA_DV:(e + 1) * GLA_DV, e * GLA_DK:(e + 1) * GLA_DK] = s0_ref[s, 0, h].T

    klane = lax.broadcasted_iota(jnp.int32, (1, GLA_K_WIDTH), 1)
    even_head = (klane // GLA_DK) % 2 == 0

    def chunk_decay(s, d, r0, f_ref):
        rows = pl.ds(s * seq_len + r0, C)
        f = f_ref[rows, :]
        f_hi = f.astype(BF16)
        f_lo = (f - f_hi.astype(F32)).astype(BF16)
        cum = _dot(w_ref[d], jnp.concatenate([f_hi, f_lo], axis=0))
        cum_scr[s, d] = cum
        return rows, cum

    def chunk_scores(s, d, rows, cum):
        def cum_rows(r, n):
            return jnp.broadcast_to(cum_scr[s, d, r:r + 1, :], (n, GLA_K_WIDTH))

        def level_factor(p):
            m = 1 << p
            edge = m - 1 if d == 0 else m
            if 2 * m >= SUBLANES:
                ref = jnp.concatenate([cum_rows(blk + edge, 2 * m) for blk in range(0, C, 2 * m)], axis=0)
            elif 2 * m == SUBLANES // 2:
                sub = lax.broadcasted_iota(jnp.int32, (SUBLANES, GLA_K_WIDTH), 0)
                ref = jnp.concatenate([jnp.where(sub < 2 * m, cum_rows(blk + edge, SUBLANES),
                                                 cum_rows(blk + 2 * m + edge, SUBLANES))
                                       for blk in range(0, C, SUBLANES)], axis=0)
            else:
                odd = lax.broadcasted_iota(jnp.int32, (C, GLA_K_WIDTH), 0) % 2 == 1
                if d == 0:
                    ref = jnp.where(odd, pltpu.roll(cum, 1, axis=0), cum)
                else:
                    ref = jnp.where(odd, cum, pltpu.roll(cum, C - 1, axis=0))
            return jnp.exp2((cum - ref) * scale_ref[d, p])

        qc = gq_ref[rows, :]
        kc = gk_ref[rows, :]
        vc = gv_ref[rows, :]
        last = C - 1 if d == 0 else 0
        eq = jnp.exp(cum)
        ek = jnp.exp(cum_rows(last, C) - cum)
        total = jnp.exp(cum_scr[s, d, last:last + 1, :])

        qb = qc.astype(BF16)
        kb = kc.astype(BF16)
        kb_even = jnp.where(even_head, kb, jnp.zeros_like(kb))
        kb_odd = jnp.where(even_head, jnp.zeros_like(kb), kb)
        zero = jnp.zeros((C, pair_k), BF16)
        a = None
        for p in range(GLA_LEVELS + 1):
            if p < GLA_LEVELS:
                ep = level_factor(p).astype(BF16)
                qe, ke_even, ke_odd = qb * ep, kb_even * ep, kb_odd * ep
            else:
                qe, ke_even, ke_odd = qb, kb_even, kb_odd
            lhs = jnp.concatenate([jnp.concatenate([qe[:, :pair_k], zero], axis=1),
                                   jnp.concatenate([zero, qe[:, pair_k:]], axis=1)], axis=0)
            sc = _dot_nt(lhs, jnp.concatenate([ke_even, ke_odd], axis=0)) * mask_ref[d, p]
            a = sc if a is None else a + sc
        q_in = (qc * eq).astype(BF16)
        k_out = (kc * ek).astype(BF16)
        return a.astype(BF16), q_in, k_out, vc, total

    def chunk_output(s, d, rows, accumulate, a, q_in, k_out, vc, total):
        zero_v = jnp.zeros((C, GLA_DV), BF16)
        for pr in range(n_pairs):
            kl = slice(pr * pair_k, (pr + 1) * pair_k)
            vp = vc[:, pr * pair_v:(pr + 1) * pair_v]
            v_diag = jnp.concatenate([jnp.concatenate([vp[:, :GLA_DV], zero_v], axis=1),
                                      jnp.concatenate([zero_v, vp[:, GLA_DV:]], axis=1)], axis=0)
            st = st_scr[s, d, pr]
            o = _dot(a[pr * C:(pr + 1) * C], v_diag) + _dot_nt(q_in[:, kl], st.astype(BF16))
            if accumulate:
                o_ref[rows, pr * pair_v:(pr + 1) * pair_v] += o
            else:
                o_ref[rows, pr * pair_v:(pr + 1) * pair_v] = o
            st_scr[s, d, pr] = st * total[:, kl] + _dot_tn(vp, k_out[:, kl]) * smask_ref[...]

    def body(accumulate, n, carry):
        fwd = pl.multiple_of(n * C, C)
        bwd = pl.multiple_of((n_chunks - 1 - n) * C, C)
        scans = [(s, d) for s in range(n_seq) for d in range(2)]
        decays = [chunk_decay(s, d, bwd if d else fwd, lb_ref if d else lf_ref) for s, d in scans]
        scores = [chunk_scores(s, d, *dec) for (s, d), dec in zip(scans, decays)]
        for (s, d), (rows, _), sc in zip(scans, decays, scores):
            chunk_output(s, d, rows, accumulate, *sc)
        return carry

    half = n_chunks // 2
    lax.fori_loop(0, half, functools.partial(body, False), 0)
    lax.fori_loop(half, n_chunks, functools.partial(body, True), 0)
    if is_ctx:
        slot = layer if creates_states else 0
        for d, s_ref in enumerate((sf_ref, sb_ref)):
            for s in range(n_seq):
                for h in range(GLA_HEADS):
                    pr, e = divmod(h, 2)
                    s_ref[s, slot, h] = st_scr[s, d, pr, e * GLA_DV:(e + 1) * GLA_DV, e * GLA_DK:(e + 1) * GLA_DK].T
            if creates_states:
                for other in range(DEPTH):
                    if other != layer:
                        s_ref[:, other] = jnp.zeros((n_seq, GLA_HEADS, GLA_DK, GLA_DV), F32)


def _gla_call(qkf, gv, consts, new_states=None, state_in=None, layer=0):
    is_ctx = state_in is None
    creates_states = is_ctx and new_states is None
    seq_len = SEQ if is_ctx else DEC_SEQ
    n_tok = qkf.shape[0]
    n_seq = GLA_CTX_SEQS_PER_STEP if is_ctx else DEC_BATCH
    rows = n_seq * seq_len
    tok = lambda width: pl.BlockSpec((rows, width), lambda b: (b, 0))
    whole = lambda a: pl.BlockSpec(a.shape, lambda b: (0,) * a.ndim)
    state_spec = pl.BlockSpec((n_seq, 1, GLA_HEADS, GLA_DK, GLA_DV), lambda b: (b, layer, 0, 0, 0))
    qkf_block = lambda j: pl.BlockSpec((rows, GLA_K_WIDTH), lambda b: (b, j))
    in_specs = [qkf_block(0), qkf_block(1), tok(GLA_V_WIDTH), qkf_block(2), qkf_block(3)]
    in_specs += [whole(a) for a in consts]
    args = [qkf, qkf, gv, qkf, qkf, *consts]
    out_shape = [jax.ShapeDtypeStruct((n_tok, GLA_V_WIDTH), F32)]
    out_specs = [tok(GLA_V_WIDTH)]
    aliases = {}
    if is_ctx:
        states_shape = (BATCH, DEPTH, GLA_HEADS, GLA_DK, GLA_DV)
        if creates_states:
            new_spec = pl.BlockSpec((n_seq,) + states_shape[1:], lambda b: (b, 0, 0, 0, 0))
        else:
            new_spec = state_spec
            aliases = {len(args): 1, len(args) + 1: 2}
            in_specs += [pl.BlockSpec(memory_space=pl.ANY)] * 2
            args += list(new_states)
        out_shape += [jax.ShapeDtypeStruct(states_shape, F32)] * 2
        out_specs += [new_spec, new_spec]
    else:
        in_specs += [state_spec, state_spec]
        args += list(state_in)
    return pl.pallas_call(
        functools.partial(_gla_kernel, seq_len, n_seq, is_ctx, creates_states, layer),
        grid=(n_tok // rows,),
        in_specs=in_specs,
        out_specs=out_specs,
        out_shape=out_shape,
        input_output_aliases=aliases,
        scratch_shapes=[
            pltpu.VMEM((n_seq, 2, GLA_HEADS // 2, 2 * GLA_DV, 2 * GLA_DK), F32),
            pltpu.VMEM((n_seq, 2, GLA_CHUNK, GLA_K_WIDTH), F32),
        ],
        compiler_params=_params(1),
        name="gla_ctx" if is_ctx else "gla_smp",
    )(*args)


def _rope_tables():
    quarter = GLA_DK // 4
    inv = ROPE_BASE ** (-jnp.arange(quarter, dtype=F32) / quarter)
    t = jnp.arange(DEC_SEQ)
    ang_r = (t // GRID_W).astype(F32)[:, None] * inv
    ang_c = (t % GRID_W).astype(F32)[:, None] * inv
    cos = jnp.concatenate([jnp.cos(ang_r)] * 2 + [jnp.cos(ang_c)] * 2, axis=-1)
    sin = jnp.concatenate([-jnp.sin(ang_r), jnp.sin(ang_r), -jnp.sin(ang_c), jnp.sin(ang_c)], axis=-1)
    return jnp.tile(cos, (1, GLA_HEADS)), jnp.tile(sin, (1, GLA_HEADS))


def _merge_mlp_kernel(layer, x_ref, mod_ref, ona_ref, ogla_ref, gate_ref, gout_ref, gmlp_ref,
                      wo_hbm, wup_hbm, wdown_hbm, o_ref,
                      wo_scr, wup_scr, wdown_scr, stage_cols, stage_rows, sem):
    first_step = pl.program_id(0) == 0
    n_chunks = D_FF // FF_CHUNK
    wo_halves = D_MODEL // FF_CHUNK

    def col_copy(j):
        slot = j % 2
        if j < wo_halves:
            src = wo_hbm.at[layer, :, j * FF_CHUNK:(j + 1) * FF_CHUNK]
        else:
            c = j - wo_halves
            src = wup_hbm.at[layer, :, c * FF_CHUNK:(c + 1) * FF_CHUNK]
        return pltpu.make_async_copy(src, stage_cols.at[slot], sem.at[slot])

    def row_copy(c):
        slot = c % 2
        src = wdown_hbm.at[layer, c * FF_CHUNK:(c + 1) * FF_CHUNK, :]
        return pltpu.make_async_copy(src, stage_rows.at[slot], sem.at[2 + slot])

    n_col_blocks = wo_halves + n_chunks

    def land_col(j):
        col_copy(j).wait()
        block = stage_cols[j % 2].astype(BF16)
        if j < wo_halves:
            wo_scr[:, j * FF_CHUNK:(j + 1) * FF_CHUNK] = block
        else:
            c = j - wo_halves
            wup_scr[:, c * FF_CHUNK:(c + 1) * FF_CHUNK] = block
        if j + 2 < n_col_blocks:
            col_copy(j + 2).start(priority=WEIGHT_DMA_PRIORITY)

    def land_row(c):
        row_copy(c).wait()
        wdown_scr[c * FF_CHUNK:(c + 1) * FF_CHUNK, :] = stage_rows[c % 2].astype(BF16)
        if c + 2 < n_chunks:
            row_copy(c + 2).start(priority=WEIGHT_DMA_PRIORITY)

    def run(streaming):
        if streaming:
            col_copy(0).start(priority=WEIGHT_DMA_PRIORITY)
            col_copy(1).start(priority=WEIGHT_DMA_PRIORITY)
            row_copy(0).start(priority=WEIGHT_DMA_PRIORITY)
            row_copy(1).start(priority=WEIGHT_DMA_PRIORITY)

        mod = mod_ref[0]
        ga1 = mod[:, 2 * D_MODEL:3 * D_MODEL]
        sh2 = mod[:, 3 * D_MODEL:4 * D_MODEL]
        sc2 = mod[:, 4 * D_MODEL:5 * D_MODEL]
        ga2 = mod[:, 5 * D_MODEL:6 * D_MODEL]

        og = ogla_ref[...]
        normed = []
        for h in range(GLA_HEADS):
            oh = og[:, h * GLA_DV:(h + 1) * GLA_DV]
            normed.append(oh * lax.rsqrt(jnp.mean(oh * oh, axis=-1, keepdims=True) + EPS))
        gate = gate_ref[...]
        g = jnp.concatenate(normed, axis=1) * gout_ref[...] * (gate * (1.0 / (1.0 + jnp.exp(-gate))))

        if streaming:
            for j in range(wo_halves):
                land_col(j)
        att = _dot(ona_ref[...], wo_scr[0:NA_WIDTH, :]) + _dot(g.astype(BF16), wo_scr[NA_WIDTH:, :])
        x = x_ref[...] + ga1 * att
        ms = jnp.mean(x * x, axis=-1, keepdims=True)
        h2 = ((x * lax.rsqrt(ms + EPS) * gmlp_ref[...]) * (1.0 + sc2) + sh2).astype(BF16)

        width = FF_CHUNK if streaming else MLP_CHUNK
        acc = None
        for c in range(D_FF // width):
            if streaming:
                land_col(wo_halves + c)
                land_row(c)
            u = jnp.maximum(_dot(h2, wup_scr[:, c * width:(c + 1) * width]), 0.0)
            down = _dot((u * u).astype(BF16), wdown_scr[c * width:(c + 1) * width, :])
            acc = down if acc is None else acc + down
        o_ref[...] = x + ga2 * acc

    pl.when(first_step)(functools.partial(run, True))
    pl.when(jnp.logical_not(first_step))(functools.partial(run, False))


def _merge_mlp_call(x, mod_l, o_na, o_gla, gate, gout_t, g_mlp_l, w_o, w_up, w_down, layer):
    tm = TOKEN_TILE
    n_tok = x.shape[0]
    is_ctx = n_tok == N_CTX
    row = lambda i: (i, 0)
    const = lambda i: (0, 0)
    in_hbm = pl.BlockSpec(memory_space=pl.ANY)
    return pl.pallas_call(
        functools.partial(_merge_mlp_kernel, layer),
        grid=(n_tok // tm,),
        in_specs=[
            pl.BlockSpec((tm, D_MODEL), row),
            pl.BlockSpec((1, 1, N_MOD * D_MODEL), _mod_index_map(is_ctx)),
            pl.BlockSpec((tm, NA_WIDTH), row),
            pl.BlockSpec((tm, GLA_V_WIDTH), row),
            pl.BlockSpec((tm, GLA_V_WIDTH), row),
            pl.BlockSpec((1, GLA_V_WIDTH), const),
            pl.BlockSpec((1, D_MODEL), const),
            in_hbm, in_hbm, in_hbm,
        ],
        out_specs=pl.BlockSpec((tm, D_MODEL), row),
        out_shape=jax.ShapeDtypeStruct((n_tok, D_MODEL), F32),
        scratch_shapes=[
            pltpu.VMEM((D_MODEL, D_MODEL), BF16),
            pltpu.VMEM((D_MODEL, D_FF), BF16),
            pltpu.VMEM((D_FF, D_MODEL), BF16),
            pltpu.VMEM((2, D_MODEL, FF_CHUNK), F32),
            pltpu.VMEM((2, FF_CHUNK, D_MODEL), F32),
            pltpu.SemaphoreType.DMA((4,)),
        ],
        compiler_params=_params(1),
        name="merge_mlp_ctx" if is_ctx else "merge_mlp_smp",
    )(x, mod_l, o_na, o_gla, gate, gout_t, g_mlp_l, w_o, w_up, w_down)


def kernel(x_prompt, x_sample, cache_k, cache_v, state_fwd, state_bwd, c, c_ctx, w_ada, b_ada, g_attn, w_in,
           g_q, g_k, rpb, w_gf, b_gf, w_gb, b_gb, g_gla_out, w_o, g_mlp, w_up, w_down):
    x_c = x_prompt.reshape(N_CTX, D_MODEL)
    x_s = x_sample.reshape(N_SMP, D_MODEL)

    c_rows = jnp.concatenate([c_ctx[None, :], c, jnp.zeros((MOD_ROWS - 1 - DEC_BATCH, D_MODEL), F32)], axis=0)
    mods = _ada_call(c_rows, w_ada, b_ada).reshape(DEPTH, MOD_ROWS, 1, N_MOD * D_MODEL)

    w_in_t = jnp.swapaxes(w_in, 1, 2)
    wgate = jnp.zeros((DEPTH, 2 * GLA_GATE_RANK, 2 * GLA_K_WIDTH), F32)
    wgate = wgate.at[:, :GLA_GATE_RANK, :GLA_K_WIDTH].set(w_gf)
    wgate = wgate.at[:, GLA_GATE_RANK:, GLA_K_WIDTH:].set(w_gb).astype(BF16)
    bgate = jnp.concatenate([b_gf, b_gb], axis=-1).reshape(DEPTH, 1, 2 * GLA_K_WIDTH)
    head_of = np.arange(MXU_WIDTH) // NA_HEAD_DIM
    seg = jnp.asarray((head_of[:, None] == head_of[None, :]).astype(np.float32) / NA_HEAD_DIM, BF16)
    gq_t = jnp.tile(g_q, (1, NA_HEADS)).reshape(DEPTH, 1, NA_WIDTH)
    gk_t = jnp.tile(g_k, (1, NA_HEADS)).reshape(DEPTH, 1, NA_WIDTH)
    gout_t = jnp.tile(g_gla_out, (1, GLA_HEADS)).reshape(DEPTH, 1, GLA_V_WIDTH)
    n_dr, n_dc = rpb.shape[2:]
    rpb_rows = jnp.pad(rpb, ((0, 0), (0, 0), (0, -n_dr % SUBLANES), (0, 2 * GRID_W - n_dc)))
    gla_w_np, gla_masks_np, gla_smask_np, gla_scales_np = _gla_constants()
    gla_consts = (jnp.asarray(gla_w_np, BF16), jnp.asarray(gla_masks_np, F32), jnp.asarray(gla_smask_np, F32),
                  jnp.asarray(gla_scales_np, F32))
    rope = _rope_tables()

    cache_kt = jnp.swapaxes(cache_k, 3, 4)
    cache_vt = jnp.swapaxes(cache_v, 3, 4)
    new_kv = None
    new_states = None

    for l in range(DEPTH):
        proj_w = (g_attn[l].reshape(1, D_MODEL), w_in_t, seg, gq_t[l], gk_t[l], wgate[l], bgate[l], rope)
        mlp_w = (gout_t[l], g_mlp[l].reshape(1, D_MODEL), w_o, w_up, w_down)

        o_na, qkf, gv, gate, *new_kv = _proj_call(x_c, mods[l], *proj_w, True, new_kv=new_kv, layer=l)
        o_gla, *new_states = _gla_call(qkf, gv, gla_consts, new_states=new_states, layer=l)
        x_c = _merge_mlp_call(x_c, mods[l], o_na, o_gla, gate, *mlp_w, layer=l)

        q, k, v, qkf, gv, gate = _proj_call(x_s, mods[l], *proj_w, False, layer=l)
        o_na = _na_call(q, k, v, cache_kt, cache_vt, rpb_rows, l)
        (o_gla,) = _gla_call(qkf, gv, gla_consts, state_in=(state_fwd, state_bwd), layer=l)
        x_s = _merge_mlp_call(x_s, mods[l], o_na, o_gla, gate, *mlp_w, layer=l)

    return (x_c.reshape(BATCH, SEQ, D_MODEL), x_s.reshape(DEC_BATCH, DEC_SEQ, D_MODEL),
            jnp.swapaxes(new_kv[0], 3, 4), jnp.swapaxes(new_kv[1], 3, 4), *new_states)
```
